```python
import math
import jax, jax.numpy as jnp
from jax import lax

D_MODEL = 1024
BATCH = 16
SEQ = 256
DEPTH = 4
DEC_BATCH = 2
DEC_SEQ = 4096
PAST_LEN = 512

GRID_W = 64
HEAD_DIM = 64
A_HEADS = 8
A_DK = HEAD_DIM
A_DV = HEAD_DIM
A_W = A_HEADS * A_DV
CONV_K = 5
CHUNK = 64
B_HEADS = 8
B_KV = 2
C_HEADS = 8
C_KV = 2
WINDOW = 128
QBLOCK = 128
D_FF = 4 * D_MODEL
ROPE_THETA = 10000.0
EPS = 1e-6
IN_SIZES = (3 * A_W, A_W, 2 * A_HEADS, 2 * A_HEADS,
            B_HEADS * HEAD_DIM, B_KV * HEAD_DIM, B_KV * HEAD_DIM,
            C_HEADS * HEAD_DIM, C_KV * HEAD_DIM, C_KV * HEAD_DIM, 3 * D_MODEL)
IN_COLS = sum(IN_SIZES)

kernel_name = "hybrid_flow_prefix_gdn_gqa_swa"


def _rmsnorm(x, w):
    xf = x.astype(jnp.float32)
    y = xf * lax.rsqrt(jnp.mean(xf * xf, axis=-1, keepdims=True) + EPS)
    return (y * w.astype(jnp.float32)).astype(x.dtype)


def _l2norm(x):
    xf = x.astype(jnp.float32)
    return xf * lax.rsqrt(jnp.sum(xf * xf, axis=-1, keepdims=True) + EPS)


def _mod_norm(x, w, shift, scale):
    return _rmsnorm(x, w) * (1 + scale) + shift


def _split_cols(proj):
    parts, off = [], 0
    for size in IN_SIZES:
        parts.append(proj[..., off:off + size])
        off += size
    return parts


def _heads(x, n_heads):
    return x.reshape(x.shape[0], x.shape[1], n_heads, HEAD_DIM)


def _axial_rope_tables(n_tokens):
    rows = n_tokens // GRID_W
    row_id = jnp.repeat(jnp.arange(rows, dtype=jnp.float32), GRID_W)
    col_id = jnp.tile(jnp.arange(GRID_W, dtype=jnp.float32), rows)
    n_freq = HEAD_DIM // 4
    inv_freq = ROPE_THETA ** (-jnp.arange(n_freq, dtype=jnp.float32) / n_freq)
    ang = jnp.concatenate([row_id[:, None] * inv_freq, col_id[:, None] * inv_freq], axis=-1)
    return jnp.cos(ang), jnp.sin(ang)


def _rope(x, cos, sin):
    half = HEAD_DIM // 2
    xf = x.astype(jnp.float32)
    x1, x2 = xf[..., :half], xf[..., half:]
    cs, sn = cos[None, :, None, :], sin[None, :, None, :]
    return jnp.concatenate([x1 * cs - x2 * sn, x2 * cs + x1 * sn], axis=-1).astype(x.dtype)


def _short_conv(x, w):
    ch = x.shape[-1]
    return lax.conv_general_dilated(x, w[:, None, :].astype(x.dtype), (1,),
                                    [(CONV_K // 2, CONV_K // 2)],
                                    dimension_numbers=('NWC', 'WIO', 'NWC'),
                                    feature_group_count=ch)


def _gdn_features(qkv_raw, beta_raw, alpha_raw, conv_w, a_log, dt_bias):
    qkv = jax.nn.silu(_short_conv(qkv_raw, conv_w))
    bn, t = qkv.shape[0], qkv.shape[1]
    q, k, v = jnp.split(qkv, 3, axis=-1)
    q = _l2norm(q.reshape(bn, t, A_HEADS, A_DK)) * (A_DK ** -0.5)
    k = _l2norm(k.reshape(bn, t, A_HEADS, A_DK))
    v = v.reshape(bn, t, A_HEADS, A_DV)
    beta = jax.nn.sigmoid(beta_raw.astype(jnp.float32)).reshape(bn, t, 2, A_HEADS)
    g = -jnp.exp(a_log.astype(jnp.float32)) * jax.nn.softplus(
        alpha_raw.astype(jnp.float32).reshape(bn, t, 2, A_HEADS) + dt_bias.astype(jnp.float32))
    return q, k, v, beta, g


def _chunked_delta(q, k, v, beta, g, s0):
    bn, t, h, dk = q.shape
    dv = v.shape[-1]
    n = t // CHUNK

    def chunks4(x):
        return x.astype(jnp.float32).reshape(bn, n, CHUNK, h, x.shape[-1]).transpose(1, 0, 3, 2, 4)

    def chunks3(x):
        return x.astype(jnp.float32).reshape(bn, n, CHUNK, h).transpose(1, 0, 3, 2)

    q_, k_, v_ = chunks4(q), chunks4(k), chunks4(v)
    b_, g_ = chunks3(beta), chunks3(g)
    gc = jnp.cumsum(g_, axis=-1)
    idx = jnp.arange(CHUNK)
    incl = idx[:, None] >= idx[None, :]
    strict = idx[:, None] > idx[None, :]
    diff = gc[..., :, None] - gc[..., None, :]
    decay = jnp.where(incl, jnp.exp(jnp.where(incl, diff, 0.0)), 0.0)
    kk = jnp.einsum('nbhid,nbhjd->nbhij', k_, k_)
    a_mat = jnp.where(strict, b_[..., :, None] * kk * decay, 0.0) + jnp.eye(CHUNK, dtype=jnp.float32)
    eg = jnp.exp(gc)
    rhs = jnp.concatenate([b_[..., None] * v_, (b_ * eg)[..., None] * k_], axis=-1)
    sol = lax.linalg.triangular_solve(a_mat, rhs, left_side=True, lower=True, unit_diagonal=True)
    u_p, w = sol[..., :dv], sol[..., dv:]
    attn = jnp.einsum('nbhid,nbhjd->nbhij', q_, k_) * decay
    q_g = q_ * eg[..., None]
    k_g = k_ * jnp.exp(gc[..., -1:] - gc)[..., None]
    g_last = jnp.exp(gc[..., -1])

    def step(s, xs):
        up_c, w_c, q_c, a_c, k_c, gl = xs
        u = up_c - jnp.einsum('bhcd,bhde->bhce', w_c, s)
        o = jnp.einsum('bhcd,bhde->bhce', q_c, s) + jnp.einsum('bhij,bhje->bhie', a_c, u)
        s_new = gl[..., None, None] * s + jnp.einsum('bhcd,bhce->bhde', k_c, u)
        return s_new, o

    s_fin, o = lax.scan(step, s0.astype(jnp.float32), (u_p, w, q_g, attn, k_g, g_last))
    o = o.transpose(1, 0, 3, 2, 4).reshape(bn, t, h, dv)
    return o, s_fin


def _gdn_bidir(q, k, v, beta, g, s_fwd, s_bwd):
    o_f, sf = _chunked_delta(q, k, v, beta[:, :, 0], g[:, :, 0], s_fwd)
    flip = lambda a: jnp.flip(a, axis=1)
    o_b, sb = _chunked_delta(flip(q), flip(k), flip(v), flip(beta[:, :, 1]), flip(g[:, :, 1]), s_bwd)
    return o_f + flip(o_b), sf, sb


def _gated_out(o, z, w):
    bn, t = o.shape[0], o.shape[1]
    on = _rmsnorm(o, w)
    zf = jax.nn.silu(z.astype(jnp.float32)).reshape(bn, t, A_HEADS, A_DV)
    return (on * zf).reshape(bn, t, A_W).astype(z.dtype)


def _attend_dense(q, k, v, sink):
    bn, t, hq, hd = q.shape
    kv = k.shape[2]
    grp = hq // kv
    nb = t // QBLOCK
    scale = hd ** -0.5
    qb = q.reshape(bn, nb, QBLOCK, kv, grp, hd).transpose(1, 0, 2, 3, 4, 5)

    def blk(qi):
        s = jnp.einsum('bqkgd,bskd->bkgqs', qi, k).astype(jnp.float32) * scale
        if sink is not None:
            sk = jnp.broadcast_to(sink.astype(jnp.float32).reshape(kv, grp)[None, :, :, None, None],
                                  s.shape[:-1] + (1,))
            s = jnp.concatenate([s, sk], axis=-1)
        p = jax.nn.softmax(s, axis=-1)
        if sink is not None:
            p = p[..., :-1]
        o = jnp.einsum('bkgqs,bskd->bqkgd', p.astype(v.dtype), v)
        return o.reshape(bn, QBLOCK, hq * hd)

    out = lax.map(blk, qb)
    return out.transpose(1, 0, 2, 3).reshape(bn, t, hq * hd)


def _attend_window(q, k, v, k_ctx, v_ctx, sink):
    bn, t, hq, hd = q.shape
    kv = k.shape[2]
    grp = hq // kv
    nb = t // QBLOCK
    n_ctx = k_ctx.shape[1]
    span = QBLOCK + 2 * WINDOW
    scale = hd ** -0.5
    pad = ((0, 0), (WINDOW, WINDOW), (0, 0), (0, 0))
    kp, vp = jnp.pad(k, pad), jnp.pad(v, pad)
    qb = q.reshape(bn, nb, QBLOCK, kv, grp, hd).transpose(1, 0, 2, 3, 4, 5)
    rel = (jnp.arange(span) - WINDOW)[None, :] - jnp.arange(QBLOCK)[:, None]
    band = jnp.abs(rel) <= WINDOW
    sink_l = sink.astype(jnp.float32).reshape(kv, grp)[None, :, :, None, None]

    def blk(args):
        qi, bi = args
        start = bi * QBLOCK
        kb = lax.dynamic_slice_in_dim(kp, start, span, axis=1)
        vb = lax.dynamic_slice_in_dim(vp, start, span, axis=1)
        kpos = start - WINDOW + jnp.arange(span)
        mask = band & ((kpos >= 0) & (kpos < t))[None, :]
        s_ctx = jnp.einsum('bqkgd,bskd->bkgqs', qi, k_ctx).astype(jnp.float32) * scale
        s_loc = jnp.einsum('bqkgd,bskd->bkgqs', qi, kb).astype(jnp.float32) * scale
        s_loc = jnp.where(mask, s_loc, -1e30)
        s_snk = jnp.broadcast_to(sink_l, s_ctx.shape[:-1] + (1,))
        p = jax.nn.softmax(jnp.concatenate([s_ctx, s_loc, s_snk], axis=-1), axis=-1)
        o = (jnp.einsum('bkgqs,bskd->bqkgd', p[..., :n_ctx].astype(v.dtype), v_ctx)
             + jnp.einsum('bkgqs,bskd->bqkgd', p[..., n_ctx:n_ctx + span].astype(v.dtype), vb))
        return o.reshape(bn, QBLOCK, hq * hd)

    out = lax.map(blk, (qb, jnp.arange(nb)))
    return out.transpose(1, 0, 2, 3).reshape(bn, t, hq * hd)


def _mixers_in(h, p):
    (a_qkv, a_z, a_beta, a_alpha, b_q, b_k, b_v, c_q, c_k, c_v, gates) = _split_cols(h @ p['w_in'])
    gdn = _gdn_features(a_qkv, a_beta, a_alpha, p['conv_qkv'], p['a_log'], p['dt_bias'])
    qn = p['qk_norm']
    att_b = (_rmsnorm(_heads(b_q, B_HEADS), qn[0]), _rmsnorm(_heads(b_k, B_KV), qn[1]), _heads(b_v, B_KV))
    att_c = (_rmsnorm(_heads(c_q, C_HEADS), qn[2]), _rmsnorm(_heads(c_k, C_KV), qn[3]), _heads(c_v, C_KV))
    return gdn, a_z, att_b, att_c, gates


def _mixers_out(x, ya, yb, yc, gates, gate1, shift2, scale2, gate2, p):
    ga, gb, gc = jnp.split(jax.nn.sigmoid(gates), 3, axis=-1)
    merged = ga * (ya @ p['w_br_a']) + gb * (yb @ p['w_br_b']) + gc * (yc @ p['w_br_c'])
    x = x + gate1 * (merged @ p['w_o'])
    h2 = _mod_norm(x, p['ln2'], shift2, scale2)
    return x + gate2 * (jnp.square(jax.nn.relu(h2 @ p['w_ff1'])) @ p['w_ff2'])


def _context_layer(x, mod, p):
    sh1, sc1, g1, sh2, sc2, g2 = jnp.split(mod, 6, axis=-1)
    h = _mod_norm(x, p['ln1'], sh1, sc1)
    (q, k, v, beta, g), a_z, (qb, kb, vb), (qc, kc, vc), gates = _mixers_in(h, p)
    zero = jnp.zeros((x.shape[0], A_HEADS, A_DK, A_DV), jnp.float32)
    oa, s_f, s_b = _gdn_bidir(q, k, v, beta, g, zero, zero)
    ya = _gated_out(oa, a_z, p['a_norm'])
    yb = _attend_dense(qb, kb, vb, None)
    yc = _attend_dense(qc, kc, vc, p['sink'])
    x = _mixers_out(x, ya, yb, yc, gates, g1, sh2, sc2, g2, p)
    return x, kb, vb, kc, vc, jnp.stack([s_f, s_b], axis=1).astype(x.dtype)


def _latent_layer(x, mod, p, cos, sin, k_glob, v_glob, k_win, v_win, s_ctx):
    sh1, sc1, g1, sh2, sc2, g2 = jnp.split(mod, 6, axis=-1)
    h = _mod_norm(x, p['ln1'], sh1, sc1)
    (q, k, v, beta, g), a_z, (qb, kb, vb), (qc, kc, vc), gates = _mixers_in(h, p)
    oa, _, _ = _gdn_bidir(q, k, v, beta, g, s_ctx[:, 0], s_ctx[:, 1])
    ya = _gated_out(oa, a_z, p['a_norm'])
    qb, kb = _rope(qb, cos, sin), _rope(kb, cos, sin)
    yb = _attend_dense(qb, jnp.concatenate([k_glob, kb], axis=1), jnp.concatenate([v_glob, vb], axis=1), None)
    qc, kc = _rope(qc, cos, sin), _rope(kc, cos, sin)
    yc = _attend_window(qc, kc, vc, k_win, v_win, p['sink'])
    return _mixers_out(x, ya, yb, yc, gates, g1, sh2, sc2, g2, p)


def setup_inputs(seed: int = 0) -> dict:
    key = jax.random.key(seed)
    ks = jax.random.split(key, 26)
    f32 = jnp.float32

    def nrm(k, shape, scale):
        return scale * jax.random.normal(k, shape, f32)

    dt = jnp.exp(jax.random.uniform(ks[13], (DEPTH, 2, A_HEADS), f32, math.log(1e-3), math.log(1e-1)))
    dt_bias = dt + jnp.log(-jnp.expm1(-dt))
    a_log = jnp.log(jax.random.uniform(ks[14], (DEPTH, 2, A_HEADS), f32, 1.0, 16.0))
    return {
        'x_prompt': nrm(ks[0], (BATCH, SEQ, D_MODEL), 1.0),
        'x_sample': nrm(ks[1], (DEC_BATCH, DEC_SEQ, D_MODEL), 1.0),
        'cache_k_glob': nrm(ks[2], (DEC_BATCH, DEPTH, PAST_LEN, B_KV, HEAD_DIM), 1.0),
        'cache_v_glob': nrm(ks[3], (DEC_BATCH, DEPTH, PAST_LEN, B_KV, HEAD_DIM), 1.0),
        'cache_k_win': nrm(ks[4], (DEC_BATCH, DEPTH, PAST_LEN, C_KV, HEAD_DIM), 1.0),
        'cache_v_win': nrm(ks[5], (DEC_BATCH, DEPTH, PAST_LEN, C_KV, HEAD_DIM), 1.0),
        'state_delta': nrm(ks[6], (DEC_BATCH, DEPTH, 2, A_HEADS, A_DK, A_DV), A_DK ** -0.5),
        'c': nrm(ks[7], (DEC_BATCH, D_MODEL), 1.0),
        'c_ctx': nrm(ks[8], (D_MODEL,), 1.0),
        'w_mod': nrm(ks[9], (DEPTH, D_MODEL, 6 * D_MODEL), 0.5 * D_MODEL ** -0.5),
        'b_mod': nrm(ks[10], (DEPTH, 6 * D_MODEL), 0.02),
        'ln1': 1.0 + nrm(ks[11], (DEPTH, D_MODEL), 0.02),
        'ln2': 1.0 + nrm(ks[12], (DEPTH, D_MODEL), 0.02),
        'w_in': nrm(ks[15], (DEPTH, D_MODEL, IN_COLS), D_MODEL ** -0.5),
        'conv_qkv': nrm(ks[16], (DEPTH, CONV_K, 3 * A_W), CONV_K ** -0.5),
        'a_log': a_log,
        'dt_bias': dt_bias,
        'a_norm': 1.0 + nrm(ks[17], (DEPTH, A_DV), 0.02),
        'qk_norm': 1.0 + nrm(ks[18], (DEPTH, 4, HEAD_DIM), 0.02),
        'sink': nrm(ks[19], (DEPTH, C_HEADS), 0.5),
        'w_br_a': nrm(ks[20], (DEPTH, A_W, D_MODEL), A_W ** -0.5),
        'w_br_b': nrm(ks[21], (DEPTH, B_HEADS * HEAD_DIM, D_MODEL), (B_HEADS * HEAD_DIM) ** -0.5),
        'w_br_c': nrm(ks[22], (DEPTH, C_HEADS * HEAD_DIM, D_MODEL), (C_HEADS * HEAD_DIM) ** -0.5),
        'w_o': nrm(ks[23], (DEPTH, D_MODEL, D_MODEL), D_MODEL ** -0.5),
        'w_ff1': nrm(ks[24], (DEPTH, D_MODEL, D_FF), D_MODEL ** -0.5),
        'w_ff2': nrm(ks[25], (DEPTH, D_FF, D_MODEL), D_FF ** -0.5),
    }


def reference(x_prompt, x_sample, cache_k_glob, cache_v_glob, cache_k_win, cache_v_win, state_delta, c,
              c_ctx, w_mod, b_mod, ln1, ln2, w_in, conv_qkv, a_log, dt_bias, a_norm, qk_norm, sink,
              w_br_a, w_br_b, w_br_c, w_o, w_ff1, w_ff2):
    cos, sin = _axial_rope_tables(x_sample.shape[1])
    xp, xs = x_prompt, x_sample
    new_kg, new_vg, new_kw, new_vw, new_st = [], [], [], [], []
    for l in range(DEPTH):
        p = dict(ln1=ln1[l], ln2=ln2[l], w_in=w_in[l], conv_qkv=conv_qkv[l], a_log=a_log[l],
                 dt_bias=dt_bias[l], a_norm=a_norm[l], qk_norm=qk_norm[l], sink=sink[l],
                 w_br_a=w_br_a[l], w_br_b=w_br_b[l], w_br_c=w_br_c[l], w_o=w_o[l],
                 w_ff1=w_ff1[l], w_ff2=w_ff2[l])
        mod_ctx = jax.nn.silu(c_ctx) @ w_mod[l] + b_mod[l]
        mod_lat = (jax.nn.silu(c) @ w_mod[l] + b_mod[l])[:, None, :]
        xp, kb, vb, kc, vc, st = _context_layer(xp, mod_ctx, p)
        new_kg.append(kb)
        new_vg.append(vb)
        new_kw.append(kc)
        new_vw.append(vc)
        new_st.append(st)
        xs = _latent_layer(xs, mod_lat, p, cos, sin, cache_k_glob[:, l], cache_v_glob[:, l],
                           cache_k_win[:, l], cache_v_win[:, l], state_delta[:, l])
    return (xp, xs, jnp.stack(new_kg, axis=1), jnp.stack(new_vg, axis=1), jnp.stack(new_kw, axis=1),
            jnp.stack(new_vw, axis=1), jnp.stack(new_st, axis=1))
```

```python
import functools
import math

import jax
import jax.numpy as jnp
from jax import lax
from jax.experimental import pallas as pl
from jax.experimental.pallas import tpu as pltpu

F32 = jnp.float32
BF16 = jnp.bfloat16

LANES = 128
SUBLANES = 8

HEAD_DIM = 64
HEADS = 8
KV_HEADS = 2
GROUP_HEADS = HEADS // KV_HEADS
CHUNK = 64
CONV_K = 5
GRID_W = 64
WINDOW = 128
ROPE_THETA = 10000.0
EPS = 1e-6
NEG = -1e30
PAIRS = HEADS // 2

G_Q, G_K, G_V, G_Z = 0, 4, 8, 12
G_BQ, G_CQ = 16, 20
G_GATES = 24
G_BK, G_BV, G_CK, G_CV = 48, 49, 50, 51
G_BA = 52
N_GROUPS = 56
IN_COLS_PAD = N_GROUPS * LANES


def _cparams(sem, vmem_mb):
    return pltpu.CompilerParams(dimension_semantics=sem, vmem_limit_bytes=vmem_mb << 20)


def _dot(a, b):
    return jnp.dot(a, b, preferred_element_type=F32)


def _dot_nt(a, b):
    return lax.dot_general(a, b, (((1,), (1,)), ((), ())), preferred_element_type=F32)


def _sigmoid(x):
    return 1.0 / (1.0 + jnp.exp(-x))


def _silu(x):
    return x * _sigmoid(x)


def _softplus(x):
    return jnp.maximum(x, 0.0) + jnp.log1p(jnp.exp(-jnp.abs(x)))


def _half_sums(x, lane_lo):
    s_lo = jnp.sum(jnp.where(lane_lo, x, 0.0), axis=-1, keepdims=True)
    s_hi = jnp.sum(jnp.where(lane_lo, 0.0, x), axis=-1, keepdims=True)
    return jnp.where(lane_lo, s_lo, s_hi)


def _lane_lo(rows):
    return lax.broadcasted_iota(jnp.int32, (rows, LANES), 1) < HEAD_DIM


def _row_tile(n_ctx, dec_seq, pref):
    return math.gcd(math.gcd(n_ctx, dec_seq), pref)


def _mod_index(i, tm, n_ctx, dec_seq):
    ctx_tiles = n_ctx // tm
    return jnp.where(i < ctx_tiles, 0, 1 + (i - ctx_tiles) // (dec_seq // tm))


def _mod_kernel(c_ref, w_ref, b_ref, o_ref):
    s = _silu(c_ref[...])
    w = w_ref[0]
    s_hi = s.astype(BF16)
    s_lo = (s - s_hi.astype(F32)).astype(BF16)
    w_hi = w.astype(BF16)
    w_lo = (w - w_hi.astype(F32)).astype(BF16)
    acc = _dot(s_hi, w_hi) + (_dot(s_lo, w_hi) + _dot(s_hi, w_lo))
    o_ref[0] = acc + b_ref[0]


def _mod_vectors(cvec, w_mod, b_mod):
    depth, d, n = w_mod.shape
    tn = n // 4
    return pl.pallas_call(
        _mod_kernel,
        out_shape=jax.ShapeDtypeStruct((depth, SUBLANES, n), F32),
        grid=(depth, n // tn),
        in_specs=[pl.BlockSpec((SUBLANES, d), lambda l, j: (0, 0)),
                  pl.BlockSpec((1, d, tn), lambda l, j: (l, 0, j)),
                  pl.BlockSpec((1, 1, tn), lambda l, j: (l, 0, j))],
        out_specs=pl.BlockSpec((1, SUBLANES, tn), lambda l, j: (l, 0, j)),
        compiler_params=_cparams(("parallel", "parallel"), 40),
        name="mod_vectors",
    )(cvec, w_mod, b_mod.reshape(depth, 1, n))


def _mod_norm(x, ln, shift, scale):
    ms = jnp.mean(x * x, axis=-1, keepdims=True)
    return (x * lax.rsqrt(ms + EPS) * ln) * (1.0 + scale) + shift


def _inproj_kernel(x_ref, mod_ref, ln_ref, w_ref, o_ref, h_scr):
    d = x_ref.shape[1]

    @pl.when(pl.program_id(1) == 0)
    def _():
        m = mod_ref[0]
        h = _mod_norm(x_ref[...], ln_ref[...], m[:, 0:d], m[:, d:2 * d])
        h_scr[...] = h.astype(BF16)

    o_ref[...] = _dot(h_scr[...], w_ref[...])


def _in_proj(x, mods, ln1, w_in, n_ctx, dec_seq):
    nt, d = x.shape
    n = w_in.shape[1]
    tm, tn = _row_tile(n_ctx, dec_seq, 1024), 1024
    return pl.pallas_call(
        _inproj_kernel,
        out_shape=jax.ShapeDtypeStruct((nt, n), F32),
        grid=(nt // tm, n // tn),
        in_specs=[pl.BlockSpec((tm, d), lambda i, j: (i, 0)),
                  pl.BlockSpec((1, 1, 6 * d), lambda i, j: (_mod_index(i, tm, n_ctx, dec_seq), 0, 0)),
                  pl.BlockSpec((1, d), lambda i, j: (0, 0)),
                  pl.BlockSpec((d, tn), lambda i, j: (0, j))],
        out_specs=pl.BlockSpec((tm, tn), lambda i, j: (i, j)),
        scratch_shapes=[pltpu.VMEM((tm, d), BF16)],
        compiler_params=_cparams(("parallel", "arbitrary"), 40),
        name="in_proj",
    )(x, mods, ln1, w_in)


_GROUP = 4
_CONV_ROWS = 256


def _gdn_kernel(*refs, t, has_s0, has_sfin):
    (q_ref, k_ref, v_ref, z_ref, ba_ref, cwq_ref, cwk_ref, cwv_ref,
     alog_ref, dtb_ref, anorm_ref) = refs[:11]
    pos = 11
    s0_ref = None
    if has_s0:
        s0_ref = refs[pos]
        pos += 1
    y_ref = refs[pos]
    pos += 1
    sfin_ref = None
    if has_sfin:
        sfin_ref = refs[pos]
        pos += 1
    (pad_scr, qn_scr, kn_scr, vv_scr, of_scr, ob_scr,
     u_scr, w_scr, qg_scr, at_scr, kt_scr, gl_scr) = refs[pos:]

    n_chunks = t // CHUNK
    group = min(_GROUP, n_chunks)
    n_groups = n_chunks // group
    rows = min(_CONV_ROWS, t)
    lane_lo_c = _lane_lo(rows)

    def conv_into(x_ref, cw_ref, dst_scr, norm_scale):
        zeros8 = jnp.zeros((SUBLANES, LANES), F32)
        pad_scr[0:SUBLANES, :] = zeros8
        pad_scr[t + SUBLANES:t + 2 * SUBLANES, :] = zeros8

        def copy_body(i, carry):
            r = pl.multiple_of(i * rows, rows)
            pad_scr[pl.ds(r + SUBLANES, rows), :] = x_ref[pl.ds(r, rows), :]
            return carry

        lax.fori_loop(0, t // rows, copy_body, 0)
        cw = cw_ref[...]

        def conv_body(i, carry):
            r = pl.multiple_of(i * rows, rows)
            win = pad_scr[pl.ds(r, rows + 2 * SUBLANES), :]
            acc = None
            for j in range(CONV_K):
                off = SUBLANES - CONV_K // 2 + j
                term = win[off:off + rows, :] * cw[j:j + 1, :]
                acc = term if acc is None else acc + term
            y = _silu(acc)
            if norm_scale is not None:
                ss = _half_sums(y * y, lane_lo_c)
                y = y * lax.rsqrt(ss + EPS) * norm_scale
            dst_scr[pl.ds(r, rows), :] = y
            return carry

        lax.fori_loop(0, t // rows, conv_body, 0)

    conv_into(q_ref, cwq_ref, qn_scr, HEAD_DIM ** -0.5)
    conv_into(k_ref, cwk_ref, kn_scr, 1.0)
    conv_into(v_ref, cwv_ref, vv_scr, None)

    n2 = 2 * CHUNK
    ri = lax.broadcasted_iota(jnp.int32, (n2, n2), 0)
    ci = lax.broadcasted_iota(jnp.int32, (n2, n2), 1)
    same = (ri >= CHUNK) == (ci >= CHUNK)
    incl = (same & (ri >= ci), same & (ri <= ci))
    strict = (same & (ri > ci), same & (ri < ci))
    eye = (ri == ci).astype(F32)
    ril, cil = ri % CHUNK, ci % CHUNK
    level_mask = []
    m = 1
    while m < CHUNK:
        level_mask.append(same & (ril // (2 * m) == cil // (2 * m)) & (ril // m != cil // m))
        m *= 2
    row_in_chunk = lax.broadcasted_iota(jnp.int32, (CHUNK, LANES), 0)
    alog = alog_ref[0]
    dtb = dtb_ref[0]

    def col_stack(arr, ca, cb):
        top = jnp.broadcast_to(arr[:, ca:ca + 1], (CHUNK, LANES))
        bot = jnp.broadcast_to(arr[:, cb:cb + 1], (CHUNK, LANES))
        return jnp.concatenate([top, bot], axis=0)

    def row_stack(arr, row):
        top = jnp.broadcast_to(arr[row:row + 1, :], (CHUNK, LANES))
        bot = jnp.broadcast_to(arr[CHUNK + row:CHUNK + row + 1, :], (CHUNK, LANES))
        return jnp.concatenate([top, bot], axis=0)

    def chunk_cumsum(g, reverse):
        s = 1
        while s < CHUNK:
            if reverse:
                sh = pltpu.roll(g, CHUNK - s, axis=0)
                g = g + jnp.where(row_in_chunk + s < CHUNK, sh, 0.0)
            else:
                sh = pltpu.roll(g, s, axis=0)
                g = g + jnp.where(row_in_chunk >= s, sh, 0.0)
            s *= 2
        return g

    def unit_pre(r0, d, slot):
        q = qn_scr[pl.ds(r0, CHUNK), :]
        k = kn_scr[pl.ds(r0, CHUNK), :]
        v = vv_scr[pl.ds(r0, CHUNK), :]
        x = ba_ref[pl.ds(r0, CHUNK), :]
        beta = _sigmoid(x)
        g = -jnp.exp(alog) * _softplus(x + dtb)
        gc = chunk_cumsum(g, reverse=(d == 1))
        b_c = col_stack(beta, 2 * d, 2 * d + 1)
        g_c = col_stack(gc, 4 + 2 * d, 5 + 2 * d)
        g_r = g_c.T
        decay = jnp.where(incl[d], jnp.exp(jnp.where(incl[d], g_c - g_r, 0.0)), 0.0)
        eg = jnp.exp(g_c)
        g_last = row_stack(g_c, CHUNK - 1 if d == 0 else 0)
        k2 = jnp.concatenate([k, k], axis=0)
        ks = jnp.where(same, k2, 0.0)
        qs = jnp.where(same, jnp.concatenate([q, q], axis=0), 0.0)
        ks_b = ks.astype(BF16)
        prod = _dot_nt(jnp.concatenate([ks_b, qs.astype(BF16)], axis=0), ks_b)
        kk = prod[:n2]
        qk = prod[n2:]
        l_mat = jnp.where(strict[d], b_c * kk * decay, 0.0)
        attn = qk * decay
        ksw2 = pltpu.roll(k2, CHUNK, axis=1)
        rhs = b_c * jnp.where(same, jnp.concatenate([v, v], axis=0), eg * ksw2)
        t_inv = eye - jnp.where(level_mask[0], l_mat, 0.0)
        for lm in level_mask[1:]:
            y = _dot(jnp.where(lm, l_mat, 0.0).astype(BF16), t_inv.astype(BF16))
            t_inv = t_inv - _dot(t_inv.astype(BF16), y.astype(BF16))
        xs = _dot(t_inv.astype(BF16), rhs.astype(BF16))
        u_scr[slot] = jnp.where(same, xs, 0.0)
        w_scr[slot] = jnp.where(same, pltpu.roll(xs, CHUNK, axis=1), 0.0).astype(BF16)
        qg_scr[slot] = (qs * eg).astype(BF16)
        at_scr[slot] = attn.astype(BF16)
        kt_scr[slot] = (ks * jnp.exp(g_last - g_c)).T.astype(BF16)
        gl_scr[slot] = jnp.exp(g_last)

    def unit_scan(s, slot):
        s_b = s.astype(BF16)
        ws = _dot(jnp.concatenate([w_scr[slot], qg_scr[slot]], axis=0), s_b)
        u = u_scr[slot] - ws[:n2]
        u_b = u.astype(BF16)
        o_st = ws[n2:] + _dot(at_scr[slot], u_b)
        s_new = gl_scr[slot] * s + _dot(kt_scr[slot], u_b)
        return s_new, o_st[:CHUNK] + o_st[CHUNK:]

    if has_s0:
        s_init = (s0_ref[0, 0, 0], s0_ref[0, 0, 1])
    else:
        s_init = (jnp.zeros((n2, n2), F32), jnp.zeros((n2, n2), F32))

    def group_body(gi, carry):
        s_f, s_b = carry
        for i in range(group):
            c_f = gi * group + i
            c_b = n_chunks - 1 - c_f
            unit_pre(pl.multiple_of(c_f * CHUNK, CHUNK), 0, 2 * i)
            unit_pre(pl.multiple_of(c_b * CHUNK, CHUNK), 1, 2 * i + 1)
        for i in range(group):
            c_f = gi * group + i
            c_b = n_chunks - 1 - c_f
            s_f, o_f = unit_scan(s_f, 2 * i)
            of_scr[pl.ds(pl.multiple_of(c_f * CHUNK, CHUNK), CHUNK), :] = o_f
            s_b, o_b = unit_scan(s_b, 2 * i + 1)
            ob_scr[pl.ds(pl.multiple_of(c_b * CHUNK, CHUNK), CHUNK), :] = o_b
        return s_f, s_b

    s_f, s_b = lax.fori_loop(0, n_groups, group_body, s_init)
    if has_sfin:
        sfin_ref[0, 0, 0] = s_f
        sfin_ref[0, 0, 1] = s_b

    anorm = anorm_ref[...]

    def out_body(i, carry):
        r = pl.multiple_of(i * rows, rows)
        o = of_scr[pl.ds(r, rows), :] + ob_scr[pl.ds(r, rows), :]
        ms = _half_sums(o * o, lane_lo_c) * (1.0 / HEAD_DIM)
        on = o * lax.rsqrt(ms + EPS) * anorm
        y_ref[pl.ds(r, rows), :] = (on * _silu(z_ref[pl.ds(r, rows), :])).astype(y_ref.dtype)
        return carry

    lax.fori_loop(0, t // rows, out_body, 0)


def _gdn(proj, conv_w, alog_rows, dtb_rows, anorm_row, s0, *, t, row0, n_seq, want_state):
    blk0 = row0 // t
    has_s0 = s0 is not None
    n2 = 2 * CHUNK
    group = min(_GROUP, t // CHUNK)

    def col(g):
        return pl.BlockSpec((t, LANES), lambda b, p, g=g: (blk0 + b, g + p))

    def cw(g):
        return pl.BlockSpec((CONV_K, LANES), lambda b, p, g=g: (0, g + p))

    in_specs = [col(G_Q), col(G_K), col(G_V), col(G_Z), col(G_BA),
                cw(G_Q), cw(G_K), cw(G_V),
                pl.BlockSpec((1, 1, LANES), lambda b, p: (p, 0, 0)),
                pl.BlockSpec((1, 1, LANES), lambda b, p: (p, 0, 0)),
                pl.BlockSpec((1, LANES), lambda b, p: (0, 0))]
    args = [proj, proj, proj, proj, proj, conv_w, conv_w, conv_w, alog_rows, dtb_rows, anorm_row]
    if has_s0:
        in_specs.append(pl.BlockSpec((1, 1, 2, n2, n2), lambda b, p: (b, p, 0, 0, 0)))
        args.append(s0)
    out_shape = [jax.ShapeDtypeStruct((n_seq * t, PAIRS * LANES), BF16)]
    out_specs = [pl.BlockSpec((t, LANES), lambda b, p: (b, p))]
    if want_state:
        out_shape.append(jax.ShapeDtypeStruct((n_seq, PAIRS, 2, n2, n2), F32))
        out_specs.append(pl.BlockSpec((1, 1, 2, n2, n2), lambda b, p: (b, p, 0, 0, 0)))
    scratch = [pltpu.VMEM((t + 2 * SUBLANES, LANES), F32)] + [pltpu.VMEM((t, LANES), F32)] * 5
    scratch += [pltpu.VMEM((2 * group, n2, n2), F32),
                pltpu.VMEM((2 * group, n2, n2), BF16),
                pltpu.VMEM((2 * group, n2, n2), BF16),
                pltpu.VMEM((2 * group, n2, n2), BF16),
                pltpu.VMEM((2 * group, n2, n2), BF16),
                pltpu.VMEM((2 * group, n2, n2), F32)]
    return pl.pallas_call(
        functools.partial(_gdn_kernel, t=t, has_s0=has_s0, has_sfin=want_state),
        out_shape=out_shape,
        grid=(n_seq, PAIRS),
        in_specs=in_specs,
        out_specs=out_specs,
        scratch_shapes=scratch,
        compiler_params=_cparams(("parallel", "parallel"), 48),
        name="gdn_ctx" if want_state else "gdn_lat",
    )(*args)


def _prep_kernel(*refs, rope, want_kv):
    bq_ref, cq_ref, bk_ref, bv_ref, ck_ref, cv_ref, qkn_ref = refs[:7]
    pos = 7
    cs = sn = None
    if rope:
        cs, sn = refs[7][...], refs[8][...]
        pos = 9
    qb_ref, qc_ref, kdb_ref, vdb_ref, kdc_ref, vdc_ref = refs[pos:pos + 6]
    pos += 6
    tr = bq_ref.shape[0]
    lane = lax.broadcasted_iota(jnp.int32, (tr, LANES), 1)
    lane_lo = lane < HEAD_DIM
    first_half = (lane % HEAD_DIM) < HEAD_DIM // 2
    qkn = qkn_ref[...]

    def norm(x, w_row):
        ms = _half_sums(x * x, lane_lo) * (1.0 / HEAD_DIM)
        return x * lax.rsqrt(ms + EPS) * w_row

    def rot(y):
        if not rope:
            return y
        partner = jnp.where(first_half, pltpu.roll(y, LANES - HEAD_DIM // 2, axis=1),
                            pltpu.roll(y, HEAD_DIM // 2, axis=1))
        return y * cs + partner * sn

    def dup(x, o_ref):
        lo = jnp.where(lane_lo, x, 0.0)
        hi = jnp.where(lane_lo, 0.0, x)
        o_ref[:, 0:LANES] = (lo + pltpu.roll(lo, HEAD_DIM, axis=1)).astype(BF16)
        o_ref[:, LANES:2 * LANES] = (hi + pltpu.roll(hi, HEAD_DIM, axis=1)).astype(BF16)

    for q_ref, o_ref, wi in ((bq_ref, qb_ref, 0), (cq_ref, qc_ref, 2)):
        for g in range(PAIRS):
            y = rot(norm(q_ref[:, g * LANES:(g + 1) * LANES], qkn[wi:wi + 1, :])) * HEAD_DIM ** -0.5
            o_ref[2 * g] = jnp.where(lane_lo, y, 0.0).astype(BF16)
            o_ref[2 * g + 1] = jnp.where(lane_lo, 0.0, y).astype(BF16)
    kb_n = norm(bk_ref[...], qkn[1:2, :])
    kc_n = norm(ck_ref[...], qkn[3:4, :])
    if want_kv:
        kbn_ref, kcn_ref = refs[pos:pos + 2]
        kbn_ref[...] = kb_n
        kcn_ref[...] = kc_n
    dup(rot(kb_n), kdb_ref)
    dup(bv_ref[...], vdb_ref)
    dup(rot(kc_n), kdc_ref)
    dup(cv_ref[...], vdc_ref)


def _attn_prep(proj, qkn_rows, cs, sn, *, row0, n_rows, want_kv):
    rope = cs is not None
    tr = math.gcd(n_rows, 512)
    blk0 = row0 // tr

    def col(g, width):
        gw = width // LANES
        return pl.BlockSpec((tr, width), lambda i, g=g, gw=gw: (blk0 + i, g // gw))

    in_specs = [col(G_BQ, 4 * LANES), col(G_CQ, 4 * LANES), col(G_BK, LANES), col(G_BV, LANES),
                col(G_CK, LANES), col(G_CV, LANES), pl.BlockSpec((4, LANES), lambda i: (0, 0))]
    args = [proj] * 6 + [qkn_rows]
    if rope:
        per_seq = cs.shape[0] // tr
        in_specs += [pl.BlockSpec((tr, LANES), lambda i: (i % per_seq, 0))] * 2
        args += [cs, sn]
    out_shape = [jax.ShapeDtypeStruct((HEADS, n_rows, LANES), BF16)] * 2
    out_shape += [jax.ShapeDtypeStruct((n_rows, 2 * LANES), BF16)] * 4
    out_specs = [pl.BlockSpec((HEADS, tr, LANES), lambda i: (0, i, 0))] * 2
    out_specs += [pl.BlockSpec((tr, 2 * LANES), lambda i: (i, 0))] * 4
    if want_kv:
        out_shape += [jax.ShapeDtypeStruct((n_rows, LANES), F32)] * 2
        out_specs += [pl.BlockSpec((tr, LANES), lambda i: (i, 0))] * 2
    return pl.pallas_call(
        functools.partial(_prep_kernel, rope=rope, want_kv=want_kv),
        out_shape=out_shape,
        grid=(n_rows // tr,),
        in_specs=in_specs,
        out_specs=out_specs,
        compiler_params=_cparams(("parallel",), 32),
        name="attn_prep_lat" if rope else "attn_prep_ctx",
    )(*args)


def _stacked_q(q_ref, g, tq):
    return q_ref[GROUP_HEADS * g:GROUP_HEADS * (g + 1)].reshape(GROUP_HEADS * tq, LANES)


def _unstack_group(o_st, tq, lane_lo):
    a = jnp.where(lane_lo, o_st[0:tq], o_st[tq:2 * tq])
    b = jnp.where(lane_lo, o_st[2 * tq:3 * tq], o_st[3 * tq:4 * tq])
    return jnp.concatenate([a, b], axis=1)


def _sink_col(sink_row, g, tq):
    parts = [jnp.broadcast_to(sink_row[:, GROUP_HEADS * g + j:GROUP_HEADS * g + j + 1], (tq, 1))
             for j in range(GROUP_HEADS)]
    return jnp.concatenate(parts, axis=0)


def _attn_ctx_kernel(qb_ref, kdb_ref, vdb_ref, qc_ref, kdc_ref, vdc_ref, sink_ref, yb_ref, yc_ref):
    tq = qb_ref.shape[1]
    lane_lo_q = _lane_lo(tq)
    sink_row = sink_ref[...]

    def attend(q_ref, kd_ref, vd_ref, y_ref, use_sink):
        for g in range(KV_HEADS):
            gs = slice(g * LANES, (g + 1) * LANES)
            s = _dot_nt(_stacked_q(q_ref, g, tq), kd_ref[:, gs])
            m = jnp.max(s, axis=-1, keepdims=True)
            if use_sink:
                snk = _sink_col(sink_row, g, tq)
                m = jnp.maximum(m, snk)
            p = jnp.exp(s - m)
            l = jnp.sum(p, axis=-1, keepdims=True)
            if use_sink:
                l = l + jnp.exp(snk - m)
            o = _dot(p.astype(BF16), vd_ref[:, gs]) / l
            y_ref[:, 2 * g * LANES:(2 * g + 2) * LANES] = _unstack_group(o, tq, lane_lo_q).astype(y_ref.dtype)

    attend(qb_ref, kdb_ref, vdb_ref, yb_ref, False)
    attend(qc_ref, kdc_ref, vdc_ref, yc_ref, True)


def _attn_ctx(qb, kdb, vdb, qc, kdc, vdc, sink_row, *, t):
    n = kdb.shape[0]
    q_spec = pl.BlockSpec((HEADS, t, LANES), lambda b: (0, b, 0))
    kv_spec = pl.BlockSpec((t, 2 * LANES), lambda b: (b, 0))
    y_spec = pl.BlockSpec((t, 4 * LANES), lambda b: (b, 0))
    return pl.pallas_call(
        _attn_ctx_kernel,
        out_shape=[jax.ShapeDtypeStruct((n, 4 * LANES), BF16)] * 2,
        grid=(n // t,),
        in_specs=[q_spec, kv_spec, kv_spec, q_spec, kv_spec, kv_spec,
                  pl.BlockSpec((1, LANES), lambda b: (0, 0))],
        out_specs=[y_spec, y_spec],
        compiler_params=_cparams(("parallel",), 32),
        name="attn_ctx",
    )(qb, kdb, vdb, qc, kdc, vdc, sink_row)


_TQ_B = 256
_TK_B = 512


def _attn_glob_kernel(q_ref, kd_ref, vd_ref, y_ref, m_scr, l_scr, acc_scr):
    kj = pl.program_id(2)
    tq = q_ref.shape[1]

    @pl.when(kj == 0)
    def _():
        m_scr[...] = jnp.full(m_scr.shape, NEG, F32)
        l_scr[...] = jnp.zeros(l_scr.shape, F32)
        acc_scr[...] = jnp.zeros(acc_scr.shape, F32)

    for g in range(KV_HEADS):
        gs = slice(g * LANES, (g + 1) * LANES)
        s = _dot_nt(_stacked_q(q_ref, g, tq), kd_ref[0, :, gs])
        m_old = m_scr[g]
        m_new = jnp.maximum(m_old, jnp.max(s, axis=-1, keepdims=True))
        p = jnp.exp(s - m_new[:, 0:1])
        alpha = jnp.exp(m_old - m_new)
        l_scr[g] = alpha * l_scr[g] + jnp.sum(p, axis=-1, keepdims=True)
        acc_scr[g] = alpha * acc_scr[g] + _dot(p.astype(BF16), vd_ref[0, :, gs])
        m_scr[g] = m_new

    @pl.when(kj == pl.num_programs(2) - 1)
    def _():
        lane_lo_q = _lane_lo(tq)
        for g in range(KV_HEADS):
            o = acc_scr[g] / l_scr[g]
            y_ref[:, 2 * g * LANES:(2 * g + 2) * LANES] = _unstack_group(o, tq, lane_lo_q).astype(y_ref.dtype)


def _attn_glob(q, kd, vd, *, t):
    n = q.shape[1]
    s_len = kd.shape[1]
    tq, tk = math.gcd(t, _TQ_B), math.gcd(s_len, _TK_B)
    per_seq = t // tq
    return pl.pallas_call(
        _attn_glob_kernel,
        out_shape=jax.ShapeDtypeStruct((n, 4 * LANES), BF16),
        grid=(n // t, per_seq, s_len // tk),
        in_specs=[pl.BlockSpec((HEADS, tq, LANES), lambda b, i, j: (0, b * per_seq + i, 0)),
                  pl.BlockSpec((1, tk, 2 * LANES), lambda b, i, j: (b, j, 0)),
                  pl.BlockSpec((1, tk, 2 * LANES), lambda b, i, j: (b, j, 0))],
        out_specs=pl.BlockSpec((tq, 4 * LANES), lambda b, i, j: (b * per_seq + i, 0)),
        scratch_shapes=[pltpu.VMEM((KV_HEADS, GROUP_HEADS * tq, LANES), F32)] * 3,
        compiler_params=_cparams(("parallel", "parallel", "arbitrary"), 40),
        name="attn_glob",
    )(q, kd, vd)


def _attn_win_kernel(q_ref, kx_ref, vx_ref, kp_ref, kc_ref, kn_ref, vp_ref, vc_ref, vn_ref,
                     sink_ref, y_ref):
    i = pl.program_id(1)
    nblk = pl.num_programs(1)
    tq = q_ref.shape[1]
    lane_lo_q = _lane_lo(tq)
    sink_row = sink_ref[...]
    rr = lax.broadcasted_iota(jnp.int32, (GROUP_HEADS * tq, tq), 0) % tq
    cc = lax.broadcasted_iota(jnp.int32, (GROUP_HEADS * tq, tq), 1)
    mask_prev = (cc >= rr) & (i > 0)
    mask_next = (cc <= rr) & (i < nblk - 1)
    rowmax = lambda a: jnp.max(a, axis=-1, keepdims=True)
    rowsum = lambda a: jnp.sum(a, axis=-1, keepdims=True)
    for g in range(KV_HEADS):
        gs = slice(g * LANES, (g + 1) * LANES)
        qs = _stacked_q(q_ref, g, tq)
        s_x = _dot_nt(qs, kx_ref[0, :, gs])
        s_p = jnp.where(mask_prev, _dot_nt(qs, kp_ref[:, gs]), NEG)
        s_c = _dot_nt(qs, kc_ref[:, gs])
        s_n = jnp.where(mask_next, _dot_nt(qs, kn_ref[:, gs]), NEG)
        snk = _sink_col(sink_row, g, tq)
        m = jnp.maximum(jnp.maximum(rowmax(s_x), rowmax(s_c)), jnp.maximum(rowmax(s_p), rowmax(s_n)))
        m = jnp.maximum(m, snk)
        p_x = jnp.exp(s_x - m)
        p_p = jnp.exp(s_p - m)
        p_c = jnp.exp(s_c - m)
        p_n = jnp.exp(s_n - m)
        l = rowsum(p_x) + rowsum(p_c) + rowsum(p_p) + rowsum(p_n) + jnp.exp(snk - m)
        o = (_dot(p_x.astype(BF16), vx_ref[0, :, gs]) + _dot(p_c.astype(BF16), vc_ref[:, gs])
             + _dot(p_p.astype(BF16), vp_ref[:, gs]) + _dot(p_n.astype(BF16), vn_ref[:, gs])) / l
        y_ref[:, 2 * g * LANES:(2 * g + 2) * LANES] = _unstack_group(o, tq, lane_lo_q).astype(y_ref.dtype)


def _attn_win(q, kd, vd, kx, vx, sink_row, *, t):
    n = q.shape[1]
    p_len = kx.shape[1]
    tq = WINDOW
    nblk = t // tq
    x_spec = pl.BlockSpec((1, p_len, 2 * LANES), lambda b, i: (b, 0, 0))
    prev = pl.BlockSpec((tq, 2 * LANES), lambda b, i: (b * nblk + jnp.maximum(i - 1, 0), 0))
    cur = pl.BlockSpec((tq, 2 * LANES), lambda b, i: (b * nblk + i, 0))
    nxt = pl.BlockSpec((tq, 2 * LANES), lambda b, i: (b * nblk + jnp.minimum(i + 1, nblk - 1), 0))
    return pl.pallas_call(
        _attn_win_kernel,
        out_shape=jax.ShapeDtypeStruct((n, 4 * LANES), BF16),
        grid=(n // t, nblk),
        in_specs=[pl.BlockSpec((HEADS, tq, LANES), lambda b, i: (0, b * nblk + i, 0)),
                  x_spec, x_spec, prev, cur, nxt, prev, cur, nxt,
                  pl.BlockSpec((1, LANES), lambda b, i: (0, 0))],
        out_specs=pl.BlockSpec((tq, 4 * LANES), lambda b, i: (b * nblk + i, 0)),
        compiler_params=_cparams(("parallel", "parallel"), 32),
        name="attn_win",
    )(q, kx, vx, kd, kd, kd, vd, vd, vd, sink_row)


def _mix_kernel(x_ref, ya_ref, yb_ref, yc_ref, ga_ref, gb_ref, gc_ref, mod_ref,
                wa_ref, wb_ref, wc_ref, wo_ref, o_ref):
    d = x_ref.shape[1]
    merged = (_sigmoid(ga_ref[...]) * _dot(ya_ref[...], wa_ref[...])
              + _sigmoid(gb_ref[...]) * _dot(yb_ref[...], wb_ref[...])
              + _sigmoid(gc_ref[...]) * _dot(yc_ref[...], wc_ref[...]))
    gate1 = mod_ref[0][:, 2 * d:3 * d]
    o_ref[...] = x_ref[...] + gate1 * _dot(merged.astype(BF16), wo_ref[...])


def _mix(x, ya, yb, yc, proj, mods, wa, wb, wc, wo, n_ctx, dec_seq):
    nt, d = x.shape
    tm = _row_tile(n_ctx, dec_seq, 512)
    y_spec = pl.BlockSpec((tm, ya.shape[1]), lambda i: (i, 0))
    gw = d // LANES

    def gate(k):
        return pl.BlockSpec((tm, d), lambda i, k=k: (i, G_GATES // gw + k))

    def full(a):
        return pl.BlockSpec(a.shape, lambda i: (0, 0))

    return pl.pallas_call(
        _mix_kernel,
        out_shape=jax.ShapeDtypeStruct((nt, d), F32),
        grid=(nt // tm,),
        in_specs=[pl.BlockSpec((tm, d), lambda i: (i, 0)), y_spec, y_spec, y_spec,
                  gate(0), gate(1), gate(2),
                  pl.BlockSpec((1, 1, 6 * d), lambda i: (_mod_index(i, tm, n_ctx, dec_seq), 0, 0)),
                  full(wa), full(wb), full(wc), full(wo)],
        out_specs=pl.BlockSpec((tm, d), lambda i: (i, 0)),
        compiler_params=_cparams(("parallel",), 48),
        name="mix_out",
    )(x, ya, yb, yc, proj, proj, proj, mods, wa, wb, wc, wo)


_FF_CHUNK = 1024


def _ffn_kernel(x_ref, mod_ref, ln_ref, w1_ref, w2_ref, o_ref):
    d = x_ref.shape[1]
    m = mod_ref[0]
    x = x_ref[...]
    h = _mod_norm(x, ln_ref[...], m[:, 3 * d:4 * d], m[:, 4 * d:5 * d]).astype(BF16)
    acc = None
    for f in range(0, w1_ref.shape[1], _FF_CHUNK):
        hid = jnp.maximum(_dot(h, w1_ref[:, f:f + _FF_CHUNK]), 0.0)
        part = _dot((hid * hid).astype(BF16), w2_ref[f:f + _FF_CHUNK, :])
        acc = part if acc is None else acc + part
    o_ref[...] = x + m[:, 5 * d:6 * d] * acc


def _ffn(x, mods, ln2, w1, w2, n_ctx, dec_seq):
    nt, d = x.shape
    tm = _row_tile(n_ctx, dec_seq, 512)
    return pl.pallas_call(
        _ffn_kernel,
        out_shape=jax.ShapeDtypeStruct((nt, d), F32),
        grid=(nt // tm,),
        in_specs=[pl.BlockSpec((tm, d), lambda i: (i, 0)),
                  pl.BlockSpec((1, 1, 6 * d), lambda i: (_mod_index(i, tm, n_ctx, dec_seq), 0, 0)),
                  pl.BlockSpec((1, d), lambda i: (0, 0)),
                  pl.BlockSpec(w1.shape, lambda i: (0, 0)),
                  pl.BlockSpec(w2.shape, lambda i: (0, 0))],
        out_specs=pl.BlockSpec((tm, d), lambda i: (i, 0)),
        compiler_params=_cparams(("parallel",), 56),
        name="ffn",
    )(x, mods, ln2, w1, w2)


def _reorder_w_in(w_in):
    a_w = HEADS * HEAD_DIM
    kvw = KV_HEADS * HEAD_DIM
    o_beta = 4 * a_w
    o_bq = o_beta + 4 * HEADS
    o_bk = o_bq + a_w
    o_bv = o_bk + kvw
    o_cq = o_bv + kvw
    o_ck = o_cq + a_w
    o_cv = o_ck + kvw
    o_g = o_cv + kvw
    seg = lambda a, n: w_in[:, :, a:a + n]
    ba = w_in[:, :, o_beta:o_bq]
    ba_groups = []
    for p in range(PAIRS):
        cols = []
        for base in (0, 2 * HEADS):
            for d in range(2):
                cols += [base + d * HEADS + 2 * p, base + d * HEADS + 2 * p + 1]
        grp = jnp.take(ba, jnp.array(cols, jnp.int32), axis=2)
        ba_groups.append(jnp.pad(grp, ((0, 0), (0, 0), (0, LANES - len(cols)))))
    out = jnp.concatenate([seg(0, o_beta), seg(o_bq, a_w), seg(o_cq, a_w), seg(o_g, w_in.shape[2] - o_g),
                           seg(o_bk, kvw), seg(o_bv, kvw), seg(o_ck, kvw), seg(o_cv, kvw)] + ba_groups, axis=2)
    assert out.shape[2] == IN_COLS_PAD
    return out.astype(BF16)


def _pair_rows(vals):
    rows = []
    for p in range(PAIRS):
        r = jnp.stack([vals[:, 0, 2 * p], vals[:, 0, 2 * p + 1], vals[:, 1, 2 * p], vals[:, 1, 2 * p + 1]], axis=-1)
        rows.append(jnp.pad(r, ((0, 0), (4, LANES - 8))))
    return jnp.stack(rows, axis=1)[:, :, None, :]


def _rope_tables(n_tokens):
    rows = n_tokens // GRID_W
    row_id = jnp.repeat(jnp.arange(rows, dtype=F32), GRID_W)
    col_id = jnp.tile(jnp.arange(GRID_W, dtype=F32), rows)
    n_freq = HEAD_DIM // 4
    inv_freq = ROPE_THETA ** (-jnp.arange(n_freq, dtype=F32) / n_freq)
    ang = jnp.concatenate([row_id[:, None] * inv_freq, col_id[:, None] * inv_freq], axis=-1)
    cos, sin = jnp.cos(ang), jnp.sin(ang)
    cs = jnp.tile(cos, (1, 4))
    sn = jnp.tile(jnp.concatenate([-sin, sin], axis=-1), (1, 2))
    return cs, sn


def _dup_kv(x):
    return jnp.concatenate([x[:, :, 0], x[:, :, 0], x[:, :, 1], x[:, :, 1]], axis=-1).astype(BF16)


def _block_diag_states(s):
    b = s.shape[0]
    s = s.reshape(b, 2, PAIRS, 2, HEAD_DIM, HEAD_DIM).transpose(0, 2, 1, 3, 4, 5)
    z = jnp.zeros_like(s[:, :, :, 0])
    top = jnp.concatenate([s[:, :, :, 0], z], axis=-1)
    bot = jnp.concatenate([z, s[:, :, :, 1]], axis=-1)
    return jnp.concatenate([top, bot], axis=-2)


def _diag_states(s):
    a = s[:, :, :, :HEAD_DIM, :HEAD_DIM]
    bb = s[:, :, :, HEAD_DIM:, HEAD_DIM:]
    out = jnp.stack([a, bb], axis=3)
    b = s.shape[0]
    return out.transpose(0, 2, 1, 3, 4, 5).reshape(b, 2, HEADS, HEAD_DIM, HEAD_DIM)


def kernel(x_prompt, x_sample, cache_k_glob, cache_v_glob, cache_k_win, cache_v_win, state_delta, c, c_ctx, w_mod, b_mod, ln1, ln2, w_in, conv_qkv, a_log, dt_bias, a_norm, qk_norm, sink, w_br_a, w_br_b, w_br_c, w_o, w_ff1, w_ff2):
    batch, seq, d = x_prompt.shape
    dec_batch, dec_seq, _ = x_sample.shape
    depth = w_mod.shape[0]
    n_ctx = batch * seq
    n_lat = dec_batch * dec_seq
    assert 1 + dec_batch <= SUBLANES

    w_in_r = _reorder_w_in(w_in)
    wa, wb, wc, wo = (w.astype(BF16) for w in (w_br_a, w_br_b, w_br_c, w_o))
    w1, w2 = w_ff1.astype(BF16), w_ff2.astype(BF16)
    alog_rows = _pair_rows(a_log)
    dtb_rows = _pair_rows(dt_bias)
    anorm_rows = jnp.tile(a_norm, (1, 2))[:, None, :]
    qkn_rows = jnp.tile(qk_norm, (1, 1, 2))
    sink_rows = jnp.pad(sink, ((0, 0), (0, LANES - HEADS)))[:, None, :]
    cs, sn = _rope_tables(dec_seq)
    cvec = jnp.concatenate([c_ctx[None], c, jnp.zeros((SUBLANES - 1 - dec_batch, d), F32)], axis=0)
    mods_all = _mod_vectors(cvec, w_mod, b_mod)

    x = jnp.concatenate([x_prompt.reshape(n_ctx, d), x_sample.reshape(n_lat, d)], axis=0)
    new_kg, new_vg, new_kw, new_vw, new_st = [], [], [], [], []
    for l in range(depth):
        mods = mods_all[l][:, None, :]
        proj = _in_proj(x, mods, ln1[l][None], w_in_r[l], n_ctx, dec_seq)

        ya_ctx, s_fin = _gdn(proj, conv_qkv[l], alog_rows[l], dtb_rows[l], anorm_rows[l], None,
                             t=seq, row0=0, n_seq=batch, want_state=True)
        (ya_lat,) = _gdn(proj, conv_qkv[l], alog_rows[l], dtb_rows[l], anorm_rows[l],
                         _block_diag_states(state_delta[:, l]),
                         t=dec_seq, row0=n_ctx, n_seq=dec_batch, want_state=False)

        qb_c, qc_c, kdb_c, vdb_c, kdc_c, vdc_c, kb_n, kc_n = _attn_prep(
            proj, qkn_rows[l], None, None, row0=0, n_rows=n_ctx, want_kv=True)
        qb_l, qc_l, kdb_l, vdb_l, kdc_l, vdc_l = _attn_prep(
            proj, qkn_rows[l], cs, sn, row0=n_ctx, n_rows=n_lat, want_kv=False)

        yb_ctx, yc_ctx = _attn_ctx(qb_c, kdb_c, vdb_c, qc_c, kdc_c, vdc_c, sink_rows[l], t=seq)

        sh3 = lambda a: a.reshape(dec_batch, dec_seq, a.shape[1])
        kd_all = jnp.concatenate([_dup_kv(cache_k_glob[:, l]), sh3(kdb_l)], axis=1)
        vd_all = jnp.concatenate([_dup_kv(cache_v_glob[:, l]), sh3(vdb_l)], axis=1)
        yb_lat = _attn_glob(qb_l, kd_all, vd_all, t=dec_seq)
        yc_lat = _attn_win(qc_l, kdc_l, vdc_l, _dup_kv(cache_k_win[:, l]), _dup_kv(cache_v_win[:, l]),
                           sink_rows[l], t=dec_seq)

        ya = jnp.concatenate([ya_ctx, ya_lat], axis=0)
        yb = jnp.concatenate([yb_ctx, yb_lat], axis=0)
        yc = jnp.concatenate([yc_ctx, yc_lat], axis=0)
        x = _mix(x, ya, yb, yc, proj, mods, wa[l], wb[l], wc[l], wo[l], n_ctx, dec_seq)
        x = _ffn(x, mods, ln2[l][None], w1[l], w2[l], n_ctx, dec_seq)

        kv_shape = (batch, seq, KV_HEADS, HEAD_DIM)
        new_kg.append(kb_n.reshape(kv_shape))
        new_vg.append(proj[:n_ctx, G_BV * LANES:(G_BV + 1) * LANES].reshape(kv_shape))
        new_kw.append(kc_n.reshape(kv_shape))
        new_vw.append(proj[:n_ctx, G_CV * LANES:(G_CV + 1) * LANES].reshape(kv_shape))
        new_st.append(_diag_states(s_fin))

    y_prompt = x[:n_ctx].reshape(batch, seq, d)
    y_sample = x[n_ctx:].reshape(dec_batch, dec_seq, d)
    return (y_prompt, y_sample, jnp.stack(new_kg, axis=1), jnp.stack(new_vg, axis=1),
            jnp.stack(new_kw, axis=1), jnp.stack(new_vw, axis=1), jnp.stack(new_st, axis=1))
```

```python
import functools
import math

import jax
import jax.numpy as jnp
from jax import lax
from jax.experimental import pallas as pl
from jax.experimental.pallas import tpu as pltpu

F32 = jnp.float32
BF16 = jnp.bfloat16

LANES = 128
SUBLANES = 8

HEAD_DIM = 64
HEADS = 8
KV_HEADS = 2
GROUP_HEADS = HEADS // KV_HEADS
CHUNK = 64
CONV_K = 5
GRID_W = 64
WINDOW = 128
ROPE_THETA = 10000.0
EPS = 1e-6
NEG = -1e30
PAIRS = HEADS // 2

G_Q, G_K, G_V, G_Z = 0, 4, 8, 12
G_BQ, G_CQ = 16, 20
G_GATES = 24
G_BK, G_BV, G_CK, G_CV = 48, 49, 50, 51
G_BA = 52
N_GROUPS = 56
IN_COLS_PAD = N_GROUPS * LANES


def _cparams(sem, vmem_mb):
    return pltpu.CompilerParams(dimension_semantics=sem, vmem_limit_bytes=vmem_mb << 20)


def _dot(a, b):
    return jnp.dot(a, b, preferred_element_type=F32)


def _dot_nt(a, b):
    return lax.dot_general(a, b, (((1,), (1,)), ((), ())), preferred_element_type=F32)


def _sigmoid(x):
    return 1.0 / (1.0 + jnp.exp(-x))


def _silu(x):
    return x * _sigmoid(x)


def _softplus(x):
    return jnp.maximum(x, 0.0) + jnp.log1p(jnp.exp(-jnp.abs(x)))


def _half_sums(x, lane_lo):
    s_lo = jnp.sum(jnp.where(lane_lo, x, 0.0), axis=-1, keepdims=True)
    s_hi = jnp.sum(jnp.where(lane_lo, 0.0, x), axis=-1, keepdims=True)
    return jnp.where(lane_lo, s_lo, s_hi)


def _lane_lo(rows):
    return lax.broadcasted_iota(jnp.int32, (rows, LANES), 1) < HEAD_DIM


def _row_tile(n_ctx, dec_seq, pref):
    return math.gcd(math.gcd(n_ctx, dec_seq), pref)


def _mod_index(i, tm, n_ctx, dec_seq):
    ctx_tiles = n_ctx // tm
    return jnp.where(i < ctx_tiles, 0, 1 + (i - ctx_tiles) // (dec_seq // tm))


def _mod_kernel(c_ref, w_ref, b_ref, o_ref):
    s = _silu(c_ref[...])
    w = w_ref[0]
    s_hi = s.astype(BF16)
    s_lo = (s - s_hi.astype(F32)).astype(BF16)
    w_hi = w.astype(BF16)
    w_lo = (w - w_hi.astype(F32)).astype(BF16)
    acc = _dot(s_hi, w_hi) + (_dot(s_lo, w_hi) + _dot(s_hi, w_lo))
    o_ref[0] = acc + b_ref[0]


def _mod_vectors(cvec, w_mod, b_mod):
    depth, d, n = w_mod.shape
    tn = n // 4
    return pl.pallas_call(
        _mod_kernel,
        out_shape=jax.ShapeDtypeStruct((depth, SUBLANES, n), F32),
        grid=(depth, n // tn),
        in_specs=[pl.BlockSpec((SUBLANES, d), lambda l, j: (0, 0)),
                  pl.BlockSpec((1, d, tn), lambda l, j: (l, 0, j)),
                  pl.BlockSpec((1, 1, tn), lambda l, j: (l, 0, j))],
        out_specs=pl.BlockSpec((1, SUBLANES, tn), lambda l, j: (l, 0, j)),
        compiler_params=_cparams(("parallel", "parallel"), 40),
        name="mod_vectors",
    )(cvec, w_mod, b_mod.reshape(depth, 1, n))


def _mod_norm(x, ln, shift, scale):
    ms = jnp.mean(x * x, axis=-1, keepdims=True)
    return (x * lax.rsqrt(ms + EPS) * ln) * (1.0 + scale) + shift


def _inproj_kernel(x_ref, mod_ref, ln_ref, w_ref, o_ref, h_scr):
    d = x_ref.shape[1]

    @pl.when(pl.program_id(1) == 0)
    def _():
        m = mod_ref[0]
        h = _mod_norm(x_ref[...], ln_ref[...], m[:, 0:d], m[:, d:2 * d])
        h_scr[...] = h.astype(BF16)

    o_ref[...] = _dot(h_scr[...], w_ref[...])


def _in_proj(x, mods, ln1, w_in, n_ctx, dec_seq):
    nt, d = x.shape
    n = w_in.shape[1]
    tm, tn = _row_tile(n_ctx, dec_seq, 1024), 1024
    return pl.pallas_call(
        _inproj_kernel,
        out_shape=jax.ShapeDtypeStruct((nt, n), F32),
        grid=(nt // tm, n // tn),
        in_specs=[pl.BlockSpec((tm, d), lambda i, j: (i, 0)),
                  pl.BlockSpec((1, 1, 6 * d), lambda i, j: (_mod_index(i, tm, n_ctx, dec_seq), 0, 0)),
                  pl.BlockSpec((1, d), lambda i, j: (0, 0)),
                  pl.BlockSpec((d, tn), lambda i, j: (0, j))],
        out_specs=pl.BlockSpec((tm, tn), lambda i, j: (i, j)),
        scratch_shapes=[pltpu.VMEM((tm, d), BF16)],
        compiler_params=_cparams(("parallel", "arbitrary"), 40),
        name="in_proj",
    )(x, mods, ln1, w_in)


_GROUP = 4
_CONV_ROWS = 256


def _gdn_kernel(*refs, t, has_s0, has_sfin):
    (q_ref, k_ref, v_ref, z_ref, ba_ref, cwq_ref, cwk_ref, cwv_ref,
     alog_ref, dtb_ref, anorm_ref) = refs[:11]
    pos = 11
    s0_ref = None
    if has_s0:
        s0_ref = refs[pos]
        pos += 1
    y_ref = refs[pos]
    pos += 1
    sfin_ref = None
    if has_sfin:
        sfin_ref = refs[pos]
        pos += 1
    (pad_scr, qn_scr, kn_scr, vv_scr, of_scr, ob_scr,
     u_scr, w_scr, qg_scr, at_scr, kt_scr, gl_scr) = refs[pos:]

    n_chunks = t // CHUNK
    group = min(_GROUP, n_chunks)
    n_groups = n_chunks // group
    rows = min(_CONV_ROWS, t)
    lane_lo_c = _lane_lo(rows)

    def conv_into(x_ref, cw_ref, dst_scr, norm_scale):
        zeros8 = jnp.zeros((SUBLANES, LANES), F32)
        pad_scr[0:SUBLANES, :] = zeros8
        pad_scr[t + SUBLANES:t + 2 * SUBLANES, :] = zeros8

        def copy_body(i, carry):
            r = pl.multiple_of(i * rows, rows)
            pad_scr[pl.ds(r + SUBLANES, rows), :] = x_ref[pl.ds(r, rows), :]
            return carry

        lax.fori_loop(0, t // rows, copy_body, 0)
        cw = cw_ref[...]

        def conv_body(i, carry):
            r = pl.multiple_of(i * rows, rows)
            win = pad_scr[pl.ds(r, rows + 2 * SUBLANES), :]
            acc = None
            for j in range(CONV_K):
                off = SUBLANES - CONV_K // 2 + j
                term = win[off:off + rows, :] * cw[j:j + 1, :]
                acc = term if acc is None else acc + term
            y = _silu(acc)
            if norm_scale is not None:
                ss = _half_sums(y * y, lane_lo_c)
                y = y * lax.rsqrt(ss + EPS) * norm_scale
            dst_scr[pl.ds(r, rows), :] = y
            return carry

        lax.fori_loop(0, t // rows, conv_body, 0)

    conv_into(q_ref, cwq_ref, qn_scr, HEAD_DIM ** -0.5)
    conv_into(k_ref, cwk_ref, kn_scr, 1.0)
    conv_into(v_ref, cwv_ref, vv_scr, None)

    n2 = 2 * CHUNK
    ri = lax.broadcasted_iota(jnp.int32, (n2, n2), 0)
    ci = lax.broadcasted_iota(jnp.int32, (n2, n2), 1)
    same = (ri >= CHUNK) == (ci >= CHUNK)
    incl = (same & (ri >= ci), same & (ri <= ci))
    strict = (same & (ri > ci), same & (ri < ci))
    eye = (ri == ci).astype(F32)
    ril, cil = ri % CHUNK, ci % CHUNK
    level_mask = []
    m = 1
    while m < CHUNK:
        level_mask.append(same & (ril // (2 * m) == cil // (2 * m)) & (ril // m != cil // m))
        m *= 2
    row_in_chunk = lax.broadcasted_iota(jnp.int32, (CHUNK, LANES), 0)
    alog = alog_ref[0]
    dtb = dtb_ref[0]

    def col_stack(arr, ca, cb):
        top = jnp.broadcast_to(arr[:, ca:ca + 1], (CHUNK, LANES))
        bot = jnp.broadcast_to(arr[:, cb:cb + 1], (CHUNK, LANES))
        return jnp.concatenate([top, bot], axis=0)

    def row_stack(arr, row):
        top = jnp.broadcast_to(arr[row:row + 1, :], (CHUNK, LANES))
        bot = jnp.broadcast_to(arr[CHUNK + row:CHUNK + row + 1, :], (CHUNK, LANES))
        return jnp.concatenate([top, bot], axis=0)

    def chunk_cumsum(g, reverse):
        s = 1
        while s < CHUNK:
            if reverse:
                sh = pltpu.roll(g, CHUNK - s, axis=0)
                g = g + jnp.where(row_in_chunk + s < CHUNK, sh, 0.0)
            else:
                sh = pltpu.roll(g, s, axis=0)
                g = g + jnp.where(row_in_chunk >= s, sh, 0.0)
            s *= 2
        return g

    def units_pre(units):
        idx = range(len(units))
        dirs = [d for _, d, _ in units]
        g_c, b_c, ks, qs, k2, v2 = [], [], [], [], [], []
        for r0, d, _ in units:
            x = ba_ref[pl.ds(r0, CHUNK), :]
            g = -jnp.exp(alog) * _softplus(x + dtb)
            gc = chunk_cumsum(g, reverse=(d == 1))
            b_c.append(col_stack(_sigmoid(x), 2 * d, 2 * d + 1))
            g_c.append(col_stack(gc, 4 + 2 * d, 5 + 2 * d))
            k = kn_scr[pl.ds(r0, CHUNK), :]
            q = qn_scr[pl.ds(r0, CHUNK), :]
            v = vv_scr[pl.ds(r0, CHUNK), :]
            k2.append(jnp.concatenate([k, k], axis=0))
            ks.append(jnp.where(same, k2[-1], 0.0))
            qs.append(jnp.where(same, jnp.concatenate([q, q], axis=0), 0.0))
            v2.append(jnp.concatenate([v, v], axis=0))
        ks_b = [ks[i].astype(BF16) for i in idx]
        prod = [_dot_nt(jnp.concatenate([ks_b[i], qs[i].astype(BF16)], axis=0), ks_b[i]) for i in idx]
        decay = [jnp.where(incl[dirs[i]], jnp.exp(jnp.where(incl[dirs[i]], g_c[i] - g_c[i].T, 0.0)), 0.0) for i in idx]
        eg = [jnp.exp(g_c[i]) for i in idx]
        g_last = [row_stack(g_c[i], CHUNK - 1 if dirs[i] == 0 else 0) for i in idx]
        l_mat = [jnp.where(strict[dirs[i]], b_c[i] * prod[i][:n2] * decay[i], 0.0) for i in idx]
        rhs = [(b_c[i] * jnp.where(same, v2[i], eg[i] * pltpu.roll(k2[i], CHUNK, axis=1))).astype(BF16) for i in idx]
        for i, (_, _, slot) in enumerate(units):
            at_scr[slot] = (prod[i][n2:] * decay[i]).astype(BF16)
            qg_scr[slot] = (qs[i] * eg[i]).astype(BF16)
            kt_scr[slot] = (ks[i] * jnp.exp(g_last[i] - g_c[i])).T.astype(BF16)
            gl_scr[slot] = jnp.exp(g_last[i])
        t_inv = [eye - jnp.where(level_mask[0], l_mat[i], 0.0) for i in idx]
        for lm in level_mask[1:]:
            t_b = [t_inv[i].astype(BF16) for i in idx]
            y = [_dot(jnp.where(lm, l_mat[i], 0.0).astype(BF16), t_b[i]).astype(BF16) for i in idx]
            t_inv = [t_inv[i] - _dot(t_b[i], y[i]) for i in idx]
        xs = [_dot(t_inv[i].astype(BF16), rhs[i]) for i in idx]
        for i, (_, _, slot) in enumerate(units):
            u_scr[slot] = jnp.where(same, xs[i], 0.0)
            w_scr[slot] = jnp.where(same, pltpu.roll(xs[i], CHUNK, axis=1), 0.0).astype(BF16)

    def units_scan(states, slots):
        idx = range(len(slots))
        ws = [_dot(jnp.concatenate([w_scr[slots[i]], qg_scr[slots[i]]], axis=0), states[i].astype(BF16))
              for i in idx]
        u_b = [(u_scr[slots[i]] - ws[i][:n2]).astype(BF16) for i in idx]
        o_st = [ws[i][n2:] + _dot(at_scr[slots[i]], u_b[i]) for i in idx]
        s_new = [gl_scr[slots[i]] * states[i] + _dot(kt_scr[slots[i]], u_b[i]) for i in idx]
        return s_new, [o[:CHUNK] + o[CHUNK:] for o in o_st]

    if has_s0:
        s_init = (s0_ref[0, 0, 0], s0_ref[0, 0, 1])
    else:
        s_init = (jnp.zeros((n2, n2), F32), jnp.zeros((n2, n2), F32))

    def group_body(gi, carry):
        s_f, s_b = carry
        units = []
        for i in range(group):
            c_f = gi * group + i
            c_b = n_chunks - 1 - c_f
            units.append((pl.multiple_of(c_f * CHUNK, CHUNK), 0, 2 * i))
            units.append((pl.multiple_of(c_b * CHUNK, CHUNK), 1, 2 * i + 1))
        units_pre(units)
        for i in range(group):
            (s_f, s_b), (o_f, o_b) = units_scan([s_f, s_b], [2 * i, 2 * i + 1])
            of_scr[pl.ds(units[2 * i][0], CHUNK), :] = o_f
            ob_scr[pl.ds(units[2 * i + 1][0], CHUNK), :] = o_b
        return s_f, s_b

    s_f, s_b = lax.fori_loop(0, n_groups, group_body, s_init)
    if has_sfin:
        sfin_ref[0, 0, 0] = s_f
        sfin_ref[0, 0, 1] = s_b

    anorm = anorm_ref[...]

    def out_body(i, carry):
        r = pl.multiple_of(i * rows, rows)
        o = of_scr[pl.ds(r, rows), :] + ob_scr[pl.ds(r, rows), :]
        ms = _half_sums(o * o, lane_lo_c) * (1.0 / HEAD_DIM)
        on = o * lax.rsqrt(ms + EPS) * anorm
        y_ref[pl.ds(r, rows), :] = (on * _silu(z_ref[pl.ds(r, rows), :])).astype(y_ref.dtype)
        return carry

    lax.fori_loop(0, t // rows, out_body, 0)


def _gdn(proj, conv_w, alog_rows, dtb_rows, anorm_row, s0, *, t, row0, n_seq, want_state):
    blk0 = row0 // t
    has_s0 = s0 is not None
    n2 = 2 * CHUNK
    group = min(_GROUP, t // CHUNK)

    def col(g):
        return pl.BlockSpec((t, LANES), lambda b, p, g=g: (blk0 + b, g + p))

    def cw(g):
        return pl.BlockSpec((CONV_K, LANES), lambda b, p, g=g: (0, g + p))

    in_specs = [col(G_Q), col(G_K), col(G_V), col(G_Z), col(G_BA),
                cw(G_Q), cw(G_K), cw(G_V),
                pl.BlockSpec((1, 1, LANES), lambda b, p: (p, 0, 0)),
                pl.BlockSpec((1, 1, LANES), lambda b, p: (p, 0, 0)),
                pl.BlockSpec((1, LANES), lambda b, p: (0, 0))]
    args = [proj, proj, proj, proj, proj, conv_w, conv_w, conv_w, alog_rows, dtb_rows, anorm_row]
    if has_s0:
        in_specs.append(pl.BlockSpec((1, 1, 2, n2, n2), lambda b, p: (b, p, 0, 0, 0)))
        args.append(s0)
    out_shape = [jax.ShapeDtypeStruct((n_seq * t, PAIRS * LANES), BF16)]
    out_specs = [pl.BlockSpec((t, LANES), lambda b, p: (b, p))]
    if want_state:
        out_shape.append(jax.ShapeDtypeStruct((n_seq, PAIRS, 2, n2, n2), F32))
        out_specs.append(pl.BlockSpec((1, 1, 2, n2, n2), lambda b, p: (b, p, 0, 0, 0)))
    scratch = [pltpu.VMEM((t + 2 * SUBLANES, LANES), F32)] + [pltpu.VMEM((t, LANES), F32)] * 5
    scratch += [pltpu.VMEM((2 * group, n2, n2), F32),
                pltpu.VMEM((2 * group, n2, n2), BF16),
                pltpu.VMEM((2 * group, n2, n2), BF16),
                pltpu.VMEM((2 * group, n2, n2), BF16),
                pltpu.VMEM((2 * group, n2, n2), BF16),
                pltpu.VMEM((2 * group, n2, n2), F32)]
    return pl.pallas_call(
        functools.partial(_gdn_kernel, t=t, has_s0=has_s0, has_sfin=want_state),
        out_shape=out_shape,
        grid=(n_seq, PAIRS),
        in_specs=in_specs,
        out_specs=out_specs,
        scratch_shapes=scratch,
        compiler_params=_cparams(("parallel", "parallel"), 48),
        name="gdn_ctx" if want_state else "gdn_lat",
    )(*args)


def _prep_kernel(*refs, rope, want_kv):
    bq_ref, cq_ref, bk_ref, bv_ref, ck_ref, cv_ref, qkn_ref = refs[:7]
    pos = 7
    cs = sn = None
    if rope:
        cs, sn = refs[7][...], refs[8][...]
        pos = 9
    qb_ref, qc_ref, kdb_ref, vdb_ref, kdc_ref, vdc_ref = refs[pos:pos + 6]
    pos += 6
    tr = bq_ref.shape[0]
    lane = lax.broadcasted_iota(jnp.int32, (tr, LANES), 1)
    lane_lo = lane < HEAD_DIM
    first_half = (lane % HEAD_DIM) < HEAD_DIM // 2
    qkn = qkn_ref[...]

    def norm(x, w_row):
        ms = _half_sums(x * x, lane_lo) * (1.0 / HEAD_DIM)
        return x * lax.rsqrt(ms + EPS) * w_row

    def rot(y):
        if not rope:
            return y
        partner = jnp.where(first_half, pltpu.roll(y, LANES - HEAD_DIM // 2, axis=1),
                            pltpu.roll(y, HEAD_DIM // 2, axis=1))
        return y * cs + partner * sn

    def dup(x, o_ref):
        lo = jnp.where(lane_lo, x, 0.0)
        hi = jnp.where(lane_lo, 0.0, x)
        o_ref[:, 0:LANES] = (lo + pltpu.roll(lo, HEAD_DIM, axis=1)).astype(BF16)
        o_ref[:, LANES:2 * LANES] = (hi + pltpu.roll(hi, HEAD_DIM, axis=1)).astype(BF16)

    for q_ref, o_ref, wi in ((bq_ref, qb_ref, 0), (cq_ref, qc_ref, 2)):
        for g in range(PAIRS):
            y = rot(norm(q_ref[:, g * LANES:(g + 1) * LANES], qkn[wi:wi + 1, :])) * HEAD_DIM ** -0.5
            o_ref[2 * g] = jnp.where(lane_lo, y, 0.0).astype(BF16)
            o_ref[2 * g + 1] = jnp.where(lane_lo, 0.0, y).astype(BF16)
    kb_n = norm(bk_ref[...], qkn[1:2, :])
    kc_n = norm(ck_ref[...], qkn[3:4, :])
    if want_kv:
        kbn_ref, kcn_ref = refs[pos:pos + 2]
        kbn_ref[...] = kb_n
        kcn_ref[...] = kc_n
    dup(rot(kb_n), kdb_ref)
    dup(bv_ref[...], vdb_ref)
    dup(rot(kc_n), kdc_ref)
    dup(cv_ref[...], vdc_ref)


def _attn_prep(proj, qkn_rows, cs, sn, *, row0, n_rows, want_kv):
    rope = cs is not None
    tr = math.gcd(n_rows, 512)
    blk0 = row0 // tr

    def col(g, width):
        gw = width // LANES
        return pl.BlockSpec((tr, width), lambda i, g=g, gw=gw: (blk0 + i, g // gw))

    in_specs = [col(G_BQ, 4 * LANES), col(G_CQ, 4 * LANES), col(G_BK, LANES), col(G_BV, LANES),
                col(G_CK, LANES), col(G_CV, LANES), pl.BlockSpec((4, LANES), lambda i: (0, 0))]
    args = [proj] * 6 + [qkn_rows]
    if rope:
        per_seq = cs.shape[0] // tr
        in_specs += [pl.BlockSpec((tr, LANES), lambda i: (i % per_seq, 0))] * 2
        args += [cs, sn]
    out_shape = [jax.ShapeDtypeStruct((HEADS, n_rows, LANES), BF16)] * 2
    out_shape += [jax.ShapeDtypeStruct((n_rows, 2 * LANES), BF16)] * 4
    out_specs = [pl.BlockSpec((HEADS, tr, LANES), lambda i: (0, i, 0))] * 2
    out_specs += [pl.BlockSpec((tr, 2 * LANES), lambda i: (i, 0))] * 4
    if want_kv:
        out_shape += [jax.ShapeDtypeStruct((n_rows, LANES), F32)] * 2
        out_specs += [pl.BlockSpec((tr, LANES), lambda i: (i, 0))] * 2
    return pl.pallas_call(
        functools.partial(_prep_kernel, rope=rope, want_kv=want_kv),
        out_shape=out_shape,
        grid=(n_rows // tr,),
        in_specs=in_specs,
        out_specs=out_specs,
        compiler_params=_cparams(("parallel",), 32),
        name="attn_prep_lat" if rope else "attn_prep_ctx",
    )(*args)


def _stacked_q(q_ref, g, tq):
    return q_ref[GROUP_HEADS * g:GROUP_HEADS * (g + 1)].reshape(GROUP_HEADS * tq, LANES)


def _unstack_group(o_st, tq, lane_lo):
    a = jnp.where(lane_lo, o_st[0:tq], o_st[tq:2 * tq])
    b = jnp.where(lane_lo, o_st[2 * tq:3 * tq], o_st[3 * tq:4 * tq])
    return jnp.concatenate([a, b], axis=1)


def _sink_col(sink_row, g, tq):
    parts = [jnp.broadcast_to(sink_row[:, GROUP_HEADS * g + j:GROUP_HEADS * g + j + 1], (tq, 1))
             for j in range(GROUP_HEADS)]
    return jnp.concatenate(parts, axis=0)


def _attn_ctx_kernel(qb_ref, kdb_ref, vdb_ref, qc_ref, kdc_ref, vdc_ref, sink_ref, yb_ref, yc_ref):
    tq = qb_ref.shape[1]
    lane_lo_q = _lane_lo(tq)
    sink_row = sink_ref[...]

    def attend(q_ref, kd_ref, vd_ref, y_ref, use_sink):
        for g in range(KV_HEADS):
            gs = slice(g * LANES, (g + 1) * LANES)
            s = _dot_nt(_stacked_q(q_ref, g, tq), kd_ref[:, gs])
            m = jnp.max(s, axis=-1, keepdims=True)
            if use_sink:
                snk = _sink_col(sink_row, g, tq)
                m = jnp.maximum(m, snk)
            p = jnp.exp(s - m)
            l = jnp.sum(p, axis=-1, keepdims=True)
            if use_sink:
                l = l + jnp.exp(snk - m)
            o = _dot(p.astype(BF16), vd_ref[:, gs]) / l
            y_ref[:, 2 * g * LANES:(2 * g + 2) * LANES] = _unstack_group(o, tq, lane_lo_q).astype(y_ref.dtype)

    attend(qb_ref, kdb_ref, vdb_ref, yb_ref, False)
    attend(qc_ref, kdc_ref, vdc_ref, yc_ref, True)


def _attn_ctx(qb, kdb, vdb, qc, kdc, vdc, sink_row, *, t):
    n = kdb.shape[0]
    q_spec = pl.BlockSpec((HEADS, t, LANES), lambda b: (0, b, 0))
    kv_spec = pl.BlockSpec((t, 2 * LANES), lambda b: (b, 0))
    y_spec = pl.BlockSpec((t, 4 * LANES), lambda b: (b, 0))
    return pl.pallas_call(
        _attn_ctx_kernel,
        out_shape=[jax.ShapeDtypeStruct((n, 4 * LANES), BF16)] * 2,
        grid=(n // t,),
        in_specs=[q_spec, kv_spec, kv_spec, q_spec, kv_spec, kv_spec,
                  pl.BlockSpec((1, LANES), lambda b: (0, 0))],
        out_specs=[y_spec, y_spec],
        compiler_params=_cparams(("parallel",), 32),
        name="attn_ctx",
    )(qb, kdb, vdb, qc, kdc, vdc, sink_row)


_TQ_B = 256
_TK_B = 512


def _attn_glob_kernel(q_ref, kd_ref, vd_ref, y_ref, m_scr, l_scr, acc_scr):
    kj = pl.program_id(2)
    tq = q_ref.shape[1]

    @pl.when(kj == 0)
    def _():
        m_scr[...] = jnp.full(m_scr.shape, NEG, F32)
        l_scr[...] = jnp.zeros(l_scr.shape, F32)
        acc_scr[...] = jnp.zeros(acc_scr.shape, F32)

    for g in range(KV_HEADS):
        gs = slice(g * LANES, (g + 1) * LANES)
        s = _dot_nt(_stacked_q(q_ref, g, tq), kd_ref[0, :, gs])
        m_old = m_scr[g]
        m_new = jnp.maximum(m_old, jnp.max(s, axis=-1, keepdims=True))
        p = jnp.exp(s - m_new[:, 0:1])
        alpha = jnp.exp(m_old - m_new)
        l_scr[g] = alpha * l_scr[g] + jnp.sum(p, axis=-1, keepdims=True)
        acc_scr[g] = alpha * acc_scr[g] + _dot(p.astype(BF16), vd_ref[0, :, gs])
        m_scr[g] = m_new

    @pl.when(kj == pl.num_programs(2) - 1)
    def _():
        lane_lo_q = _lane_lo(tq)
        for g in range(KV_HEADS):
            o = acc_scr[g] / l_scr[g]
            y_ref[:, 2 * g * LANES:(2 * g + 2) * LANES] = _unstack_group(o, tq, lane_lo_q).astype(y_ref.dtype)


def _attn_glob(q, kd, vd, *, t):
    n = q.shape[1]
    s_len = kd.shape[1]
    tq, tk = math.gcd(t, _TQ_B), math.gcd(s_len, _TK_B)
    per_seq = t // tq
    return pl.pallas_call(
        _attn_glob_kernel,
        out_shape=jax.ShapeDtypeStruct((n, 4 * LANES), BF16),
        grid=(n // t, per_seq, s_len // tk),
        in_specs=[pl.BlockSpec((HEADS, tq, LANES), lambda b, i, j: (0, b * per_seq + i, 0)),
                  pl.BlockSpec((1, tk, 2 * LANES), lambda b, i, j: (b, j, 0)),
                  pl.BlockSpec((1, tk, 2 * LANES), lambda b, i, j: (b, j, 0))],
        out_specs=pl.BlockSpec((tq, 4 * LANES), lambda b, i, j: (b * per_seq + i, 0)),
        scratch_shapes=[pltpu.VMEM((KV_HEADS, GROUP_HEADS * tq, LANES), F32)] * 3,
        compiler_params=_cparams(("parallel", "parallel", "arbitrary"), 40),
        name="attn_glob",
    )(q, kd, vd)


def _attn_win_kernel(q_ref, kx_ref, vx_ref, kp_ref, kc_ref, kn_ref, vp_ref, vc_ref, vn_ref,
                     sink_ref, y_ref):
    i = pl.program_id(1)
    nblk = pl.num_programs(1)
    tq = q_ref.shape[1]
    lane_lo_q = _lane_lo(tq)
    sink_row = sink_ref[...]
    rr = lax.broadcasted_iota(jnp.int32, (GROUP_HEADS * tq, tq), 0) % tq
    cc = lax.broadcasted_iota(jnp.int32, (GROUP_HEADS * tq, tq), 1)
    mask_prev = (cc >= rr) & (i > 0)
    mask_next = (cc <= rr) & (i < nblk - 1)
    rowmax = lambda a: jnp.max(a, axis=-1, keepdims=True)
    rowsum = lambda a: jnp.sum(a, axis=-1, keepdims=True)
    for g in range(KV_HEADS):
        gs = slice(g * LANES, (g + 1) * LANES)
        qs = _stacked_q(q_ref, g, tq)
        s_x = _dot_nt(qs, kx_ref[0, :, gs])
        s_p = jnp.where(mask_prev, _dot_nt(qs, kp_ref[:, gs]), NEG)
        s_c = _dot_nt(qs, kc_ref[:, gs])
        s_n = jnp.where(mask_next, _dot_nt(qs, kn_ref[:, gs]), NEG)
        snk = _sink_col(sink_row, g, tq)
        m = jnp.maximum(jnp.maximum(rowmax(s_x), rowmax(s_c)), jnp.maximum(rowmax(s_p), rowmax(s_n)))
        m = jnp.maximum(m, snk)
        p_x = jnp.exp(s_x - m)
        p_p = jnp.exp(s_p - m)
        p_c = jnp.exp(s_c - m)
        p_n = jnp.exp(s_n - m)
        l = rowsum(p_x) + rowsum(p_c) + rowsum(p_p) + rowsum(p_n) + jnp.exp(snk - m)
        o = (_dot(p_x.astype(BF16), vx_ref[0, :, gs]) + _dot(p_c.astype(BF16), vc_ref[:, gs])
             + _dot(p_p.astype(BF16), vp_ref[:, gs]) + _dot(p_n.astype(BF16), vn_ref[:, gs])) / l
        y_ref[:, 2 * g * LANES:(2 * g + 2) * LANES] = _unstack_group(o, tq, lane_lo_q).astype(y_ref.dtype)


def _attn_win(q, kd, vd, kx, vx, sink_row, *, t):
    n = q.shape[1]
    p_len = kx.shape[1]
    tq = WINDOW
    nblk = t // tq
    x_spec = pl.BlockSpec((1, p_len, 2 * LANES), lambda b, i: (b, 0, 0))
    prev = pl.BlockSpec((tq, 2 * LANES), lambda b, i: (b * nblk + jnp.maximum(i - 1, 0), 0))
    cur = pl.BlockSpec((tq, 2 * LANES), lambda b, i: (b * nblk + i, 0))
    nxt = pl.BlockSpec((tq, 2 * LANES), lambda b, i: (b * nblk + jnp.minimum(i + 1, nblk - 1), 0))
    return pl.pallas_call(
        _attn_win_kernel,
        out_shape=jax.ShapeDtypeStruct((n, 4 * LANES), BF16),
        grid=(n // t, nblk),
        in_specs=[pl.BlockSpec((HEADS, tq, LANES), lambda b, i: (0, b * nblk + i, 0)),
                  x_spec, x_spec, prev, cur, nxt, prev, cur, nxt,
                  pl.BlockSpec((1, LANES), lambda b, i: (0, 0))],
        out_specs=pl.BlockSpec((tq, 4 * LANES), lambda b, i: (b * nblk + i, 0)),
        compiler_params=_cparams(("parallel", "parallel"), 32),
        name="attn_win",
    )(q, kx, vx, kd, kd, kd, vd, vd, vd, sink_row)


def _mix_kernel(x_ref, ya_ref, yb_ref, yc_ref, ga_ref, gb_ref, gc_ref, mod_ref,
                wa_ref, wb_ref, wc_ref, wo_ref, o_ref):
    d = x_ref.shape[1]
    merged = (_sigmoid(ga_ref[...]) * _dot(ya_ref[...], wa_ref[...])
              + _sigmoid(gb_ref[...]) * _dot(yb_ref[...], wb_ref[...])
              + _sigmoid(gc_ref[...]) * _dot(yc_ref[...], wc_ref[...]))
    gate1 = mod_ref[0][:, 2 * d:3 * d]
    o_ref[...] = x_ref[...] + gate1 * _dot(merged.astype(BF16), wo_ref[...])


def _mix(x, ya, yb, yc, proj, mods, wa, wb, wc, wo, n_ctx, dec_seq):
    nt, d = x.shape
    tm = _row_tile(n_ctx, dec_seq, 512)
    y_spec = pl.BlockSpec((tm, ya.shape[1]), lambda i: (i, 0))
    gw = d // LANES

    def gate(k):
        return pl.BlockSpec((tm, d), lambda i, k=k: (i, G_GATES // gw + k))

    def full(a):
        return pl.BlockSpec(a.shape, lambda i: (0, 0))

    return pl.pallas_call(
        _mix_kernel,
        out_shape=jax.ShapeDtypeStruct((nt, d), F32),
        grid=(nt // tm,),
        in_specs=[pl.BlockSpec((tm, d), lambda i: (i, 0)), y_spec, y_spec, y_spec,
                  gate(0), gate(1), gate(2),
                  pl.BlockSpec((1, 1, 6 * d), lambda i: (_mod_index(i, tm, n_ctx, dec_seq), 0, 0)),
                  full(wa), full(wb), full(wc), full(wo)],
        out_specs=pl.BlockSpec((tm, d), lambda i: (i, 0)),
        compiler_params=_cparams(("parallel",), 48),
        name="mix_out",
    )(x, ya, yb, yc, proj, proj, proj, mods, wa, wb, wc, wo)


_FF_CHUNK = 1024


def _ffn_kernel(x_ref, mod_ref, ln_ref, w1_ref, w2_ref, o_ref):
    d = x_ref.shape[1]
    m = mod_ref[0]
    x = x_ref[...]
    h = _mod_norm(x, ln_ref[...], m[:, 3 * d:4 * d], m[:, 4 * d:5 * d]).astype(BF16)
    acc = None
    for f in range(0, w1_ref.shape[1], _FF_CHUNK):
        hid = jnp.maximum(_dot(h, w1_ref[:, f:f + _FF_CHUNK]), 0.0)
        part = _dot((hid * hid).astype(BF16), w2_ref[f:f + _FF_CHUNK, :])
        acc = part if acc is None else acc + part
    o_ref[...] = x + m[:, 5 * d:6 * d] * acc


def _ffn(x, mods, ln2, w1, w2, n_ctx, dec_seq):
    nt, d = x.shape
    tm = _row_tile(n_ctx, dec_seq, 512)
    return pl.pallas_call(
        _ffn_kernel,
        out_shape=jax.ShapeDtypeStruct((nt, d), F32),
        grid=(nt // tm,),
        in_specs=[pl.BlockSpec((tm, d), lambda i: (i, 0)),
                  pl.BlockSpec((1, 1, 6 * d), lambda i: (_mod_index(i, tm, n_ctx, dec_seq), 0, 0)),
                  pl.BlockSpec((1, d), lambda i: (0, 0)),
                  pl.BlockSpec(w1.shape, lambda i: (0, 0)),
                  pl.BlockSpec(w2.shape, lambda i: (0, 0))],
        out_specs=pl.BlockSpec((tm, d), lambda i: (i, 0)),
        compiler_params=_cparams(("parallel",), 56),
        name="ffn",
    )(x, mods, ln2, w1, w2)


def _reorder_w_in(w_in):
    a_w = HEADS * HEAD_DIM
    kvw = KV_HEADS * HEAD_DIM
    o_beta = 4 * a_w
    o_bq = o_beta + 4 * HEADS
    o_bk = o_bq + a_w
    o_bv = o_bk + kvw
    o_cq = o_bv + kvw
    o_ck = o_cq + a_w
    o_cv = o_ck + kvw
    o_g = o_cv + kvw
    seg = lambda a, n: w_in[:, :, a:a + n]
    ba = w_in[:, :, o_beta:o_bq]
    ba_groups = []
    for p in range(PAIRS):
        cols = []
        for base in (0, 2 * HEADS):
            for d in range(2):
                cols += [base + d * HEADS + 2 * p, base + d * HEADS + 2 * p + 1]
        grp = jnp.take(ba, jnp.array(cols, jnp.int32), axis=2)
        ba_groups.append(jnp.pad(grp, ((0, 0), (0, 0), (0, LANES - len(cols)))))
    out = jnp.concatenate([seg(0, o_beta), seg(o_bq, a_w), seg(o_cq, a_w), seg(o_g, w_in.shape[2] - o_g),
                           seg(o_bk, kvw), seg(o_bv, kvw), seg(o_ck, kvw), seg(o_cv, kvw)] + ba_groups, axis=2)
    assert out.shape[2] == IN_COLS_PAD
    return out.astype(BF16)


def _pair_rows(vals):
    rows = []
    for p in range(PAIRS):
        r = jnp.stack([vals[:, 0, 2 * p], vals[:, 0, 2 * p + 1], vals[:, 1, 2 * p], vals[:, 1, 2 * p + 1]], axis=-1)
        rows.append(jnp.pad(r, ((0, 0), (4, LANES - 8))))
    return jnp.stack(rows, axis=1)[:, :, None, :]


def _rope_tables(n_tokens):
    rows = n_tokens // GRID_W
    row_id = jnp.repeat(jnp.arange(rows, dtype=F32), GRID_W)
    col_id = jnp.tile(jnp.arange(GRID_W, dtype=F32), rows)
    n_freq = HEAD_DIM // 4
    inv_freq = ROPE_THETA ** (-jnp.arange(n_freq, dtype=F32) / n_freq)
    ang = jnp.concatenate([row_id[:, None] * inv_freq, col_id[:, None] * inv_freq], axis=-1)
    cos, sin = jnp.cos(ang), jnp.sin(ang)
    cs = jnp.tile(cos, (1, 4))
    sn = jnp.tile(jnp.concatenate([-sin, sin], axis=-1), (1, 2))
    return cs, sn


def _dup_kv(x):
    return jnp.concatenate([x[:, :, 0], x[:, :, 0], x[:, :, 1], x[:, :, 1]], axis=-1).astype(BF16)


def _block_diag_states(s):
    b = s.shape[0]
    s = s.reshape(b, 2, PAIRS, 2, HEAD_DIM, HEAD_DIM).transpose(0, 2, 1, 3, 4, 5)
    z = jnp.zeros_like(s[:, :, :, 0])
    top = jnp.concatenate([s[:, :, :, 0], z], axis=-1)
    bot = jnp.concatenate([z, s[:, :, :, 1]], axis=-1)
    return jnp.concatenate([top, bot], axis=-2)


def _diag_states(s):
    a = s[:, :, :, :HEAD_DIM, :HEAD_DIM]
    bb = s[:, :, :, HEAD_DIM:, HEAD_DIM:]
    out = jnp.stack([a, bb], axis=3)
    b = s.shape[0]
    return out.transpose(0, 2, 1, 3, 4, 5).reshape(b, 2, HEADS, HEAD_DIM, HEAD_DIM)


def kernel(x_prompt, x_sample, cache_k_glob, cache_v_glob, cache_k_win, cache_v_win, state_delta, c, c_ctx, w_mod, b_mod, ln1, ln2, w_in, conv_qkv, a_log, dt_bias, a_norm, qk_norm, sink, w_br_a, w_br_b, w_br_c, w_o, w_ff1, w_ff2):
    batch, seq, d = x_prompt.shape
    dec_batch, dec_seq, _ = x_sample.shape
    depth = w_mod.shape[0]
    n_ctx = batch * seq
    n_lat = dec_batch * dec_seq
    assert 1 + dec_batch <= SUBLANES

    w_in_r = _reorder_w_in(w_in)
    wa, wb, wc, wo = (w.astype(BF16) for w in (w_br_a, w_br_b, w_br_c, w_o))
    w1, w2 = w_ff1.astype(BF16), w_ff2.astype(BF16)
    alog_rows = _pair_rows(a_log)
    dtb_rows = _pair_rows(dt_bias)
    anorm_rows = jnp.tile(a_norm, (1, 2))[:, None, :]
    qkn_rows = jnp.tile(qk_norm, (1, 1, 2))
    sink_rows = jnp.pad(sink, ((0, 0), (0, LANES - HEADS)))[:, None, :]
    cs, sn = _rope_tables(dec_seq)
    cvec = jnp.concatenate([c_ctx[None], c, jnp.zeros((SUBLANES - 1 - dec_batch, d), F32)], axis=0)
    mods_all = _mod_vectors(cvec, w_mod, b_mod)

    x = jnp.concatenate([x_prompt.reshape(n_ctx, d), x_sample.reshape(n_lat, d)], axis=0)
    new_kg, new_vg, new_kw, new_vw, new_st = [], [], [], [], []
    for l in range(depth):
        mods = mods_all[l][:, None, :]
        proj = _in_proj(x, mods, ln1[l][None], w_in_r[l], n_ctx, dec_seq)

        ya_ctx, s_fin = _gdn(proj, conv_qkv[l], alog_rows[l], dtb_rows[l], anorm_rows[l], None,
                             t=seq, row0=0, n_seq=batch, want_state=True)
        (ya_lat,) = _gdn(proj, conv_qkv[l], alog_rows[l], dtb_rows[l], anorm_rows[l],
                         _block_diag_states(state_delta[:, l]),
                         t=dec_seq, row0=n_ctx, n_seq=dec_batch, want_state=False)

        qb_c, qc_c, kdb_c, vdb_c, kdc_c, vdc_c, kb_n, kc_n = _attn_prep(
            proj, qkn_rows[l], None, None, row0=0, n_rows=n_ctx, want_kv=True)
        qb_l, qc_l, kdb_l, vdb_l, kdc_l, vdc_l = _attn_prep(
            proj, qkn_rows[l], cs, sn, row0=n_ctx, n_rows=n_lat, want_kv=False)

        yb_ctx, yc_ctx = _attn_ctx(qb_c, kdb_c, vdb_c, qc_c, kdc_c, vdc_c, sink_rows[l], t=seq)

        sh3 = lambda a: a.reshape(dec_batch, dec_seq, a.shape[1])
        kd_all = jnp.concatenate([_dup_kv(cache_k_glob[:, l]), sh3(kdb_l)], axis=1)
        vd_all = jnp.concatenate([_dup_kv(cache_v_glob[:, l]), sh3(vdb_l)], axis=1)
        yb_lat = _attn_glob(qb_l, kd_all, vd_all, t=dec_seq)
        yc_lat = _attn_win(qc_l, kdc_l, vdc_l, _dup_kv(cache_k_win[:, l]), _dup_kv(cache_v_win[:, l]),
                           sink_rows[l], t=dec_seq)

        ya = jnp.concatenate([ya_ctx, ya_lat], axis=0)
        yb = jnp.concatenate([yb_ctx, yb_lat], axis=0)
        yc = jnp.concatenate([yc_ctx, yc_lat], axis=0)
        x = _mix(x, ya, yb, yc, proj, mods, wa[l], wb[l], wc[l], wo[l], n_ctx, dec_seq)
        x = _ffn(x, mods, ln2[l][None], w1[l], w2[l], n_ctx, dec_seq)

        kv_shape = (batch, seq, KV_HEADS, HEAD_DIM)
        new_kg.append(kb_n.reshape(kv_shape))
        new_vg.append(proj[:n_ctx, G_BV * LANES:(G_BV + 1) * LANES].reshape(kv_shape))
        new_kw.append(kc_n.reshape(kv_shape))
        new_vw.append(proj[:n_ctx, G_CV * LANES:(G_CV + 1) * LANES].reshape(kv_shape))
        new_st.append(_diag_states(s_fin))

    y_prompt = x[:n_ctx].reshape(batch, seq, d)
    y_sample = x[n_ctx:].reshape(dec_batch, dec_seq, d)
    return (y_prompt, y_sample, jnp.stack(new_kg, axis=1), jnp.stack(new_vg, axis=1),
            jnp.stack(new_kw, axis=1), jnp.stack(new_vw, axis=1), jnp.stack(new_st, axis=1))
```

```python
import functools
import math

import jax
import jax.numpy as jnp
from jax import lax
from jax.experimental import pallas as pl
from jax.experimental.pallas import tpu as pltpu

F32 = jnp.float32
BF16 = jnp.bfloat16

LANES = 128
SUBLANES = 8

HEAD_DIM = 64
HEADS = 8
KV_HEADS = 2
GROUP_HEADS = HEADS // KV_HEADS
CHUNK = 64
CONV_K = 5
GRID_W = 64
WINDOW = 128
ROPE_THETA = 10000.0
EPS = 1e-6
NEG = -1e30
PAIRS = HEADS // 2

G_Q, G_K, G_V, G_Z = 0, 4, 8, 12
G_BQ, G_CQ = 16, 20
G_GATES = 24
G_BK, G_BV, G_CK, G_CV = 48, 49, 50, 51
G_BA = 52
N_GROUPS = 56
IN_COLS_PAD = N_GROUPS * LANES


def _cparams(sem, vmem_mb):
    return pltpu.CompilerParams(dimension_semantics=sem, vmem_limit_bytes=vmem_mb << 20)


def _dot(a, b):
    return jnp.dot(a, b, preferred_element_type=F32)


def _dot_nt(a, b):
    return lax.dot_general(a, b, (((1,), (1,)), ((), ())), preferred_element_type=F32)


def _sigmoid(x):
    return 1.0 / (1.0 + jnp.exp(-x))


def _silu(x):
    return x * _sigmoid(x)


def _softplus(x):
    return jnp.maximum(x, 0.0) + jnp.log1p(jnp.exp(-jnp.abs(x)))


def _half_sums(x, lane_lo):
    s_lo = jnp.sum(jnp.where(lane_lo, x, 0.0), axis=-1, keepdims=True)
    s_hi = jnp.sum(jnp.where(lane_lo, 0.0, x), axis=-1, keepdims=True)
    return jnp.where(lane_lo, s_lo, s_hi)


def _lane_lo(rows):
    return lax.broadcasted_iota(jnp.int32, (rows, LANES), 1) < HEAD_DIM


def _row_tile(n_ctx, dec_seq, pref):
    return math.gcd(math.gcd(n_ctx, dec_seq), pref)


def _mod_index(i, tm, n_ctx, dec_seq):
    ctx_tiles = n_ctx // tm
    return jnp.where(i < ctx_tiles, 0, 1 + (i - ctx_tiles) // (dec_seq // tm))


def _mod_kernel(c_ref, w_ref, b_ref, o_ref):
    s = _silu(c_ref[...])
    w = w_ref[0]
    s_hi = s.astype(BF16)
    s_lo = (s - s_hi.astype(F32)).astype(BF16)
    w_hi = w.astype(BF16)
    w_lo = (w - w_hi.astype(F32)).astype(BF16)
    acc = _dot(s_hi, w_hi) + (_dot(s_lo, w_hi) + _dot(s_hi, w_lo))
    o_ref[0] = acc + b_ref[0]


def _mod_vectors(cvec, w_mod, b_mod):
    depth, d, n = w_mod.shape
    tn = n // 4
    return pl.pallas_call(
        _mod_kernel,
        out_shape=jax.ShapeDtypeStruct((depth, SUBLANES, n), F32),
        grid=(depth, n // tn),
        in_specs=[pl.BlockSpec((SUBLANES, d), lambda l, j: (0, 0)),
                  pl.BlockSpec((1, d, tn), lambda l, j: (l, 0, j)),
                  pl.BlockSpec((1, 1, tn), lambda l, j: (l, 0, j))],
        out_specs=pl.BlockSpec((1, SUBLANES, tn), lambda l, j: (l, 0, j)),
        compiler_params=_cparams(("parallel", "parallel"), 40),
        name="mod_vectors",
    )(cvec, w_mod, b_mod.reshape(depth, 1, n))


def _mod_norm(x, ln, shift, scale):
    ms = jnp.mean(x * x, axis=-1, keepdims=True)
    return (x * lax.rsqrt(ms + EPS) * ln) * (1.0 + scale) + shift


def _inproj_kernel(x_ref, mod_ref, ln_ref, w_ref, o_ref, h_scr):
    d = x_ref.shape[1]

    @pl.when(pl.program_id(1) == 0)
    def _():
        m = mod_ref[0]
        h = _mod_norm(x_ref[...], ln_ref[...], m[:, 0:d], m[:, d:2 * d])
        h_scr[...] = h.astype(BF16)

    o_ref[...] = _dot(h_scr[...], w_ref[...])


def _in_proj(x, mods, ln1, w_in, n_ctx, dec_seq):
    nt, d = x.shape
    n = w_in.shape[1]
    tm, tn = _row_tile(n_ctx, dec_seq, 1024), 1024
    return pl.pallas_call(
        _inproj_kernel,
        out_shape=jax.ShapeDtypeStruct((nt, n), F32),
        grid=(nt // tm, n // tn),
        in_specs=[pl.BlockSpec((tm, d), lambda i, j: (i, 0)),
                  pl.BlockSpec((1, 1, 6 * d), lambda i, j: (_mod_index(i, tm, n_ctx, dec_seq), 0, 0)),
                  pl.BlockSpec((1, d), lambda i, j: (0, 0)),
                  pl.BlockSpec((d, tn), lambda i, j: (0, j))],
        out_specs=pl.BlockSpec((tm, tn), lambda i, j: (i, j)),
        scratch_shapes=[pltpu.VMEM((tm, d), BF16)],
        compiler_params=_cparams(("parallel", "arbitrary"), 40),
        name="in_proj",
    )(x, mods, ln1, w_in)


_GROUP = 4
_CONV_ROWS = 256


def _gdn_kernel(*refs, t, has_s0, has_sfin):
    (q_ref, k_ref, v_ref, z_ref, ba_ref, cwq_ref, cwk_ref, cwv_ref,
     alog_ref, dtb_ref, anorm_ref) = refs[:11]
    pos = 11
    s0_ref = None
    if has_s0:
        s0_ref = refs[pos]
        pos += 1
    y_ref = refs[pos]
    pos += 1
    sfin_ref = None
    if has_sfin:
        sfin_ref = refs[pos]
        pos += 1
    (pad_scr, qn_scr, kn_scr, vv_scr, of_scr, ob_scr,
     u_scr, w_scr, qg_scr, at_scr, kt_scr, gl_scr) = refs[pos:]

    n_chunks = t // CHUNK
    group = min(_GROUP, n_chunks)
    n_groups = n_chunks // group
    rows = min(_CONV_ROWS, t)
    lane_lo_c = _lane_lo(rows)

    def conv_into(x_ref, cw_ref, dst_scr, norm_scale):
        zeros8 = jnp.zeros((SUBLANES, LANES), F32)
        pad_scr[0:SUBLANES, :] = zeros8
        pad_scr[t + SUBLANES:t + 2 * SUBLANES, :] = zeros8

        def copy_body(i, carry):
            r = pl.multiple_of(i * rows, rows)
            pad_scr[pl.ds(r + SUBLANES, rows), :] = x_ref[pl.ds(r, rows), :]
            return carry

        lax.fori_loop(0, t // rows, copy_body, 0)
        cw = cw_ref[...]

        def conv_body(i, carry):
            r = pl.multiple_of(i * rows, rows)
            win = pad_scr[pl.ds(r, rows + 2 * SUBLANES), :]
            acc = None
            for j in range(CONV_K):
                off = SUBLANES - CONV_K // 2 + j
                term = win[off:off + rows, :] * cw[j:j + 1, :]
                acc = term if acc is None else acc + term
            y = _silu(acc)
            if norm_scale is not None:
                ss = _half_sums(y * y, lane_lo_c)
                y = y * lax.rsqrt(ss + EPS) * norm_scale
            dst_scr[pl.ds(r, rows), :] = y
            return carry

        lax.fori_loop(0, t // rows, conv_body, 0)

    conv_into(q_ref, cwq_ref, qn_scr, HEAD_DIM ** -0.5)
    conv_into(k_ref, cwk_ref, kn_scr, 1.0)
    conv_into(v_ref, cwv_ref, vv_scr, None)

    n2 = 2 * CHUNK
    ri = lax.broadcasted_iota(jnp.int32, (n2, n2), 0)
    ci = lax.broadcasted_iota(jnp.int32, (n2, n2), 1)
    same = (ri >= CHUNK) == (ci >= CHUNK)
    incl = (same & (ri >= ci), same & (ri <= ci))
    strict = (same & (ri > ci), same & (ri < ci))
    eye = (ri == ci).astype(F32)
    ril, cil = ri % CHUNK, ci % CHUNK
    level_mask = []
    m = 1
    while m < CHUNK:
        level_mask.append(same & (ril // (2 * m) == cil // (2 * m)) & (ril // m != cil // m))
        m *= 2
    row_in_chunk = lax.broadcasted_iota(jnp.int32, (CHUNK, LANES), 0)
    alog = alog_ref[0]
    dtb = dtb_ref[0]

    def col_stack(arr, ca, cb):
        top = jnp.broadcast_to(arr[:, ca:ca + 1], (CHUNK, LANES))
        bot = jnp.broadcast_to(arr[:, cb:cb + 1], (CHUNK, LANES))
        return jnp.concatenate([top, bot], axis=0)

    def row_stack(arr, row):
        top = jnp.broadcast_to(arr[row:row + 1, :], (CHUNK, LANES))
        bot = jnp.broadcast_to(arr[CHUNK + row:CHUNK + row + 1, :], (CHUNK, LANES))
        return jnp.concatenate([top, bot], axis=0)

    def chunk_cumsum(g, reverse):
        s = 1
        while s < CHUNK:
            if reverse:
                sh = pltpu.roll(g, CHUNK - s, axis=0)
                g = g + jnp.where(row_in_chunk + s < CHUNK, sh, 0.0)
            else:
                sh = pltpu.roll(g, s, axis=0)
                g = g + jnp.where(row_in_chunk >= s, sh, 0.0)
            s *= 2
        return g

    def units_pre(units):
        idx = range(len(units))
        dirs = [d for _, d, _ in units]
        g_c, b_c, ks, qs, k2, v2 = [], [], [], [], [], []
        for r0, d, _ in units:
            x = ba_ref[pl.ds(r0, CHUNK), :]
            g = -jnp.exp(alog) * _softplus(x + dtb)
            gc = chunk_cumsum(g, reverse=(d == 1))
            b_c.append(col_stack(_sigmoid(x), 2 * d, 2 * d + 1))
            g_c.append(col_stack(gc, 4 + 2 * d, 5 + 2 * d))
            k = kn_scr[pl.ds(r0, CHUNK), :]
            q = qn_scr[pl.ds(r0, CHUNK), :]
            v = vv_scr[pl.ds(r0, CHUNK), :]
            k2.append(jnp.concatenate([k, k], axis=0))
            ks.append(jnp.where(same, k2[-1], 0.0))
            qs.append(jnp.where(same, jnp.concatenate([q, q], axis=0), 0.0))
            v2.append(jnp.concatenate([v, v], axis=0))
        ks_b = [ks[i].astype(BF16) for i in idx]
        prod = [_dot_nt(jnp.concatenate([ks_b[i], qs[i].astype(BF16)], axis=0), ks_b[i]) for i in idx]
        decay = [jnp.where(incl[dirs[i]], jnp.exp(jnp.where(incl[dirs[i]], g_c[i] - g_c[i].T, 0.0)), 0.0) for i in idx]
        eg = [jnp.exp(g_c[i]) for i in idx]
        g_last = [row_stack(g_c[i], CHUNK - 1 if dirs[i] == 0 else 0) for i in idx]
        l_mat = [jnp.where(strict[dirs[i]], b_c[i] * prod[i][:n2] * decay[i], 0.0) for i in idx]
        rhs = [(b_c[i] * jnp.where(same, v2[i], eg[i] * pltpu.roll(k2[i], CHUNK, axis=1))).astype(BF16) for i in idx]
        for i, (_, _, slot) in enumerate(units):
            at_scr[slot] = (prod[i][n2:] * decay[i]).astype(BF16)
            qg_scr[slot] = (qs[i] * eg[i]).astype(BF16)
            kt_scr[slot] = (ks[i] * jnp.exp(g_last[i] - g_c[i])).T.astype(BF16)
            gl_scr[slot] = jnp.exp(g_last[i])
        t_inv = [eye - jnp.where(level_mask[0], l_mat[i], 0.0) for i in idx]
        for lm in level_mask[1:]:
            t_b = [t_inv[i].astype(BF16) for i in idx]
            y = [_dot(jnp.where(lm, l_mat[i], 0.0).astype(BF16), t_b[i]).astype(BF16) for i in idx]
            t_inv = [t_inv[i] - _dot(t_b[i], y[i]) for i in idx]
        xs = [_dot(t_inv[i].astype(BF16), rhs[i]) for i in idx]
        for i, (_, _, slot) in enumerate(units):
            u_scr[slot] = jnp.where(same, xs[i], 0.0)
            w_scr[slot] = jnp.where(same, pltpu.roll(xs[i], CHUNK, axis=1), 0.0).astype(BF16)

    def units_scan(states, slots):
        idx = range(len(slots))
        ws = [_dot(jnp.concatenate([w_scr[slots[i]], qg_scr[slots[i]]], axis=0), states[i].astype(BF16))
              for i in idx]
        u_b = [(u_scr[slots[i]] - ws[i][:n2]).astype(BF16) for i in idx]
        o_st = [ws[i][n2:] + _dot(at_scr[slots[i]], u_b[i]) for i in idx]
        s_new = [gl_scr[slots[i]] * states[i] + _dot(kt_scr[slots[i]], u_b[i]) for i in idx]
        return s_new, [o[:CHUNK] + o[CHUNK:] for o in o_st]

    if has_s0:
        s_init = (s0_ref[0, 0, 0], s0_ref[0, 0, 1])
    else:
        s_init = (jnp.zeros((n2, n2), F32), jnp.zeros((n2, n2), F32))

    def group_body(gi, carry):
        s_f, s_b = carry
        units = []
        for i in range(group):
            c_f = gi * group + i
            c_b = n_chunks - 1 - c_f
            units.append((pl.multiple_of(c_f * CHUNK, CHUNK), 0, 2 * i))
            units.append((pl.multiple_of(c_b * CHUNK, CHUNK), 1, 2 * i + 1))
        units_pre(units)
        for i in range(group):
            (s_f, s_b), (o_f, o_b) = units_scan([s_f, s_b], [2 * i, 2 * i + 1])
            of_scr[pl.ds(units[2 * i][0], CHUNK), :] = o_f
            ob_scr[pl.ds(units[2 * i + 1][0], CHUNK), :] = o_b
        return s_f, s_b

    s_f, s_b = lax.fori_loop(0, n_groups, group_body, s_init)
    if has_sfin:
        sfin_ref[0, 0, 0] = s_f
        sfin_ref[0, 0, 1] = s_b

    anorm = anorm_ref[...]

    def out_body(i, carry):
        r = pl.multiple_of(i * rows, rows)
        o = of_scr[pl.ds(r, rows), :] + ob_scr[pl.ds(r, rows), :]
        ms = _half_sums(o * o, lane_lo_c) * (1.0 / HEAD_DIM)
        on = o * lax.rsqrt(ms + EPS) * anorm
        y_ref[pl.ds(r, rows), :] = (on * _silu(z_ref[pl.ds(r, rows), :])).astype(y_ref.dtype)
        return carry

    lax.fori_loop(0, t // rows, out_body, 0)


def _gdn(proj, conv_w, alog_rows, dtb_rows, anorm_row, s0, *, t, row0, n_seq, want_state):
    blk0 = row0 // t
    has_s0 = s0 is not None
    n2 = 2 * CHUNK
    group = min(_GROUP, t // CHUNK)

    def col(g):
        return pl.BlockSpec((t, LANES), lambda b, p, g=g: (blk0 + b, g + p))

    def cw(g):
        return pl.BlockSpec((CONV_K, LANES), lambda b, p, g=g: (0, g + p))

    in_specs = [col(G_Q), col(G_K), col(G_V), col(G_Z), col(G_BA),
                cw(G_Q), cw(G_K), cw(G_V),
                pl.BlockSpec((1, 1, LANES), lambda b, p: (p, 0, 0)),
                pl.BlockSpec((1, 1, LANES), lambda b, p: (p, 0, 0)),
                pl.BlockSpec((1, LANES), lambda b, p: (0, 0))]
    args = [proj, proj, proj, proj, proj, conv_w, conv_w, conv_w, alog_rows, dtb_rows, anorm_row]
    if has_s0:
        in_specs.append(pl.BlockSpec((1, 1, 2, n2, n2), lambda b, p: (b, p, 0, 0, 0)))
        args.append(s0)
    out_shape = [jax.ShapeDtypeStruct((n_seq * t, PAIRS * LANES), BF16)]
    out_specs = [pl.BlockSpec((t, LANES), lambda b, p: (b, p))]
    if want_state:
        out_shape.append(jax.ShapeDtypeStruct((n_seq, PAIRS, 2, n2, n2), F32))
        out_specs.append(pl.BlockSpec((1, 1, 2, n2, n2), lambda b, p: (b, p, 0, 0, 0)))
    scratch = [pltpu.VMEM((t + 2 * SUBLANES, LANES), F32)] + [pltpu.VMEM((t, LANES), F32)] * 5
    scratch += [pltpu.VMEM((2 * group, n2, n2), F32),
                pltpu.VMEM((2 * group, n2, n2), BF16),
                pltpu.VMEM((2 * group, n2, n2), BF16),
                pltpu.VMEM((2 * group, n2, n2), BF16),
                pltpu.VMEM((2 * group, n2, n2), BF16),
                pltpu.VMEM((2 * group, n2, n2), F32)]
    return pl.pallas_call(
        functools.partial(_gdn_kernel, t=t, has_s0=has_s0, has_sfin=want_state),
        out_shape=out_shape,
        grid=(n_seq, PAIRS),
        in_specs=in_specs,
        out_specs=out_specs,
        scratch_shapes=scratch,
        compiler_params=_cparams(("parallel", "parallel"), 48),
        name="gdn_ctx" if want_state else "gdn_lat",
    )(*args)


def _prep_kernel(*refs, rope, want_kv):
    bq_ref, cq_ref, bk_ref, bv_ref, ck_ref, cv_ref, qkn_ref = refs[:7]
    pos = 7
    cs = sn = None
    if rope:
        cs, sn = refs[7][...], refs[8][...]
        pos = 9
    qb_ref, qc_ref, kdb_ref, vdb_ref, kdc_ref, vdc_ref = refs[pos:pos + 6]
    pos += 6
    tr = bq_ref.shape[0]
    lane = lax.broadcasted_iota(jnp.int32, (tr, LANES), 1)
    lane_lo = lane < HEAD_DIM
    first_half = (lane % HEAD_DIM) < HEAD_DIM // 2
    qkn = qkn_ref[...]

    def norm(x, w_row):
        ms = _half_sums(x * x, lane_lo) * (1.0 / HEAD_DIM)
        return x * lax.rsqrt(ms + EPS) * w_row

    def rot(y):
        if not rope:
            return y
        partner = jnp.where(first_half, pltpu.roll(y, LANES - HEAD_DIM // 2, axis=1),
                            pltpu.roll(y, HEAD_DIM // 2, axis=1))
        return y * cs + partner * sn

    def dup(x, o_ref):
        lo = jnp.where(lane_lo, x, 0.0)
        hi = jnp.where(lane_lo, 0.0, x)
        o_ref[:, 0:LANES] = (lo + pltpu.roll(lo, HEAD_DIM, axis=1)).astype(BF16)
        o_ref[:, LANES:2 * LANES] = (hi + pltpu.roll(hi, HEAD_DIM, axis=1)).astype(BF16)

    for q_ref, o_ref, wi in ((bq_ref, qb_ref, 0), (cq_ref, qc_ref, 2)):
        for g in range(PAIRS):
            y = rot(norm(q_ref[:, g * LANES:(g + 1) * LANES], qkn[wi:wi + 1, :])) * HEAD_DIM ** -0.5
            o_ref[2 * g] = jnp.where(lane_lo, y, 0.0).astype(BF16)
            o_ref[2 * g + 1] = jnp.where(lane_lo, 0.0, y).astype(BF16)
    kb_n = norm(bk_ref[...], qkn[1:2, :])
    kc_n = norm(ck_ref[...], qkn[3:4, :])
    if want_kv:
        kbn_ref, kcn_ref = refs[pos:pos + 2]
        kbn_ref[...] = kb_n
        kcn_ref[...] = kc_n
    dup(rot(kb_n), kdb_ref)
    dup(bv_ref[...], vdb_ref)
    dup(rot(kc_n), kdc_ref)
    dup(cv_ref[...], vdc_ref)


def _attn_prep(proj, qkn_rows, cs, sn, *, row0, n_rows, want_kv):
    rope = cs is not None
    tr = math.gcd(n_rows, 512)
    blk0 = row0 // tr

    def col(g, width):
        gw = width // LANES
        return pl.BlockSpec((tr, width), lambda i, g=g, gw=gw: (blk0 + i, g // gw))

    in_specs = [col(G_BQ, 4 * LANES), col(G_CQ, 4 * LANES), col(G_BK, LANES), col(G_BV, LANES),
                col(G_CK, LANES), col(G_CV, LANES), pl.BlockSpec((4, LANES), lambda i: (0, 0))]
    args = [proj] * 6 + [qkn_rows]
    if rope:
        per_seq = cs.shape[0] // tr
        in_specs += [pl.BlockSpec((tr, LANES), lambda i: (i % per_seq, 0))] * 2
        args += [cs, sn]
    out_shape = [jax.ShapeDtypeStruct((HEADS, n_rows, LANES), BF16)] * 2
    out_shape += [jax.ShapeDtypeStruct((n_rows, 2 * LANES), BF16)] * 4
    out_specs = [pl.BlockSpec((HEADS, tr, LANES), lambda i: (0, i, 0))] * 2
    out_specs += [pl.BlockSpec((tr, 2 * LANES), lambda i: (i, 0))] * 4
    if want_kv:
        out_shape += [jax.ShapeDtypeStruct((n_rows, LANES), F32)] * 2
        out_specs += [pl.BlockSpec((tr, LANES), lambda i: (i, 0))] * 2
    return pl.pallas_call(
        functools.partial(_prep_kernel, rope=rope, want_kv=want_kv),
        out_shape=out_shape,
        grid=(n_rows // tr,),
        in_specs=in_specs,
        out_specs=out_specs,
        compiler_params=_cparams(("parallel",), 32),
        name="attn_prep_lat" if rope else "attn_prep_ctx",
    )(*args)


def _stacked_q(q_ref, g, tq):
    return q_ref[GROUP_HEADS * g:GROUP_HEADS * (g + 1)].reshape(GROUP_HEADS * tq, LANES)


def _unstack_group(o_st, tq, lane_lo):
    a = jnp.where(lane_lo, o_st[0:tq], o_st[tq:2 * tq])
    b = jnp.where(lane_lo, o_st[2 * tq:3 * tq], o_st[3 * tq:4 * tq])
    return jnp.concatenate([a, b], axis=1)


def _sink_col(sink_row, g, tq):
    parts = [jnp.broadcast_to(sink_row[:, GROUP_HEADS * g + j:GROUP_HEADS * g + j + 1], (tq, 1))
             for j in range(GROUP_HEADS)]
    return jnp.concatenate(parts, axis=0)


def _attn_ctx_kernel(qb_ref, kdb_ref, vdb_ref, qc_ref, kdc_ref, vdc_ref, sink_ref, yb_ref, yc_ref):
    tq = qb_ref.shape[1]
    lane_lo_q = _lane_lo(tq)
    sink_row = sink_ref[...]

    def attend(q_ref, kd_ref, vd_ref, y_ref, use_sink):
        for g in range(KV_HEADS):
            gs = slice(g * LANES, (g + 1) * LANES)
            s = _dot_nt(_stacked_q(q_ref, g, tq), kd_ref[:, gs])
            m = jnp.max(s, axis=-1, keepdims=True)
            if use_sink:
                snk = _sink_col(sink_row, g, tq)
                m = jnp.maximum(m, snk)
            p = jnp.exp(s - m)
            l = jnp.sum(p, axis=-1, keepdims=True)
            if use_sink:
                l = l + jnp.exp(snk - m)
            o = _dot(p.astype(BF16), vd_ref[:, gs]) / l
            y_ref[:, 2 * g * LANES:(2 * g + 2) * LANES] = _unstack_group(o, tq, lane_lo_q).astype(y_ref.dtype)

    attend(qb_ref, kdb_ref, vdb_ref, yb_ref, False)
    attend(qc_ref, kdc_ref, vdc_ref, yc_ref, True)


def _attn_ctx(qb, kdb, vdb, qc, kdc, vdc, sink_row, *, t):
    n = kdb.shape[0]
    q_spec = pl.BlockSpec((HEADS, t, LANES), lambda b: (0, b, 0))
    kv_spec = pl.BlockSpec((t, 2 * LANES), lambda b: (b, 0))
    y_spec = pl.BlockSpec((t, 4 * LANES), lambda b: (b, 0))
    return pl.pallas_call(
        _attn_ctx_kernel,
        out_shape=[jax.ShapeDtypeStruct((n, 4 * LANES), BF16)] * 2,
        grid=(n // t,),
        in_specs=[q_spec, kv_spec, kv_spec, q_spec, kv_spec, kv_spec,
                  pl.BlockSpec((1, LANES), lambda b: (0, 0))],
        out_specs=[y_spec, y_spec],
        compiler_params=_cparams(("parallel",), 32),
        name="attn_ctx",
    )(qb, kdb, vdb, qc, kdc, vdc, sink_row)


_TQ_B = 256
_TK_B = 1536


def _attn_glob_kernel(q_ref, kd_ref, vd_ref, y_ref, m_scr, l_scr, acc_scr):
    kj = pl.program_id(2)
    tq = q_ref.shape[1]

    @pl.when(kj == 0)
    def _():
        m_scr[...] = jnp.full(m_scr.shape, NEG, F32)
        l_scr[...] = jnp.zeros(l_scr.shape, F32)
        acc_scr[...] = jnp.zeros(acc_scr.shape, F32)

    groups = range(KV_HEADS)
    gs = [slice(g * LANES, (g + 1) * LANES) for g in groups]
    s = [_dot_nt(_stacked_q(q_ref, g, tq), kd_ref[0, :, gs[g]]) for g in groups]
    for g in groups:
        m_old = m_scr[g]
        m_new = jnp.maximum(m_old, jnp.max(s[g], axis=-1, keepdims=True))
        p = jnp.exp(s[g] - m_new[:, 0:1])
        alpha = jnp.exp(m_old - m_new)
        l_scr[g] = alpha * l_scr[g] + jnp.sum(p, axis=-1, keepdims=True)
        acc_scr[g] = alpha * acc_scr[g] + _dot(p.astype(BF16), vd_ref[0, :, gs[g]])
        m_scr[g] = m_new

    @pl.when(kj == pl.num_programs(2) - 1)
    def _():
        lane_lo_q = _lane_lo(tq)
        for g in range(KV_HEADS):
            o = acc_scr[g] / l_scr[g]
            y_ref[:, 2 * g * LANES:(2 * g + 2) * LANES] = _unstack_group(o, tq, lane_lo_q).astype(y_ref.dtype)


def _attn_glob(q, kd, vd, *, t):
    n = q.shape[1]
    s_len = kd.shape[1]
    tq, tk = math.gcd(t, _TQ_B), math.gcd(s_len, _TK_B)
    per_seq = t // tq
    return pl.pallas_call(
        _attn_glob_kernel,
        out_shape=jax.ShapeDtypeStruct((n, 4 * LANES), BF16),
        grid=(n // t, per_seq, s_len // tk),
        in_specs=[pl.BlockSpec((HEADS, tq, LANES), lambda b, i, j: (0, b * per_seq + i, 0)),
                  pl.BlockSpec((1, tk, 2 * LANES), lambda b, i, j: (b, j, 0)),
                  pl.BlockSpec((1, tk, 2 * LANES), lambda b, i, j: (b, j, 0))],
        out_specs=pl.BlockSpec((tq, 4 * LANES), lambda b, i, j: (b * per_seq + i, 0)),
        scratch_shapes=[pltpu.VMEM((KV_HEADS, GROUP_HEADS * tq, LANES), F32)] * 3,
        compiler_params=_cparams(("parallel", "parallel", "arbitrary"), 56),
        name="attn_glob",
    )(q, kd, vd)


def _attn_win_kernel(q_ref, kx_ref, vx_ref, kp_ref, kc_ref, kn_ref, vp_ref, vc_ref, vn_ref,
                     sink_ref, y_ref):
    i = pl.program_id(1)
    nblk = pl.num_programs(1)
    tq = q_ref.shape[1]
    lane_lo_q = _lane_lo(tq)
    sink_row = sink_ref[...]
    rr = lax.broadcasted_iota(jnp.int32, (GROUP_HEADS * tq, tq), 0) % tq
    cc = lax.broadcasted_iota(jnp.int32, (GROUP_HEADS * tq, tq), 1)
    mask_prev = (cc >= rr) & (i > 0)
    mask_next = (cc <= rr) & (i < nblk - 1)
    rowmax = lambda a: jnp.max(a, axis=-1, keepdims=True)
    rowsum = lambda a: jnp.sum(a, axis=-1, keepdims=True)
    for g in range(KV_HEADS):
        gs = slice(g * LANES, (g + 1) * LANES)
        qs = _stacked_q(q_ref, g, tq)
        s_x = _dot_nt(qs, kx_ref[0, :, gs])
        s_p = jnp.where(mask_prev, _dot_nt(qs, kp_ref[:, gs]), NEG)
        s_c = _dot_nt(qs, kc_ref[:, gs])
        s_n = jnp.where(mask_next, _dot_nt(qs, kn_ref[:, gs]), NEG)
        snk = _sink_col(sink_row, g, tq)
        m = jnp.maximum(jnp.maximum(rowmax(s_x), rowmax(s_c)), jnp.maximum(rowmax(s_p), rowmax(s_n)))
        m = jnp.maximum(m, snk)
        p_x = jnp.exp(s_x - m)
        p_p = jnp.exp(s_p - m)
        p_c = jnp.exp(s_c - m)
        p_n = jnp.exp(s_n - m)
        l = rowsum(p_x) + rowsum(p_c) + rowsum(p_p) + rowsum(p_n) + jnp.exp(snk - m)
        o = (_dot(p_x.astype(BF16), vx_ref[0, :, gs]) + _dot(p_c.astype(BF16), vc_ref[:, gs])
             + _dot(p_p.astype(BF16), vp_ref[:, gs]) + _dot(p_n.astype(BF16), vn_ref[:, gs])) / l
        y_ref[:, 2 * g * LANES:(2 * g + 2) * LANES] = _unstack_group(o, tq, lane_lo_q).astype(y_ref.dtype)


def _attn_win(q, kd, vd, kx, vx, sink_row, *, t):
    n = q.shape[1]
    p_len = kx.shape[1]
    tq = WINDOW
    nblk = t // tq
    x_spec = pl.BlockSpec((1, p_len, 2 * LANES), lambda b, i: (b, 0, 0))
    prev = pl.BlockSpec((tq, 2 * LANES), lambda b, i: (b * nblk + jnp.maximum(i - 1, 0), 0))
    cur = pl.BlockSpec((tq, 2 * LANES), lambda b, i: (b * nblk + i, 0))
    nxt = pl.BlockSpec((tq, 2 * LANES), lambda b, i: (b * nblk + jnp.minimum(i + 1, nblk - 1), 0))
    return pl.pallas_call(
        _attn_win_kernel,
        out_shape=jax.ShapeDtypeStruct((n, 4 * LANES), BF16),
        grid=(n // t, nblk),
        in_specs=[pl.BlockSpec((HEADS, tq, LANES), lambda b, i: (0, b * nblk + i, 0)),
                  x_spec, x_spec, prev, cur, nxt, prev, cur, nxt,
                  pl.BlockSpec((1, LANES), lambda b, i: (0, 0))],
        out_specs=pl.BlockSpec((tq, 4 * LANES), lambda b, i: (b * nblk + i, 0)),
        compiler_params=_cparams(("parallel", "parallel"), 32),
        name="attn_win",
    )(q, kx, vx, kd, kd, kd, vd, vd, vd, sink_row)


def _mix_kernel(x_ref, ya_ref, yb_ref, yc_ref, ga_ref, gb_ref, gc_ref, mod_ref,
                wa_ref, wb_ref, wc_ref, wo_ref, o_ref):
    d = x_ref.shape[1]
    merged = (_sigmoid(ga_ref[...]) * _dot(ya_ref[...], wa_ref[...])
              + _sigmoid(gb_ref[...]) * _dot(yb_ref[...], wb_ref[...])
              + _sigmoid(gc_ref[...]) * _dot(yc_ref[...], wc_ref[...]))
    gate1 = mod_ref[0][:, 2 * d:3 * d]
    o_ref[...] = x_ref[...] + gate1 * _dot(merged.astype(BF16), wo_ref[...])


def _mix(x, ya, yb, yc, proj, mods, wa, wb, wc, wo, n_ctx, dec_seq):
    nt, d = x.shape
    tm = _row_tile(n_ctx, dec_seq, 512)
    y_spec = pl.BlockSpec((tm, ya.shape[1]), lambda i: (i, 0))
    gw = d // LANES

    def gate(k):
        return pl.BlockSpec((tm, d), lambda i, k=k: (i, G_GATES // gw + k))

    def full(a):
        return pl.BlockSpec(a.shape, lambda i: (0, 0))

    return pl.pallas_call(
        _mix_kernel,
        out_shape=jax.ShapeDtypeStruct((nt, d), F32),
        grid=(nt // tm,),
        in_specs=[pl.BlockSpec((tm, d), lambda i: (i, 0)), y_spec, y_spec, y_spec,
                  gate(0), gate(1), gate(2),
                  pl.BlockSpec((1, 1, 6 * d), lambda i: (_mod_index(i, tm, n_ctx, dec_seq), 0, 0)),
                  full(wa), full(wb), full(wc), full(wo)],
        out_specs=pl.BlockSpec((tm, d), lambda i: (i, 0)),
        compiler_params=_cparams(("parallel",), 48),
        name="mix_out",
    )(x, ya, yb, yc, proj, proj, proj, mods, wa, wb, wc, wo)


_FF_CHUNK = 1024


def _ffn_kernel(x_ref, mod_ref, ln_ref, w1_ref, w2_ref, o_ref):
    d = x_ref.shape[1]
    m = mod_ref[0]
    x = x_ref[...]
    h = _mod_norm(x, ln_ref[...], m[:, 3 * d:4 * d], m[:, 4 * d:5 * d]).astype(BF16)
    acc = None
    for f in range(0, w1_ref.shape[1], _FF_CHUNK):
        hid = jnp.maximum(_dot(h, w1_ref[:, f:f + _FF_CHUNK]), 0.0)
        part = _dot((hid * hid).astype(BF16), w2_ref[f:f + _FF_CHUNK, :])
        acc = part if acc is None else acc + part
    o_ref[...] = x + m[:, 5 * d:6 * d] * acc


def _ffn(x, mods, ln2, w1, w2, n_ctx, dec_seq):
    nt, d = x.shape
    tm = _row_tile(n_ctx, dec_seq, 512)
    return pl.pallas_call(
        _ffn_kernel,
        out_shape=jax.ShapeDtypeStruct((nt, d), F32),
        grid=(nt // tm,),
        in_specs=[pl.BlockSpec((tm, d), lambda i: (i, 0)),
                  pl.BlockSpec((1, 1, 6 * d), lambda i: (_mod_index(i, tm, n_ctx, dec_seq), 0, 0)),
                  pl.BlockSpec((1, d), lambda i: (0, 0)),
                  pl.BlockSpec(w1.shape, lambda i: (0, 0)),
                  pl.BlockSpec(w2.shape, lambda i: (0, 0))],
        out_specs=pl.BlockSpec((tm, d), lambda i: (i, 0)),
        compiler_params=_cparams(("parallel",), 56),
        name="ffn",
    )(x, mods, ln2, w1, w2)


def _reorder_w_in(w_in):
    a_w = HEADS * HEAD_DIM
    kvw = KV_HEADS * HEAD_DIM
    o_beta = 4 * a_w
    o_bq = o_beta + 4 * HEADS
    o_bk = o_bq + a_w
    o_bv = o_bk + kvw
    o_cq = o_bv + kvw
    o_ck = o_cq + a_w
    o_cv = o_ck + kvw
    o_g = o_cv + kvw
    seg = lambda a, n: w_in[:, :, a:a + n]
    ba = w_in[:, :, o_beta:o_bq]
    ba_groups = []
    for p in range(PAIRS):
        cols = []
        for base in (0, 2 * HEADS):
            for d in range(2):
                cols += [base + d * HEADS + 2 * p, base + d * HEADS + 2 * p + 1]
        grp = jnp.take(ba, jnp.array(cols, jnp.int32), axis=2)
        ba_groups.append(jnp.pad(grp, ((0, 0), (0, 0), (0, LANES - len(cols)))))
    out = jnp.concatenate([seg(0, o_beta), seg(o_bq, a_w), seg(o_cq, a_w), seg(o_g, w_in.shape[2] - o_g),
                           seg(o_bk, kvw), seg(o_bv, kvw), seg(o_ck, kvw), seg(o_cv, kvw)] + ba_groups, axis=2)
    assert out.shape[2] == IN_COLS_PAD
    return out.astype(BF16)


def _pair_rows(vals):
    rows = []
    for p in range(PAIRS):
        r = jnp.stack([vals[:, 0, 2 * p], vals[:, 0, 2 * p + 1], vals[:, 1, 2 * p], vals[:, 1, 2 * p + 1]], axis=-1)
        rows.append(jnp.pad(r, ((0, 0), (4, LANES - 8))))
    return jnp.stack(rows, axis=1)[:, :, None, :]


def _rope_tables(n_tokens):
    rows = n_tokens // GRID_W
    row_id = jnp.repeat(jnp.arange(rows, dtype=F32), GRID_W)
    col_id = jnp.tile(jnp.arange(GRID_W, dtype=F32), rows)
    n_freq = HEAD_DIM // 4
    inv_freq = ROPE_THETA ** (-jnp.arange(n_freq, dtype=F32) / n_freq)
    ang = jnp.concatenate([row_id[:, None] * inv_freq, col_id[:, None] * inv_freq], axis=-1)
    cos, sin = jnp.cos(ang), jnp.sin(ang)
    cs = jnp.tile(cos, (1, 4))
    sn = jnp.tile(jnp.concatenate([-sin, sin], axis=-1), (1, 2))
    return cs, sn


def _dup_kv(x):
    return jnp.concatenate([x[:, :, 0], x[:, :, 0], x[:, :, 1], x[:, :, 1]], axis=-1).astype(BF16)


def _block_diag_states(s):
    b = s.shape[0]
    s = s.reshape(b, 2, PAIRS, 2, HEAD_DIM, HEAD_DIM).transpose(0, 2, 1, 3, 4, 5)
    z = jnp.zeros_like(s[:, :, :, 0])
    top = jnp.concatenate([s[:, :, :, 0], z], axis=-1)
    bot = jnp.concatenate([z, s[:, :, :, 1]], axis=-1)
    return jnp.concatenate([top, bot], axis=-2)


def _diag_states(s):
    a = s[:, :, :, :HEAD_DIM, :HEAD_DIM]
    bb = s[:, :, :, HEAD_DIM:, HEAD_DIM:]
    out = jnp.stack([a, bb], axis=3)
    b = s.shape[0]
    return out.transpose(0, 2, 1, 3, 4, 5).reshape(b, 2, HEADS, HEAD_DIM, HEAD_DIM)


def kernel(x_prompt, x_sample, cache_k_glob, cache_v_glob, cache_k_win, cache_v_win, state_delta, c, c_ctx, w_mod, b_mod, ln1, ln2, w_in, conv_qkv, a_log, dt_bias, a_norm, qk_norm, sink, w_br_a, w_br_b, w_br_c, w_o, w_ff1, w_ff2):
    batch, seq, d = x_prompt.shape
    dec_batch, dec_seq, _ = x_sample.shape
    depth = w_mod.shape[0]
    n_ctx = batch * seq
    n_lat = dec_batch * dec_seq
    assert 1 + dec_batch <= SUBLANES

    w_in_r = _reorder_w_in(w_in)
    wa, wb, wc, wo = (w.astype(BF16) for w in (w_br_a, w_br_b, w_br_c, w_o))
    w1, w2 = w_ff1.astype(BF16), w_ff2.astype(BF16)
    alog_rows = _pair_rows(a_log)
    dtb_rows = _pair_rows(dt_bias)
    anorm_rows = jnp.tile(a_norm, (1, 2))[:, None, :]
    qkn_rows = jnp.tile(qk_norm, (1, 1, 2))
    sink_rows = jnp.pad(sink, ((0, 0), (0, LANES - HEADS)))[:, None, :]
    cs, sn = _rope_tables(dec_seq)
    cvec = jnp.concatenate([c_ctx[None], c, jnp.zeros((SUBLANES - 1 - dec_batch, d), F32)], axis=0)
    mods_all = _mod_vectors(cvec, w_mod, b_mod)

    x = jnp.concatenate([x_prompt.reshape(n_ctx, d), x_sample.reshape(n_lat, d)], axis=0)
    new_kg, new_vg, new_kw, new_vw, new_st = [], [], [], [], []
    for l in range(depth):
        mods = mods_all[l][:, None, :]
        proj = _in_proj(x, mods, ln1[l][None], w_in_r[l], n_ctx, dec_seq)

        ya_ctx, s_fin = _gdn(proj, conv_qkv[l], alog_rows[l], dtb_rows[l], anorm_rows[l], None,
                             t=seq, row0=0, n_seq=batch, want_state=True)
        (ya_lat,) = _gdn(proj, conv_qkv[l], alog_rows[l], dtb_rows[l], anorm_rows[l],
                         _block_diag_states(state_delta[:, l]),
                         t=dec_seq, row0=n_ctx, n_seq=dec_batch, want_state=False)

        qb_c, qc_c, kdb_c, vdb_c, kdc_c, vdc_c, kb_n, kc_n = _attn_prep(
            proj, qkn_rows[l], None, None, row0=0, n_rows=n_ctx, want_kv=True)
        qb_l, qc_l, kdb_l, vdb_l, kdc_l, vdc_l = _attn_prep(
            proj, qkn_rows[l], cs, sn, row0=n_ctx, n_rows=n_lat, want_kv=False)

        yb_ctx, yc_ctx = _attn_ctx(qb_c, kdb_c, vdb_c, qc_c, kdc_c, vdc_c, sink_rows[l], t=seq)

        sh3 = lambda a: a.reshape(dec_batch, dec_seq, a.shape[1])
        kd_all = jnp.concatenate([_dup_kv(cache_k_glob[:, l]), sh3(kdb_l)], axis=1)
        vd_all = jnp.concatenate([_dup_kv(cache_v_glob[:, l]), sh3(vdb_l)], axis=1)
        yb_lat = _attn_glob(qb_l, kd_all, vd_all, t=dec_seq)
        yc_lat = _attn_win(qc_l, kdc_l, vdc_l, _dup_kv(cache_k_win[:, l]), _dup_kv(cache_v_win[:, l]),
                           sink_rows[l], t=dec_seq)

        ya = jnp.concatenate([ya_ctx, ya_lat], axis=0)
        yb = jnp.concatenate([yb_ctx, yb_lat], axis=0)
        yc = jnp.concatenate([yc_ctx, yc_lat], axis=0)
        x = _mix(x, ya, yb, yc, proj, mods, wa[l], wb[l], wc[l], wo[l], n_ctx, dec_seq)
        x = _ffn(x, mods, ln2[l][None], w1[l], w2[l], n_ctx, dec_seq)

        kv_shape = (batch, seq, KV_HEADS, HEAD_DIM)
        new_kg.append(kb_n.reshape(kv_shape))
        new_vg.append(proj[:n_ctx, G_BV * LANES:(G_BV + 1) * LANES].reshape(kv_shape))
        new_kw.append(kc_n.reshape(kv_shape))
        new_vw.append(proj[:n_ctx, G_CV * LANES:(G_CV + 1) * LANES].reshape(kv_shape))
        new_st.append(_diag_states(s_fin))

    y_prompt = x[:n_ctx].reshape(batch, seq, d)
    y_sample = x[n_ctx:].reshape(dec_batch, dec_seq, d)
    return (y_prompt, y_sample, jnp.stack(new_kg, axis=1), jnp.stack(new_vg, axis=1),
            jnp.stack(new_kw, axis=1), jnp.stack(new_vw, axis=1), jnp.stack(new_st, axis=1))
```

```python
import functools
import math

import jax
import jax.numpy as jnp
from jax import lax
from jax.experimental import pallas as pl
from jax.experimental.pallas import tpu as pltpu

F32 = jnp.float32
BF16 = jnp.bfloat16

LANES = 128
SUBLANES = 8

HEAD_DIM = 64
HEADS = 8
KV_HEADS = 2
GROUP_HEADS = HEADS // KV_HEADS
CHUNK = 64
CONV_K = 5
GRID_W = 64
WINDOW = 128
ROPE_THETA = 10000.0
EPS = 1e-6
NEG = -1e30
PAIRS = HEADS // 2

G_Q, G_K, G_V, G_Z = 0, 4, 8, 12
G_BQ, G_CQ = 16, 20
G_GATES = 24
G_BK, G_BV, G_CK, G_CV = 48, 49, 50, 51
G_BA = 52
N_GROUPS = 54
IN_COLS_PAD = N_GROUPS * LANES


def _cparams(sem, vmem_mb):
    return pltpu.CompilerParams(dimension_semantics=sem, vmem_limit_bytes=vmem_mb << 20)


def _dot(a, b):
    return jnp.dot(a, b, preferred_element_type=F32)


def _dot_nt(a, b):
    return lax.dot_general(a, b, (((1,), (1,)), ((), ())), preferred_element_type=F32)


def _sigmoid(x):
    return 1.0 / (1.0 + jnp.exp(-x))


def _silu(x):
    return x * _sigmoid(x)


def _softplus(x):
    return jnp.maximum(x, 0.0) + jnp.log1p(jnp.exp(-jnp.abs(x)))


def _half_sums(x, lane_lo):
    s_lo = jnp.sum(jnp.where(lane_lo, x, 0.0), axis=-1, keepdims=True)
    s_hi = jnp.sum(jnp.where(lane_lo, 0.0, x), axis=-1, keepdims=True)
    return jnp.where(lane_lo, s_lo, s_hi)


def _lane_lo(rows):
    return lax.broadcasted_iota(jnp.int32, (rows, LANES), 1) < HEAD_DIM


def _row_tile(n_ctx, dec_seq, pref):
    return math.gcd(math.gcd(n_ctx, dec_seq), pref)


def _mod_index(i, tm, n_ctx, dec_seq):
    ctx_tiles = n_ctx // tm
    return jnp.where(i < ctx_tiles, 0, 1 + (i - ctx_tiles) // (dec_seq // tm))


def _any_spec():
    return pl.BlockSpec(memory_space=pl.ANY)


def _mod_kernel(c_ref, w_ref, b_ref, o_ref):
    s = _silu(c_ref[...])
    w = w_ref[0]
    s_hi = s.astype(BF16)
    s_lo = (s - s_hi.astype(F32)).astype(BF16)
    w_hi = w.astype(BF16)
    w_lo = (w - w_hi.astype(F32)).astype(BF16)
    acc = _dot(s_hi, w_hi) + (_dot(s_lo, w_hi) + _dot(s_hi, w_lo))
    o_ref[0] = acc + b_ref[0]


def _mod_vectors(cvec, w_mod, b_mod):
    depth, d, n = w_mod.shape
    tn = n // 4
    return pl.pallas_call(
        _mod_kernel,
        out_shape=jax.ShapeDtypeStruct((depth, SUBLANES, n), F32),
        grid=(depth, n // tn),
        in_specs=[pl.BlockSpec((SUBLANES, d), lambda l, j: (0, 0)),
                  pl.BlockSpec((1, d, tn), lambda l, j: (l, 0, j)),
                  pl.BlockSpec((1, 1, tn), lambda l, j: (l, 0, j))],
        out_specs=pl.BlockSpec((1, SUBLANES, tn), lambda l, j: (l, 0, j)),
        compiler_params=_cparams(("parallel", "parallel"), 40),
        name="mod_vectors",
    )(cvec, w_mod, b_mod.reshape(depth, 1, n))


def _mod_norm(x, ln, shift, scale):
    ms = jnp.mean(x * x, axis=-1, keepdims=True)
    return (x * lax.rsqrt(ms + EPS) * ln) * (1.0 + scale) + shift


def _inproj_kernel(x_ref, mod_ref, ln_ref, w_ref, o_ref, h_scr):
    d = x_ref.shape[1]

    @pl.when(pl.program_id(1) == 0)
    def _():
        m = mod_ref[0]
        h = _mod_norm(x_ref[...], ln_ref[...], m[:, 0:d], m[:, d:2 * d])
        h_scr[...] = h.astype(BF16)

    o_ref[...] = _dot(h_scr[...], w_ref[...])


def _in_proj(x, mods, ln1, w_in, n_ctx, dec_seq):
    nt, d = x.shape
    n = w_in.shape[1]
    tm, tn = _row_tile(n_ctx, dec_seq, 1024), n // 6
    return pl.pallas_call(
        _inproj_kernel,
        out_shape=jax.ShapeDtypeStruct((nt, n), F32),
        grid=(nt // tm, n // tn),
        in_specs=[pl.BlockSpec((tm, d), lambda i, j: (i, 0)),
                  pl.BlockSpec((1, 1, 6 * d), lambda i, j: (_mod_index(i, tm, n_ctx, dec_seq), 0, 0)),
                  pl.BlockSpec((1, d), lambda i, j: (0, 0)),
                  pl.BlockSpec((d, tn), lambda i, j: (0, j))],
        out_specs=pl.BlockSpec((tm, tn), lambda i, j: (i, j)),
        scratch_shapes=[pltpu.VMEM((tm, d), BF16)],
        compiler_params=_cparams(("parallel", "arbitrary"), 40),
        name="in_proj",
    )(x, mods, ln1, w_in)


_TB = 256


def _gdn_prep_kernel(q_ref, k_ref, v_ref, qp_ref, kp_ref, vp_ref, qx_ref, kx_ref, vx_ref, ba_ref,
                     cw_ref, alog_ref, dtb_ref, qo_ref, ko_ref, vo_ref, bg_ref, *, n_ctx, seq, dec_seq):
    rows = q_ref.shape[0]
    r0 = pl.program_id(0) * rows
    in_ctx = r0 < n_ctx
    off = jnp.where(in_ctx, r0, r0 - n_ctx)
    length = jnp.where(in_ctx, seq, dec_seq)
    is_start = lax.rem(off, length) == 0
    is_end = lax.rem(off + rows, length) == 0
    lane_lo = _lane_lo(rows)
    cw = cw_ref[...]
    width = q_ref.shape[1]

    def conv(x_ref, prev_ref, next_ref, o_ref, a, norm_scale):
        prev = jnp.where(is_start, 0.0, prev_ref[...])
        nxt = jnp.where(is_end, 0.0, next_ref[...])
        ext = jnp.concatenate([prev, x_ref[...], nxt], axis=0)
        acc = None
        for j in range(CONV_K):
            o = SUBLANES - CONV_K // 2 + j
            term = ext[o:o + rows, :] * cw[j:j + 1, a * width:(a + 1) * width]
            acc = term if acc is None else acc + term
        y = _silu(acc)
        for g in range(width // LANES):
            yg = y[:, g * LANES:(g + 1) * LANES]
            if norm_scale is not None:
                yg = yg * lax.rsqrt(_half_sums(yg * yg, lane_lo) + EPS) * norm_scale
            o_ref[:, g * LANES:(g + 1) * LANES] = yg

    conv(q_ref, qp_ref, qx_ref, qo_ref, 0, HEAD_DIM ** -0.5)
    conv(k_ref, kp_ref, kx_ref, ko_ref, 1, 1.0)
    conv(v_ref, vp_ref, vx_ref, vo_ref, 2, None)

    x = ba_ref[...]
    g = -jnp.exp(alog_ref[...]) * _softplus(x + dtb_ref[...])
    row_in_chunk = lax.broadcasted_iota(jnp.int32, (rows, LANES), 0) % CHUNK
    pre = suf = g
    s = 1
    while s < CHUNK:
        pre = pre + jnp.where(row_in_chunk >= s, pltpu.roll(pre, s, axis=0), 0.0)
        suf = suf + jnp.where(row_in_chunk + s < CHUNK, pltpu.roll(suf, rows - s, axis=0), 0.0)
        s *= 2
    c = lax.broadcasted_iota(jnp.int32, (rows, LANES), 1) % 8
    bg_ref[...] = jnp.where(c < 4, _sigmoid(x), jnp.where(c < 6, pre, suf))


def _gdn_prep(proj, conv_w, alog_row, dtb_row, n_ctx, seq, dec_seq):
    nt = proj.shape[0]
    rows = _TB
    assert seq % rows == 0 and dec_seq % rows == 0
    w = PAIRS * LANES
    per = rows // SUBLANES
    last = nt // SUBLANES - 1
    main = [pl.BlockSpec((rows, w), lambda i, a=a: (i, a)) for a in range(3)]
    prev = [pl.BlockSpec((SUBLANES, w), lambda i, a=a: (jnp.maximum(i * per - 1, 0), a)) for a in range(3)]
    nxt = [pl.BlockSpec((SUBLANES, w), lambda i, a=a: (jnp.minimum((i + 1) * per, last), a)) for a in range(3)]
    row = pl.BlockSpec((1, LANES), lambda i: (0, 0))
    return pl.pallas_call(
        functools.partial(_gdn_prep_kernel, n_ctx=n_ctx, seq=seq, dec_seq=dec_seq),
        out_shape=[jax.ShapeDtypeStruct((nt, w), F32)] * 3 + [jax.ShapeDtypeStruct((nt, LANES), F32)],
        grid=(nt // rows,),
        in_specs=main + prev + nxt + [pl.BlockSpec((rows, LANES), lambda i: (i, G_BA)),
                                      pl.BlockSpec(conv_w.shape, lambda i: (0, 0)), row, row],
        out_specs=[pl.BlockSpec((rows, w), lambda i: (i, 0))] * 3 + [pl.BlockSpec((rows, LANES), lambda i: (i, 0))],
        compiler_params=_cparams(("parallel",), 32),
        name="gdn_prep",
    )(*([proj] * 10), conv_w, alog_row, dtb_row)


def _run_interleaved(gens):
    gens = list(gens)
    while gens:
        for g in list(gens):
            try:
                next(g)
            except StopIteration:
                gens.remove(g)


def _gdn_scan_kernel(*refs, nch, has_s0, has_sfin, aliased):
    fwd_refs, bwd_refs = refs[0:4], refs[4:8]
    pos = 8
    s0_ref = None
    if has_s0:
        s0_ref = refs[pos]
        pos += 1
    if aliased:
        pos += 2
    of_ref, ob_ref = refs[pos:pos + 2]
    pos += 2
    sfin_ref = None
    if has_sfin:
        sfin_ref = refs[pos]
        pos += 1
    s_scr = refs[pos]
    j = pl.program_id(1)
    n2 = 2 * CHUNK
    n_units = 2 * PAIRS
    units = [(p, d) for p in range(PAIRS) for d in range(2)]
    idx = range(n_units)

    @pl.when(j == 0)
    def _():
        for u, (p, d) in enumerate(units):
            s_scr[u] = s0_ref[0, p, d] if has_s0 else jnp.zeros((n2, n2), F32)

    ri = lax.broadcasted_iota(jnp.int32, (n2, n2), 0)
    ci = lax.broadcasted_iota(jnp.int32, (n2, n2), 1)
    same = (ri >= CHUNK) == (ci >= CHUNK)
    incl = (same & (ri >= ci), same & (ri <= ci))
    strict = (same & (ri > ci), same & (ri < ci))
    eye = (ri == ci).astype(F32)
    ril, cil = ri % CHUNK, ci % CHUNK
    level_mask = []
    m = 1
    while m < CHUNK:
        level_mask.append(same & (ril // (2 * m) == cil // (2 * m)) & (ril // m != cil // m))
        m *= 2

    def col_stack(arr, ca, cb):
        top = jnp.broadcast_to(arr[:, ca:ca + 1], (CHUNK, LANES))
        bot = jnp.broadcast_to(arr[:, cb:cb + 1], (CHUNK, LANES))
        return jnp.concatenate([top, bot], axis=0)

    def row_stack(arr, row):
        top = jnp.broadcast_to(arr[row:row + 1, :], (CHUNK, LANES))
        bot = jnp.broadcast_to(arr[CHUNK + row:CHUNK + row + 1, :], (CHUNK, LANES))
        return jnp.concatenate([top, bot], axis=0)

    def rows_of(c, d):
        r = c * CHUNK if d == 0 else (nch - 1 - c) * CHUNK
        return slice(r, r + CHUNK)

    feats = [dict() for _ in range(nch)]
    solved = [dict() for _ in range(nch)]
    states = [s_scr[u] for u in idx]

    def features(c):
        f = feats[c]
        bg = [(fwd_refs, bwd_refs)[d][3][rows_of(c, d), :] for d in range(2)]
        b_c = [col_stack(bg[d], 8 * p + 2 * d, 8 * p + 2 * d + 1) for p, d in units]
        g_c = [col_stack(bg[d], 8 * p + 4 + 2 * d, 8 * p + 5 + 2 * d) for p, d in units]
        yield
        ld = lambda a, p, d: (fwd_refs, bwd_refs)[d][a][rows_of(c, d), p * LANES:(p + 1) * LANES]
        k2 = [jnp.concatenate([ld(1, p, d)] * 2, axis=0) for p, d in units]
        ks = [jnp.where(same, k2[i], 0.0) for i in idx]
        qs = [jnp.where(same, jnp.concatenate([ld(0, p, d)] * 2, axis=0), 0.0) for p, d in units]
        ks_b = [ks[i].astype(BF16) for i in idx]
        prod = [_dot_nt(jnp.concatenate([ks_b[i], qs[i].astype(BF16)], axis=0), ks_b[i]) for i in idx]
        yield
        decay = [jnp.where(incl[d], jnp.exp(jnp.where(incl[d], g_c[i] - g_c[i].T, 0.0)), 0.0)
                 for i, (p, d) in enumerate(units)]
        yield
        eg = [jnp.exp(g_c[i]) for i in idx]
        g_last = [row_stack(g_c[i], CHUNK - 1 if d == 0 else 0) for i, (p, d) in enumerate(units)]
        f["l"] = [jnp.where(strict[d], b_c[i] * prod[i][:n2] * decay[i], 0.0) for i, (p, d) in enumerate(units)]
        yield
        v2 = [jnp.concatenate([ld(2, p, d)] * 2, axis=0) for p, d in units]
        f["rhs"] = [(b_c[i] * jnp.where(same, v2[i], eg[i] * pltpu.roll(k2[i], CHUNK, axis=1))).astype(BF16)
                    for i in idx]
        yield
        f["at"] = [(prod[i][n2:] * decay[i]).astype(BF16) for i in idx]
        f["qg"] = [(qs[i] * eg[i]).astype(BF16) for i in idx]
        yield
        f["kt"] = [(ks[i] * jnp.exp(g_last[i] - g_c[i])).T.astype(BF16) for i in idx]
        f["gl"] = [jnp.exp(g_last[i]) for i in idx]
        yield

    def inverse(c):
        f, o = feats[c], solved[c]
        l_mat = f["l"]
        t_inv = [eye - jnp.where(level_mask[0], l_mat[i], 0.0) for i in idx]
        for lm in level_mask[1:]:
            t_b = [t_inv[i].astype(BF16) for i in idx]
            y = [_dot(jnp.where(lm, l_mat[i], 0.0).astype(BF16), t_b[i]).astype(BF16) for i in idx]
            yield
            t_inv = [t_inv[i] - _dot(t_b[i], y[i]) for i in idx]
            yield
        xs = [_dot(t_inv[i].astype(BF16), f["rhs"][i]) for i in idx]
        yield
        o["u"] = [jnp.where(same, xs[i], 0.0) for i in idx]
        o["wq"] = [jnp.concatenate([jnp.where(same, pltpu.roll(xs[i], CHUNK, axis=1), 0.0).astype(BF16),
                                    f["qg"][i]], axis=0) for i in idx]
        yield

    def scan(c):
        f, o = feats[c], solved[c]
        ws = [_dot(o["wq"][i], states[i].astype(BF16)) for i in idx]
        yield
        u_b = [(o["u"][i] - ws[i][:n2]).astype(BF16) for i in idx]
        yield
        o_st = [ws[i][n2:] + _dot(f["at"][i], u_b[i]) for i in idx]
        for i in idx:
            states[i] = f["gl"][i] * states[i] + _dot(f["kt"][i], u_b[i])
        yield
        for i, (p, d) in enumerate(units):
            dst = of_ref if d == 0 else ob_ref
            dst[rows_of(c, d), p * LANES:(p + 1) * LANES] = o_st[i][:CHUNK] + o_st[i][CHUNK:]
        feats[c] = solved[c] = None
        yield

    for t in range(nch + 2):
        gens = []
        if 1 <= t <= nch:
            gens.append(inverse(t - 1))
        if t < nch:
            gens.append(features(t))
        if 2 <= t:
            gens.append(scan(t - 2))
        _run_interleaved(gens)

    for u in idx:
        s_scr[u] = states[u]
    if has_sfin:
        @pl.when(j == pl.num_programs(1) - 1)
        def _():
            for u, (p, d) in enumerate(units):
                sfin_ref[0, p, d] = states[u]


def _gdn_scan(feat, s0, prev_out, *, t, row0, n_seq, nt, want_state):
    nb = t // _TB
    blk0 = row0 // _TB
    n2 = 2 * CHUNK
    w = PAIRS * LANES
    has_s0 = s0 is not None
    aliased = prev_out is not None

    def blocks(mirror):
        row = (lambda b, j: blk0 + b * nb + (nb - 1 - j)) if mirror else (lambda b, j: blk0 + b * nb + j)
        return ([pl.BlockSpec((_TB, w), lambda b, j: (row(b, j), 0))] * 3
                + [pl.BlockSpec((_TB, LANES), lambda b, j: (row(b, j), 0))]), row

    f_specs, f_row = blocks(False)
    b_specs, b_row = blocks(True)
    in_specs = f_specs + b_specs
    args = list(feat) * 2
    if has_s0:
        in_specs.append(pl.BlockSpec((1, PAIRS, 2, n2, n2), lambda b, j: (b, 0, 0, 0, 0)))
        args.append(s0)
    aliases = {}
    if aliased:
        aliases = {len(args): 0, len(args) + 1: 1}
        in_specs += [_any_spec(), _any_spec()]
        args += list(prev_out)
    out_shape = [jax.ShapeDtypeStruct((nt, w), F32)] * 2
    out_specs = [pl.BlockSpec((_TB, w), lambda b, j: (f_row(b, j), 0)),
                 pl.BlockSpec((_TB, w), lambda b, j: (b_row(b, j), 0))]
    if want_state:
        out_shape.append(jax.ShapeDtypeStruct((n_seq, PAIRS, 2, n2, n2), F32))
        out_specs.append(pl.BlockSpec((1, PAIRS, 2, n2, n2), lambda b, j: (b, 0, 0, 0, 0)))
    return pl.pallas_call(
        functools.partial(_gdn_scan_kernel, nch=_TB // CHUNK, has_s0=has_s0, has_sfin=want_state, aliased=aliased),
        out_shape=out_shape,
        grid=(n_seq, nb),
        in_specs=in_specs,
        out_specs=out_specs,
        scratch_shapes=[pltpu.VMEM((2 * PAIRS, n2, n2), F32)],
        input_output_aliases=aliases,
        compiler_params=_cparams(("parallel", "arbitrary"), 48),
        name="gdn_scan_ctx" if want_state else "gdn_scan_lat",
    )(*args)


def _prep_kernel(*refs, rope):
    bq_ref, cq_ref, bk_ref, bv_ref, ck_ref, cv_ref, qkn_ref = refs[:7]
    pos = 7
    cs = sn = None
    if rope:
        cs, sn = refs[7][...], refs[8][...]
        pos = 11
    qb_ref, qc_ref, kdb_ref, vdb_ref, kdc_ref, vdc_ref = refs[pos:pos + 6]
    pos += 6
    tr = bq_ref.shape[0]
    lane = lax.broadcasted_iota(jnp.int32, (tr, LANES), 1)
    lane_lo = lane < HEAD_DIM
    first_half = (lane % HEAD_DIM) < HEAD_DIM // 2
    qkn = qkn_ref[...]

    def norm(x, w_row):
        ms = _half_sums(x * x, lane_lo) * (1.0 / HEAD_DIM)
        return x * lax.rsqrt(ms + EPS) * w_row

    def rot(y):
        if not rope:
            return y
        partner = jnp.where(first_half, pltpu.roll(y, LANES - HEAD_DIM // 2, axis=1),
                            pltpu.roll(y, HEAD_DIM // 2, axis=1))
        return y * cs + partner * sn

    def dup(x, o_ref):
        lo = jnp.where(lane_lo, x, 0.0)
        hi = jnp.where(lane_lo, 0.0, x)
        d0 = (lo + pltpu.roll(lo, HEAD_DIM, axis=1)).astype(BF16)
        d1 = (hi + pltpu.roll(hi, HEAD_DIM, axis=1)).astype(BF16)
        if len(o_ref.shape) == 3:
            o_ref[0, :, 0:LANES] = d0
            o_ref[0, :, LANES:2 * LANES] = d1
        else:
            o_ref[:, 0:LANES] = d0
            o_ref[:, LANES:2 * LANES] = d1

    for q_ref, o_ref, wi in ((bq_ref, qb_ref, 0), (cq_ref, qc_ref, 2)):
        for g in range(PAIRS):
            y = rot(norm(q_ref[:, g * LANES:(g + 1) * LANES], qkn[wi:wi + 1, :])) * HEAD_DIM ** -0.5
            o_ref[2 * g] = jnp.where(lane_lo, y, 0.0).astype(BF16)
            o_ref[2 * g + 1] = jnp.where(lane_lo, 0.0, y).astype(BF16)
    kb_n = norm(bk_ref[...], qkn[1:2, :])
    kc_n = norm(ck_ref[...], qkn[3:4, :])
    if not rope:
        kbn_ref, vbn_ref, kcn_ref, vcn_ref = refs[pos:pos + 4]
        kbn_ref[...] = kb_n
        vbn_ref[...] = bv_ref[...]
        kcn_ref[...] = kc_n
        vcn_ref[...] = cv_ref[...]
    dup(rot(kb_n), kdb_ref)
    dup(bv_ref[...], vdb_ref)
    dup(rot(kc_n), kdc_ref)
    dup(cv_ref[...], vdc_ref)


def _attn_prep(proj, qkn_rows, rope_tables, glob_kv, *, row0, n_rows, t):
    rope = rope_tables is not None
    tr = math.gcd(t, 512)
    blk0 = row0 // tr

    def col(g, width):
        gw = width // LANES
        return pl.BlockSpec((tr, width), lambda i, g=g, gw=gw: (blk0 + i, g // gw))

    in_specs = [col(G_BQ, 4 * LANES), col(G_CQ, 4 * LANES), col(G_BK, LANES), col(G_BV, LANES),
                col(G_CK, LANES), col(G_CV, LANES), pl.BlockSpec((4, LANES), lambda i: (0, 0))]
    args = [proj] * 6 + [qkn_rows]
    q_shape = jax.ShapeDtypeStruct((HEADS, n_rows, LANES), BF16)
    kv_shape = jax.ShapeDtypeStruct((n_rows, 2 * LANES), BF16)
    q_spec = pl.BlockSpec((HEADS, tr, LANES), lambda i: (0, i, 0))
    kv_spec = pl.BlockSpec((tr, 2 * LANES), lambda i: (i, 0))
    aliases = {}
    if rope:
        per_seq = t // tr
        in_specs += [pl.BlockSpec((tr, LANES), lambda i: (i % per_seq, 0))] * 2 + [_any_spec()] * 2
        aliases = {len(args) + 2: 2, len(args) + 3: 3}
        args += list(rope_tables) + list(glob_kv)
        past = glob_kv[0].shape[1] - t
        assert past % tr == 0
        glob_spec = pl.BlockSpec((1, tr, 2 * LANES), lambda i: (i // per_seq, past // tr + i % per_seq, 0))
        glob_shape = jax.ShapeDtypeStruct(glob_kv[0].shape, BF16)
        out_shape = [q_shape, q_shape, glob_shape, glob_shape, kv_shape, kv_shape]
        out_specs = [q_spec, q_spec, glob_spec, glob_spec, kv_spec, kv_spec]
    else:
        out_shape = [q_shape, q_shape] + [kv_shape] * 4 + [jax.ShapeDtypeStruct((n_rows, LANES), F32)] * 4
        out_specs = [q_spec, q_spec] + [kv_spec] * 4 + [pl.BlockSpec((tr, LANES), lambda i: (i, 0))] * 4
    return pl.pallas_call(
        functools.partial(_prep_kernel, rope=rope),
        out_shape=out_shape,
        grid=(n_rows // tr,),
        in_specs=in_specs,
        out_specs=out_specs,
        input_output_aliases=aliases,
        compiler_params=_cparams(("parallel",), 32),
        name="attn_prep_lat" if rope else "attn_prep_ctx",
    )(*args)


def _stacked_q(q_ref, g, tq):
    return q_ref[GROUP_HEADS * g:GROUP_HEADS * (g + 1)].reshape(GROUP_HEADS * tq, LANES)


def _unstack_group(o_st, tq, lane_lo):
    a = jnp.where(lane_lo, o_st[0:tq], o_st[tq:2 * tq])
    b = jnp.where(lane_lo, o_st[2 * tq:3 * tq], o_st[3 * tq:4 * tq])
    return jnp.concatenate([a, b], axis=1)


def _sink_col(sink_row, g, tq):
    parts = [jnp.broadcast_to(sink_row[:, GROUP_HEADS * g + j:GROUP_HEADS * g + j + 1], (tq, 1))
             for j in range(GROUP_HEADS)]
    return jnp.concatenate(parts, axis=0)


def _attn_ctx_kernel(qb_ref, kdb_ref, vdb_ref, qc_ref, kdc_ref, vdc_ref, sink_ref, yb_ref, yc_ref):
    tq = qb_ref.shape[1]
    lane_lo_q = _lane_lo(tq)
    sink_row = sink_ref[...]
    groups = range(KV_HEADS)
    gs = [slice(g * LANES, (g + 1) * LANES) for g in groups]
    jobs = [(qb_ref, kdb_ref, vdb_ref, yb_ref, False, g) for g in groups]
    jobs += [(qc_ref, kdc_ref, vdc_ref, yc_ref, True, g) for g in groups]
    s = [_dot_nt(_stacked_q(q_ref, g, tq), kd_ref[:, gs[g]]) for q_ref, kd_ref, _, _, _, g in jobs]
    for si, (_, _, vd_ref, y_ref, use_sink, g) in zip(s, jobs):
        m = jnp.max(si, axis=-1, keepdims=True)
        if use_sink:
            snk = _sink_col(sink_row, g, tq)
            m = jnp.maximum(m, snk)
        p = jnp.exp(si - m)
        l = jnp.sum(p, axis=-1, keepdims=True)
        if use_sink:
            l = l + jnp.exp(snk - m)
        o = _dot(p.astype(BF16), vd_ref[:, gs[g]]) / l
        y_ref[:, 2 * g * LANES:(2 * g + 2) * LANES] = _unstack_group(o, tq, lane_lo_q).astype(y_ref.dtype)


def _attn_ctx(qb, kdb, vdb, qc, kdc, vdc, sink_row, *, t, nt):
    n = kdb.shape[0]
    q_spec = pl.BlockSpec((HEADS, t, LANES), lambda b: (0, b, 0))
    kv_spec = pl.BlockSpec((t, 2 * LANES), lambda b: (b, 0))
    y_spec = pl.BlockSpec((t, 4 * LANES), lambda b: (b, 0))
    return pl.pallas_call(
        _attn_ctx_kernel,
        out_shape=[jax.ShapeDtypeStruct((nt, 4 * LANES), BF16)] * 2,
        grid=(n // t,),
        in_specs=[q_spec, kv_spec, kv_spec, q_spec, kv_spec, kv_spec,
                  pl.BlockSpec((1, LANES), lambda b: (0, 0))],
        out_specs=[y_spec, y_spec],
        compiler_params=_cparams(("parallel",), 32),
        name="attn_ctx",
    )(qb, kdb, vdb, qc, kdc, vdc, sink_row)


_TQ_B = 256
_TK_B = 1536


def _attn_glob_kernel(q_ref, kd_ref, vd_ref, y_in_ref, y_ref, m_scr, l_scr, acc_scr):
    del y_in_ref
    kj = pl.program_id(2)
    tq = q_ref.shape[1]

    @pl.when(kj == 0)
    def _():
        m_scr[...] = jnp.full(m_scr.shape, NEG, F32)
        l_scr[...] = jnp.zeros(l_scr.shape, F32)
        acc_scr[...] = jnp.zeros(acc_scr.shape, F32)

    groups = range(KV_HEADS)
    gs = [slice(g * LANES, (g + 1) * LANES) for g in groups]
    s = [_dot_nt(_stacked_q(q_ref, g, tq), kd_ref[0, :, gs[g]]) for g in groups]
    for g in groups:
        m_old = m_scr[g]
        m_new = jnp.maximum(m_old, jnp.max(s[g], axis=-1, keepdims=True))
        p = jnp.exp(s[g] - m_new[:, 0:1])
        alpha = jnp.exp(m_old - m_new)
        l_scr[g] = alpha * l_scr[g] + jnp.sum(p, axis=-1, keepdims=True)
        acc_scr[g] = alpha * acc_scr[g] + _dot(p.astype(BF16), vd_ref[0, :, gs[g]])
        m_scr[g] = m_new

    @pl.when(kj == pl.num_programs(2) - 1)
    def _():
        lane_lo_q = _lane_lo(tq)
        for g in groups:
            o = acc_scr[g] / l_scr[g]
            y_ref[:, 2 * g * LANES:(2 * g + 2) * LANES] = _unstack_group(o, tq, lane_lo_q).astype(y_ref.dtype)


def _attn_glob(q, kd, vd, y_prev, *, t, row0):
    n = q.shape[1]
    s_len = kd.shape[1]
    tq, tk = math.gcd(t, _TQ_B), math.gcd(s_len, _TK_B)
    per_seq = t // tq
    blk0 = row0 // tq
    return pl.pallas_call(
        _attn_glob_kernel,
        out_shape=jax.ShapeDtypeStruct(y_prev.shape, BF16),
        grid=(n // t, per_seq, s_len // tk),
        in_specs=[pl.BlockSpec((HEADS, tq, LANES), lambda b, i, j: (0, b * per_seq + i, 0)),
                  pl.BlockSpec((1, tk, 2 * LANES), lambda b, i, j: (b, j, 0)),
                  pl.BlockSpec((1, tk, 2 * LANES), lambda b, i, j: (b, j, 0)),
                  _any_spec()],
        out_specs=pl.BlockSpec((tq, 4 * LANES), lambda b, i, j: (blk0 + b * per_seq + i, 0)),
        scratch_shapes=[pltpu.VMEM((KV_HEADS, GROUP_HEADS * tq, LANES), F32)] * 3,
        input_output_aliases={3: 0},
        compiler_params=_cparams(("parallel", "parallel", "arbitrary"), 56),
        name="attn_glob",
    )(q, kd, vd, y_prev)


def _attn_win_kernel(q_ref, kx_ref, vx_ref, kp_ref, kc_ref, kn_ref, vp_ref, vc_ref, vn_ref,
                     sink_ref, y_in_ref, y_ref):
    del y_in_ref
    i = pl.program_id(1)
    nblk = pl.num_programs(1)
    tq = q_ref.shape[1]
    lane_lo_q = _lane_lo(tq)
    sink_row = sink_ref[...]
    rr = lax.broadcasted_iota(jnp.int32, (GROUP_HEADS * tq, tq), 0) % tq
    cc = lax.broadcasted_iota(jnp.int32, (GROUP_HEADS * tq, tq), 1)
    mask_prev = (cc >= rr) & (i > 0)
    mask_next = (cc <= rr) & (i < nblk - 1)
    rowmax = lambda a: jnp.max(a, axis=-1, keepdims=True)
    rowsum = lambda a: jnp.sum(a, axis=-1, keepdims=True)
    groups = range(KV_HEADS)
    gs = [slice(g * LANES, (g + 1) * LANES) for g in groups]
    qs = [_stacked_q(q_ref, g, tq) for g in groups]
    s_x = [_dot_nt(qs[g], kx_ref[0, :, gs[g]]) for g in groups]
    s_c = [_dot_nt(qs[g], kc_ref[:, gs[g]]) for g in groups]
    s_p = [jnp.where(mask_prev, _dot_nt(qs[g], kp_ref[:, gs[g]]), NEG) for g in groups]
    s_n = [jnp.where(mask_next, _dot_nt(qs[g], kn_ref[:, gs[g]]), NEG) for g in groups]
    for g in groups:
        snk = _sink_col(sink_row, g, tq)
        m = jnp.maximum(jnp.maximum(rowmax(s_x[g]), rowmax(s_c[g])), jnp.maximum(rowmax(s_p[g]), rowmax(s_n[g])))
        m = jnp.maximum(m, snk)
        p_x = jnp.exp(s_x[g] - m)
        p_p = jnp.exp(s_p[g] - m)
        p_c = jnp.exp(s_c[g] - m)
        p_n = jnp.exp(s_n[g] - m)
        l = rowsum(p_x) + rowsum(p_c) + rowsum(p_p) + rowsum(p_n) + jnp.exp(snk - m)
        o = (_dot(p_x.astype(BF16), vx_ref[0, :, gs[g]]) + _dot(p_c.astype(BF16), vc_ref[:, gs[g]])
             + _dot(p_p.astype(BF16), vp_ref[:, gs[g]]) + _dot(p_n.astype(BF16), vn_ref[:, gs[g]])) / l
        y_ref[:, 2 * g * LANES:(2 * g + 2) * LANES] = _unstack_group(o, tq, lane_lo_q).astype(y_ref.dtype)


def _attn_win(q, kd, vd, kx, vx, sink_row, y_prev, *, t, row0):
    n = q.shape[1]
    p_len = kx.shape[1]
    tq = WINDOW
    nblk = t // tq
    blk0 = row0 // tq
    x_spec = pl.BlockSpec((1, p_len, 2 * LANES), lambda b, i: (b, 0, 0))
    prev = pl.BlockSpec((tq, 2 * LANES), lambda b, i: (b * nblk + jnp.maximum(i - 1, 0), 0))
    cur = pl.BlockSpec((tq, 2 * LANES), lambda b, i: (b * nblk + i, 0))
    nxt = pl.BlockSpec((tq, 2 * LANES), lambda b, i: (b * nblk + jnp.minimum(i + 1, nblk - 1), 0))
    return pl.pallas_call(
        _attn_win_kernel,
        out_shape=jax.ShapeDtypeStruct(y_prev.shape, BF16),
        grid=(n // t, nblk),
        in_specs=[pl.BlockSpec((HEADS, tq, LANES), lambda b, i: (0, b * nblk + i, 0)),
                  x_spec, x_spec, prev, cur, nxt, prev, cur, nxt,
                  pl.BlockSpec((1, LANES), lambda b, i: (0, 0)), _any_spec()],
        out_specs=pl.BlockSpec((tq, 4 * LANES), lambda b, i: (blk0 + b * nblk + i, 0)),
        input_output_aliases={10: 0},
        compiler_params=_cparams(("parallel", "parallel"), 32),
        name="attn_win",
    )(q, kx, vx, kd, kd, kd, vd, vd, vd, sink_row, y_prev)


def _mix_kernel(x_ref, of_ref, ob_ref, z_ref, yb_ref, yc_ref, ga_ref, gb_ref, gc_ref, mod_ref, anorm_ref,
                wa_ref, wb_ref, wc_ref, wo_ref, o_ref):
    d = x_ref.shape[1]
    tm = x_ref.shape[0]
    lane_lo = _lane_lo(tm)
    anorm = anorm_ref[...]
    ya = []
    for g in range(PAIRS):
        gl = slice(g * LANES, (g + 1) * LANES)
        o = of_ref[:, gl] + ob_ref[:, gl]
        ms = _half_sums(o * o, lane_lo) * (1.0 / HEAD_DIM)
        ya.append((o * lax.rsqrt(ms + EPS) * anorm * _silu(z_ref[:, gl])).astype(BF16))
    ya = jnp.concatenate(ya, axis=1)
    merged = (_sigmoid(ga_ref[...]) * _dot(ya, wa_ref[...])
              + _sigmoid(gb_ref[...]) * _dot(yb_ref[...], wb_ref[...])
              + _sigmoid(gc_ref[...]) * _dot(yc_ref[...], wc_ref[...]))
    gate1 = mod_ref[0][:, 2 * d:3 * d]
    o_ref[...] = x_ref[...] + gate1 * _dot(merged.astype(BF16), wo_ref[...])


def _mix(x, o_f, o_b, yb, yc, proj, mods, anorm_row, wa, wb, wc, wo, n_ctx, dec_seq):
    nt, d = x.shape
    tm = _row_tile(n_ctx, dec_seq, 512)
    w = PAIRS * LANES
    y_spec = pl.BlockSpec((tm, w), lambda i: (i, 0))
    gw = d // LANES

    def gate(k):
        return pl.BlockSpec((tm, d), lambda i, k=k: (i, G_GATES // gw + k))

    def full(a):
        return pl.BlockSpec(a.shape, lambda i: (0, 0))

    return pl.pallas_call(
        _mix_kernel,
        out_shape=jax.ShapeDtypeStruct((nt, d), F32),
        grid=(nt // tm,),
        in_specs=[pl.BlockSpec((tm, d), lambda i: (i, 0)), y_spec, y_spec,
                  pl.BlockSpec((tm, w), lambda i: (i, G_Z // PAIRS)), y_spec, y_spec,
                  gate(0), gate(1), gate(2),
                  pl.BlockSpec((1, 1, 6 * d), lambda i: (_mod_index(i, tm, n_ctx, dec_seq), 0, 0)),
                  full(anorm_row), full(wa), full(wb), full(wc), full(wo)],
        out_specs=pl.BlockSpec((tm, d), lambda i: (i, 0)),
        compiler_params=_cparams(("parallel",), 48),
        name="mix_out",
    )(x, o_f, o_b, proj, yb, yc, proj, proj, proj, mods, anorm_row, wa, wb, wc, wo)


_FF_CHUNK = 1024


def _ffn_kernel(x_ref, mod_ref, ln_ref, w1_ref, w2_ref, o_ref):
    d = x_ref.shape[1]
    m = mod_ref[0]
    x = x_ref[...]
    h = _mod_norm(x, ln_ref[...], m[:, 3 * d:4 * d], m[:, 4 * d:5 * d]).astype(BF16)
    acc = None
    for f in range(0, w1_ref.shape[1], _FF_CHUNK):
        hid = jnp.maximum(_dot(h, w1_ref[:, f:f + _FF_CHUNK]), 0.0)
        part = _dot((hid * hid).astype(BF16), w2_ref[f:f + _FF_CHUNK, :])
        acc = part if acc is None else acc + part
    o_ref[...] = x + m[:, 5 * d:6 * d] * acc


def _ffn(x, mods, ln2, w1, w2, n_ctx, dec_seq):
    nt, d = x.shape
    tm = _row_tile(n_ctx, dec_seq, 512)
    return pl.pallas_call(
        _ffn_kernel,
        out_shape=jax.ShapeDtypeStruct((nt, d), F32),
        grid=(nt // tm,),
        in_specs=[pl.BlockSpec((tm, d), lambda i: (i, 0)),
                  pl.BlockSpec((1, 1, 6 * d), lambda i: (_mod_index(i, tm, n_ctx, dec_seq), 0, 0)),
                  pl.BlockSpec((1, d), lambda i: (0, 0)),
                  pl.BlockSpec(w1.shape, lambda i: (0, 0)),
                  pl.BlockSpec(w2.shape, lambda i: (0, 0))],
        out_specs=pl.BlockSpec((tm, d), lambda i: (i, 0)),
        compiler_params=_cparams(("parallel",), 56),
        name="ffn",
    )(x, mods, ln2, w1, w2)


def _ba_lane_order():
    cols = []
    for p in range(PAIRS):
        for base in (0, 2 * HEADS):
            for d in range(2):
                cols += [base + d * HEADS + 2 * p, base + d * HEADS + 2 * p + 1]
    return cols


def _reorder_w_in(w_in):
    a_w = HEADS * HEAD_DIM
    kvw = KV_HEADS * HEAD_DIM
    o_beta = 4 * a_w
    o_bq = o_beta + 4 * HEADS
    o_bk = o_bq + a_w
    o_bv = o_bk + kvw
    o_cq = o_bv + kvw
    o_ck = o_cq + a_w
    o_cv = o_ck + kvw
    o_g = o_cv + kvw
    seg = lambda a, n: w_in[:, :, a:a + n]
    ba = jnp.take(seg(o_beta, 4 * HEADS), jnp.array(_ba_lane_order(), jnp.int32), axis=2)
    ba = jnp.pad(ba, ((0, 0), (0, 0), (0, 2 * LANES - ba.shape[2])))
    out = jnp.concatenate([seg(0, o_beta), seg(o_bq, a_w), seg(o_cq, a_w), seg(o_g, w_in.shape[2] - o_g),
                           seg(o_bk, kvw), seg(o_bv, kvw), seg(o_ck, kvw), seg(o_cv, kvw), ba], axis=2)
    assert out.shape[2] == IN_COLS_PAD
    return out.astype(BF16)


def _decay_rows(vals):
    rows = []
    for p in range(PAIRS):
        r = jnp.stack([vals[:, 0, 2 * p], vals[:, 0, 2 * p + 1], vals[:, 1, 2 * p], vals[:, 1, 2 * p + 1]], axis=-1)
        rows.append(jnp.pad(r, ((0, 0), (4, 0))))
    out = jnp.concatenate(rows, axis=-1)
    return jnp.pad(out, ((0, 0), (0, LANES - out.shape[1])))[:, None, :]


def _rope_tables(n_tokens):
    rows = n_tokens // GRID_W
    row_id = jnp.repeat(jnp.arange(rows, dtype=F32), GRID_W)
    col_id = jnp.tile(jnp.arange(GRID_W, dtype=F32), rows)
    n_freq = HEAD_DIM // 4
    inv_freq = ROPE_THETA ** (-jnp.arange(n_freq, dtype=F32) / n_freq)
    ang = jnp.concatenate([row_id[:, None] * inv_freq, col_id[:, None] * inv_freq], axis=-1)
    cos, sin = jnp.cos(ang), jnp.sin(ang)
    cs = jnp.tile(cos, (1, 4))
    sn = jnp.tile(jnp.concatenate([-sin, sin], axis=-1), (1, 2))
    return cs, sn


def _dup_kv(x):
    return jnp.concatenate([x[:, :, 0], x[:, :, 0], x[:, :, 1], x[:, :, 1]], axis=-1).astype(BF16)


def _block_diag_states(s):
    b = s.shape[0]
    s = s.reshape(b, 2, PAIRS, 2, HEAD_DIM, HEAD_DIM).transpose(0, 2, 1, 3, 4, 5)
    z = jnp.zeros_like(s[:, :, :, 0])
    top = jnp.concatenate([s[:, :, :, 0], z], axis=-1)
    bot = jnp.concatenate([z, s[:, :, :, 1]], axis=-1)
    return jnp.concatenate([top, bot], axis=-2)


def _diag_states(s):
    a = s[:, :, :, :HEAD_DIM, :HEAD_DIM]
    bb = s[:, :, :, HEAD_DIM:, HEAD_DIM:]
    out = jnp.stack([a, bb], axis=3)
    b = s.shape[0]
    return out.transpose(0, 2, 1, 3, 4, 5).reshape(b, 2, HEADS, HEAD_DIM, HEAD_DIM)


def kernel(x_prompt, x_sample, cache_k_glob, cache_v_glob, cache_k_win, cache_v_win, state_delta, c, c_ctx, w_mod, b_mod, ln1, ln2, w_in, conv_qkv, a_log, dt_bias, a_norm, qk_norm, sink, w_br_a, w_br_b, w_br_c, w_o, w_ff1, w_ff2):
    batch, seq, d = x_prompt.shape
    dec_batch, dec_seq, _ = x_sample.shape
    depth = w_mod.shape[0]
    n_ctx = batch * seq
    n_lat = dec_batch * dec_seq
    nt = n_ctx + n_lat
    assert 1 + dec_batch <= SUBLANES

    w_in_r = _reorder_w_in(w_in)
    wa, wb, wc, wo = (w.astype(BF16) for w in (w_br_a, w_br_b, w_br_c, w_o))
    w1, w2 = w_ff1.astype(BF16), w_ff2.astype(BF16)
    alog_rows = _decay_rows(a_log)
    dtb_rows = _decay_rows(dt_bias)
    anorm_rows = jnp.tile(a_norm, (1, 2))[:, None, :]
    qkn_rows = jnp.tile(qk_norm, (1, 1, 2))
    sink_rows = jnp.pad(sink, ((0, 0), (0, LANES - HEADS)))[:, None, :]
    rope_tables = _rope_tables(dec_seq)
    cvec = jnp.concatenate([c_ctx[None], c, jnp.zeros((SUBLANES - 1 - dec_batch, d), F32)], axis=0)
    mods_all = _mod_vectors(cvec, w_mod, b_mod)
    pad_lat = ((0, 0), (0, dec_seq), (0, 0))

    x = jnp.concatenate([x_prompt.reshape(n_ctx, d), x_sample.reshape(n_lat, d)], axis=0)
    new_kg, new_vg, new_kw, new_vw, new_st = [], [], [], [], []
    for l in range(depth):
        mods = mods_all[l][:, None, :]
        proj = _in_proj(x, mods, ln1[l][None], w_in_r[l], n_ctx, dec_seq)

        feat = _gdn_prep(proj, conv_qkv[l], alog_rows[l], dtb_rows[l], n_ctx, seq, dec_seq)
        o_f, o_b, s_fin = _gdn_scan(feat, None, None, t=seq, row0=0, n_seq=batch, nt=nt, want_state=True)
        o_f, o_b = _gdn_scan(feat, _block_diag_states(state_delta[:, l]), (o_f, o_b),
                             t=dec_seq, row0=n_ctx, n_seq=dec_batch, nt=nt, want_state=False)

        qb_c, qc_c, kdb_c, vdb_c, kdc_c, vdc_c, kb_n, vb_n, kc_n, vc_n = _attn_prep(
            proj, qkn_rows[l], None, None, row0=0, n_rows=n_ctx, t=seq)
        glob_kv = (jnp.pad(_dup_kv(cache_k_glob[:, l]), pad_lat), jnp.pad(_dup_kv(cache_v_glob[:, l]), pad_lat))
        qb_l, qc_l, kd_all, vd_all, kdc_l, vdc_l = _attn_prep(
            proj, qkn_rows[l], rope_tables, glob_kv, row0=n_ctx, n_rows=n_lat, t=dec_seq)

        yb, yc = _attn_ctx(qb_c, kdb_c, vdb_c, qc_c, kdc_c, vdc_c, sink_rows[l], t=seq, nt=nt)
        yb = _attn_glob(qb_l, kd_all, vd_all, yb, t=dec_seq, row0=n_ctx)
        yc = _attn_win(qc_l, kdc_l, vdc_l, _dup_kv(cache_k_win[:, l]), _dup_kv(cache_v_win[:, l]),
                       sink_rows[l], yc, t=dec_seq, row0=n_ctx)

        x = _mix(x, o_f, o_b, yb, yc, proj, mods, anorm_rows[l], wa[l], wb[l], wc[l], wo[l], n_ctx, dec_seq)
        x = _ffn(x, mods, ln2[l][None], w1[l], w2[l], n_ctx, dec_seq)

        kv_shape = (batch, seq, KV_HEADS, HEAD_DIM)
        new_kg.append(kb_n.reshape(kv_shape))
        new_vg.append(vb_n.reshape(kv_shape))
        new_kw.append(kc_n.reshape(kv_shape))
        new_vw.append(vc_n.reshape(kv_shape))
        new_st.append(_diag_states(s_fin))

    y_prompt = x[:n_ctx].reshape(batch, seq, d)
    y_sample = x[n_ctx:].reshape(dec_batch, dec_seq, d)
    return (y_prompt, y_sample, jnp.stack(new_kg, axis=1), jnp.stack(new_vg, axis=1),
            jnp.stack(new_kw, axis=1), jnp.stack(new_vw, axis=1), jnp.stack(new_st, axis=1))
```

```python
import functools
import math

import jax
import jax.numpy as jnp
from jax import lax
from jax.experimental import pallas as pl
from jax.experimental.pallas import tpu as pltpu

F32 = jnp.float32
BF16 = jnp.bfloat16

LANES = 128
SUBLANES = 8

HEAD_DIM = 64
HEADS = 8
KV_HEADS = 2
GROUP_HEADS = HEADS // KV_HEADS
CHUNK = 64
CONV_K = 5
GRID_W = 64
WINDOW = 128
ROPE_THETA = 10000.0
EPS = 1e-6
NEG = -1e30
PAIRS = HEADS // 2

G_Q, G_K, G_V, G_Z = 0, 4, 8, 12
G_BQ, G_CQ = 16, 20
G_GATES = 24
G_BK, G_BV, G_CK, G_CV = 48, 49, 50, 51
G_BA = 52
N_GROUPS = 54
IN_COLS_PAD = N_GROUPS * LANES


def _cparams(sem, vmem_mb):
    return pltpu.CompilerParams(dimension_semantics=sem, vmem_limit_bytes=vmem_mb << 20)


def _dot(a, b):
    return jnp.dot(a, b, preferred_element_type=F32)


def _dot_nt(a, b):
    return lax.dot_general(a, b, (((1,), (1,)), ((), ())), preferred_element_type=F32)


def _sigmoid(x):
    return 1.0 / (1.0 + jnp.exp(-x))


def _silu(x):
    return x * _sigmoid(x)


def _softplus(x):
    return jnp.maximum(x, 0.0) + jnp.log1p(jnp.exp(-jnp.abs(x)))


def _half_sums(x, lane_lo):
    s_lo = jnp.sum(jnp.where(lane_lo, x, 0.0), axis=-1, keepdims=True)
    s_hi = jnp.sum(jnp.where(lane_lo, 0.0, x), axis=-1, keepdims=True)
    return jnp.where(lane_lo, s_lo, s_hi)


def _lane_lo(rows):
    return lax.broadcasted_iota(jnp.int32, (rows, LANES), 1) < HEAD_DIM


def _row_tile(n_ctx, dec_seq, pref):
    return math.gcd(math.gcd(n_ctx, dec_seq), pref)


def _mod_index(i, tm, n_ctx, dec_seq):
    ctx_tiles = n_ctx // tm
    return jnp.where(i < ctx_tiles, 0, 1 + (i - ctx_tiles) // (dec_seq // tm))


def _any_spec():
    return pl.BlockSpec(memory_space=pl.ANY)


def _mod_kernel(c_ref, w_ref, b_ref, o_ref):
    s = _silu(c_ref[...])
    w = w_ref[0]
    s_hi = s.astype(BF16)
    s_lo = (s - s_hi.astype(F32)).astype(BF16)
    w_hi = w.astype(BF16)
    w_lo = (w - w_hi.astype(F32)).astype(BF16)
    acc = _dot(s_hi, w_hi) + (_dot(s_lo, w_hi) + _dot(s_hi, w_lo))
    o_ref[0] = acc + b_ref[0]


def _mod_vectors(cvec, w_mod, b_mod):
    depth, d, n = w_mod.shape
    tn = n // 4
    return pl.pallas_call(
        _mod_kernel,
        out_shape=jax.ShapeDtypeStruct((depth, SUBLANES, n), F32),
        grid=(depth, n // tn),
        in_specs=[pl.BlockSpec((SUBLANES, d), lambda l, j: (0, 0)),
                  pl.BlockSpec((1, d, tn), lambda l, j: (l, 0, j)),
                  pl.BlockSpec((1, 1, tn), lambda l, j: (l, 0, j))],
        out_specs=pl.BlockSpec((1, SUBLANES, tn), lambda l, j: (l, 0, j)),
        compiler_params=_cparams(("parallel", "parallel"), 40),
        name="mod_vectors",
    )(cvec, w_mod, b_mod.reshape(depth, 1, n))


def _mod_norm(x, ln, shift, scale):
    ms = jnp.mean(x * x, axis=-1, keepdims=True)
    return (x * lax.rsqrt(ms + EPS) * ln) * (1.0 + scale) + shift


def _inproj_kernel(x_ref, mod_ref, ln_ref, w_ref, o_ref, h_scr):
    d = x_ref.shape[1]

    @pl.when(pl.program_id(1) == 0)
    def _():
        m = mod_ref[0]
        h = _mod_norm(x_ref[...], ln_ref[...], m[:, 0:d], m[:, d:2 * d])
        h_scr[...] = h.astype(BF16)

    o_ref[...] = _dot(h_scr[...], w_ref[...])


def _in_proj(x, mods, ln1, w_in, n_ctx, dec_seq):
    nt, d = x.shape
    n = w_in.shape[1]
    tm, tn = _row_tile(n_ctx, dec_seq, 1024), n // 3
    return pl.pallas_call(
        _inproj_kernel,
        out_shape=jax.ShapeDtypeStruct((nt, n), F32),
        grid=(nt // tm, n // tn),
        in_specs=[pl.BlockSpec((tm, d), lambda i, j: (i, 0)),
                  pl.BlockSpec((1, 1, 6 * d), lambda i, j: (_mod_index(i, tm, n_ctx, dec_seq), 0, 0)),
                  pl.BlockSpec((1, d), lambda i, j: (0, 0)),
                  pl.BlockSpec((d, tn), lambda i, j: (0, j))],
        out_specs=pl.BlockSpec((tm, tn), lambda i, j: (i, j)),
        scratch_shapes=[pltpu.VMEM((tm, d), BF16)],
        compiler_params=_cparams(("parallel", "arbitrary"), 48),
        name="in_proj",
    )(x, mods, ln1, w_in)


_TB = 256
_SCAN_ROWS = 512


def _gdn_prep_kernel(q_ref, k_ref, v_ref, qp_ref, kp_ref, vp_ref, qx_ref, kx_ref, vx_ref, ba_ref,
                     cw_ref, alog_ref, dtb_ref, qo_ref, ko_ref, vo_ref, bg_ref, ext_scr, *, n_ctx, seq, dec_seq):
    rows = q_ref.shape[0]
    r0 = pl.program_id(0) * rows
    in_ctx = r0 < n_ctx
    off = jnp.where(in_ctx, r0, r0 - n_ctx)
    length = jnp.where(in_ctx, seq, dec_seq)
    is_start = lax.rem(off, length) == 0
    is_end = lax.rem(off + rows, length) == 0
    lane_lo = _lane_lo(rows)
    cw = cw_ref[...]
    width = q_ref.shape[1]

    def conv(x_ref, prev_ref, next_ref, o_ref, a, norm_scale):
        ext_scr[0:SUBLANES, :] = jnp.where(is_start, 0.0, prev_ref[...])
        ext_scr[SUBLANES:SUBLANES + rows, :] = x_ref[...]
        ext_scr[SUBLANES + rows:, :] = jnp.where(is_end, 0.0, next_ref[...])
        acc = None
        for j in range(CONV_K):
            o = SUBLANES - CONV_K // 2 + j
            term = ext_scr[o:o + rows, :] * cw[j:j + 1, a * width:(a + 1) * width]
            acc = term if acc is None else acc + term
        y = _silu(acc)
        for g in range(width // LANES):
            yg = y[:, g * LANES:(g + 1) * LANES]
            if norm_scale is not None:
                yg = yg * lax.rsqrt(_half_sums(yg * yg, lane_lo) + EPS) * norm_scale
            o_ref[:, g * LANES:(g + 1) * LANES] = yg

    conv(q_ref, qp_ref, qx_ref, qo_ref, 0, HEAD_DIM ** -0.5)
    conv(k_ref, kp_ref, kx_ref, ko_ref, 1, 1.0)
    conv(v_ref, vp_ref, vx_ref, vo_ref, 2, None)

    x = ba_ref[...]
    g = -jnp.exp(alog_ref[...]) * _softplus(x + dtb_ref[...])
    row_in_chunk = lax.broadcasted_iota(jnp.int32, (rows, LANES), 0) % CHUNK
    pre = suf = g
    s = 1
    while s < CHUNK:
        pre = pre + jnp.where(row_in_chunk >= s, pltpu.roll(pre, s, axis=0), 0.0)
        suf = suf + jnp.where(row_in_chunk + s < CHUNK, pltpu.roll(suf, rows - s, axis=0), 0.0)
        s *= 2
    c = lax.broadcasted_iota(jnp.int32, (rows, LANES), 1) % 8
    bg_ref[...] = jnp.where(c < 4, _sigmoid(x), jnp.where(c < 6, pre, suf))


def _gdn_prep(proj, conv_w, alog_row, dtb_row, n_ctx, seq, dec_seq):
    nt = proj.shape[0]
    rows = _TB
    assert seq % rows == 0 and dec_seq % rows == 0
    w = PAIRS * LANES
    per = rows // SUBLANES
    last = nt // SUBLANES - 1
    main = [pl.BlockSpec((rows, w), lambda i, a=a: (i, a)) for a in range(3)]
    prev = [pl.BlockSpec((SUBLANES, w), lambda i, a=a: (jnp.maximum(i * per - 1, 0), a)) for a in range(3)]
    nxt = [pl.BlockSpec((SUBLANES, w), lambda i, a=a: (jnp.minimum((i + 1) * per, last), a)) for a in range(3)]
    row = pl.BlockSpec((1, LANES), lambda i: (0, 0))
    return pl.pallas_call(
        functools.partial(_gdn_prep_kernel, n_ctx=n_ctx, seq=seq, dec_seq=dec_seq),
        out_shape=[jax.ShapeDtypeStruct((nt, w), F32)] * 3 + [jax.ShapeDtypeStruct((nt, LANES), F32)],
        grid=(nt // rows,),
        in_specs=main + prev + nxt + [pl.BlockSpec((rows, LANES), lambda i: (i, G_BA)),
                                      pl.BlockSpec(conv_w.shape, lambda i: (0, 0)), row, row],
        out_specs=[pl.BlockSpec((rows, w), lambda i: (i, 0))] * 3 + [pl.BlockSpec((rows, LANES), lambda i: (i, 0))],
        scratch_shapes=[pltpu.VMEM((rows + 2 * SUBLANES, w), F32)],
        compiler_params=_cparams(("parallel",), 32),
        name="gdn_prep",
    )(*([proj] * 10), conv_w, alog_row, dtb_row)


def _run_interleaved(gens):
    gens = list(gens)
    while gens:
        for g in list(gens):
            try:
                next(g)
            except StopIteration:
                gens.remove(g)


def _gdn_scan_kernel(*refs, nch, ns, has_s0, has_sfin, aliased):
    fwd_refs, bwd_refs = refs[0:4], refs[4:8]
    pos = 8
    s0_ref = None
    if has_s0:
        s0_ref = refs[pos]
        pos += 1
    if aliased:
        pos += 2
    of_ref, ob_ref = refs[pos:pos + 2]
    pos += 2
    sfin_ref = None
    if has_sfin:
        sfin_ref = refs[pos]
        pos += 1
    s_scr = refs[pos]
    j = pl.program_id(1)
    n2 = 2 * CHUNK
    units = [(sq, p, d) for sq in range(ns) for p in range(PAIRS) for d in range(2)]
    n_units = len(units)
    idx = range(n_units)

    @pl.when(j == 0)
    def _():
        for u, (sq, p, d) in enumerate(units):
            s_scr[u] = s0_ref[sq, p, d] if has_s0 else jnp.zeros((n2, n2), F32)

    ri = lax.broadcasted_iota(jnp.int32, (n2, n2), 0)
    ci = lax.broadcasted_iota(jnp.int32, (n2, n2), 1)
    same = (ri >= CHUNK) == (ci >= CHUNK)
    incl = (same & (ri >= ci), same & (ri <= ci))
    strict = (same & (ri > ci), same & (ri < ci))
    eye = (ri == ci).astype(F32)
    ril, cil = ri % CHUNK, ci % CHUNK
    level_mask = []
    m = 1
    while m < CHUNK:
        level_mask.append(same & (ril // (2 * m) == cil // (2 * m)) & (ril // m != cil // m))
        m *= 2

    def col_stack(arr, ca, cb):
        top = jnp.broadcast_to(arr[:, ca:ca + 1], (CHUNK, LANES))
        bot = jnp.broadcast_to(arr[:, cb:cb + 1], (CHUNK, LANES))
        return jnp.concatenate([top, bot], axis=0)

    def row_stack(arr, row):
        top = jnp.broadcast_to(arr[row:row + 1, :], (CHUNK, LANES))
        bot = jnp.broadcast_to(arr[CHUNK + row:CHUNK + row + 1, :], (CHUNK, LANES))
        return jnp.concatenate([top, bot], axis=0)

    def rows_of(c, sq, d):
        r = (sq * nch + (c if d == 0 else nch - 1 - c)) * CHUNK
        return slice(r, r + CHUNK)

    feats = [dict() for _ in range(nch)]
    solved = [dict() for _ in range(nch)]
    states = [s_scr[u] for u in idx]

    def features(c):
        f = feats[c]
        bg = {(sq, d): (fwd_refs, bwd_refs)[d][3][rows_of(c, sq, d), :] for sq in range(ns) for d in range(2)}
        b_c = [col_stack(bg[sq, d], 8 * p + 2 * d, 8 * p + 2 * d + 1) for sq, p, d in units]
        g_c = [col_stack(bg[sq, d], 8 * p + 4 + 2 * d, 8 * p + 5 + 2 * d) for sq, p, d in units]
        yield
        ld = lambda a, sq, p, d: (fwd_refs, bwd_refs)[d][a][rows_of(c, sq, d), p * LANES:(p + 1) * LANES]
        k2 = [jnp.concatenate([ld(1, *u)] * 2, axis=0) for u in units]
        ks = [jnp.where(same, k2[i], 0.0) for i in idx]
        qs = [jnp.where(same, jnp.concatenate([ld(0, *u)] * 2, axis=0), 0.0) for u in units]
        ks_b = [ks[i].astype(BF16) for i in idx]
        prod = [_dot_nt(jnp.concatenate([ks_b[i], qs[i].astype(BF16)], axis=0), ks_b[i]) for i in idx]
        yield
        dirs = [d for _, _, d in units]
        decay = [jnp.where(incl[dirs[i]], jnp.exp(jnp.where(incl[dirs[i]], g_c[i] - g_c[i].T, 0.0)), 0.0) for i in idx]
        yield
        eg = [jnp.exp(g_c[i]) for i in idx]
        g_last = [row_stack(g_c[i], CHUNK - 1 if dirs[i] == 0 else 0) for i in idx]
        f["l"] = [jnp.where(strict[dirs[i]], b_c[i] * prod[i][:n2] * decay[i], 0.0) for i in idx]
        yield
        v2 = [jnp.concatenate([ld(2, *u)] * 2, axis=0) for u in units]
        f["rhs"] = [(b_c[i] * jnp.where(same, v2[i], eg[i] * pltpu.roll(k2[i], CHUNK, axis=1))).astype(BF16)
                    for i in idx]
        yield
        f["at"] = [(prod[i][n2:] * decay[i]).astype(BF16) for i in idx]
        f["qg"] = [(qs[i] * eg[i]).astype(BF16) for i in idx]
        yield
        f["kt"] = [(ks[i] * jnp.exp(g_last[i] - g_c[i])).T.astype(BF16) for i in idx]
        f["gl"] = [jnp.exp(g_last[i]) for i in idx]
        yield

    def inverse(c):
        f, o = feats[c], solved[c]
        l_mat = f["l"]
        t_inv = [eye - jnp.where(level_mask[0], l_mat[i], 0.0) for i in idx]
        for lm in level_mask[1:]:
            t_b = [t_inv[i].astype(BF16) for i in idx]
            y = [_dot(jnp.where(lm, l_mat[i], 0.0).astype(BF16), t_b[i]).astype(BF16) for i in idx]
            yield
            t_inv = [t_inv[i] - _dot(t_b[i], y[i]) for i in idx]
            yield
        xs = [_dot(t_inv[i].astype(BF16), f["rhs"][i]) for i in idx]
        yield
        o["u"] = [jnp.where(same, xs[i], 0.0) for i in idx]
        o["wq"] = [jnp.concatenate([jnp.where(same, pltpu.roll(xs[i], CHUNK, axis=1), 0.0).astype(BF16),
                                    f["qg"][i]], axis=0) for i in idx]
        yield

    def scan(c):
        f, o = feats[c], solved[c]
        ws = [_dot(o["wq"][i], states[i].astype(BF16)) for i in idx]
        yield
        u_b = [(o["u"][i] - ws[i][:n2]).astype(BF16) for i in idx]
        yield
        o_st = [ws[i][n2:] + _dot(f["at"][i], u_b[i]) for i in idx]
        for i in idx:
            states[i] = f["gl"][i] * states[i] + _dot(f["kt"][i], u_b[i])
        yield
        for i, (sq, p, d) in enumerate(units):
            dst = of_ref if d == 0 else ob_ref
            dst[rows_of(c, sq, d), p * LANES:(p + 1) * LANES] = o_st[i][:CHUNK] + o_st[i][CHUNK:]
        feats[c] = solved[c] = None
        yield

    for t in range(nch + 2):
        gens = []
        if 1 <= t <= nch:
            gens.append(inverse(t - 1))
        if t < nch:
            gens.append(features(t))
        if 2 <= t:
            gens.append(scan(t - 2))
        _run_interleaved(gens)

    for u in idx:
        s_scr[u] = states[u]
    if has_sfin:
        @pl.when(j == pl.num_programs(1) - 1)
        def _():
            for u, (sq, p, d) in enumerate(units):
                sfin_ref[sq, p, d] = states[u]


def _gdn_scan(feat, s0, prev_out, *, t, row0, n_seq, nt, want_state):
    tb = math.gcd(t, _SCAN_ROWS)
    nb = t // tb
    ns = math.gcd(n_seq, _SCAN_ROWS // tb) if nb == 1 else 1
    rows = ns * tb
    assert row0 % rows == 0
    blk0 = row0 // rows
    n2 = 2 * CHUNK
    w = PAIRS * LANES
    has_s0 = s0 is not None
    aliased = prev_out is not None

    def blocks(mirror):
        row = (lambda b, j: blk0 + b * nb + (nb - 1 - j)) if mirror else (lambda b, j: blk0 + b * nb + j)
        return ([pl.BlockSpec((rows, w), lambda b, j: (row(b, j), 0))] * 3
                + [pl.BlockSpec((rows, LANES), lambda b, j: (row(b, j), 0))]), row

    f_specs, f_row = blocks(False)
    b_specs, b_row = blocks(True)
    in_specs = f_specs + b_specs
    args = list(feat) * 2
    state_spec = pl.BlockSpec((ns, PAIRS, 2, n2, n2), lambda b, j: (b, 0, 0, 0, 0))
    if has_s0:
        in_specs.append(state_spec)
        args.append(s0)
    aliases = {}
    if aliased:
        aliases = {len(args): 0, len(args) + 1: 1}
        in_specs += [_any_spec(), _any_spec()]
        args += list(prev_out)
    out_shape = [jax.ShapeDtypeStruct((nt, w), F32)] * 2
    out_specs = [pl.BlockSpec((rows, w), lambda b, j: (f_row(b, j), 0)),
                 pl.BlockSpec((rows, w), lambda b, j: (b_row(b, j), 0))]
    if want_state:
        out_shape.append(jax.ShapeDtypeStruct((n_seq, PAIRS, 2, n2, n2), F32))
        out_specs.append(state_spec)
    return pl.pallas_call(
        functools.partial(_gdn_scan_kernel, nch=tb // CHUNK, ns=ns, has_s0=has_s0, has_sfin=want_state,
                          aliased=aliased),
        out_shape=out_shape,
        grid=(n_seq // ns, nb),
        in_specs=in_specs,
        out_specs=out_specs,
        scratch_shapes=[pltpu.VMEM((ns * 2 * PAIRS, n2, n2), F32)],
        input_output_aliases=aliases,
        compiler_params=_cparams(("parallel", "arbitrary"), 48),
        name="gdn_scan_ctx" if want_state else "gdn_scan_lat",
    )(*args)


def _prep_kernel(*refs, rope):
    bq_ref, cq_ref, bk_ref, bv_ref, ck_ref, cv_ref, qkn_ref = refs[:7]
    pos = 7
    cs = sn = None
    if rope:
        cs, sn = refs[7][...], refs[8][...]
        pos = 11
    qb_ref, qc_ref, kdb_ref, vdb_ref, kdc_ref, vdc_ref = refs[pos:pos + 6]
    pos += 6
    tr = bq_ref.shape[0]
    lane = lax.broadcasted_iota(jnp.int32, (tr, LANES), 1)
    lane_lo = lane < HEAD_DIM
    first_half = (lane % HEAD_DIM) < HEAD_DIM // 2
    qkn = qkn_ref[...]

    def norm(x, w_row):
        ms = _half_sums(x * x, lane_lo) * (1.0 / HEAD_DIM)
        return x * lax.rsqrt(ms + EPS) * w_row

    def rot(y):
        if not rope:
            return y
        partner = jnp.where(first_half, pltpu.roll(y, LANES - HEAD_DIM // 2, axis=1),
                            pltpu.roll(y, HEAD_DIM // 2, axis=1))
        return y * cs + partner * sn

    def dup(x, o_ref):
        lo = jnp.where(lane_lo, x, 0.0)
        hi = jnp.where(lane_lo, 0.0, x)
        d0 = (lo + pltpu.roll(lo, HEAD_DIM, axis=1)).astype(BF16)
        d1 = (hi + pltpu.roll(hi, HEAD_DIM, axis=1)).astype(BF16)
        if len(o_ref.shape) == 3:
            o_ref[0, :, 0:LANES] = d0
            o_ref[0, :, LANES:2 * LANES] = d1
        else:
            o_ref[:, 0:LANES] = d0
            o_ref[:, LANES:2 * LANES] = d1

    for q_ref, o_ref, wi in ((bq_ref, qb_ref, 0), (cq_ref, qc_ref, 2)):
        for g in range(PAIRS):
            y = rot(norm(q_ref[:, g * LANES:(g + 1) * LANES], qkn[wi:wi + 1, :])) * HEAD_DIM ** -0.5
            o_ref[2 * g] = jnp.where(lane_lo, y, 0.0).astype(BF16)
            o_ref[2 * g + 1] = jnp.where(lane_lo, 0.0, y).astype(BF16)
    kb_n = norm(bk_ref[...], qkn[1:2, :])
    kc_n = norm(ck_ref[...], qkn[3:4, :])
    if not rope:
        kbn_ref, vbn_ref, kcn_ref, vcn_ref = refs[pos:pos + 4]
        kbn_ref[...] = kb_n
        vbn_ref[...] = bv_ref[...]
        kcn_ref[...] = kc_n
        vcn_ref[...] = cv_ref[...]
    dup(rot(kb_n), kdb_ref)
    dup(bv_ref[...], vdb_ref)
    dup(rot(kc_n), kdc_ref)
    dup(cv_ref[...], vdc_ref)


def _attn_prep(proj, qkn_rows, rope_tables, glob_kv, *, row0, n_rows, t):
    rope = rope_tables is not None
    tr = math.gcd(t, 512)
    blk0 = row0 // tr

    def col(g, width):
        gw = width // LANES
        return pl.BlockSpec((tr, width), lambda i, g=g, gw=gw: (blk0 + i, g // gw))

    in_specs = [col(G_BQ, 4 * LANES), col(G_CQ, 4 * LANES), col(G_BK, LANES), col(G_BV, LANES),
                col(G_CK, LANES), col(G_CV, LANES), pl.BlockSpec((4, LANES), lambda i: (0, 0))]
    args = [proj] * 6 + [qkn_rows]
    q_shape = jax.ShapeDtypeStruct((HEADS, n_rows, LANES), BF16)
    kv_shape = jax.ShapeDtypeStruct((n_rows, 2 * LANES), BF16)
    q_spec = pl.BlockSpec((HEADS, tr, LANES), lambda i: (0, i, 0))
    kv_spec = pl.BlockSpec((tr, 2 * LANES), lambda i: (i, 0))
    aliases = {}
    if rope:
        per_seq = t // tr
        in_specs += [pl.BlockSpec((tr, LANES), lambda i: (i % per_seq, 0))] * 2 + [_any_spec()] * 2
        aliases = {len(args) + 2: 2, len(args) + 3: 3}
        args += list(rope_tables) + list(glob_kv)
        past = glob_kv[0].shape[1] - t
        assert past % tr == 0
        glob_spec = pl.BlockSpec((1, tr, 2 * LANES), lambda i: (i // per_seq, past // tr + i % per_seq, 0))
        glob_shape = jax.ShapeDtypeStruct(glob_kv[0].shape, BF16)
        out_shape = [q_shape, q_shape, glob_shape, glob_shape, kv_shape, kv_shape]
        out_specs = [q_spec, q_spec, glob_spec, glob_spec, kv_spec, kv_spec]
    else:
        out_shape = [q_shape, q_shape] + [kv_shape] * 4 + [jax.ShapeDtypeStruct((n_rows, LANES), F32)] * 4
        out_specs = [q_spec, q_spec] + [kv_spec] * 4 + [pl.BlockSpec((tr, LANES), lambda i: (i, 0))] * 4
    return pl.pallas_call(
        functools.partial(_prep_kernel, rope=rope),
        out_shape=out_shape,
        grid=(n_rows // tr,),
        in_specs=in_specs,
        out_specs=out_specs,
        input_output_aliases=aliases,
        compiler_params=_cparams(("parallel",), 32),
        name="attn_prep_lat" if rope else "attn_prep_ctx",
    )(*args)


def _stacked_q(q_ref, g, tq):
    return q_ref[GROUP_HEADS * g:GROUP_HEADS * (g + 1)].reshape(GROUP_HEADS * tq, LANES)


def _unstack_group(o_st, tq, lane_lo):
    a = jnp.where(lane_lo, o_st[0:tq], o_st[tq:2 * tq])
    b = jnp.where(lane_lo, o_st[2 * tq:3 * tq], o_st[3 * tq:4 * tq])
    return jnp.concatenate([a, b], axis=1)


def _with_ones(v, ones):
    return jnp.concatenate([v, ones], axis=1)


def _sink_col(sink_row, g, tq):
    parts = [jnp.broadcast_to(sink_row[:, GROUP_HEADS * g + j:GROUP_HEADS * g + j + 1], (tq, 1))
             for j in range(GROUP_HEADS)]
    return jnp.concatenate(parts, axis=0)


def _attn_ctx_kernel(qb_ref, kdb_ref, vdb_ref, qc_ref, kdc_ref, vdc_ref, sink_ref, yb_ref, yc_ref):
    tq = qb_ref.shape[1]
    lane_lo_q = _lane_lo(tq)
    sink_row = sink_ref[...]
    groups = range(KV_HEADS)
    gs = [slice(g * LANES, (g + 1) * LANES) for g in groups]
    jobs = [(qb_ref, kdb_ref, vdb_ref, yb_ref, False, g) for g in groups]
    jobs += [(qc_ref, kdc_ref, vdc_ref, yc_ref, True, g) for g in groups]
    s = [_dot_nt(_stacked_q(q_ref, g, tq), kd_ref[:, gs[g]]) for q_ref, kd_ref, _, _, _, g in jobs]
    for si, (_, _, vd_ref, y_ref, use_sink, g) in zip(s, jobs):
        m = jnp.max(si, axis=-1, keepdims=True)
        if use_sink:
            snk = _sink_col(sink_row, g, tq)
            m = jnp.maximum(m, snk)
        p = jnp.exp(si - m)
        l = jnp.sum(p, axis=-1, keepdims=True)
        if use_sink:
            l = l + jnp.exp(snk - m)
        o = _dot(p.astype(BF16), vd_ref[:, gs[g]]) / l
        y_ref[:, 2 * g * LANES:(2 * g + 2) * LANES] = _unstack_group(o, tq, lane_lo_q).astype(y_ref.dtype)


def _attn_ctx(qb, kdb, vdb, qc, kdc, vdc, sink_row, *, t, nt):
    n = kdb.shape[0]
    q_spec = pl.BlockSpec((HEADS, t, LANES), lambda b: (0, b, 0))
    kv_spec = pl.BlockSpec((t, 2 * LANES), lambda b: (b, 0))
    y_spec = pl.BlockSpec((t, 4 * LANES), lambda b: (b, 0))
    return pl.pallas_call(
        _attn_ctx_kernel,
        out_shape=[jax.ShapeDtypeStruct((nt, 4 * LANES), BF16)] * 2,
        grid=(n // t,),
        in_specs=[q_spec, kv_spec, kv_spec, q_spec, kv_spec, kv_spec,
                  pl.BlockSpec((1, LANES), lambda b: (0, 0))],
        out_specs=[y_spec, y_spec],
        compiler_params=_cparams(("parallel",), 32),
        name="attn_ctx",
    )(qb, kdb, vdb, qc, kdc, vdc, sink_row)


_TQ_B = 256
_TK_B = 1536


def _attn_glob_kernel(q_ref, kd_ref, vd_ref, y_in_ref, y_ref, s_scr, *, tk):
    del y_in_ref
    tq = q_ref.shape[1]
    n_blk = kd_ref.shape[1] // tk
    groups = range(KV_HEADS)
    gs = [slice(g * LANES, (g + 1) * LANES) for g in groups]
    q = [_stacked_q(q_ref, g, tq) for g in groups]

    def scores_into(j, slot):
        for g in groups:
            s_scr[slot, g] = _dot_nt(q[g], kd_ref[0, j * tk:(j + 1) * tk, gs[g]])

    rows = GROUP_HEADS * tq
    m = [jnp.full((rows, LANES), NEG, F32) for _ in groups]
    l = [jnp.zeros((rows, LANES), F32) for _ in groups]
    acc = [jnp.zeros((rows, LANES), F32) for _ in groups]
    scores_into(0, 0)
    for j in range(n_blk):
        if j + 1 < n_blk:
            scores_into(j + 1, (j + 1) % 2)
        for g in groups:
            s = s_scr[j % 2, g]
            m_new = jnp.maximum(m[g], jnp.max(s, axis=-1, keepdims=True))
            p = jnp.exp(s - m_new[:, 0:1])
            alpha = jnp.exp(m[g] - m_new)
            l[g] = alpha * l[g] + jnp.sum(p, axis=-1, keepdims=True)
            acc[g] = alpha * acc[g] + _dot(p.astype(BF16), vd_ref[0, j * tk:(j + 1) * tk, gs[g]])
            m[g] = m_new
    lane_lo_q = _lane_lo(tq)
    for g in groups:
        o = acc[g] / l[g]
        y_ref[:, 2 * g * LANES:(2 * g + 2) * LANES] = _unstack_group(o, tq, lane_lo_q).astype(y_ref.dtype)


def _attn_glob(q, kd, vd, y_prev, *, t, row0):
    n = q.shape[1]
    s_len = kd.shape[1]
    tq, tk = math.gcd(t, _TQ_B), math.gcd(s_len, _TK_B)
    per_seq = t // tq
    blk0 = row0 // tq
    kv_spec = pl.BlockSpec((1, s_len, 2 * LANES), lambda b, i: (b, 0, 0))
    return pl.pallas_call(
        functools.partial(_attn_glob_kernel, tk=tk),
        out_shape=jax.ShapeDtypeStruct(y_prev.shape, BF16),
        grid=(n // t, per_seq),
        in_specs=[pl.BlockSpec((HEADS, tq, LANES), lambda b, i: (0, b * per_seq + i, 0)),
                  kv_spec, kv_spec, _any_spec()],
        out_specs=pl.BlockSpec((tq, 4 * LANES), lambda b, i: (blk0 + b * per_seq + i, 0)),
        scratch_shapes=[pltpu.VMEM((2, KV_HEADS, GROUP_HEADS * tq, tk), F32)],
        input_output_aliases={3: 0},
        compiler_params=_cparams(("parallel", "parallel"), 56),
        name="attn_glob",
    )(q, kd, vd, y_prev)


def _attn_win_kernel(q_ref, kx_ref, vx_ref, kp_ref, kc_ref, kn_ref, vp_ref, vc_ref, vn_ref,
                     sink_ref, y_in_ref, y_ref):
    del y_in_ref
    i = pl.program_id(1)
    nblk = pl.num_programs(1)
    tq = q_ref.shape[1]
    lane_lo_q = _lane_lo(tq)
    sink_row = sink_ref[...]
    rr = lax.broadcasted_iota(jnp.int32, (GROUP_HEADS * tq, tq), 0) % tq
    cc = lax.broadcasted_iota(jnp.int32, (GROUP_HEADS * tq, tq), 1)
    mask_prev = (cc >= rr) & (i > 0)
    mask_next = (cc <= rr) & (i < nblk - 1)
    groups = range(KV_HEADS)
    gs = [slice(g * LANES, (g + 1) * LANES) for g in groups]
    qs = [_stacked_q(q_ref, g, tq) for g in groups]
    p_len = kx_ref.shape[1]
    s = [jnp.concatenate([_dot_nt(qs[g], kx_ref[0, :, gs[g]]),
                          jnp.where(mask_prev, _dot_nt(qs[g], kp_ref[:, gs[g]]), NEG),
                          _dot_nt(qs[g], kc_ref[:, gs[g]]),
                          jnp.where(mask_next, _dot_nt(qs[g], kn_ref[:, gs[g]]), NEG)], axis=1) for g in groups]
    ones_x = jnp.ones((p_len, LANES), BF16)
    ones_b = jnp.ones((tq, LANES), BF16)
    for g in groups:
        snk = _sink_col(sink_row, g, tq)
        m = jnp.maximum(jnp.max(s[g], axis=-1, keepdims=True), snk)
        p = jnp.exp(s[g] - m).astype(BF16)
        ol = _dot(p[:, :p_len], _with_ones(vx_ref[0, :, gs[g]], ones_x))
        for b, v_ref in enumerate((vp_ref, vc_ref, vn_ref)):
            ol = ol + _dot(p[:, p_len + b * tq:p_len + (b + 1) * tq], _with_ones(v_ref[:, gs[g]], ones_b))
        o = ol[:, :LANES] / (ol[:, LANES:] + jnp.exp(snk - m))
        y_ref[:, 2 * g * LANES:(2 * g + 2) * LANES] = _unstack_group(o, tq, lane_lo_q).astype(y_ref.dtype)


def _attn_win(q, kd, vd, kx, vx, sink_row, y_prev, *, t, row0):
    n = q.shape[1]
    p_len = kx.shape[1]
    tq = WINDOW
    nblk = t // tq
    blk0 = row0 // tq
    x_spec = pl.BlockSpec((1, p_len, 2 * LANES), lambda b, i: (b, 0, 0))
    prev = pl.BlockSpec((tq, 2 * LANES), lambda b, i: (b * nblk + jnp.maximum(i - 1, 0), 0))
    cur = pl.BlockSpec((tq, 2 * LANES), lambda b, i: (b * nblk + i, 0))
    nxt = pl.BlockSpec((tq, 2 * LANES), lambda b, i: (b * nblk + jnp.minimum(i + 1, nblk - 1), 0))
    return pl.pallas_call(
        _attn_win_kernel,
        out_shape=jax.ShapeDtypeStruct(y_prev.shape, BF16),
        grid=(n // t, nblk),
        in_specs=[pl.BlockSpec((HEADS, tq, LANES), lambda b, i: (0, b * nblk + i, 0)),
                  x_spec, x_spec, prev, cur, nxt, prev, cur, nxt,
                  pl.BlockSpec((1, LANES), lambda b, i: (0, 0)), _any_spec()],
        out_specs=pl.BlockSpec((tq, 4 * LANES), lambda b, i: (blk0 + b * nblk + i, 0)),
        input_output_aliases={10: 0},
        compiler_params=_cparams(("parallel", "parallel"), 32),
        name="attn_win",
    )(q, kx, vx, kd, kd, kd, vd, vd, vd, sink_row, y_prev)


def _mix_kernel(x_ref, of_ref, ob_ref, z_ref, yb_ref, yc_ref, ga_ref, gb_ref, gc_ref, mod_ref, anorm_ref,
                wa_ref, wb_ref, wc_ref, wo_ref, o_ref):
    d = x_ref.shape[1]
    tm = x_ref.shape[0]
    lane_lo = _lane_lo(tm)
    anorm = anorm_ref[...]
    ya = []
    for g in range(PAIRS):
        gl = slice(g * LANES, (g + 1) * LANES)
        o = of_ref[:, gl] + ob_ref[:, gl]
        ms = _half_sums(o * o, lane_lo) * (1.0 / HEAD_DIM)
        ya.append((o * lax.rsqrt(ms + EPS) * anorm * _silu(z_ref[:, gl])).astype(BF16))
    ya = jnp.concatenate(ya, axis=1)
    merged = (_sigmoid(ga_ref[...]) * _dot(ya, wa_ref[...])
              + _sigmoid(gb_ref[...]) * _dot(yb_ref[...], wb_ref[...])
              + _sigmoid(gc_ref[...]) * _dot(yc_ref[...], wc_ref[...]))
    gate1 = mod_ref[0][:, 2 * d:3 * d]
    o_ref[...] = x_ref[...] + gate1 * _dot(merged.astype(BF16), wo_ref[...])


def _mix(x, o_f, o_b, yb, yc, proj, mods, anorm_row, wa, wb, wc, wo, n_ctx, dec_seq):
    nt, d = x.shape
    tm = _row_tile(n_ctx, dec_seq, 512)
    w = PAIRS * LANES
    y_spec = pl.BlockSpec((tm, w), lambda i: (i, 0))
    gw = d // LANES

    def gate(k):
        return pl.BlockSpec((tm, d), lambda i, k=k: (i, G_GATES // gw + k))

    def full(a):
        return pl.BlockSpec(a.shape, lambda i: (0, 0))

    return pl.pallas_call(
        _mix_kernel,
        out_shape=jax.ShapeDtypeStruct((nt, d), F32),
        grid=(nt // tm,),
        in_specs=[pl.BlockSpec((tm, d), lambda i: (i, 0)), y_spec, y_spec,
                  pl.BlockSpec((tm, w), lambda i: (i, G_Z // PAIRS)), y_spec, y_spec,
                  gate(0), gate(1), gate(2),
                  pl.BlockSpec((1, 1, 6 * d), lambda i: (_mod_index(i, tm, n_ctx, dec_seq), 0, 0)),
                  full(anorm_row), full(wa), full(wb), full(wc), full(wo)],
        out_specs=pl.BlockSpec((tm, d), lambda i: (i, 0)),
        compiler_params=_cparams(("parallel",), 48),
        name="mix_out",
    )(x, o_f, o_b, proj, yb, yc, proj, proj, proj, mods, anorm_row, wa, wb, wc, wo)


_FF_CHUNK = 1024


def _ffn_kernel(x_ref, mod_ref, ln_ref, w1_ref, w2_ref, o_ref):
    d = x_ref.shape[1]
    m = mod_ref[0]
    x = x_ref[...]
    h = _mod_norm(x, ln_ref[...], m[:, 3 * d:4 * d], m[:, 4 * d:5 * d]).astype(BF16)
    acc = None
    for f in range(0, w1_ref.shape[1], _FF_CHUNK):
        hid = jnp.maximum(_dot(h, w1_ref[:, f:f + _FF_CHUNK]), 0.0)
        part = _dot((hid * hid).astype(BF16), w2_ref[f:f + _FF_CHUNK, :])
        acc = part if acc is None else acc + part
    o_ref[...] = x + m[:, 5 * d:6 * d] * acc


def _ffn(x, mods, ln2, w1, w2, n_ctx, dec_seq):
    nt, d = x.shape
    tm = _row_tile(n_ctx, dec_seq, 512)
    return pl.pallas_call(
        _ffn_kernel,
        out_shape=jax.ShapeDtypeStruct((nt, d), F32),
        grid=(nt // tm,),
        in_specs=[pl.BlockSpec((tm, d), lambda i: (i, 0)),
                  pl.BlockSpec((1, 1, 6 * d), lambda i: (_mod_index(i, tm, n_ctx, dec_seq), 0, 0)),
                  pl.BlockSpec((1, d), lambda i: (0, 0)),
                  pl.BlockSpec(w1.shape, lambda i: (0, 0)),
                  pl.BlockSpec(w2.shape, lambda i: (0, 0))],
        out_specs=pl.BlockSpec((tm, d), lambda i: (i, 0)),
        compiler_params=_cparams(("parallel",), 56),
        name="ffn",
    )(x, mods, ln2, w1, w2)


def _ba_lane_order():
    cols = []
    for p in range(PAIRS):
        for base in (0, 2 * HEADS):
            for d in range(2):
                cols += [base + d * HEADS + 2 * p, base + d * HEADS + 2 * p + 1]
    return cols


def _reorder_w_in(w_in):
    a_w = HEADS * HEAD_DIM
    kvw = KV_HEADS * HEAD_DIM
    o_beta = 4 * a_w
    o_bq = o_beta + 4 * HEADS
    o_bk = o_bq + a_w
    o_bv = o_bk + kvw
    o_cq = o_bv + kvw
    o_ck = o_cq + a_w
    o_cv = o_ck + kvw
    o_g = o_cv + kvw
    seg = lambda a, n: w_in[:, :, a:a + n]
    ba = jnp.take(seg(o_beta, 4 * HEADS), jnp.array(_ba_lane_order(), jnp.int32), axis=2)
    ba = jnp.pad(ba, ((0, 0), (0, 0), (0, 2 * LANES - ba.shape[2])))
    out = jnp.concatenate([seg(0, o_beta), seg(o_bq, a_w), seg(o_cq, a_w), seg(o_g, w_in.shape[2] - o_g),
                           seg(o_bk, kvw), seg(o_bv, kvw), seg(o_ck, kvw), seg(o_cv, kvw), ba], axis=2)
    assert out.shape[2] == IN_COLS_PAD
    return out.astype(BF16)


def _decay_rows(vals):
    rows = []
    for p in range(PAIRS):
        r = jnp.stack([vals[:, 0, 2 * p], vals[:, 0, 2 * p + 1], vals[:, 1, 2 * p], vals[:, 1, 2 * p + 1]], axis=-1)
        rows.append(jnp.pad(r, ((0, 0), (4, 0))))
    out = jnp.concatenate(rows, axis=-1)
    return jnp.pad(out, ((0, 0), (0, LANES - out.shape[1])))[:, None, :]


def _rope_tables(n_tokens):
    rows = n_tokens // GRID_W
    row_id = jnp.repeat(jnp.arange(rows, dtype=F32), GRID_W)
    col_id = jnp.tile(jnp.arange(GRID_W, dtype=F32), rows)
    n_freq = HEAD_DIM // 4
    inv_freq = ROPE_THETA ** (-jnp.arange(n_freq, dtype=F32) / n_freq)
    ang = jnp.concatenate([row_id[:, None] * inv_freq, col_id[:, None] * inv_freq], axis=-1)
    cos, sin = jnp.cos(ang), jnp.sin(ang)
    cs = jnp.tile(cos, (1, 4))
    sn = jnp.tile(jnp.concatenate([-sin, sin], axis=-1), (1, 2))
    return cs, sn


def _dup_kv(x):
    return jnp.concatenate([x[:, :, 0], x[:, :, 0], x[:, :, 1], x[:, :, 1]], axis=-1).astype(BF16)


def _block_diag_states(s):
    b = s.shape[0]
    s = s.reshape(b, 2, PAIRS, 2, HEAD_DIM, HEAD_DIM).transpose(0, 2, 1, 3, 4, 5)
    z = jnp.zeros_like(s[:, :, :, 0])
    top = jnp.concatenate([s[:, :, :, 0], z], axis=-1)
    bot = jnp.concatenate([z, s[:, :, :, 1]], axis=-1)
    return jnp.concatenate([top, bot], axis=-2)


def _diag_states(s):
    a = s[:, :, :, :HEAD_DIM, :HEAD_DIM]
    bb = s[:, :, :, HEAD_DIM:, HEAD_DIM:]
    out = jnp.stack([a, bb], axis=3)
    b = s.shape[0]
    return out.transpose(0, 2, 1, 3, 4, 5).reshape(b, 2, HEADS, HEAD_DIM, HEAD_DIM)


def kernel(x_prompt, x_sample, cache_k_glob, cache_v_glob, cache_k_win, cache_v_win, state_delta, c, c_ctx, w_mod, b_mod, ln1, ln2, w_in, conv_qkv, a_log, dt_bias, a_norm, qk_norm, sink, w_br_a, w_br_b, w_br_c, w_o, w_ff1, w_ff2):
    batch, seq, d = x_prompt.shape
    dec_batch, dec_seq, _ = x_sample.shape
    depth = w_mod.shape[0]
    n_ctx = batch * seq
    n_lat = dec_batch * dec_seq
    nt = n_ctx + n_lat
    assert 1 + dec_batch <= SUBLANES

    w_in_r = _reorder_w_in(w_in)
    wa, wb, wc, wo = (w.astype(BF16) for w in (w_br_a, w_br_b, w_br_c, w_o))
    w1, w2 = w_ff1.astype(BF16), w_ff2.astype(BF16)
    alog_rows = _decay_rows(a_log)
    dtb_rows = _decay_rows(dt_bias)
    anorm_rows = jnp.tile(a_norm, (1, 2))[:, None, :]
    qkn_rows = jnp.tile(qk_norm, (1, 1, 2))
    sink_rows = jnp.pad(sink, ((0, 0), (0, LANES - HEADS)))[:, None, :]
    rope_tables = _rope_tables(dec_seq)
    cvec = jnp.concatenate([c_ctx[None], c, jnp.zeros((SUBLANES - 1 - dec_batch, d), F32)], axis=0)
    mods_all = _mod_vectors(cvec, w_mod, b_mod)
    pad_lat = ((0, 0), (0, dec_seq), (0, 0))

    x = jnp.concatenate([x_prompt.reshape(n_ctx, d), x_sample.reshape(n_lat, d)], axis=0)
    new_kg, new_vg, new_kw, new_vw, new_st = [], [], [], [], []
    for l in range(depth):
        mods = mods_all[l][:, None, :]
        proj = _in_proj(x, mods, ln1[l][None], w_in_r[l], n_ctx, dec_seq)

        feat = _gdn_prep(proj, conv_qkv[l], alog_rows[l], dtb_rows[l], n_ctx, seq, dec_seq)
        o_f, o_b, s_fin = _gdn_scan(feat, None, None, t=seq, row0=0, n_seq=batch, nt=nt, want_state=True)
        o_f, o_b = _gdn_scan(feat, _block_diag_states(state_delta[:, l]), (o_f, o_b),
                             t=dec_seq, row0=n_ctx, n_seq=dec_batch, nt=nt, want_state=False)

        qb_c, qc_c, kdb_c, vdb_c, kdc_c, vdc_c, kb_n, vb_n, kc_n, vc_n = _attn_prep(
            proj, qkn_rows[l], None, None, row0=0, n_rows=n_ctx, t=seq)
        glob_kv = (jnp.pad(_dup_kv(cache_k_glob[:, l]), pad_lat), jnp.pad(_dup_kv(cache_v_glob[:, l]), pad_lat))
        qb_l, qc_l, kd_all, vd_all, kdc_l, vdc_l = _attn_prep(
            proj, qkn_rows[l], rope_tables, glob_kv, row0=n_ctx, n_rows=n_lat, t=dec_seq)

        yb, yc = _attn_ctx(qb_c, kdb_c, vdb_c, qc_c, kdc_c, vdc_c, sink_rows[l], t=seq, nt=nt)
        yb = _attn_glob(qb_l, kd_all, vd_all, yb, t=dec_seq, row0=n_ctx)
        yc = _attn_win(qc_l, kdc_l, vdc_l, _dup_kv(cache_k_win[:, l]), _dup_kv(cache_v_win[:, l]),
                       sink_rows[l], yc, t=dec_seq, row0=n_ctx)

        x = _mix(x, o_f, o_b, yb, yc, proj, mods, anorm_rows[l], wa[l], wb[l], wc[l], wo[l], n_ctx, dec_seq)
        x = _ffn(x, mods, ln2[l][None], w1[l], w2[l], n_ctx, dec_seq)

        kv_shape = (batch, seq, KV_HEADS, HEAD_DIM)
        new_kg.append(kb_n.reshape(kv_shape))
        new_vg.append(vb_n.reshape(kv_shape))
        new_kw.append(kc_n.reshape(kv_shape))
        new_vw.append(vc_n.reshape(kv_shape))
        new_st.append(_diag_states(s_fin))

    y_prompt = x[:n_ctx].reshape(batch, seq, d)
    y_sample = x[n_ctx:].reshape(dec_batch, dec_seq, d)
    return (y_prompt, y_sample, jnp.stack(new_kg, axis=1), jnp.stack(new_vg, axis=1),
            jnp.stack(new_kw, axis=1), jnp.stack(new_vw, axis=1), jnp.stack(new_st, axis=1))
```

```python
import functools
import math

import jax
import jax.numpy as jnp
from jax import lax
from jax.experimental import pallas as pl
from jax.experimental.pallas import tpu as pltpu

F32 = jnp.float32
BF16 = jnp.bfloat16

LANES = 128
SUBLANES = 8

HEAD_DIM = 64
HEADS = 8
KV_HEADS = 2
GROUP_HEADS = HEADS // KV_HEADS
CHUNK = 64
CONV_K = 5
GRID_W = 64
WINDOW = 128
ROPE_THETA = 10000.0
EPS = 1e-6
NEG = -1e30
PAIRS = HEADS // 2

G_Q, G_K, G_V, G_Z = 0, 4, 8, 12
G_BQ, G_CQ = 16, 20
G_GATES = 24
G_BK, G_BV, G_CK, G_CV = 48, 49, 50, 51
G_BA = 52
N_GROUPS = 54
IN_COLS_PAD = N_GROUPS * LANES


def _cparams(sem, vmem_mb):
    return pltpu.CompilerParams(dimension_semantics=sem, vmem_limit_bytes=vmem_mb << 20)


def _dot(a, b):
    return jnp.dot(a, b, preferred_element_type=F32)


def _dot_nt(a, b):
    return lax.dot_general(a, b, (((1,), (1,)), ((), ())), preferred_element_type=F32)


def _sigmoid(x):
    return 1.0 / (1.0 + jnp.exp(-x))


def _silu(x):
    return x * _sigmoid(x)


def _softplus(x):
    return jnp.maximum(x, 0.0) + jnp.log1p(jnp.exp(-jnp.abs(x)))


def _half_sums(x, lane_lo):
    s_lo = jnp.sum(jnp.where(lane_lo, x, 0.0), axis=-1, keepdims=True)
    s_hi = jnp.sum(jnp.where(lane_lo, 0.0, x), axis=-1, keepdims=True)
    return jnp.where(lane_lo, s_lo, s_hi)


def _lane_lo(rows):
    return lax.broadcasted_iota(jnp.int32, (rows, LANES), 1) < HEAD_DIM


def _row_tile(n_ctx, dec_seq, pref):
    return math.gcd(math.gcd(n_ctx, dec_seq), pref)


def _mod_index(i, tm, n_ctx, dec_seq):
    ctx_tiles = n_ctx // tm
    return jnp.where(i < ctx_tiles, 0, 1 + (i - ctx_tiles) // (dec_seq // tm))


def _any_spec():
    return pl.BlockSpec(memory_space=pl.ANY)


def _mod_kernel(c_ref, w_ref, b_ref, o_ref):
    s = _silu(c_ref[...])
    w = w_ref[0]
    s_hi = s.astype(BF16)
    s_lo = (s - s_hi.astype(F32)).astype(BF16)
    w_hi = w.astype(BF16)
    w_lo = (w - w_hi.astype(F32)).astype(BF16)
    acc = _dot(s_hi, w_hi) + (_dot(s_lo, w_hi) + _dot(s_hi, w_lo))
    o_ref[0] = acc + b_ref[0]


def _mod_vectors(cvec, w_mod, b_mod):
    depth, d, n = w_mod.shape
    tn = n // 4
    return pl.pallas_call(
        _mod_kernel,
        out_shape=jax.ShapeDtypeStruct((depth, SUBLANES, n), F32),
        grid=(depth, n // tn),
        in_specs=[pl.BlockSpec((SUBLANES, d), lambda l, j: (0, 0)),
                  pl.BlockSpec((1, d, tn), lambda l, j: (l, 0, j)),
                  pl.BlockSpec((1, 1, tn), lambda l, j: (l, 0, j))],
        out_specs=pl.BlockSpec((1, SUBLANES, tn), lambda l, j: (l, 0, j)),
        compiler_params=_cparams(("parallel", "parallel"), 40),
        name="mod_vectors",
    )(cvec, w_mod, b_mod.reshape(depth, 1, n))


def _mod_norm(x, ln, shift, scale):
    ms = jnp.mean(x * x, axis=-1, keepdims=True)
    return (x * lax.rsqrt(ms + EPS) * ln) * (1.0 + scale) + shift


def _inproj_kernel(x_ref, mod_ref, ln_ref, w_ref, o_ref, h_scr):
    d = x_ref.shape[1]

    @pl.when(pl.program_id(1) == 0)
    def _():
        m = mod_ref[0]
        h = _mod_norm(x_ref[...], ln_ref[...], m[:, 0:d], m[:, d:2 * d])
        h_scr[...] = h.astype(BF16)

    o_ref[...] = _dot(h_scr[...], w_ref[...])


def _in_proj(x, mods, ln1, w_in, n_ctx, dec_seq):
    nt, d = x.shape
    n = w_in.shape[1]
    tm, tn = _row_tile(n_ctx, dec_seq, 1024), n // 3
    return pl.pallas_call(
        _inproj_kernel,
        out_shape=jax.ShapeDtypeStruct((nt, n), F32),
        grid=(nt // tm, n // tn),
        in_specs=[pl.BlockSpec((tm, d), lambda i, j: (i, 0)),
                  pl.BlockSpec((1, 1, 6 * d), lambda i, j: (_mod_index(i, tm, n_ctx, dec_seq), 0, 0)),
                  pl.BlockSpec((1, d), lambda i, j: (0, 0)),
                  pl.BlockSpec((d, tn), lambda i, j: (0, j))],
        out_specs=pl.BlockSpec((tm, tn), lambda i, j: (i, j)),
        scratch_shapes=[pltpu.VMEM((tm, d), BF16)],
        compiler_params=_cparams(("parallel", "arbitrary"), 48),
        name="in_proj",
    )(x, mods, ln1, w_in)


_TB = 256
_SCAN_ROWS = 512
_SCAN_UNITS = 8


def _gdn_prep_kernel(q_ref, k_ref, v_ref, qp_ref, kp_ref, vp_ref, qx_ref, kx_ref, vx_ref, ba_ref,
                     cw_ref, alog_ref, dtb_ref, qo_ref, ko_ref, vo_ref, bg_ref, ext_scr, *, n_ctx, seq, dec_seq):
    rows = q_ref.shape[0]
    r0 = pl.program_id(0) * rows
    in_ctx = r0 < n_ctx
    off = jnp.where(in_ctx, r0, r0 - n_ctx)
    length = jnp.where(in_ctx, seq, dec_seq)
    is_start = lax.rem(off, length) == 0
    is_end = lax.rem(off + rows, length) == 0
    lane_lo = _lane_lo(rows)
    cw = cw_ref[...]
    width = q_ref.shape[1]

    def conv(x_ref, prev_ref, next_ref, o_ref, a, norm_scale):
        ext_scr[0:SUBLANES, :] = jnp.where(is_start, 0.0, prev_ref[...])
        ext_scr[SUBLANES:SUBLANES + rows, :] = x_ref[...]
        ext_scr[SUBLANES + rows:, :] = jnp.where(is_end, 0.0, next_ref[...])
        acc = None
        for j in range(CONV_K):
            o = SUBLANES - CONV_K // 2 + j
            term = ext_scr[o:o + rows, :] * cw[j:j + 1, a * width:(a + 1) * width]
            acc = term if acc is None else acc + term
        y = _silu(acc)
        for g in range(width // LANES):
            yg = y[:, g * LANES:(g + 1) * LANES]
            if norm_scale is not None:
                yg = yg * lax.rsqrt(_half_sums(yg * yg, lane_lo) + EPS) * norm_scale
            o_ref[:, g * LANES:(g + 1) * LANES] = yg

    conv(q_ref, qp_ref, qx_ref, qo_ref, 0, HEAD_DIM ** -0.5)
    conv(k_ref, kp_ref, kx_ref, ko_ref, 1, 1.0)
    conv(v_ref, vp_ref, vx_ref, vo_ref, 2, None)

    x = ba_ref[...]
    g = -jnp.exp(alog_ref[...]) * _softplus(x + dtb_ref[...])
    row_in_chunk = lax.broadcasted_iota(jnp.int32, (rows, LANES), 0) % CHUNK
    pre = suf = g
    s = 1
    while s < CHUNK:
        pre = pre + jnp.where(row_in_chunk >= s, pltpu.roll(pre, s, axis=0), 0.0)
        suf = suf + jnp.where(row_in_chunk + s < CHUNK, pltpu.roll(suf, rows - s, axis=0), 0.0)
        s *= 2
    c = lax.broadcasted_iota(jnp.int32, (rows, LANES), 1) % 8
    bg_ref[...] = jnp.where(c < 4, _sigmoid(x), jnp.where(c < 6, pre, suf))


def _gdn_prep(proj, conv_w, alog_row, dtb_row, n_ctx, seq, dec_seq):
    nt = proj.shape[0]
    rows = _TB
    assert seq % rows == 0 and dec_seq % rows == 0
    w = PAIRS * LANES
    per = rows // SUBLANES
    last = nt // SUBLANES - 1
    main = [pl.BlockSpec((rows, w), lambda i, a=a: (i, a)) for a in range(3)]
    prev = [pl.BlockSpec((SUBLANES, w), lambda i, a=a: (jnp.maximum(i * per - 1, 0), a)) for a in range(3)]
    nxt = [pl.BlockSpec((SUBLANES, w), lambda i, a=a: (jnp.minimum((i + 1) * per, last), a)) for a in range(3)]
    row = pl.BlockSpec((1, LANES), lambda i: (0, 0))
    return pl.pallas_call(
        functools.partial(_gdn_prep_kernel, n_ctx=n_ctx, seq=seq, dec_seq=dec_seq),
        out_shape=[jax.ShapeDtypeStruct((nt, w), F32)] * 3 + [jax.ShapeDtypeStruct((nt, LANES), F32)],
        grid=(nt // rows,),
        in_specs=main + prev + nxt + [pl.BlockSpec((rows, LANES), lambda i: (i, G_BA)),
                                      pl.BlockSpec(conv_w.shape, lambda i: (0, 0)), row, row],
        out_specs=[pl.BlockSpec((rows, w), lambda i: (i, 0))] * 3 + [pl.BlockSpec((rows, LANES), lambda i: (i, 0))],
        scratch_shapes=[pltpu.VMEM((rows + 2 * SUBLANES, w), F32)],
        compiler_params=_cparams(("parallel",), 32),
        name="gdn_prep",
    )(*([proj] * 10), conv_w, alog_row, dtb_row)


def _run_interleaved(gens):
    gens = list(gens)
    while gens:
        for g in list(gens):
            try:
                next(g)
            except StopIteration:
                gens.remove(g)


def _gdn_scan_kernel(*refs, nch, ns, nsub, has_s0, has_sfin):
    fwd_refs, bwd_refs = refs[0:4], refs[4:8]
    pos = 8
    s0_ref = None
    if has_s0:
        s0_ref = refs[pos]
        pos += 1
    of_ref, ob_ref = refs[pos:pos + 2]
    pos += 2
    sfin_ref = None
    if has_sfin:
        sfin_ref = refs[pos]
        pos += 1
    s_scr = refs[pos]
    j = pl.program_id(1)
    n2 = 2 * CHUNK
    chains = [(sq, p, d) for sq in range(ns) for p in range(PAIRS) for d in range(2)]
    units = [ch + (sub,) for sub in range(nsub) for ch in chains]
    idx = range(len(units))
    n_steps = nch // nsub

    @pl.when(j == 0)
    def _():
        for u, (sq, p, d) in enumerate(chains):
            s_scr[u] = s0_ref[sq, p, d] if has_s0 else jnp.zeros((n2, n2), F32)

    ri = lax.broadcasted_iota(jnp.int32, (n2, n2), 0)
    ci = lax.broadcasted_iota(jnp.int32, (n2, n2), 1)
    same = (ri >= CHUNK) == (ci >= CHUNK)
    incl = (same & (ri >= ci), same & (ri <= ci))
    strict = (same & (ri > ci), same & (ri < ci))
    eye = (ri == ci).astype(F32)
    ril, cil = ri % CHUNK, ci % CHUNK
    level_mask = []
    m = 1
    while m < CHUNK:
        level_mask.append(same & (ril // (2 * m) == cil // (2 * m)) & (ril // m != cil // m))
        m *= 2

    def col_stack(arr, ca, cb):
        top = jnp.broadcast_to(arr[:, ca:ca + 1], (CHUNK, LANES))
        bot = jnp.broadcast_to(arr[:, cb:cb + 1], (CHUNK, LANES))
        return jnp.concatenate([top, bot], axis=0)

    def row_stack(arr, row):
        top = jnp.broadcast_to(arr[row:row + 1, :], (CHUNK, LANES))
        bot = jnp.broadcast_to(arr[CHUNK + row:CHUNK + row + 1, :], (CHUNK, LANES))
        return jnp.concatenate([top, bot], axis=0)

    def rows_of(c, sq, d, sub):
        k = c * nsub + sub
        r = (sq * nch + (k if d == 0 else nch - 1 - k)) * CHUNK
        return slice(r, r + CHUNK)

    feats = [dict() for _ in range(n_steps)]
    solved = [dict() for _ in range(n_steps)]
    states = [s_scr[u] for u in range(len(chains))]
    dirs = [u[2] for u in units]

    def features(c):
        f = feats[c]
        src = lambda a, sq, p, d, sub: (fwd_refs, bwd_refs)[d][a][rows_of(c, sq, d, sub), :]
        bg = [src(3, *u) for u in units]
        b_c = [col_stack(bg[i], 8 * p + 2 * d, 8 * p + 2 * d + 1)
               for i, (sq, p, d, sub) in enumerate(units)]
        g_c = [col_stack(bg[i], 8 * p + 4 + 2 * d, 8 * p + 5 + 2 * d)
               for i, (sq, p, d, sub) in enumerate(units)]
        yield
        ld = lambda a, u: (fwd_refs, bwd_refs)[u[2]][a][rows_of(c, u[0], u[2], u[3]), u[1] * LANES:(u[1] + 1) * LANES]
        k2 = [jnp.concatenate([ld(1, u)] * 2, axis=0) for u in units]
        ks = [jnp.where(same, k2[i], 0.0) for i in idx]
        qs = [jnp.where(same, jnp.concatenate([ld(0, u)] * 2, axis=0), 0.0) for u in units]
        ks_b = [ks[i].astype(BF16) for i in idx]
        prod = [_dot_nt(jnp.concatenate([ks_b[i], qs[i].astype(BF16)], axis=0), ks_b[i]) for i in idx]
        yield
        decay = [jnp.where(incl[dirs[i]], jnp.exp(jnp.where(incl[dirs[i]], g_c[i] - g_c[i].T, 0.0)), 0.0) for i in idx]
        yield
        eg = [jnp.exp(g_c[i]) for i in idx]
        g_last = [row_stack(g_c[i], CHUNK - 1 if dirs[i] == 0 else 0) for i in idx]
        f["l"] = [jnp.where(strict[dirs[i]], b_c[i] * prod[i][:n2] * decay[i], 0.0) for i in idx]
        yield
        v2 = [jnp.concatenate([ld(2, u)] * 2, axis=0) for u in units]
        f["rhs"] = [(b_c[i] * jnp.where(same, v2[i], eg[i] * pltpu.roll(k2[i], CHUNK, axis=1))).astype(BF16)
                    for i in idx]
        yield
        f["at"] = [(prod[i][n2:] * decay[i]).astype(BF16) for i in idx]
        f["qg"] = [(qs[i] * eg[i]).astype(BF16) for i in idx]
        yield
        f["kt"] = [(ks[i] * jnp.exp(g_last[i] - g_c[i])).T.astype(BF16) for i in idx]
        f["gl"] = [jnp.exp(g_last[i]) for i in idx]
        yield

    def inverse(c):
        f, o = feats[c], solved[c]
        l_mat = f["l"]
        t_inv = [eye - jnp.where(level_mask[0], l_mat[i], 0.0) for i in idx]
        for lm in level_mask[1:]:
            t_b = [t_inv[i].astype(BF16) for i in idx]
            y = [_dot(jnp.where(lm, l_mat[i], 0.0).astype(BF16), t_b[i]).astype(BF16) for i in idx]
            yield
            t_inv = [t_inv[i] - _dot(t_b[i], y[i]) for i in idx]
            yield
        xs = [_dot(t_inv[i].astype(BF16), f["rhs"][i]) for i in idx]
        yield
        o["u"] = [jnp.where(same, xs[i], 0.0) for i in idx]
        o["wq"] = [jnp.concatenate([jnp.where(same, pltpu.roll(xs[i], CHUNK, axis=1), 0.0).astype(BF16),
                                    f["qg"][i]], axis=0) for i in idx]
        yield

    def scan(c):
        f, o = feats[c], solved[c]
        for sub in range(nsub):
            ui = [sub * len(chains) + k for k in range(len(chains))]
            ws = [_dot(o["wq"][i], states[k].astype(BF16)) for k, i in enumerate(ui)]
            yield
            u_b = [(o["u"][i] - ws[k][:n2]).astype(BF16) for k, i in enumerate(ui)]
            yield
            o_st = [ws[k][n2:] + _dot(f["at"][i], u_b[k]) for k, i in enumerate(ui)]
            for k, i in enumerate(ui):
                states[k] = f["gl"][i] * states[k] + _dot(f["kt"][i], u_b[k])
            yield
            for k, i in enumerate(ui):
                sq, p, d, _ = units[i]
                dst = of_ref if d == 0 else ob_ref
                dst[rows_of(c, sq, d, sub), p * LANES:(p + 1) * LANES] = o_st[k][:CHUNK] + o_st[k][CHUNK:]
            yield
        feats[c] = solved[c] = None

    for t in range(n_steps + 2):
        gens = []
        if 1 <= t <= n_steps:
            gens.append(inverse(t - 1))
        if t < n_steps:
            gens.append(features(t))
        if 2 <= t:
            gens.append(scan(t - 2))
        _run_interleaved(gens)

    for u in range(len(chains)):
        s_scr[u] = states[u]
    if has_sfin:
        @pl.when(j == pl.num_programs(1) - 1)
        def _():
            for u, (sq, p, d) in enumerate(chains):
                sfin_ref[sq, p, d] = states[u]


def _gdn_scan(feat, s0, *, t, row0, n_seq, want_state):
    tb = math.gcd(t, _SCAN_ROWS)
    nb = t // tb
    ns = math.gcd(n_seq, _SCAN_ROWS // tb) if nb == 1 else 1
    nsub = math.gcd(tb // CHUNK, max(1, _SCAN_UNITS // (ns * PAIRS)))
    rows = ns * tb
    assert row0 % rows == 0
    blk0 = row0 // rows
    n2 = 2 * CHUNK
    w = PAIRS * LANES
    has_s0 = s0 is not None

    def blocks(mirror):
        row = (lambda b, j: b * nb + (nb - 1 - j)) if mirror else (lambda b, j: b * nb + j)
        return ([pl.BlockSpec((rows, w), lambda b, j: (blk0 + row(b, j), 0))] * 3
                + [pl.BlockSpec((rows, LANES), lambda b, j: (blk0 + row(b, j), 0))]), row

    f_specs, f_row = blocks(False)
    b_specs, b_row = blocks(True)
    in_specs = f_specs + b_specs
    args = list(feat) * 2
    state_spec = pl.BlockSpec((ns, PAIRS, 2, n2, n2), lambda b, j: (b, 0, 0, 0, 0))
    if has_s0:
        in_specs.append(state_spec)
        args.append(s0)
    out_shape = [jax.ShapeDtypeStruct((n_seq * t, w), F32)] * 2
    out_specs = [pl.BlockSpec((rows, w), lambda b, j: (f_row(b, j), 0)),
                 pl.BlockSpec((rows, w), lambda b, j: (b_row(b, j), 0))]
    if want_state:
        out_shape.append(jax.ShapeDtypeStruct((n_seq, PAIRS, 2, n2, n2), F32))
        out_specs.append(state_spec)
    return pl.pallas_call(
        functools.partial(_gdn_scan_kernel, nch=tb // CHUNK, ns=ns, nsub=nsub, has_s0=has_s0, has_sfin=want_state),
        out_shape=out_shape,
        grid=(n_seq // ns, nb),
        in_specs=in_specs,
        out_specs=out_specs,
        scratch_shapes=[pltpu.VMEM((ns * 2 * PAIRS, n2, n2), F32)],
        compiler_params=_cparams(("parallel", "arbitrary"), 48),
        name="gdn_scan_ctx" if want_state else "gdn_scan_lat",
    )(*args)


def _prep_kernel(*refs, rope):
    bq_ref, cq_ref, bk_ref, bv_ref, ck_ref, cv_ref, qkn_ref = refs[:7]
    pos = 7
    cs = sn = None
    if rope:
        cs, sn = refs[7][...], refs[8][...]
        pos = 11
    qb_ref, qc_ref, kdb_ref, vdb_ref, kdc_ref, vdc_ref = refs[pos:pos + 6]
    pos += 6
    tr = bq_ref.shape[0]
    lane = lax.broadcasted_iota(jnp.int32, (tr, LANES), 1)
    lane_lo = lane < HEAD_DIM
    first_half = (lane % HEAD_DIM) < HEAD_DIM // 2
    qkn = qkn_ref[...]

    ones_bd = ((lax.broadcasted_iota(jnp.int32, (LANES, LANES), 0) < HEAD_DIM)
               == (lax.broadcasted_iota(jnp.int32, (LANES, LANES), 1) < HEAD_DIM)).astype(BF16)

    def norm(x, w_row):
        sq = x * x
        hi = sq.astype(BF16)
        lo = (sq - hi.astype(F32)).astype(BF16)
        ms = (_dot(hi, ones_bd) + _dot(lo, ones_bd)) * (1.0 / HEAD_DIM)
        return x * lax.rsqrt(ms + EPS) * w_row

    def rot(y):
        if not rope:
            return y
        partner = jnp.where(first_half, pltpu.roll(y, LANES - HEAD_DIM // 2, axis=1),
                            pltpu.roll(y, HEAD_DIM // 2, axis=1))
        return y * cs + partner * sn

    def dup(x, o_ref):
        lo = jnp.where(lane_lo, x, 0.0)
        hi = jnp.where(lane_lo, 0.0, x)
        d0 = (lo + pltpu.roll(lo, HEAD_DIM, axis=1)).astype(BF16)
        d1 = (hi + pltpu.roll(hi, HEAD_DIM, axis=1)).astype(BF16)
        if len(o_ref.shape) == 3:
            o_ref[0, :, 0:LANES] = d0
            o_ref[0, :, LANES:2 * LANES] = d1
        else:
            o_ref[:, 0:LANES] = d0
            o_ref[:, LANES:2 * LANES] = d1

    for q_ref, o_ref, wi in ((bq_ref, qb_ref, 0), (cq_ref, qc_ref, 2)):
        for g in range(PAIRS):
            y = rot(norm(q_ref[:, g * LANES:(g + 1) * LANES], qkn[wi:wi + 1, :])) * HEAD_DIM ** -0.5
            o_ref[2 * g] = jnp.where(lane_lo, y, 0.0).astype(BF16)
            o_ref[2 * g + 1] = jnp.where(lane_lo, 0.0, y).astype(BF16)
    kb_n = norm(bk_ref[...], qkn[1:2, :])
    kc_n = norm(ck_ref[...], qkn[3:4, :])
    if not rope:
        kbn_ref, vbn_ref, kcn_ref, vcn_ref = refs[pos:pos + 4]
        kbn_ref[...] = kb_n
        vbn_ref[...] = bv_ref[...]
        kcn_ref[...] = kc_n
        vcn_ref[...] = cv_ref[...]
    dup(rot(kb_n), kdb_ref)
    dup(bv_ref[...], vdb_ref)
    dup(rot(kc_n), kdc_ref)
    dup(cv_ref[...], vdc_ref)


def _attn_prep(proj, qkn_rows, rope_tables, glob_kv, *, row0, n_rows, t):
    rope = rope_tables is not None
    tr = math.gcd(t, 512)
    blk0 = row0 // tr

    def col(g, width):
        gw = width // LANES
        return pl.BlockSpec((tr, width), lambda i, g=g, gw=gw: (blk0 + i, g // gw))

    in_specs = [col(G_BQ, 4 * LANES), col(G_CQ, 4 * LANES), col(G_BK, LANES), col(G_BV, LANES),
                col(G_CK, LANES), col(G_CV, LANES), pl.BlockSpec((4, LANES), lambda i: (0, 0))]
    args = [proj] * 6 + [qkn_rows]
    q_shape = jax.ShapeDtypeStruct((HEADS, n_rows, LANES), BF16)
    kv_shape = jax.ShapeDtypeStruct((n_rows, 2 * LANES), BF16)
    q_spec = pl.BlockSpec((HEADS, tr, LANES), lambda i: (0, i, 0))
    kv_spec = pl.BlockSpec((tr, 2 * LANES), lambda i: (i, 0))
    aliases = {}
    if rope:
        per_seq = t // tr
        in_specs += [pl.BlockSpec((tr, LANES), lambda i: (i % per_seq, 0))] * 2 + [_any_spec()] * 2
        aliases = {len(args) + 2: 2, len(args) + 3: 3}
        args += list(rope_tables) + list(glob_kv)
        past = glob_kv[0].shape[1] - t
        assert past % tr == 0
        glob_spec = pl.BlockSpec((1, tr, 2 * LANES), lambda i: (i // per_seq, past // tr + i % per_seq, 0))
        glob_shape = jax.ShapeDtypeStruct(glob_kv[0].shape, BF16)
        out_shape = [q_shape, q_shape, glob_shape, glob_shape, kv_shape, kv_shape]
        out_specs = [q_spec, q_spec, glob_spec, glob_spec, kv_spec, kv_spec]
    else:
        out_shape = [q_shape, q_shape] + [kv_shape] * 4 + [jax.ShapeDtypeStruct((n_rows, LANES), F32)] * 4
        out_specs = [q_spec, q_spec] + [kv_spec] * 4 + [pl.BlockSpec((tr, LANES), lambda i: (i, 0))] * 4
    return pl.pallas_call(
        functools.partial(_prep_kernel, rope=rope),
        out_shape=out_shape,
        grid=(n_rows // tr,),
        in_specs=in_specs,
        out_specs=out_specs,
        input_output_aliases=aliases,
        compiler_params=_cparams(("parallel",), 32),
        name="attn_prep_lat" if rope else "attn_prep_ctx",
    )(*args)


def _stacked_q(q_ref, g, tq):
    return q_ref[GROUP_HEADS * g:GROUP_HEADS * (g + 1)].reshape(GROUP_HEADS * tq, LANES)


def _unstack_group(o_st, tq, lane_lo):
    a = jnp.where(lane_lo, o_st[0:tq], o_st[tq:2 * tq])
    b = jnp.where(lane_lo, o_st[2 * tq:3 * tq], o_st[3 * tq:4 * tq])
    return jnp.concatenate([a, b], axis=1)


def _with_ones(v, ones):
    return jnp.concatenate([v, ones], axis=1)


def _sink_col(sink_row, g, tq):
    parts = [jnp.broadcast_to(sink_row[:, GROUP_HEADS * g + j:GROUP_HEADS * g + j + 1], (tq, 1))
             for j in range(GROUP_HEADS)]
    return jnp.concatenate(parts, axis=0)


def _attn_ctx_kernel(qb_ref, kdb_ref, vdb_ref, qc_ref, kdc_ref, vdc_ref, sink_ref, yb_ref, yc_ref):
    tq = qb_ref.shape[1]
    lane_lo_q = _lane_lo(tq)
    sink_row = sink_ref[...]
    groups = range(KV_HEADS)
    gs = [slice(g * LANES, (g + 1) * LANES) for g in groups]
    jobs = [(qb_ref, kdb_ref, vdb_ref, yb_ref, False, g) for g in groups]
    jobs += [(qc_ref, kdc_ref, vdc_ref, yc_ref, True, g) for g in groups]
    s = [_dot_nt(_stacked_q(q_ref, g, tq), kd_ref[:, gs[g]]) for q_ref, kd_ref, _, _, _, g in jobs]
    for si, (_, _, vd_ref, y_ref, use_sink, g) in zip(s, jobs):
        m = jnp.max(si, axis=-1, keepdims=True)
        if use_sink:
            snk = _sink_col(sink_row, g, tq)
            m = jnp.maximum(m, snk)
        p = jnp.exp(si - m)
        l = jnp.sum(p, axis=-1, keepdims=True)
        if use_sink:
            l = l + jnp.exp(snk - m)
        o = _dot(p.astype(BF16), vd_ref[:, gs[g]]) / l
        y_ref[:, 2 * g * LANES:(2 * g + 2) * LANES] = _unstack_group(o, tq, lane_lo_q).astype(y_ref.dtype)


def _attn_ctx(qb, kdb, vdb, qc, kdc, vdc, sink_row, *, t):
    n = kdb.shape[0]
    q_spec = pl.BlockSpec((HEADS, t, LANES), lambda b: (0, b, 0))
    kv_spec = pl.BlockSpec((t, 2 * LANES), lambda b: (b, 0))
    y_spec = pl.BlockSpec((t, 4 * LANES), lambda b: (b, 0))
    return pl.pallas_call(
        _attn_ctx_kernel,
        out_shape=[jax.ShapeDtypeStruct((n, 4 * LANES), BF16)] * 2,
        grid=(n // t,),
        in_specs=[q_spec, kv_spec, kv_spec, q_spec, kv_spec, kv_spec,
                  pl.BlockSpec((1, LANES), lambda b: (0, 0))],
        out_specs=[y_spec, y_spec],
        compiler_params=_cparams(("parallel",), 32),
        name="attn_ctx",
    )(qb, kdb, vdb, qc, kdc, vdc, sink_row)


_TQ_B = 256
_TK_B = 1536


def _attn_glob_kernel(q_ref, kd_ref, vd_ref, y_ref, s_scr, *, tk):
    tq = q_ref.shape[1]
    n_blk = kd_ref.shape[1] // tk
    groups = range(KV_HEADS)
    gs = [slice(g * LANES, (g + 1) * LANES) for g in groups]
    q = [_stacked_q(q_ref, g, tq) for g in groups]

    def scores_into(j, slot):
        for g in groups:
            s_scr[slot, g] = _dot_nt(q[g], kd_ref[0, j * tk:(j + 1) * tk, gs[g]])

    rows = GROUP_HEADS * tq
    m = [jnp.full((rows, LANES), NEG, F32) for _ in groups]
    l = [jnp.zeros((rows, LANES), F32) for _ in groups]
    acc = [jnp.zeros((rows, LANES), F32) for _ in groups]
    scores_into(0, 0)
    for j in range(n_blk):
        if j + 1 < n_blk:
            scores_into(j + 1, (j + 1) % 2)
        for g in groups:
            s = s_scr[j % 2, g]
            m_new = jnp.maximum(m[g], jnp.max(s, axis=-1, keepdims=True))
            p = jnp.exp(s - m_new[:, 0:1])
            alpha = jnp.exp(m[g] - m_new)
            l[g] = alpha * l[g] + jnp.sum(p, axis=-1, keepdims=True)
            acc[g] = alpha * acc[g] + _dot(p.astype(BF16), vd_ref[0, j * tk:(j + 1) * tk, gs[g]])
            m[g] = m_new
    lane_lo_q = _lane_lo(tq)
    for g in groups:
        o = acc[g] / l[g]
        y_ref[:, 2 * g * LANES:(2 * g + 2) * LANES] = _unstack_group(o, tq, lane_lo_q).astype(y_ref.dtype)


def _attn_glob(q, kd, vd, *, t):
    n = q.shape[1]
    s_len = kd.shape[1]
    tq, tk = math.gcd(t, _TQ_B), math.gcd(s_len, _TK_B)
    per_seq = t // tq
    kv_spec = pl.BlockSpec((1, s_len, 2 * LANES), lambda b, i: (b, 0, 0))
    return pl.pallas_call(
        functools.partial(_attn_glob_kernel, tk=tk),
        out_shape=jax.ShapeDtypeStruct((n, 4 * LANES), BF16),
        grid=(n // t, per_seq),
        in_specs=[pl.BlockSpec((HEADS, tq, LANES), lambda b, i: (0, b * per_seq + i, 0)), kv_spec, kv_spec],
        out_specs=pl.BlockSpec((tq, 4 * LANES), lambda b, i: (b * per_seq + i, 0)),
        scratch_shapes=[pltpu.VMEM((2, KV_HEADS, GROUP_HEADS * tq, tk), F32)],
        compiler_params=_cparams(("parallel", "parallel"), 56),
        name="attn_glob",
    )(q, kd, vd)


def _attn_win_kernel(q_ref, kx_ref, vx_ref, kp_ref, kc_ref, kn_ref, vp_ref, vc_ref, vn_ref,
                     sink_ref, y_ref):
    i = pl.program_id(1)
    nblk = pl.num_programs(1)
    tq = q_ref.shape[1]
    lane_lo_q = _lane_lo(tq)
    sink_row = sink_ref[...]
    rr = lax.broadcasted_iota(jnp.int32, (GROUP_HEADS * tq, tq), 0) % tq
    cc = lax.broadcasted_iota(jnp.int32, (GROUP_HEADS * tq, tq), 1)
    mask_prev = (cc >= rr) & (i > 0)
    mask_next = (cc <= rr) & (i < nblk - 1)
    groups = range(KV_HEADS)
    gs = [slice(g * LANES, (g + 1) * LANES) for g in groups]
    qs = [_stacked_q(q_ref, g, tq) for g in groups]
    p_len = kx_ref.shape[1]
    s = [jnp.concatenate([_dot_nt(qs[g], kx_ref[0, :, gs[g]]),
                          jnp.where(mask_prev, _dot_nt(qs[g], kp_ref[:, gs[g]]), NEG),
                          _dot_nt(qs[g], kc_ref[:, gs[g]]),
                          jnp.where(mask_next, _dot_nt(qs[g], kn_ref[:, gs[g]]), NEG)], axis=1) for g in groups]
    ones_x = jnp.ones((p_len, LANES), BF16)
    ones_b = jnp.ones((tq, LANES), BF16)
    for g in groups:
        snk = _sink_col(sink_row, g, tq)
        m = jnp.maximum(jnp.max(s[g], axis=-1, keepdims=True), snk)
        p = jnp.exp(s[g] - m).astype(BF16)
        ol = _dot(p[:, :p_len], _with_ones(vx_ref[0, :, gs[g]], ones_x))
        for b, v_ref in enumerate((vp_ref, vc_ref, vn_ref)):
            ol = ol + _dot(p[:, p_len + b * tq:p_len + (b + 1) * tq], _with_ones(v_ref[:, gs[g]], ones_b))
        o = ol[:, :LANES] / (ol[:, LANES:] + jnp.exp(snk - m))
        y_ref[:, 2 * g * LANES:(2 * g + 2) * LANES] = _unstack_group(o, tq, lane_lo_q).astype(y_ref.dtype)


def _attn_win(q, kd, vd, kx, vx, sink_row, *, t):
    n = q.shape[1]
    p_len = kx.shape[1]
    tq = WINDOW
    nblk = t // tq
    x_spec = pl.BlockSpec((1, p_len, 2 * LANES), lambda b, i: (b, 0, 0))
    prev = pl.BlockSpec((tq, 2 * LANES), lambda b, i: (b * nblk + jnp.maximum(i - 1, 0), 0))
    cur = pl.BlockSpec((tq, 2 * LANES), lambda b, i: (b * nblk + i, 0))
    nxt = pl.BlockSpec((tq, 2 * LANES), lambda b, i: (b * nblk + jnp.minimum(i + 1, nblk - 1), 0))
    return pl.pallas_call(
        _attn_win_kernel,
        out_shape=jax.ShapeDtypeStruct((n, 4 * LANES), BF16),
        grid=(n // t, nblk),
        in_specs=[pl.BlockSpec((HEADS, tq, LANES), lambda b, i: (0, b * nblk + i, 0)),
                  x_spec, x_spec, prev, cur, nxt, prev, cur, nxt,
                  pl.BlockSpec((1, LANES), lambda b, i: (0, 0))],
        out_specs=pl.BlockSpec((tq, 4 * LANES), lambda b, i: (b * nblk + i, 0)),
        compiler_params=_cparams(("parallel", "parallel"), 32),
        name="attn_win",
    )(q, kx, vx, kd, kd, kd, vd, vd, vd, sink_row)


def _mix_kernel(x_ref, ofc_ref, obc_ref, ybc_ref, ycc_ref, ofl_ref, obl_ref, ybl_ref, ycl_ref, z_ref,
                ga_ref, gb_ref, gc_ref, mod_ref, anorm_ref, wa_ref, wb_ref, wc_ref, wo_ref, o_ref, *, ctx_tiles):
    d = x_ref.shape[1]
    tm = x_ref.shape[0]
    lane_lo = _lane_lo(tm)
    anorm = anorm_ref[...]
    is_ctx = pl.program_id(0) < ctx_tiles

    def pick(c_ref, l_ref, cols=slice(None)):
        return jnp.where(is_ctx, c_ref[:, cols], l_ref[:, cols])

    ya = []
    for g in range(PAIRS):
        gl = slice(g * LANES, (g + 1) * LANES)
        o = pick(ofc_ref, ofl_ref, gl) + pick(obc_ref, obl_ref, gl)
        ms = _half_sums(o * o, lane_lo) * (1.0 / HEAD_DIM)
        ya.append((o * lax.rsqrt(ms + EPS) * anorm * _silu(z_ref[:, gl])).astype(BF16))
    ya = jnp.concatenate(ya, axis=1)
    merged = (_sigmoid(ga_ref[...]) * _dot(ya, wa_ref[...])
              + _sigmoid(gb_ref[...]) * _dot(pick(ybc_ref, ybl_ref), wb_ref[...])
              + _sigmoid(gc_ref[...]) * _dot(pick(ycc_ref, ycl_ref), wc_ref[...]))
    gate1 = mod_ref[0][:, 2 * d:3 * d]
    o_ref[...] = x_ref[...] + gate1 * _dot(merged.astype(BF16), wo_ref[...])


def _mix(x, ctx_parts, lat_parts, proj, mods, anorm_row, wa, wb, wc, wo, n_ctx, dec_seq):
    nt, d = x.shape
    tm = _row_tile(n_ctx, dec_seq, 512)
    ctx_tiles = n_ctx // tm
    w = PAIRS * LANES
    c_spec = pl.BlockSpec((tm, w), lambda i: (jnp.minimum(i, ctx_tiles - 1), 0))
    l_spec = pl.BlockSpec((tm, w), lambda i: (jnp.maximum(i - ctx_tiles, 0), 0))
    gw = d // LANES

    def gate(k):
        return pl.BlockSpec((tm, d), lambda i, k=k: (i, G_GATES // gw + k))

    def full(a):
        return pl.BlockSpec(a.shape, lambda i: (0, 0))

    return pl.pallas_call(
        functools.partial(_mix_kernel, ctx_tiles=ctx_tiles),
        out_shape=jax.ShapeDtypeStruct((nt, d), F32),
        grid=(nt // tm,),
        in_specs=[pl.BlockSpec((tm, d), lambda i: (i, 0))] + [c_spec] * 4 + [l_spec] * 4
                 + [pl.BlockSpec((tm, w), lambda i: (i, G_Z // PAIRS)), gate(0), gate(1), gate(2),
                    pl.BlockSpec((1, 1, 6 * d), lambda i: (_mod_index(i, tm, n_ctx, dec_seq), 0, 0)),
                    full(anorm_row), full(wa), full(wb), full(wc), full(wo)],
        out_specs=pl.BlockSpec((tm, d), lambda i: (i, 0)),
        compiler_params=_cparams(("parallel",), 48),
        name="mix_out",
    )(x, *ctx_parts, *lat_parts, proj, proj, proj, proj, mods, anorm_row, wa, wb, wc, wo)


_FF_CHUNK = 1024


def _ffn_kernel(x_ref, mod_ref, ln_ref, w1_ref, w2_ref, o_ref):
    d = x_ref.shape[1]
    m = mod_ref[0]
    x = x_ref[...]
    h = _mod_norm(x, ln_ref[...], m[:, 3 * d:4 * d], m[:, 4 * d:5 * d]).astype(BF16)
    acc = None
    for f in range(0, w1_ref.shape[1], _FF_CHUNK):
        hid = jnp.maximum(_dot(h, w1_ref[:, f:f + _FF_CHUNK]), 0.0)
        part = _dot((hid * hid).astype(BF16), w2_ref[f:f + _FF_CHUNK, :])
        acc = part if acc is None else acc + part
    o_ref[...] = x + m[:, 5 * d:6 * d] * acc


def _ffn(x, mods, ln2, w1, w2, n_ctx, dec_seq):
    nt, d = x.shape
    tm = _row_tile(n_ctx, dec_seq, 512)
    return pl.pallas_call(
        _ffn_kernel,
        out_shape=jax.ShapeDtypeStruct((nt, d), F32),
        grid=(nt // tm,),
        in_specs=[pl.BlockSpec((tm, d), lambda i: (i, 0)),
                  pl.BlockSpec((1, 1, 6 * d), lambda i: (_mod_index(i, tm, n_ctx, dec_seq), 0, 0)),
                  pl.BlockSpec((1, d), lambda i: (0, 0)),
                  pl.BlockSpec(w1.shape, lambda i: (0, 0)),
                  pl.BlockSpec(w2.shape, lambda i: (0, 0))],
        out_specs=pl.BlockSpec((tm, d), lambda i: (i, 0)),
        compiler_params=_cparams(("parallel",), 56),
        name="ffn",
    )(x, mods, ln2, w1, w2)


def _ba_lane_order():
    cols = []
    for p in range(PAIRS):
        for base in (0, 2 * HEADS):
            for d in range(2):
                cols += [base + d * HEADS + 2 * p, base + d * HEADS + 2 * p + 1]
    return cols


def _reorder_w_in(w_in):
    a_w = HEADS * HEAD_DIM
    kvw = KV_HEADS * HEAD_DIM
    o_beta = 4 * a_w
    o_bq = o_beta + 4 * HEADS
    o_bk = o_bq + a_w
    o_bv = o_bk + kvw
    o_cq = o_bv + kvw
    o_ck = o_cq + a_w
    o_cv = o_ck + kvw
    o_g = o_cv + kvw
    seg = lambda a, n: w_in[:, :, a:a + n]
    ba = jnp.take(seg(o_beta, 4 * HEADS), jnp.array(_ba_lane_order(), jnp.int32), axis=2)
    ba = jnp.pad(ba, ((0, 0), (0, 0), (0, 2 * LANES - ba.shape[2])))
    out = jnp.concatenate([seg(0, o_beta), seg(o_bq, a_w), seg(o_cq, a_w), seg(o_g, w_in.shape[2] - o_g),
                           seg(o_bk, kvw), seg(o_bv, kvw), seg(o_ck, kvw), seg(o_cv, kvw), ba], axis=2)
    assert out.shape[2] == IN_COLS_PAD
    return out.astype(BF16)


def _decay_rows(vals):
    rows = []
    for p in range(PAIRS):
        r = jnp.stack([vals[:, 0, 2 * p], vals[:, 0, 2 * p + 1], vals[:, 1, 2 * p], vals[:, 1, 2 * p + 1]], axis=-1)
        rows.append(jnp.pad(r, ((0, 0), (4, 0))))
    out = jnp.concatenate(rows, axis=-1)
    return jnp.pad(out, ((0, 0), (0, LANES - out.shape[1])))[:, None, :]


def _rope_tables(n_tokens):
    rows = n_tokens // GRID_W
    row_id = jnp.repeat(jnp.arange(rows, dtype=F32), GRID_W)
    col_id = jnp.tile(jnp.arange(GRID_W, dtype=F32), rows)
    n_freq = HEAD_DIM // 4
    inv_freq = ROPE_THETA ** (-jnp.arange(n_freq, dtype=F32) / n_freq)
    ang = jnp.concatenate([row_id[:, None] * inv_freq, col_id[:, None] * inv_freq], axis=-1)
    cos, sin = jnp.cos(ang), jnp.sin(ang)
    cs = jnp.tile(cos, (1, 4))
    sn = jnp.tile(jnp.concatenate([-sin, sin], axis=-1), (1, 2))
    return cs, sn


def _dup_kv(x):
    return jnp.concatenate([x[:, :, 0], x[:, :, 0], x[:, :, 1], x[:, :, 1]], axis=-1).astype(BF16)


def _block_diag_states(s):
    b = s.shape[0]
    s = s.reshape(b, 2, PAIRS, 2, HEAD_DIM, HEAD_DIM).transpose(0, 2, 1, 3, 4, 5)
    z = jnp.zeros_like(s[:, :, :, 0])
    top = jnp.concatenate([s[:, :, :, 0], z], axis=-1)
    bot = jnp.concatenate([z, s[:, :, :, 1]], axis=-1)
    return jnp.concatenate([top, bot], axis=-2)


def _diag_states(s):
    a = s[:, :, :, :HEAD_DIM, :HEAD_DIM]
    bb = s[:, :, :, HEAD_DIM:, HEAD_DIM:]
    out = jnp.stack([a, bb], axis=3)
    b = s.shape[0]
    return out.transpose(0, 2, 1, 3, 4, 5).reshape(b, 2, HEADS, HEAD_DIM, HEAD_DIM)


def kernel(x_prompt, x_sample, cache_k_glob, cache_v_glob, cache_k_win, cache_v_win, state_delta, c, c_ctx, w_mod, b_mod, ln1, ln2, w_in, conv_qkv, a_log, dt_bias, a_norm, qk_norm, sink, w_br_a, w_br_b, w_br_c, w_o, w_ff1, w_ff2):
    batch, seq, d = x_prompt.shape
    dec_batch, dec_seq, _ = x_sample.shape
    depth = w_mod.shape[0]
    n_ctx = batch * seq
    n_lat = dec_batch * dec_seq
    nt = n_ctx + n_lat
    assert 1 + dec_batch <= SUBLANES

    w_in_r = _reorder_w_in(w_in)
    wa, wb, wc, wo = (w.astype(BF16) for w in (w_br_a, w_br_b, w_br_c, w_o))
    w1, w2 = w_ff1.astype(BF16), w_ff2.astype(BF16)
    alog_rows = _decay_rows(a_log)
    dtb_rows = _decay_rows(dt_bias)
    anorm_rows = jnp.tile(a_norm, (1, 2))[:, None, :]
    qkn_rows = jnp.tile(qk_norm, (1, 1, 2))
    sink_rows = jnp.pad(sink, ((0, 0), (0, LANES - HEADS)))[:, None, :]
    rope_tables = _rope_tables(dec_seq)
    cvec = jnp.concatenate([c_ctx[None], c, jnp.zeros((SUBLANES - 1 - dec_batch, d), F32)], axis=0)
    mods_all = _mod_vectors(cvec, w_mod, b_mod)
    pad_lat = ((0, 0), (0, dec_seq), (0, 0))

    x = jnp.concatenate([x_prompt.reshape(n_ctx, d), x_sample.reshape(n_lat, d)], axis=0)
    new_kg, new_vg, new_kw, new_vw, new_st = [], [], [], [], []
    for l in range(depth):
        mods = mods_all[l][:, None, :]
        proj = _in_proj(x, mods, ln1[l][None], w_in_r[l], n_ctx, dec_seq)

        feat = _gdn_prep(proj, conv_qkv[l], alog_rows[l], dtb_rows[l], n_ctx, seq, dec_seq)
        of_c, ob_c, s_fin = _gdn_scan(feat, None, t=seq, row0=0, n_seq=batch, want_state=True)
        of_l, ob_l = _gdn_scan(feat, _block_diag_states(state_delta[:, l]),
                               t=dec_seq, row0=n_ctx, n_seq=dec_batch, want_state=False)

        qb_c, qc_c, kdb_c, vdb_c, kdc_c, vdc_c, kb_n, vb_n, kc_n, vc_n = _attn_prep(
            proj, qkn_rows[l], None, None, row0=0, n_rows=n_ctx, t=seq)
        glob_kv = (jnp.pad(_dup_kv(cache_k_glob[:, l]), pad_lat), jnp.pad(_dup_kv(cache_v_glob[:, l]), pad_lat))
        qb_l, qc_l, kd_all, vd_all, kdc_l, vdc_l = _attn_prep(
            proj, qkn_rows[l], rope_tables, glob_kv, row0=n_ctx, n_rows=n_lat, t=dec_seq)

        yb_c, yc_c = _attn_ctx(qb_c, kdb_c, vdb_c, qc_c, kdc_c, vdc_c, sink_rows[l], t=seq)
        yb_l = _attn_glob(qb_l, kd_all, vd_all, t=dec_seq)
        yc_l = _attn_win(qc_l, kdc_l, vdc_l, _dup_kv(cache_k_win[:, l]), _dup_kv(cache_v_win[:, l]),
                         sink_rows[l], t=dec_seq)

        x = _mix(x, (of_c, ob_c, yb_c, yc_c), (of_l, ob_l, yb_l, yc_l), proj, mods, anorm_rows[l],
                 wa[l], wb[l], wc[l], wo[l], n_ctx, dec_seq)
        x = _ffn(x, mods, ln2[l][None], w1[l], w2[l], n_ctx, dec_seq)

        kv_shape = (batch, seq, KV_HEADS, HEAD_DIM)
        new_kg.append(kb_n.reshape(kv_shape))
        new_vg.append(vb_n.reshape(kv_shape))
        new_kw.append(kc_n.reshape(kv_shape))
        new_vw.append(vc_n.reshape(kv_shape))
        new_st.append(_diag_states(s_fin))

    y_prompt = x[:n_ctx].reshape(batch, seq, d)
    y_sample = x[n_ctx:].reshape(dec_batch, dec_seq, d)
    return (y_prompt, y_sample, jnp.stack(new_kg, axis=1), jnp.stack(new_vg, axis=1),
            jnp.stack(new_kw, axis=1), jnp.stack(new_vw, axis=1), jnp.stack(new_st, axis=1))
```

```python
import functools
import math

import jax
import jax.numpy as jnp
from jax import lax
from jax.experimental import pallas as pl
from jax.experimental.pallas import tpu as pltpu

F32 = jnp.float32
BF16 = jnp.bfloat16

LANES = 128
SUBLANES = 8

HEAD_DIM = 64
HEADS = 8
KV_HEADS = 2
GROUP_HEADS = HEADS // KV_HEADS
CHUNK = 64
CONV_K = 5
GRID_W = 64
WINDOW = 128
ROPE_THETA = 10000.0
EPS = 1e-6
NEG = -1e30
PAIRS = HEADS // 2

G_Q, G_K, G_V, G_Z = 0, 4, 8, 12
G_BQ, G_CQ = 16, 20
G_GATES = 24
G_BK, G_BV, G_CK, G_CV = 48, 49, 50, 51
G_BA = 52
N_GROUPS = 54
IN_COLS_PAD = N_GROUPS * LANES


VMEM_MIB_V7X = 64
MXU_TILE = 256

_VMEM_MIB = {
    "mod_vectors": 40,
    "in_proj": 48,
    "gdn_prep": 32,
    "gdn_scan": 48,
    "attn_prep": 32,
    "attn_ctx": 32,
    "attn_glob": 56,
    "attn_win": 32,
    "mix_out": 48,
    "ffn": 56,
}
assert max(_VMEM_MIB.values()) < VMEM_MIB_V7X


def _cparams(sem, call):
    return pltpu.CompilerParams(dimension_semantics=sem, vmem_limit_bytes=_VMEM_MIB[call] << 20)


def _dot(a, b):
    return jnp.dot(a, b, preferred_element_type=F32)


def _dot_nt(a, b):
    return lax.dot_general(a, b, (((1,), (1,)), ((), ())), preferred_element_type=F32)


def _sigmoid(x):
    return 1.0 / (1.0 + jnp.exp(-x))


def _silu(x):
    return x * _sigmoid(x)


def _softplus(x):
    return jnp.maximum(x, 0.0) + jnp.log1p(jnp.exp(-jnp.abs(x)))


def _half_sums(x, lane_lo):
    s_lo = jnp.sum(jnp.where(lane_lo, x, 0.0), axis=-1, keepdims=True)
    s_hi = jnp.sum(jnp.where(lane_lo, 0.0, x), axis=-1, keepdims=True)
    return jnp.where(lane_lo, s_lo, s_hi)


def _lane_lo(rows):
    return lax.broadcasted_iota(jnp.int32, (rows, LANES), 1) < HEAD_DIM


def _row_tile(n_ctx, dec_seq, pref):
    return math.gcd(math.gcd(n_ctx, dec_seq), pref)


def _mod_index(i, tm, n_ctx, dec_seq):
    ctx_tiles = n_ctx // tm
    return jnp.where(i < ctx_tiles, 0, 1 + (i - ctx_tiles) // (dec_seq // tm))


def _any_spec():
    return pl.BlockSpec(memory_space=pl.ANY)


def _mod_kernel(c_ref, w_ref, b_ref, o_ref):
    s = _silu(c_ref[...])
    w = w_ref[0]
    s_hi = s.astype(BF16)
    s_lo = (s - s_hi.astype(F32)).astype(BF16)
    w_hi = w.astype(BF16)
    w_lo = (w - w_hi.astype(F32)).astype(BF16)
    acc = _dot(s_hi, w_hi) + (_dot(s_lo, w_hi) + _dot(s_hi, w_lo))
    o_ref[0] = acc + b_ref[0]


def _mod_vectors(cvec, w_mod, b_mod):
    depth, d, n = w_mod.shape
    tn = n // 4
    return pl.pallas_call(
        _mod_kernel,
        out_shape=jax.ShapeDtypeStruct((depth, SUBLANES, n), F32),
        grid=(depth, n // tn),
        in_specs=[pl.BlockSpec((SUBLANES, d), lambda l, j: (0, 0)),
                  pl.BlockSpec((1, d, tn), lambda l, j: (l, 0, j)),
                  pl.BlockSpec((1, 1, tn), lambda l, j: (l, 0, j))],
        out_specs=pl.BlockSpec((1, SUBLANES, tn), lambda l, j: (l, 0, j)),
        compiler_params=_cparams(("parallel", "parallel"), "mod_vectors"),
        name="mod_vectors",
    )(cvec, w_mod, b_mod.reshape(depth, 1, n))


def _mod_norm(x, ln, shift, scale):
    ms = jnp.mean(x * x, axis=-1, keepdims=True)
    return (x * lax.rsqrt(ms + EPS) * ln) * (1.0 + scale) + shift


def _inproj_kernel(x_ref, mod_ref, ln_ref, w_ref, o_ref, h_scr):
    d = x_ref.shape[1]

    @pl.when(pl.program_id(1) == 0)
    def _():
        m = mod_ref[0]
        h = _mod_norm(x_ref[...], ln_ref[...], m[:, 0:d], m[:, d:2 * d])
        h_scr[...] = h.astype(BF16)

    o_ref[...] = _dot(h_scr[...], w_ref[...])


def _in_proj(x, mods, ln1, w_in, n_ctx, dec_seq):
    nt, d = x.shape
    n = w_in.shape[1]
    tm = _row_tile(n_ctx, dec_seq, 1024)
    tn = n // 3
    assert tn % MXU_TILE == 0
    return pl.pallas_call(
        _inproj_kernel,
        out_shape=jax.ShapeDtypeStruct((nt, n), F32),
        grid=(nt // tm, n // tn),
        in_specs=[pl.BlockSpec((tm, d), lambda i, j: (i, 0)),
                  pl.BlockSpec((1, 1, 6 * d), lambda i, j: (_mod_index(i, tm, n_ctx, dec_seq), 0, 0)),
                  pl.BlockSpec((1, d), lambda i, j: (0, 0)),
                  pl.BlockSpec((d, tn), lambda i, j: (0, j))],
        out_specs=pl.BlockSpec((tm, tn), lambda i, j: (i, j)),
        scratch_shapes=[pltpu.VMEM((tm, d), BF16)],
        compiler_params=_cparams(("parallel", "arbitrary"), "in_proj"),
        name="in_proj",
    )(x, mods, ln1, w_in)


_TB = 256
_SCAN_ROWS = 512
_SCAN_UNITS = 8


def _gdn_prep_kernel(q_ref, k_ref, v_ref, qp_ref, kp_ref, vp_ref, qx_ref, kx_ref, vx_ref, ba_ref,
                     cw_ref, alog_ref, dtb_ref, qo_ref, ko_ref, vo_ref, bg_ref, ext_scr, *, n_ctx, seq, dec_seq):
    rows = q_ref.shape[0]
    r0 = pl.program_id(0) * rows
    in_ctx = r0 < n_ctx
    off = jnp.where(in_ctx, r0, r0 - n_ctx)
    length = jnp.where(in_ctx, seq, dec_seq)
    is_start = lax.rem(off, length) == 0
    is_end = lax.rem(off + rows, length) == 0
    lane_lo = _lane_lo(rows)
    cw = cw_ref[...]
    width = q_ref.shape[1]

    def conv(x_ref, prev_ref, next_ref, o_ref, a, norm_scale):
        ext_scr[0:SUBLANES, :] = jnp.where(is_start, 0.0, prev_ref[...])
        ext_scr[SUBLANES:SUBLANES + rows, :] = x_ref[...]
        ext_scr[SUBLANES + rows:, :] = jnp.where(is_end, 0.0, next_ref[...])
        acc = None
        for j in range(CONV_K):
            o = SUBLANES - CONV_K // 2 + j
            term = ext_scr[o:o + rows, :] * cw[j:j + 1, a * width:(a + 1) * width]
            acc = term if acc is None else acc + term
        y = _silu(acc)
        for g in range(width // LANES):
            yg = y[:, g * LANES:(g + 1) * LANES]
            if norm_scale is not None:
                yg = yg * lax.rsqrt(_half_sums(yg * yg, lane_lo) + EPS) * norm_scale
            o_ref[:, g * LANES:(g + 1) * LANES] = yg

    conv(q_ref, qp_ref, qx_ref, qo_ref, 0, HEAD_DIM ** -0.5)
    conv(k_ref, kp_ref, kx_ref, ko_ref, 1, 1.0)
    conv(v_ref, vp_ref, vx_ref, vo_ref, 2, None)

    x = ba_ref[...]
    g = -jnp.exp(alog_ref[...]) * _softplus(x + dtb_ref[...])
    row_in_chunk = lax.broadcasted_iota(jnp.int32, (rows, LANES), 0) % CHUNK
    pre = suf = g
    s = 1
    while s < CHUNK:
        pre = pre + jnp.where(row_in_chunk >= s, pltpu.roll(pre, s, axis=0), 0.0)
        suf = suf + jnp.where(row_in_chunk + s < CHUNK, pltpu.roll(suf, rows - s, axis=0), 0.0)
        s *= 2
    c = lax.broadcasted_iota(jnp.int32, (rows, LANES), 1) % 8
    bg_ref[...] = jnp.where(c < 4, _sigmoid(x), jnp.where(c < 6, pre, suf))


def _gdn_prep(proj, conv_w, alog_row, dtb_row, n_ctx, seq, dec_seq):
    nt = proj.shape[0]
    rows = _TB
    assert seq % rows == 0 and dec_seq % rows == 0
    w = PAIRS * LANES
    per = rows // SUBLANES
    last = nt // SUBLANES - 1
    main = [pl.BlockSpec((rows, w), lambda i, a=a: (i, a)) for a in range(3)]
    prev = [pl.BlockSpec((SUBLANES, w), lambda i, a=a: (jnp.maximum(i * per - 1, 0), a)) for a in range(3)]
    nxt = [pl.BlockSpec((SUBLANES, w), lambda i, a=a: (jnp.minimum((i + 1) * per, last), a)) for a in range(3)]
    row = pl.BlockSpec((1, LANES), lambda i: (0, 0))
    return pl.pallas_call(
        functools.partial(_gdn_prep_kernel, n_ctx=n_ctx, seq=seq, dec_seq=dec_seq),
        out_shape=[jax.ShapeDtypeStruct((nt, w), F32)] * 3 + [jax.ShapeDtypeStruct((nt, LANES), F32)],
        grid=(nt // rows,),
        in_specs=main + prev + nxt + [pl.BlockSpec((rows, LANES), lambda i: (i, G_BA)),
                                      pl.BlockSpec(conv_w.shape, lambda i: (0, 0)), row, row],
        out_specs=[pl.BlockSpec((rows, w), lambda i: (i, 0))] * 3 + [pl.BlockSpec((rows, LANES), lambda i: (i, 0))],
        scratch_shapes=[pltpu.VMEM((rows + 2 * SUBLANES, w), F32)],
        compiler_params=_cparams(("parallel",), "gdn_prep"),
        name="gdn_prep",
    )(*([proj] * 10), conv_w, alog_row, dtb_row)


def _run_interleaved(gens):
    gens = list(gens)
    while gens:
        for g in list(gens):
            try:
                next(g)
            except StopIteration:
                gens.remove(g)


def _gdn_scan_kernel(*refs, nch, ns, nsub, has_s0, has_sfin):
    fwd_refs, bwd_refs = refs[0:4], refs[4:8]
    pos = 8
    s0_ref = None
    if has_s0:
        s0_ref = refs[pos]
        pos += 1
    of_ref, ob_ref = refs[pos:pos + 2]
    pos += 2
    sfin_ref = None
    if has_sfin:
        sfin_ref = refs[pos]
        pos += 1
    s_scr = refs[pos]
    j = pl.program_id(1)
    n2 = 2 * CHUNK
    chains = [(sq, p, d) for sq in range(ns) for p in range(PAIRS) for d in range(2)]
    units = [ch + (sub,) for sub in range(nsub) for ch in chains]
    idx = range(len(units))
    n_steps = nch // nsub

    @pl.when(j == 0)
    def _():
        for u, (sq, p, d) in enumerate(chains):
            s_scr[u] = s0_ref[sq, p, d] if has_s0 else jnp.zeros((n2, n2), F32)

    ri = lax.broadcasted_iota(jnp.int32, (n2, n2), 0)
    ci = lax.broadcasted_iota(jnp.int32, (n2, n2), 1)
    same = (ri >= CHUNK) == (ci >= CHUNK)
    incl = (same & (ri >= ci), same & (ri <= ci))
    strict = (same & (ri > ci), same & (ri < ci))
    eye = (ri == ci).astype(F32)
    ril, cil = ri % CHUNK, ci % CHUNK
    level_mask = []
    m = 1
    while m < CHUNK:
        level_mask.append(same & (ril // (2 * m) == cil // (2 * m)) & (ril // m != cil // m))
        m *= 2

    def col_stack(arr, ca, cb):
        top = jnp.broadcast_to(arr[:, ca:ca + 1], (CHUNK, LANES))
        bot = jnp.broadcast_to(arr[:, cb:cb + 1], (CHUNK, LANES))
        return jnp.concatenate([top, bot], axis=0)

    def row_stack(arr, row):
        top = jnp.broadcast_to(arr[row:row + 1, :], (CHUNK, LANES))
        bot = jnp.broadcast_to(arr[CHUNK + row:CHUNK + row + 1, :], (CHUNK, LANES))
        return jnp.concatenate([top, bot], axis=0)

    def rows_of(c, sq, d, sub):
        k = c * nsub + sub
        r = (sq * nch + (k if d == 0 else nch - 1 - k)) * CHUNK
        return slice(r, r + CHUNK)

    feats = [dict() for _ in range(n_steps)]
    solved = [dict() for _ in range(n_steps)]
    states = [s_scr[u] for u in range(len(chains))]
    dirs = [u[2] for u in units]

    def features(c):
        f = feats[c]
        src = lambda a, sq, p, d, sub: (fwd_refs, bwd_refs)[d][a][rows_of(c, sq, d, sub), :]
        bg = [src(3, *u) for u in units]
        b_c = [col_stack(bg[i], 8 * p + 2 * d, 8 * p + 2 * d + 1)
               for i, (sq, p, d, sub) in enumerate(units)]
        g_c = [col_stack(bg[i], 8 * p + 4 + 2 * d, 8 * p + 5 + 2 * d)
               for i, (sq, p, d, sub) in enumerate(units)]
        yield
        ld = lambda a, u: (fwd_refs, bwd_refs)[u[2]][a][rows_of(c, u[0], u[2], u[3]), u[1] * LANES:(u[1] + 1) * LANES]
        k2 = [jnp.concatenate([ld(1, u)] * 2, axis=0) for u in units]
        ks = [jnp.where(same, k2[i], 0.0) for i in idx]
        qs = [jnp.where(same, jnp.concatenate([ld(0, u)] * 2, axis=0), 0.0) for u in units]
        ks_b = [ks[i].astype(BF16) for i in idx]
        prod = [_dot_nt(jnp.concatenate([ks_b[i], qs[i].astype(BF16)], axis=0), ks_b[i]) for i in idx]
        yield
        decay = [jnp.where(incl[dirs[i]], jnp.exp(jnp.where(incl[dirs[i]], g_c[i] - g_c[i].T, 0.0)), 0.0) for i in idx]
        yield
        eg = [jnp.exp(g_c[i]) for i in idx]
        g_last = [row_stack(g_c[i], CHUNK - 1 if dirs[i] == 0 else 0) for i in idx]
        f["l"] = [jnp.where(strict[dirs[i]], b_c[i] * prod[i][:n2] * decay[i], 0.0) for i in idx]
        yield
        v2 = [jnp.concatenate([ld(2, u)] * 2, axis=0) for u in units]
        f["rhs"] = [(b_c[i] * jnp.where(same, v2[i], eg[i] * pltpu.roll(k2[i], CHUNK, axis=1))).astype(BF16)
                    for i in idx]
        yield
        f["at"] = [(prod[i][n2:] * decay[i]).astype(BF16) for i in idx]
        f["qg"] = [(qs[i] * eg[i]).astype(BF16) for i in idx]
        yield
        f["kt"] = [(ks[i] * jnp.exp(g_last[i] - g_c[i])).T.astype(BF16) for i in idx]
        f["gl"] = [jnp.exp(g_last[i]) for i in idx]
        yield

    def inverse(c):
        f, o = feats[c], solved[c]
        l_mat = f["l"]
        t_inv = [eye - jnp.where(level_mask[0], l_mat[i], 0.0) for i in idx]
        for lm in level_mask[1:]:
            t_b = [t_inv[i].astype(BF16) for i in idx]
            y = [_dot(jnp.where(lm, l_mat[i], 0.0).astype(BF16), t_b[i]).astype(BF16) for i in idx]
            yield
            t_inv = [t_inv[i] - _dot(t_b[i], y[i]) for i in idx]
            yield
        xs = [_dot(t_inv[i].astype(BF16), f["rhs"][i]) for i in idx]
        yield
        o["u"] = [jnp.where(same, xs[i], 0.0) for i in idx]
        o["wq"] = [jnp.concatenate([jnp.where(same, pltpu.roll(xs[i], CHUNK, axis=1), 0.0).astype(BF16),
                                    f["qg"][i]], axis=0) for i in idx]
        yield

    def scan(c):
        f, o = feats[c], solved[c]
        for sub in range(nsub):
            ui = [sub * len(chains) + k for k in range(len(chains))]
            ws = [_dot(o["wq"][i], states[k].astype(BF16)) for k, i in enumerate(ui)]
            yield
            u_b = [(o["u"][i] - ws[k][:n2]).astype(BF16) for k, i in enumerate(ui)]
            yield
            o_st = [ws[k][n2:] + _dot(f["at"][i], u_b[k]) for k, i in enumerate(ui)]
            for k, i in enumerate(ui):
                states[k] = f["gl"][i] * states[k] + _dot(f["kt"][i], u_b[k])
            yield
            for k, i in enumerate(ui):
                sq, p, d, _ = units[i]
                dst = of_ref if d == 0 else ob_ref
                dst[rows_of(c, sq, d, sub), p * LANES:(p + 1) * LANES] = o_st[k][:CHUNK] + o_st[k][CHUNK:]
            yield
        feats[c] = solved[c] = None

    for t in range(n_steps + 2):
        gens = []
        if 1 <= t <= n_steps:
            gens.append(inverse(t - 1))
        if t < n_steps:
            gens.append(features(t))
        if 2 <= t:
            gens.append(scan(t - 2))
        _run_interleaved(gens)

    for u in range(len(chains)):
        s_scr[u] = states[u]
    if has_sfin:
        @pl.when(j == pl.num_programs(1) - 1)
        def _():
            for u, (sq, p, d) in enumerate(chains):
                sfin_ref[sq, p, d] = states[u]


def _gdn_scan(feat, s0, *, t, row0, n_seq, want_state):
    tb = math.gcd(t, _SCAN_ROWS)
    nb = t // tb
    ns = math.gcd(n_seq, _SCAN_ROWS // tb) if nb == 1 else 1
    nsub = math.gcd(tb // CHUNK, max(1, _SCAN_UNITS // (ns * PAIRS)))
    rows = ns * tb
    assert row0 % rows == 0
    blk0 = row0 // rows
    n2 = 2 * CHUNK
    w = PAIRS * LANES
    has_s0 = s0 is not None

    def blocks(mirror):
        row = (lambda b, j: b * nb + (nb - 1 - j)) if mirror else (lambda b, j: b * nb + j)
        return ([pl.BlockSpec((rows, w), lambda b, j: (blk0 + row(b, j), 0))] * 3
                + [pl.BlockSpec((rows, LANES), lambda b, j: (blk0 + row(b, j), 0))]), row

    f_specs, f_row = blocks(False)
    b_specs, b_row = blocks(True)
    in_specs = f_specs + b_specs
    args = list(feat) * 2
    state_spec = pl.BlockSpec((ns, PAIRS, 2, n2, n2), lambda b, j: (b, 0, 0, 0, 0))
    if has_s0:
        in_specs.append(state_spec)
        args.append(s0)
    out_shape = [jax.ShapeDtypeStruct((n_seq * t, w), F32)] * 2
    out_specs = [pl.BlockSpec((rows, w), lambda b, j: (f_row(b, j), 0)),
                 pl.BlockSpec((rows, w), lambda b, j: (b_row(b, j), 0))]
    if want_state:
        out_shape.append(jax.ShapeDtypeStruct((n_seq, PAIRS, 2, n2, n2), F32))
        out_specs.append(state_spec)
    return pl.pallas_call(
        functools.partial(_gdn_scan_kernel, nch=tb // CHUNK, ns=ns, nsub=nsub, has_s0=has_s0, has_sfin=want_state),
        out_shape=out_shape,
        grid=(n_seq // ns, nb),
        in_specs=in_specs,
        out_specs=out_specs,
        scratch_shapes=[pltpu.VMEM((ns * 2 * PAIRS, n2, n2), F32)],
        compiler_params=_cparams(("parallel", "arbitrary"), "gdn_scan"),
        name="gdn_scan_ctx" if want_state else "gdn_scan_lat",
    )(*args)


def _prep_kernel(*refs, rope):
    bq_ref, cq_ref, bk_ref, bv_ref, ck_ref, cv_ref, qkn_ref = refs[:7]
    pos = 7
    cs = sn = None
    if rope:
        cs, sn = refs[7][...], refs[8][...]
        pos = 11
    qb_ref, qc_ref, kdb_ref, vdb_ref, kdc_ref, vdc_ref = refs[pos:pos + 6]
    pos += 6
    tr = bq_ref.shape[0]
    lane = lax.broadcasted_iota(jnp.int32, (tr, LANES), 1)
    lane_lo = lane < HEAD_DIM
    first_half = (lane % HEAD_DIM) < HEAD_DIM // 2
    qkn = qkn_ref[...]

    ones_bd = ((lax.broadcasted_iota(jnp.int32, (LANES, LANES), 0) < HEAD_DIM)
               == (lax.broadcasted_iota(jnp.int32, (LANES, LANES), 1) < HEAD_DIM)).astype(BF16)

    def norm(x, w_row):
        sq = x * x
        hi = sq.astype(BF16)
        lo = (sq - hi.astype(F32)).astype(BF16)
        ms = (_dot(hi, ones_bd) + _dot(lo, ones_bd)) * (1.0 / HEAD_DIM)
        return x * lax.rsqrt(ms + EPS) * w_row

    def rot(y):
        if not rope:
            return y
        partner = jnp.where(first_half, pltpu.roll(y, LANES - HEAD_DIM // 2, axis=1),
                            pltpu.roll(y, HEAD_DIM // 2, axis=1))
        return y * cs + partner * sn

    def dup(x, o_ref):
        lo = jnp.where(lane_lo, x, 0.0)
        hi = jnp.where(lane_lo, 0.0, x)
        d0 = (lo + pltpu.roll(lo, HEAD_DIM, axis=1)).astype(BF16)
        d1 = (hi + pltpu.roll(hi, HEAD_DIM, axis=1)).astype(BF16)
        if len(o_ref.shape) == 3:
            o_ref[0, :, 0:LANES] = d0
            o_ref[0, :, LANES:2 * LANES] = d1
        else:
            o_ref[:, 0:LANES] = d0
            o_ref[:, LANES:2 * LANES] = d1

    for q_ref, o_ref, wi in ((bq_ref, qb_ref, 0), (cq_ref, qc_ref, 2)):
        for g in range(PAIRS):
            y = rot(norm(q_ref[:, g * LANES:(g + 1) * LANES], qkn[wi:wi + 1, :])) * HEAD_DIM ** -0.5
            o_ref[2 * g] = jnp.where(lane_lo, y, 0.0).astype(BF16)
            o_ref[2 * g + 1] = jnp.where(lane_lo, 0.0, y).astype(BF16)
    kb_n = norm(bk_ref[...], qkn[1:2, :])
    kc_n = norm(ck_ref[...], qkn[3:4, :])
    if not rope:
        kbn_ref, vbn_ref, kcn_ref, vcn_ref = refs[pos:pos + 4]
        kbn_ref[...] = kb_n
        vbn_ref[...] = bv_ref[...]
        kcn_ref[...] = kc_n
        vcn_ref[...] = cv_ref[...]
    dup(rot(kb_n), kdb_ref)
    dup(bv_ref[...], vdb_ref)
    dup(rot(kc_n), kdc_ref)
    dup(cv_ref[...], vdc_ref)


def _attn_prep(proj, qkn_rows, rope_tables, glob_kv, *, row0, n_rows, t):
    rope = rope_tables is not None
    tr = math.gcd(t, 512)
    blk0 = row0 // tr

    def col(g, width):
        gw = width // LANES
        return pl.BlockSpec((tr, width), lambda i, g=g, gw=gw: (blk0 + i, g // gw))

    in_specs = [col(G_BQ, 4 * LANES), col(G_CQ, 4 * LANES), col(G_BK, LANES), col(G_BV, LANES),
                col(G_CK, LANES), col(G_CV, LANES), pl.BlockSpec((4, LANES), lambda i: (0, 0))]
    args = [proj] * 6 + [qkn_rows]
    q_shape = jax.ShapeDtypeStruct((HEADS, n_rows, LANES), BF16)
    kv_shape = jax.ShapeDtypeStruct((n_rows, 2 * LANES), BF16)
    q_spec = pl.BlockSpec((HEADS, tr, LANES), lambda i: (0, i, 0))
    kv_spec = pl.BlockSpec((tr, 2 * LANES), lambda i: (i, 0))
    aliases = {}
    if rope:
        per_seq = t // tr
        in_specs += [pl.BlockSpec((tr, LANES), lambda i: (i % per_seq, 0))] * 2 + [_any_spec()] * 2
        aliases = {len(args) + 2: 2, len(args) + 3: 3}
        args += list(rope_tables) + list(glob_kv)
        past = glob_kv[0].shape[1] - t
        assert past % tr == 0
        glob_spec = pl.BlockSpec((1, tr, 2 * LANES), lambda i: (i // per_seq, past // tr + i % per_seq, 0))
        glob_shape = jax.ShapeDtypeStruct(glob_kv[0].shape, BF16)
        out_shape = [q_shape, q_shape, glob_shape, glob_shape, kv_shape, kv_shape]
        out_specs = [q_spec, q_spec, glob_spec, glob_spec, kv_spec, kv_spec]
    else:
        out_shape = [q_shape, q_shape] + [kv_shape] * 4 + [jax.ShapeDtypeStruct((n_rows, LANES), F32)] * 4
        out_specs = [q_spec, q_spec] + [kv_spec] * 4 + [pl.BlockSpec((tr, LANES), lambda i: (i, 0))] * 4
    return pl.pallas_call(
        functools.partial(_prep_kernel, rope=rope),
        out_shape=out_shape,
        grid=(n_rows // tr,),
        in_specs=in_specs,
        out_specs=out_specs,
        input_output_aliases=aliases,
        compiler_params=_cparams(("parallel",), "attn_prep"),
        name="attn_prep_lat" if rope else "attn_prep_ctx",
    )(*args)


def _stacked_q(q_ref, g, tq):
    return q_ref[GROUP_HEADS * g:GROUP_HEADS * (g + 1)].reshape(GROUP_HEADS * tq, LANES)


def _unstack_group(o_st, tq, lane_lo):
    a = jnp.where(lane_lo, o_st[0:tq], o_st[tq:2 * tq])
    b = jnp.where(lane_lo, o_st[2 * tq:3 * tq], o_st[3 * tq:4 * tq])
    return jnp.concatenate([a, b], axis=1)


def _with_ones(v, ones):
    return jnp.concatenate([v, ones], axis=1)


def _sink_col(sink_row, g, tq):
    parts = [jnp.broadcast_to(sink_row[:, GROUP_HEADS * g + j:GROUP_HEADS * g + j + 1], (tq, 1))
             for j in range(GROUP_HEADS)]
    return jnp.concatenate(parts, axis=0)


def _attn_ctx_kernel(qb_ref, kdb_ref, vdb_ref, qc_ref, kdc_ref, vdc_ref, sink_ref, yb_ref, yc_ref):
    tq = qb_ref.shape[1]
    lane_lo_q = _lane_lo(tq)
    sink_row = sink_ref[...]
    groups = range(KV_HEADS)
    gs = [slice(g * LANES, (g + 1) * LANES) for g in groups]
    jobs = [(qb_ref, kdb_ref, vdb_ref, yb_ref, False, g) for g in groups]
    jobs += [(qc_ref, kdc_ref, vdc_ref, yc_ref, True, g) for g in groups]
    s = [_dot_nt(_stacked_q(q_ref, g, tq), kd_ref[:, gs[g]]) for q_ref, kd_ref, _, _, _, g in jobs]
    for si, (_, _, vd_ref, y_ref, use_sink, g) in zip(s, jobs):
        m = jnp.max(si, axis=-1, keepdims=True)
        if use_sink:
            snk = _sink_col(sink_row, g, tq)
            m = jnp.maximum(m, snk)
        p = jnp.exp(si - m)
        l = jnp.sum(p, axis=-1, keepdims=True)
        if use_sink:
            l = l + jnp.exp(snk - m)
        o = _dot(p.astype(BF16), vd_ref[:, gs[g]]) / l
        y_ref[:, 2 * g * LANES:(2 * g + 2) * LANES] = _unstack_group(o, tq, lane_lo_q).astype(y_ref.dtype)


def _attn_ctx(qb, kdb, vdb, qc, kdc, vdc, sink_row, *, t):
    n = kdb.shape[0]
    q_spec = pl.BlockSpec((HEADS, t, LANES), lambda b: (0, b, 0))
    kv_spec = pl.BlockSpec((t, 2 * LANES), lambda b: (b, 0))
    y_spec = pl.BlockSpec((t, 4 * LANES), lambda b: (b, 0))
    return pl.pallas_call(
        _attn_ctx_kernel,
        out_shape=[jax.ShapeDtypeStruct((n, 4 * LANES), BF16)] * 2,
        grid=(n // t,),
        in_specs=[q_spec, kv_spec, kv_spec, q_spec, kv_spec, kv_spec,
                  pl.BlockSpec((1, LANES), lambda b: (0, 0))],
        out_specs=[y_spec, y_spec],
        compiler_params=_cparams(("parallel",), "attn_ctx"),
        name="attn_ctx",
    )(qb, kdb, vdb, qc, kdc, vdc, sink_row)


_TQ_B = 256
_TK_B = 1536


def _attn_glob_kernel(q_ref, kd_ref, vd_ref, y_ref, s_scr, *, tk):
    tq = q_ref.shape[1]
    n_blk = kd_ref.shape[1] // tk
    groups = range(KV_HEADS)
    gs = [slice(g * LANES, (g + 1) * LANES) for g in groups]
    q = [_stacked_q(q_ref, g, tq) for g in groups]

    def scores_into(j, slot):
        for g in groups:
            s_scr[slot, g] = _dot_nt(q[g], kd_ref[0, j * tk:(j + 1) * tk, gs[g]])

    rows = GROUP_HEADS * tq
    m = [jnp.full((rows, LANES), NEG, F32) for _ in groups]
    l = [jnp.zeros((rows, LANES), F32) for _ in groups]
    acc = [jnp.zeros((rows, LANES), F32) for _ in groups]
    scores_into(0, 0)
    for j in range(n_blk):
        if j + 1 < n_blk:
            scores_into(j + 1, (j + 1) % 2)
        for g in groups:
            s = s_scr[j % 2, g]
            m_new = jnp.maximum(m[g], jnp.max(s, axis=-1, keepdims=True))
            p = jnp.exp(s - m_new[:, 0:1])
            alpha = jnp.exp(m[g] - m_new)
            l[g] = alpha * l[g] + jnp.sum(p, axis=-1, keepdims=True)
            acc[g] = alpha * acc[g] + _dot(p.astype(BF16), vd_ref[0, j * tk:(j + 1) * tk, gs[g]])
            m[g] = m_new
    lane_lo_q = _lane_lo(tq)
    for g in groups:
        o = acc[g] / l[g]
        y_ref[:, 2 * g * LANES:(2 * g + 2) * LANES] = _unstack_group(o, tq, lane_lo_q).astype(y_ref.dtype)


def _attn_glob(q, kd, vd, *, t):
    n = q.shape[1]
    s_len = kd.shape[1]
    tq, tk = math.gcd(t, _TQ_B), math.gcd(s_len, _TK_B)
    per_seq = t // tq
    kv_spec = pl.BlockSpec((1, s_len, 2 * LANES), lambda b, i: (b, 0, 0))
    return pl.pallas_call(
        functools.partial(_attn_glob_kernel, tk=tk),
        out_shape=jax.ShapeDtypeStruct((n, 4 * LANES), BF16),
        grid=(n // t, per_seq),
        in_specs=[pl.BlockSpec((HEADS, tq, LANES), lambda b, i: (0, b * per_seq + i, 0)), kv_spec, kv_spec],
        out_specs=pl.BlockSpec((tq, 4 * LANES), lambda b, i: (b * per_seq + i, 0)),
        scratch_shapes=[pltpu.VMEM((2, KV_HEADS, GROUP_HEADS * tq, tk), F32)],
        compiler_params=_cparams(("parallel", "parallel"), "attn_glob"),
        name="attn_glob",
    )(q, kd, vd)


def _attn_win_kernel(q_ref, kx_ref, vx_ref, kp_ref, kc_ref, kn_ref, vp_ref, vc_ref, vn_ref,
                     sink_ref, y_ref):
    i = pl.program_id(1)
    nblk = pl.num_programs(1)
    tq = q_ref.shape[1]
    rows = GROUP_HEADS * tq
    lane_lo_q = _lane_lo(tq)
    sink_row = sink_ref[...]
    rr_w = lax.broadcasted_iota(jnp.int32, (rows, WINDOW), 0) % tq
    cc_w = lax.broadcasted_iota(jnp.int32, (rows, WINDOW), 1)
    rr_q = lax.broadcasted_iota(jnp.int32, (rows, tq), 0) % tq
    cc_q = lax.broadcasted_iota(jnp.int32, (rows, tq), 1)
    mask_prev = (cc_w >= rr_w) & (i > 0)
    mask_cur = jnp.abs(rr_q - cc_q) <= WINDOW
    mask_next = (cc_w <= rr_w - (tq - WINDOW)) & (i < nblk - 1)
    groups = range(KV_HEADS)
    gs = [slice(g * LANES, (g + 1) * LANES) for g in groups]
    qs = [_stacked_q(q_ref, g, tq) for g in groups]
    p_len = kx_ref.shape[1]
    mask_edge = jnp.concatenate([mask_prev, mask_next], axis=1)
    edge = lambda p_ref, n_ref, g: jnp.concatenate([p_ref[:, gs[g]], n_ref[:, gs[g]]], axis=0)
    s = [jnp.concatenate([_dot_nt(qs[g], kx_ref[0, :, gs[g]]),
                          jnp.where(mask_edge, _dot_nt(qs[g], edge(kp_ref, kn_ref, g)), NEG),
                          jnp.where(mask_cur, _dot_nt(qs[g], kc_ref[:, gs[g]]), NEG)], axis=1) for g in groups]
    parts = ((0, p_len, lambda g: vx_ref[0, :, gs[g]]), (p_len, 2 * WINDOW, lambda g: edge(vp_ref, vn_ref, g)),
             (p_len + 2 * WINDOW, tq, lambda g: vc_ref[:, gs[g]]))
    for g in groups:
        snk = _sink_col(sink_row, g, tq)
        m = jnp.maximum(jnp.max(s[g], axis=-1, keepdims=True), snk)
        p = jnp.exp(s[g] - m).astype(BF16)
        ol = None
        for start, size, val in parts:
            d = _dot(p[:, start:start + size], _with_ones(val(g), jnp.ones((size, LANES), BF16)))
            ol = d if ol is None else ol + d
        o = ol[:, :LANES] / (ol[:, LANES:] + jnp.exp(snk - m))
        y_ref[:, 2 * g * LANES:(2 * g + 2) * LANES] = _unstack_group(o, tq, lane_lo_q).astype(y_ref.dtype)


def _attn_win(q, kd, vd, kx, vx, sink_row, *, t):
    n = q.shape[1]
    p_len = kx.shape[1]
    tq = math.gcd(t, 2 * WINDOW)
    nblk = t // tq
    per = tq // WINDOW
    wblk = t // WINDOW
    x_spec = pl.BlockSpec((1, p_len, 2 * LANES), lambda b, i: (b, 0, 0))
    prev = pl.BlockSpec((WINDOW, 2 * LANES), lambda b, i: (b * wblk + jnp.maximum(i * per - 1, 0), 0))
    cur = pl.BlockSpec((tq, 2 * LANES), lambda b, i: (b * nblk + i, 0))
    nxt = pl.BlockSpec((WINDOW, 2 * LANES), lambda b, i: (b * wblk + jnp.minimum((i + 1) * per, wblk - 1), 0))
    return pl.pallas_call(
        _attn_win_kernel,
        out_shape=jax.ShapeDtypeStruct((n, 4 * LANES), BF16),
        grid=(n // t, nblk),
        in_specs=[pl.BlockSpec((HEADS, tq, LANES), lambda b, i: (0, b * nblk + i, 0)),
                  x_spec, x_spec, prev, cur, nxt, prev, cur, nxt,
                  pl.BlockSpec((1, LANES), lambda b, i: (0, 0))],
        out_specs=pl.BlockSpec((tq, 4 * LANES), lambda b, i: (b * nblk + i, 0)),
        compiler_params=_cparams(("parallel", "parallel"), "attn_win"),
        name="attn_win",
    )(q, kx, vx, kd, kd, kd, vd, vd, vd, sink_row)


def _mix_kernel(x_ref, ofc_ref, obc_ref, ybc_ref, ycc_ref, ofl_ref, obl_ref, ybl_ref, ycl_ref, z_ref,
                ga_ref, gb_ref, gc_ref, mod_ref, anorm_ref, wa_ref, wb_ref, wc_ref, wo_ref, o_ref, *, ctx_tiles):
    d = x_ref.shape[1]
    tm = x_ref.shape[0]
    lane_lo = _lane_lo(tm)
    anorm = anorm_ref[...]
    is_ctx = pl.program_id(0) < ctx_tiles

    def pick(c_ref, l_ref, cols=slice(None)):
        return jnp.where(is_ctx, c_ref[:, cols], l_ref[:, cols])

    ya = []
    for g in range(PAIRS):
        gl = slice(g * LANES, (g + 1) * LANES)
        o = pick(ofc_ref, ofl_ref, gl) + pick(obc_ref, obl_ref, gl)
        ms = _half_sums(o * o, lane_lo) * (1.0 / HEAD_DIM)
        ya.append((o * lax.rsqrt(ms + EPS) * anorm * _silu(z_ref[:, gl])).astype(BF16))
    ya = jnp.concatenate(ya, axis=1)
    merged = (_sigmoid(ga_ref[...]) * _dot(ya, wa_ref[...])
              + _sigmoid(gb_ref[...]) * _dot(pick(ybc_ref, ybl_ref), wb_ref[...])
              + _sigmoid(gc_ref[...]) * _dot(pick(ycc_ref, ycl_ref), wc_ref[...]))
    gate1 = mod_ref[0][:, 2 * d:3 * d]
    o_ref[...] = x_ref[...] + gate1 * _dot(merged.astype(BF16), wo_ref[...])


def _mix(x, ctx_parts, lat_parts, proj, mods, anorm_row, wa, wb, wc, wo, n_ctx, dec_seq):
    nt, d = x.shape
    tm = _row_tile(n_ctx, dec_seq, 512)
    ctx_tiles = n_ctx // tm
    w = PAIRS * LANES
    c_spec = pl.BlockSpec((tm, w), lambda i: (jnp.minimum(i, ctx_tiles - 1), 0))
    l_spec = pl.BlockSpec((tm, w), lambda i: (jnp.maximum(i - ctx_tiles, 0), 0))
    gw = d // LANES

    def gate(k):
        return pl.BlockSpec((tm, d), lambda i, k=k: (i, G_GATES // gw + k))

    def full(a):
        return pl.BlockSpec(a.shape, lambda i: (0, 0))

    return pl.pallas_call(
        functools.partial(_mix_kernel, ctx_tiles=ctx_tiles),
        out_shape=jax.ShapeDtypeStruct((nt, d), F32),
        grid=(nt // tm,),
        in_specs=[pl.BlockSpec((tm, d), lambda i: (i, 0))] + [c_spec] * 4 + [l_spec] * 4
                 + [pl.BlockSpec((tm, w), lambda i: (i, G_Z // PAIRS)), gate(0), gate(1), gate(2),
                    pl.BlockSpec((1, 1, 6 * d), lambda i: (_mod_index(i, tm, n_ctx, dec_seq), 0, 0)),
                    full(anorm_row), full(wa), full(wb), full(wc), full(wo)],
        out_specs=pl.BlockSpec((tm, d), lambda i: (i, 0)),
        compiler_params=_cparams(("parallel",), "mix_out"),
        name="mix_out",
    )(x, *ctx_parts, *lat_parts, proj, proj, proj, proj, mods, anorm_row, wa, wb, wc, wo)


_FF_CHUNK = 1024


def _ffn_kernel(x_ref, mod_ref, ln_ref, w1_ref, w2_ref, o_ref):
    d = x_ref.shape[1]
    m = mod_ref[0]
    x = x_ref[...]
    h = _mod_norm(x, ln_ref[...], m[:, 3 * d:4 * d], m[:, 4 * d:5 * d]).astype(BF16)
    acc = None
    for f in range(0, w1_ref.shape[1], _FF_CHUNK):
        hid = jnp.maximum(_dot(h, w1_ref[:, f:f + _FF_CHUNK]), 0.0)
        part = _dot((hid * hid).astype(BF16), w2_ref[f:f + _FF_CHUNK, :])
        acc = part if acc is None else acc + part
    o_ref[...] = x + m[:, 5 * d:6 * d] * acc


def _ffn(x, mods, ln2, w1, w2, n_ctx, dec_seq):
    nt, d = x.shape
    tm = _row_tile(n_ctx, dec_seq, 512)
    return pl.pallas_call(
        _ffn_kernel,
        out_shape=jax.ShapeDtypeStruct((nt, d), F32),
        grid=(nt // tm,),
        in_specs=[pl.BlockSpec((tm, d), lambda i: (i, 0)),
                  pl.BlockSpec((1, 1, 6 * d), lambda i: (_mod_index(i, tm, n_ctx, dec_seq), 0, 0)),
                  pl.BlockSpec((1, d), lambda i: (0, 0)),
                  pl.BlockSpec(w1.shape, lambda i: (0, 0)),
                  pl.BlockSpec(w2.shape, lambda i: (0, 0))],
        out_specs=pl.BlockSpec((tm, d), lambda i: (i, 0)),
        compiler_params=_cparams(("parallel",), "ffn"),
        name="ffn",
    )(x, mods, ln2, w1, w2)


def _ba_lane_order():
    cols = []
    for p in range(PAIRS):
        for base in (0, 2 * HEADS):
            for d in range(2):
                cols += [base + d * HEADS + 2 * p, base + d * HEADS + 2 * p + 1]
    return cols


def _reorder_w_in(w_in):
    a_w = HEADS * HEAD_DIM
    kvw = KV_HEADS * HEAD_DIM
    o_beta = 4 * a_w
    o_bq = o_beta + 4 * HEADS
    o_bk = o_bq + a_w
    o_bv = o_bk + kvw
    o_cq = o_bv + kvw
    o_ck = o_cq + a_w
    o_cv = o_ck + kvw
    o_g = o_cv + kvw
    seg = lambda a, n: w_in[:, :, a:a + n]
    ba = jnp.take(seg(o_beta, 4 * HEADS), jnp.array(_ba_lane_order(), jnp.int32), axis=2)
    ba = jnp.pad(ba, ((0, 0), (0, 0), (0, 2 * LANES - ba.shape[2])))
    out = jnp.concatenate([seg(0, o_beta), seg(o_bq, a_w), seg(o_cq, a_w), seg(o_g, w_in.shape[2] - o_g),
                           seg(o_bk, kvw), seg(o_bv, kvw), seg(o_ck, kvw), seg(o_cv, kvw), ba], axis=2)
    assert out.shape[2] == IN_COLS_PAD
    return out.astype(BF16)


def _decay_rows(vals):
    rows = []
    for p in range(PAIRS):
        r = jnp.stack([vals[:, 0, 2 * p], vals[:, 0, 2 * p + 1], vals[:, 1, 2 * p], vals[:, 1, 2 * p + 1]], axis=-1)
        rows.append(jnp.pad(r, ((0, 0), (4, 0))))
    out = jnp.concatenate(rows, axis=-1)
    return jnp.pad(out, ((0, 0), (0, LANES - out.shape[1])))[:, None, :]


def _rope_tables(n_tokens):
    rows = n_tokens // GRID_W
    row_id = jnp.repeat(jnp.arange(rows, dtype=F32), GRID_W)
    col_id = jnp.tile(jnp.arange(GRID_W, dtype=F32), rows)
    n_freq = HEAD_DIM // 4
    inv_freq = ROPE_THETA ** (-jnp.arange(n_freq, dtype=F32) / n_freq)
    ang = jnp.concatenate([row_id[:, None] * inv_freq, col_id[:, None] * inv_freq], axis=-1)
    cos, sin = jnp.cos(ang), jnp.sin(ang)
    cs = jnp.tile(cos, (1, 4))
    sn = jnp.tile(jnp.concatenate([-sin, sin], axis=-1), (1, 2))
    return cs, sn


def _dup_kv(x):
    return jnp.concatenate([x[:, :, 0], x[:, :, 0], x[:, :, 1], x[:, :, 1]], axis=-1).astype(BF16)


def _block_diag_states(s):
    b = s.shape[0]
    s = s.reshape(b, 2, PAIRS, 2, HEAD_DIM, HEAD_DIM).transpose(0, 2, 1, 3, 4, 5)
    z = jnp.zeros_like(s[:, :, :, 0])
    top = jnp.concatenate([s[:, :, :, 0], z], axis=-1)
    bot = jnp.concatenate([z, s[:, :, :, 1]], axis=-1)
    return jnp.concatenate([top, bot], axis=-2)


def _diag_states(s):
    a = s[:, :, :, :HEAD_DIM, :HEAD_DIM]
    bb = s[:, :, :, HEAD_DIM:, HEAD_DIM:]
    out = jnp.stack([a, bb], axis=3)
    b = s.shape[0]
    return out.transpose(0, 2, 1, 3, 4, 5).reshape(b, 2, HEADS, HEAD_DIM, HEAD_DIM)


def kernel(x_prompt, x_sample, cache_k_glob, cache_v_glob, cache_k_win, cache_v_win, state_delta, c, c_ctx, w_mod, b_mod, ln1, ln2, w_in, conv_qkv, a_log, dt_bias, a_norm, qk_norm, sink, w_br_a, w_br_b, w_br_c, w_o, w_ff1, w_ff2):
    batch, seq, d = x_prompt.shape
    dec_batch, dec_seq, _ = x_sample.shape
    depth = w_mod.shape[0]
    n_ctx = batch * seq
    n_lat = dec_batch * dec_seq
    assert 1 + dec_batch <= SUBLANES

    w_in_r = _reorder_w_in(w_in)
    wa, wb, wc, wo = (w.astype(BF16) for w in (w_br_a, w_br_b, w_br_c, w_o))
    w1, w2 = w_ff1.astype(BF16), w_ff2.astype(BF16)
    alog_rows = _decay_rows(a_log)
    dtb_rows = _decay_rows(dt_bias)
    anorm_rows = jnp.tile(a_norm, (1, 2))[:, None, :]
    qkn_rows = jnp.tile(qk_norm, (1, 1, 2))
    sink_rows = jnp.pad(sink, ((0, 0), (0, LANES - HEADS)))[:, None, :]
    rope_tables = _rope_tables(dec_seq)
    cvec = jnp.concatenate([c_ctx[None], c, jnp.zeros((SUBLANES - 1 - dec_batch, d), F32)], axis=0)
    mods_all = _mod_vectors(cvec, w_mod, b_mod)
    pad_lat = ((0, 0), (0, dec_seq), (0, 0))

    x = jnp.concatenate([x_prompt.reshape(n_ctx, d), x_sample.reshape(n_lat, d)], axis=0)
    new_kg, new_vg, new_kw, new_vw, new_st = [], [], [], [], []
    for l in range(depth):
        mods = mods_all[l][:, None, :]
        proj = _in_proj(x, mods, ln1[l][None], w_in_r[l], n_ctx, dec_seq)

        feat = _gdn_prep(proj, conv_qkv[l], alog_rows[l], dtb_rows[l], n_ctx, seq, dec_seq)
        of_c, ob_c, s_fin = _gdn_scan(feat, None, t=seq, row0=0, n_seq=batch, want_state=True)
        of_l, ob_l = _gdn_scan(feat, _block_diag_states(state_delta[:, l]),
                               t=dec_seq, row0=n_ctx, n_seq=dec_batch, want_state=False)

        qb_c, qc_c, kdb_c, vdb_c, kdc_c, vdc_c, kb_n, vb_n, kc_n, vc_n = _attn_prep(
            proj, qkn_rows[l], None, None, row0=0, n_rows=n_ctx, t=seq)
        glob_kv = (jnp.pad(_dup_kv(cache_k_glob[:, l]), pad_lat), jnp.pad(_dup_kv(cache_v_glob[:, l]), pad_lat))
        qb_l, qc_l, kd_all, vd_all, kdc_l, vdc_l = _attn_prep(
            proj, qkn_rows[l], rope_tables, glob_kv, row0=n_ctx, n_rows=n_lat, t=dec_seq)

        yb_c, yc_c = _attn_ctx(qb_c, kdb_c, vdb_c, qc_c, kdc_c, vdc_c, sink_rows[l], t=seq)
        yb_l = _attn_glob(qb_l, kd_all, vd_all, t=dec_seq)
        yc_l = _attn_win(qc_l, kdc_l, vdc_l, _dup_kv(cache_k_win[:, l]), _dup_kv(cache_v_win[:, l]),
                         sink_rows[l], t=dec_seq)

        x = _mix(x, (of_c, ob_c, yb_c, yc_c), (of_l, ob_l, yb_l, yc_l), proj, mods, anorm_rows[l],
                 wa[l], wb[l], wc[l], wo[l], n_ctx, dec_seq)
        x = _ffn(x, mods, ln2[l][None], w1[l], w2[l], n_ctx, dec_seq)

        kv_shape = (batch, seq, KV_HEADS, HEAD_DIM)
        new_kg.append(kb_n.reshape(kv_shape))
        new_vg.append(vb_n.reshape(kv_shape))
        new_kw.append(kc_n.reshape(kv_shape))
        new_vw.append(vc_n.reshape(kv_shape))
        new_st.append(_diag_states(s_fin))

    y_prompt = x[:n_ctx].reshape(batch, seq, d)
    y_sample = x[n_ctx:].reshape(dec_batch, dec_seq, d)
    return (y_prompt, y_sample, jnp.stack(new_kg, axis=1), jnp.stack(new_vg, axis=1),
            jnp.stack(new_kw, axis=1), jnp.stack(new_vw, axis=1), jnp.stack(new_st, axis=1))
```

```python
import functools
import math

import jax
import jax.numpy as jnp
from jax import lax
from jax.experimental import pallas as pl
from jax.experimental.pallas import tpu as pltpu

F32 = jnp.float32
BF16 = jnp.bfloat16

LANES = 128
SUBLANES = 8

HEAD_DIM = 64
HEADS = 8
KV_HEADS = 2
GROUP_HEADS = HEADS // KV_HEADS
CHUNK = 64
CONV_K = 5
GRID_W = 64
WINDOW = 128
ROPE_THETA = 10000.0
EPS = 1e-6
NEG = -1e30
PAIRS = HEADS // 2

G_Q, G_K, G_V, G_Z = 0, 4, 8, 12
G_BQ, G_CQ = 16, 20
G_GATES = 24
G_BK, G_BV, G_CK, G_CV = 48, 49, 50, 51
G_BA = 52
N_GROUPS = 54
IN_COLS_PAD = N_GROUPS * LANES


VMEM_MIB_V7X = 64
MXU_TILE = 256

_VMEM_MIB = {
    "mod_vectors": 40,
    "in_proj": 48,
    "gdn_prep": 32,
    "gdn_scan": 48,
    "attn_prep": 32,
    "attn_ctx": 32,
    "attn_glob": 56,
    "attn_win": 32,
    "mix_out": 48,
    "ffn": 56,
}
assert max(_VMEM_MIB.values()) < VMEM_MIB_V7X


def _cparams(sem, call):
    return pltpu.CompilerParams(dimension_semantics=sem, vmem_limit_bytes=_VMEM_MIB[call] << 20)


def _dot(a, b):
    return jnp.dot(a, b, preferred_element_type=F32)


def _dot_nt(a, b):
    return lax.dot_general(a, b, (((1,), (1,)), ((), ())), preferred_element_type=F32)


def _sigmoid(x):
    return 1.0 / (1.0 + jnp.exp(-x))


def _silu(x):
    return x * _sigmoid(x)


def _softplus(x):
    return jnp.maximum(x, 0.0) + jnp.log1p(jnp.exp(-jnp.abs(x)))


def _half_sums(x, lane_lo):
    s_lo = jnp.sum(jnp.where(lane_lo, x, 0.0), axis=-1, keepdims=True)
    s_hi = jnp.sum(jnp.where(lane_lo, 0.0, x), axis=-1, keepdims=True)
    return jnp.where(lane_lo, s_lo, s_hi)


def _lane_lo(rows):
    return lax.broadcasted_iota(jnp.int32, (rows, LANES), 1) < HEAD_DIM


def _row_tile(n_ctx, dec_seq, pref):
    return math.gcd(math.gcd(n_ctx, dec_seq), pref)


def _mod_index(i, tm, n_ctx, dec_seq):
    ctx_tiles = n_ctx // tm
    return jnp.where(i < ctx_tiles, 0, 1 + (i - ctx_tiles) // (dec_seq // tm))


def _any_spec():
    return pl.BlockSpec(memory_space=pl.ANY)


def _mod_kernel(c_ref, w_ref, b_ref, o_ref):
    s = _silu(c_ref[...])
    w = w_ref[0]
    s_hi = s.astype(BF16)
    s_lo = (s - s_hi.astype(F32)).astype(BF16)
    w_hi = w.astype(BF16)
    w_lo = (w - w_hi.astype(F32)).astype(BF16)
    acc = _dot(s_hi, w_hi) + (_dot(s_lo, w_hi) + _dot(s_hi, w_lo))
    o_ref[0] = acc + b_ref[0]


def _mod_vectors(cvec, w_mod, b_mod):
    depth, d, n = w_mod.shape
    tn = n // 4
    return pl.pallas_call(
        _mod_kernel,
        out_shape=jax.ShapeDtypeStruct((depth, SUBLANES, n), F32),
        grid=(depth, n // tn),
        in_specs=[pl.BlockSpec((SUBLANES, d), lambda l, j: (0, 0)),
                  pl.BlockSpec((1, d, tn), lambda l, j: (l, 0, j)),
                  pl.BlockSpec((1, 1, tn), lambda l, j: (l, 0, j))],
        out_specs=pl.BlockSpec((1, SUBLANES, tn), lambda l, j: (l, 0, j)),
        compiler_params=_cparams(("parallel", "parallel"), "mod_vectors"),
        name="mod_vectors",
    )(cvec, w_mod, b_mod.reshape(depth, 1, n))


def _mod_norm(x, ln, shift, scale):
    ms = jnp.mean(x * x, axis=-1, keepdims=True)
    return (x * lax.rsqrt(ms + EPS) * ln) * (1.0 + scale) + shift


def _inproj_kernel(x_ref, mod_ref, ln_ref, w_ref, o_ref, h_scr):
    d = x_ref.shape[1]

    @pl.when(pl.program_id(1) == 0)
    def _():
        m = mod_ref[0]
        h = _mod_norm(x_ref[...], ln_ref[...], m[:, 0:d], m[:, d:2 * d])
        h_scr[...] = h.astype(BF16)

    o_ref[...] = _dot(h_scr[...], w_ref[...])


def _in_proj(x, mods, ln1, w_in, n_ctx, dec_seq):
    nt, d = x.shape
    n = w_in.shape[1]
    tm = _row_tile(n_ctx, dec_seq, 1024)
    tn = n // 3
    assert tn % MXU_TILE == 0
    return pl.pallas_call(
        _inproj_kernel,
        out_shape=jax.ShapeDtypeStruct((nt, n), F32),
        grid=(nt // tm, n // tn),
        in_specs=[pl.BlockSpec((tm, d), lambda i, j: (i, 0)),
                  pl.BlockSpec((1, 1, 6 * d), lambda i, j: (_mod_index(i, tm, n_ctx, dec_seq), 0, 0)),
                  pl.BlockSpec((1, d), lambda i, j: (0, 0)),
                  pl.BlockSpec((d, tn), lambda i, j: (0, j))],
        out_specs=pl.BlockSpec((tm, tn), lambda i, j: (i, j)),
        scratch_shapes=[pltpu.VMEM((tm, d), BF16)],
        compiler_params=_cparams(("parallel", "arbitrary"), "in_proj"),
        name="in_proj",
    )(x, mods, ln1, w_in)


_TB = 256
_SCAN_ROWS = 512
_SCAN_UNITS = 8


def _gdn_prep_kernel(q_ref, k_ref, v_ref, qp_ref, kp_ref, vp_ref, qx_ref, kx_ref, vx_ref, ba_ref,
                     cw_ref, alog_ref, dtb_ref, qo_ref, ko_ref, vo_ref, bg_ref, ext_scr, *, n_ctx, seq, dec_seq):
    rows = q_ref.shape[0]
    r0 = pl.program_id(0) * rows
    in_ctx = r0 < n_ctx
    off = jnp.where(in_ctx, r0, r0 - n_ctx)
    length = jnp.where(in_ctx, seq, dec_seq)
    is_start = lax.rem(off, length) == 0
    is_end = lax.rem(off + rows, length) == 0
    lane_lo = _lane_lo(rows)
    cw = cw_ref[...]
    width = q_ref.shape[1]

    def conv(x_ref, prev_ref, next_ref, o_ref, a, norm_scale):
        ext_scr[0:SUBLANES, :] = jnp.where(is_start, 0.0, prev_ref[...])
        ext_scr[SUBLANES:SUBLANES + rows, :] = x_ref[...]
        ext_scr[SUBLANES + rows:, :] = jnp.where(is_end, 0.0, next_ref[...])
        acc = None
        for j in range(CONV_K):
            o = SUBLANES - CONV_K // 2 + j
            term = ext_scr[o:o + rows, :] * cw[j:j + 1, a * width:(a + 1) * width]
            acc = term if acc is None else acc + term
        y = _silu(acc)
        for g in range(width // LANES):
            yg = y[:, g * LANES:(g + 1) * LANES]
            if norm_scale is not None:
                yg = yg * lax.rsqrt(_half_sums(yg * yg, lane_lo) + EPS) * norm_scale
            o_ref[:, g * LANES:(g + 1) * LANES] = yg

    conv(q_ref, qp_ref, qx_ref, qo_ref, 0, HEAD_DIM ** -0.5)
    conv(k_ref, kp_ref, kx_ref, ko_ref, 1, 1.0)
    conv(v_ref, vp_ref, vx_ref, vo_ref, 2, None)

    x = ba_ref[...]
    g = -jnp.exp(alog_ref[...]) * _softplus(x + dtb_ref[...])
    row_in_chunk = lax.broadcasted_iota(jnp.int32, (rows, LANES), 0) % CHUNK
    pre = suf = g
    s = 1
    while s < CHUNK:
        pre = pre + jnp.where(row_in_chunk >= s, pltpu.roll(pre, s, axis=0), 0.0)
        suf = suf + jnp.where(row_in_chunk + s < CHUNK, pltpu.roll(suf, rows - s, axis=0), 0.0)
        s *= 2
    c = lax.broadcasted_iota(jnp.int32, (rows, LANES), 1) % 8
    bg_ref[...] = jnp.where(c < 4, _sigmoid(x), jnp.where(c < 6, pre, suf))


def _gdn_prep(proj, conv_w, alog_row, dtb_row, n_ctx, seq, dec_seq):
    nt = proj.shape[0]
    rows = _TB
    assert seq % rows == 0 and dec_seq % rows == 0
    w = PAIRS * LANES
    per = rows // SUBLANES
    last = nt // SUBLANES - 1
    main = [pl.BlockSpec((rows, w), lambda i, a=a: (i, a)) for a in range(3)]
    prev = [pl.BlockSpec((SUBLANES, w), lambda i, a=a: (jnp.maximum(i * per - 1, 0), a)) for a in range(3)]
    nxt = [pl.BlockSpec((SUBLANES, w), lambda i, a=a: (jnp.minimum((i + 1) * per, last), a)) for a in range(3)]
    row = pl.BlockSpec((1, LANES), lambda i: (0, 0))
    return pl.pallas_call(
        functools.partial(_gdn_prep_kernel, n_ctx=n_ctx, seq=seq, dec_seq=dec_seq),
        out_shape=[jax.ShapeDtypeStruct((nt, w), F32)] * 3 + [jax.ShapeDtypeStruct((nt, LANES), F32)],
        grid=(nt // rows,),
        in_specs=main + prev + nxt + [pl.BlockSpec((rows, LANES), lambda i: (i, G_BA)),
                                      pl.BlockSpec(conv_w.shape, lambda i: (0, 0)), row, row],
        out_specs=[pl.BlockSpec((rows, w), lambda i: (i, 0))] * 3 + [pl.BlockSpec((rows, LANES), lambda i: (i, 0))],
        scratch_shapes=[pltpu.VMEM((rows + 2 * SUBLANES, w), F32)],
        compiler_params=_cparams(("parallel",), "gdn_prep"),
        name="gdn_prep",
    )(*([proj] * 10), conv_w, alog_row, dtb_row)


def _run_interleaved(gens):
    gens = list(gens)
    while gens:
        for g in list(gens):
            try:
                next(g)
            except StopIteration:
                gens.remove(g)


def _gdn_scan_kernel(*refs, nch, ns, nsub, has_s0, has_sfin):
    fwd_refs, bwd_refs = refs[0:4], refs[4:8]
    pos = 8
    s0_ref = None
    if has_s0:
        s0_ref = refs[pos]
        pos += 1
    of_ref, ob_ref = refs[pos:pos + 2]
    pos += 2
    sfin_ref = None
    if has_sfin:
        sfin_ref = refs[pos]
        pos += 1
    s_scr = refs[pos]
    j = pl.program_id(1)
    n2 = 2 * CHUNK
    chains = [(sq, p, d) for sq in range(ns) for p in range(PAIRS) for d in range(2)]
    units = [ch + (sub,) for sub in range(nsub) for ch in chains]
    idx = range(len(units))
    n_steps = nch // nsub

    @pl.when(j == 0)
    def _():
        for u, (sq, p, d) in enumerate(chains):
            s_scr[u] = s0_ref[sq, p, d] if has_s0 else jnp.zeros((n2, n2), F32)

    ri = lax.broadcasted_iota(jnp.int32, (n2, n2), 0)
    ci = lax.broadcasted_iota(jnp.int32, (n2, n2), 1)
    same = (ri >= CHUNK) == (ci >= CHUNK)
    incl = (same & (ri >= ci), same & (ri <= ci))
    strict = (same & (ri > ci), same & (ri < ci))
    eye = (ri == ci).astype(F32)
    ril, cil = ri % CHUNK, ci % CHUNK
    level_mask = []
    m = 1
    while m < CHUNK:
        level_mask.append(same & (ril // (2 * m) == cil // (2 * m)) & (ril // m != cil // m))
        m *= 2

    def col_stack(arr, ca, cb):
        top = jnp.broadcast_to(arr[:, ca:ca + 1], (CHUNK, LANES))
        bot = jnp.broadcast_to(arr[:, cb:cb + 1], (CHUNK, LANES))
        return jnp.concatenate([top, bot], axis=0)

    def row_stack(arr, row):
        top = jnp.broadcast_to(arr[row:row + 1, :], (CHUNK, LANES))
        bot = jnp.broadcast_to(arr[CHUNK + row:CHUNK + row + 1, :], (CHUNK, LANES))
        return jnp.concatenate([top, bot], axis=0)

    def rows_of(c, sq, d, sub):
        k = c * nsub + sub
        r = (sq * nch + (k if d == 0 else nch - 1 - k)) * CHUNK
        return slice(r, r + CHUNK)

    feats = [dict() for _ in range(n_steps)]
    solved = [dict() for _ in range(n_steps)]
    states = [s_scr[u] for u in range(len(chains))]
    dirs = [u[2] for u in units]

    def features(c):
        f = feats[c]
        src = lambda a, sq, p, d, sub: (fwd_refs, bwd_refs)[d][a][rows_of(c, sq, d, sub), :]
        bg = [src(3, *u) for u in units]
        b_c = [col_stack(bg[i], 8 * p + 2 * d, 8 * p + 2 * d + 1)
               for i, (sq, p, d, sub) in enumerate(units)]
        g_c = [col_stack(bg[i], 8 * p + 4 + 2 * d, 8 * p + 5 + 2 * d)
               for i, (sq, p, d, sub) in enumerate(units)]
        yield
        ld = lambda a, u: (fwd_refs, bwd_refs)[u[2]][a][rows_of(c, u[0], u[2], u[3]), u[1] * LANES:(u[1] + 1) * LANES]
        k2 = [jnp.concatenate([ld(1, u)] * 2, axis=0) for u in units]
        ks = [jnp.where(same, k2[i], 0.0) for i in idx]
        qs = [jnp.where(same, jnp.concatenate([ld(0, u)] * 2, axis=0), 0.0) for u in units]
        ks_b = [ks[i].astype(BF16) for i in idx]
        prod = [_dot_nt(jnp.concatenate([ks_b[i], qs[i].astype(BF16)], axis=0), ks_b[i]) for i in idx]
        yield
        decay = [jnp.where(incl[dirs[i]], jnp.exp(jnp.where(incl[dirs[i]], g_c[i] - g_c[i].T, 0.0)), 0.0) for i in idx]
        yield
        eg = [jnp.exp(g_c[i]) for i in idx]
        g_last = [row_stack(g_c[i], CHUNK - 1 if dirs[i] == 0 else 0) for i in idx]
        f["l"] = [jnp.where(strict[dirs[i]], b_c[i] * prod[i][:n2] * decay[i], 0.0) for i in idx]
        yield
        v2 = [jnp.concatenate([ld(2, u)] * 2, axis=0) for u in units]
        f["rhs"] = [(b_c[i] * jnp.where(same, v2[i], eg[i] * pltpu.roll(k2[i], CHUNK, axis=1))).astype(BF16)
                    for i in idx]
        yield
        f["at"] = [(prod[i][n2:] * decay[i]).astype(BF16) for i in idx]
        f["qg"] = [(qs[i] * eg[i]).astype(BF16) for i in idx]
        yield
        f["kt"] = [(ks[i] * jnp.exp(g_last[i] - g_c[i])).T.astype(BF16) for i in idx]
        f["gl"] = [jnp.exp(g_last[i]) for i in idx]
        yield

    def inverse(c):
        f, o = feats[c], solved[c]
        l_mat = f["l"]
        t_inv = [eye - jnp.where(level_mask[0], l_mat[i], 0.0) for i in idx]
        for lm in level_mask[1:]:
            t_b = [t_inv[i].astype(BF16) for i in idx]
            y = [_dot(jnp.where(lm, l_mat[i], 0.0).astype(BF16), t_b[i]).astype(BF16) for i in idx]
            yield
            t_inv = [t_inv[i] - _dot(t_b[i], y[i]) for i in idx]
            yield
        xs = [_dot(t_inv[i].astype(BF16), f["rhs"][i]) for i in idx]
        yield
        o["u"] = [jnp.where(same, xs[i], 0.0) for i in idx]
        o["wq"] = [jnp.concatenate([jnp.where(same, pltpu.roll(xs[i], CHUNK, axis=1), 0.0).astype(BF16),
                                    f["qg"][i]], axis=0) for i in idx]
        yield

    def scan(c):
        f, o = feats[c], solved[c]
        for sub in range(nsub):
            ui = [sub * len(chains) + k for k in range(len(chains))]
            ws = [_dot(o["wq"][i], states[k].astype(BF16)) for k, i in enumerate(ui)]
            yield
            u_b = [(o["u"][i] - ws[k][:n2]).astype(BF16) for k, i in enumerate(ui)]
            yield
            o_st = [ws[k][n2:] + _dot(f["at"][i], u_b[k]) for k, i in enumerate(ui)]
            for k, i in enumerate(ui):
                states[k] = f["gl"][i] * states[k] + _dot(f["kt"][i], u_b[k])
            yield
            for k, i in enumerate(ui):
                sq, p, d, _ = units[i]
                dst = of_ref if d == 0 else ob_ref
                dst[rows_of(c, sq, d, sub), p * LANES:(p + 1) * LANES] = o_st[k][:CHUNK] + o_st[k][CHUNK:]
            yield
        feats[c] = solved[c] = None

    for t in range(n_steps + 2):
        gens = []
        if 1 <= t <= n_steps:
            gens.append(inverse(t - 1))
        if t < n_steps:
            gens.append(features(t))
        if 2 <= t:
            gens.append(scan(t - 2))
        _run_interleaved(gens)

    for u in range(len(chains)):
        s_scr[u] = states[u]
    if has_sfin:
        @pl.when(j == pl.num_programs(1) - 1)
        def _():
            for u, (sq, p, d) in enumerate(chains):
                sfin_ref[sq, p, d] = states[u]


def _gdn_scan(feat, s0, *, t, row0, n_seq, want_state):
    tb = math.gcd(t, _SCAN_ROWS)
    nb = t // tb
    ns = math.gcd(n_seq, _SCAN_ROWS // tb) if nb == 1 else 1
    nsub = math.gcd(tb // CHUNK, max(1, _SCAN_UNITS // (ns * PAIRS)))
    rows = ns * tb
    assert row0 % rows == 0
    blk0 = row0 // rows
    n2 = 2 * CHUNK
    w = PAIRS * LANES
    has_s0 = s0 is not None

    def blocks(mirror):
        row = (lambda b, j: b * nb + (nb - 1 - j)) if mirror else (lambda b, j: b * nb + j)
        return ([pl.BlockSpec((rows, w), lambda b, j: (blk0 + row(b, j), 0))] * 3
                + [pl.BlockSpec((rows, LANES), lambda b, j: (blk0 + row(b, j), 0))]), row

    f_specs, f_row = blocks(False)
    b_specs, b_row = blocks(True)
    in_specs = f_specs + b_specs
    args = list(feat) * 2
    state_spec = pl.BlockSpec((ns, PAIRS, 2, n2, n2), lambda b, j: (b, 0, 0, 0, 0))
    if has_s0:
        in_specs.append(state_spec)
        args.append(s0)
    out_shape = [jax.ShapeDtypeStruct((n_seq * t, w), F32)] * 2
    out_specs = [pl.BlockSpec((rows, w), lambda b, j: (f_row(b, j), 0)),
                 pl.BlockSpec((rows, w), lambda b, j: (b_row(b, j), 0))]
    if want_state:
        out_shape.append(jax.ShapeDtypeStruct((n_seq, PAIRS, 2, n2, n2), F32))
        out_specs.append(state_spec)
    return pl.pallas_call(
        functools.partial(_gdn_scan_kernel, nch=tb // CHUNK, ns=ns, nsub=nsub, has_s0=has_s0, has_sfin=want_state),
        out_shape=out_shape,
        grid=(n_seq // ns, nb),
        in_specs=in_specs,
        out_specs=out_specs,
        scratch_shapes=[pltpu.VMEM((ns * 2 * PAIRS, n2, n2), F32)],
        compiler_params=_cparams(("parallel", "arbitrary"), "gdn_scan"),
        name="gdn_scan_ctx" if want_state else "gdn_scan_lat",
    )(*args)


def _prep_kernel(*refs, rope):
    bq_ref, cq_ref, bk_ref, bv_ref, ck_ref, cv_ref, qkn_ref = refs[:7]
    pos = 7
    cs = sn = None
    if rope:
        cs, sn = refs[7][...], refs[8][...]
        pos = 11
    qb_ref, qc_ref, kdb_ref, vdb_ref, kdc_ref, vdc_ref = refs[pos:pos + 6]
    pos += 6
    tr = bq_ref.shape[0]
    lane = lax.broadcasted_iota(jnp.int32, (tr, LANES), 1)
    lane_lo = lane < HEAD_DIM
    first_half = (lane % HEAD_DIM) < HEAD_DIM // 2
    qkn = qkn_ref[...]

    ones_bd = ((lax.broadcasted_iota(jnp.int32, (LANES, LANES), 0) < HEAD_DIM)
               == (lax.broadcasted_iota(jnp.int32, (LANES, LANES), 1) < HEAD_DIM)).astype(BF16)

    def norm(x, w_row):
        sq = x * x
        hi = sq.astype(BF16)
        lo = (sq - hi.astype(F32)).astype(BF16)
        ms = (_dot(hi, ones_bd) + _dot(lo, ones_bd)) * (1.0 / HEAD_DIM)
        return x * lax.rsqrt(ms + EPS) * w_row

    def rot(y):
        if not rope:
            return y
        partner = jnp.where(first_half, pltpu.roll(y, LANES - HEAD_DIM // 2, axis=1),
                            pltpu.roll(y, HEAD_DIM // 2, axis=1))
        return y * cs + partner * sn

    def dup(x, o_ref):
        lo = jnp.where(lane_lo, x, 0.0)
        hi = jnp.where(lane_lo, 0.0, x)
        d0 = (lo + pltpu.roll(lo, HEAD_DIM, axis=1)).astype(BF16)
        d1 = (hi + pltpu.roll(hi, HEAD_DIM, axis=1)).astype(BF16)
        if len(o_ref.shape) == 3:
            o_ref[0, :, 0:LANES] = d0
            o_ref[0, :, LANES:2 * LANES] = d1
        else:
            o_ref[:, 0:LANES] = d0
            o_ref[:, LANES:2 * LANES] = d1

    for q_ref, o_ref, wi in ((bq_ref, qb_ref, 0), (cq_ref, qc_ref, 2)):
        for g in range(PAIRS):
            y = rot(norm(q_ref[:, g * LANES:(g + 1) * LANES], qkn[wi:wi + 1, :])) * HEAD_DIM ** -0.5
            o_ref[2 * g] = jnp.where(lane_lo, y, 0.0).astype(BF16)
            o_ref[2 * g + 1] = jnp.where(lane_lo, 0.0, y).astype(BF16)
    kb_n = norm(bk_ref[...], qkn[1:2, :])
    kc_n = norm(ck_ref[...], qkn[3:4, :])
    if not rope:
        kbn_ref, vbn_ref, kcn_ref, vcn_ref = refs[pos:pos + 4]
        kbn_ref[...] = kb_n
        vbn_ref[...] = bv_ref[...]
        kcn_ref[...] = kc_n
        vcn_ref[...] = cv_ref[...]
    dup(rot(kb_n), kdb_ref)
    dup(bv_ref[...], vdb_ref)
    dup(rot(kc_n), kdc_ref)
    dup(cv_ref[...], vdc_ref)


def _attn_prep(proj, qkn_rows, rope_tables, glob_kv, *, row0, n_rows, t):
    rope = rope_tables is not None
    tr = math.gcd(t, 512)
    blk0 = row0 // tr

    def col(g, width):
        gw = width // LANES
        return pl.BlockSpec((tr, width), lambda i, g=g, gw=gw: (blk0 + i, g // gw))

    in_specs = [col(G_BQ, 4 * LANES), col(G_CQ, 4 * LANES), col(G_BK, LANES), col(G_BV, LANES),
                col(G_CK, LANES), col(G_CV, LANES), pl.BlockSpec((4, LANES), lambda i: (0, 0))]
    args = [proj] * 6 + [qkn_rows]
    q_shape = jax.ShapeDtypeStruct((HEADS, n_rows, LANES), BF16)
    kv_shape = jax.ShapeDtypeStruct((n_rows, 2 * LANES), BF16)
    q_spec = pl.BlockSpec((HEADS, tr, LANES), lambda i: (0, i, 0))
    kv_spec = pl.BlockSpec((tr, 2 * LANES), lambda i: (i, 0))
    aliases = {}
    if rope:
        per_seq = t // tr
        in_specs += [pl.BlockSpec((tr, LANES), lambda i: (i % per_seq, 0))] * 2 + [_any_spec()] * 2
        aliases = {len(args) + 2: 2, len(args) + 3: 3}
        args += list(rope_tables) + list(glob_kv)
        past = glob_kv[0].shape[1] - t
        assert past % tr == 0
        glob_spec = pl.BlockSpec((1, tr, 2 * LANES), lambda i: (i // per_seq, past // tr + i % per_seq, 0))
        glob_shape = jax.ShapeDtypeStruct(glob_kv[0].shape, BF16)
        out_shape = [q_shape, q_shape, glob_shape, glob_shape, kv_shape, kv_shape]
        out_specs = [q_spec, q_spec, glob_spec, glob_spec, kv_spec, kv_spec]
    else:
        out_shape = [q_shape, q_shape] + [kv_shape] * 4 + [jax.ShapeDtypeStruct((n_rows, LANES), F32)] * 4
        out_specs = [q_spec, q_spec] + [kv_spec] * 4 + [pl.BlockSpec((tr, LANES), lambda i: (i, 0))] * 4
    return pl.pallas_call(
        functools.partial(_prep_kernel, rope=rope),
        out_shape=out_shape,
        grid=(n_rows // tr,),
        in_specs=in_specs,
        out_specs=out_specs,
        input_output_aliases=aliases,
        compiler_params=_cparams(("parallel",), "attn_prep"),
        name="attn_prep_lat" if rope else "attn_prep_ctx",
    )(*args)


def _stacked_q(q_ref, g, tq):
    return q_ref[GROUP_HEADS * g:GROUP_HEADS * (g + 1)].reshape(GROUP_HEADS * tq, LANES)


def _unstack_group(o_st, tq, lane_lo):
    a = jnp.where(lane_lo, o_st[0:tq], o_st[tq:2 * tq])
    b = jnp.where(lane_lo, o_st[2 * tq:3 * tq], o_st[3 * tq:4 * tq])
    return jnp.concatenate([a, b], axis=1)


def _with_ones(v, ones):
    return jnp.concatenate([v, ones], axis=1)


def _sink_col(sink_row, g, tq):
    parts = [jnp.broadcast_to(sink_row[:, GROUP_HEADS * g + j:GROUP_HEADS * g + j + 1], (tq, 1))
             for j in range(GROUP_HEADS)]
    return jnp.concatenate(parts, axis=0)


def _attn_ctx_kernel(qb_ref, kdb_ref, vdb_ref, qc_ref, kdc_ref, vdc_ref, sink_ref, yb_ref, yc_ref):
    tq = qb_ref.shape[1]
    lane_lo_q = _lane_lo(tq)
    sink_row = sink_ref[...]
    groups = range(KV_HEADS)
    gs = [slice(g * LANES, (g + 1) * LANES) for g in groups]
    jobs = [(qb_ref, kdb_ref, vdb_ref, yb_ref, False, g) for g in groups]
    jobs += [(qc_ref, kdc_ref, vdc_ref, yc_ref, True, g) for g in groups]
    s = [_dot_nt(_stacked_q(q_ref, g, tq), kd_ref[:, gs[g]]) for q_ref, kd_ref, _, _, _, g in jobs]
    for si, (_, _, vd_ref, y_ref, use_sink, g) in zip(s, jobs):
        m = jnp.max(si, axis=-1, keepdims=True)
        if use_sink:
            snk = _sink_col(sink_row, g, tq)
            m = jnp.maximum(m, snk)
        p = jnp.exp(si - m)
        l = jnp.sum(p, axis=-1, keepdims=True)
        if use_sink:
            l = l + jnp.exp(snk - m)
        o = _dot(p.astype(BF16), vd_ref[:, gs[g]]) / l
        y_ref[:, 2 * g * LANES:(2 * g + 2) * LANES] = _unstack_group(o, tq, lane_lo_q).astype(y_ref.dtype)


def _attn_ctx(qb, kdb, vdb, qc, kdc, vdc, sink_row, *, t):
    n = kdb.shape[0]
    q_spec = pl.BlockSpec((HEADS, t, LANES), lambda b: (0, b, 0))
    kv_spec = pl.BlockSpec((t, 2 * LANES), lambda b: (b, 0))
    y_spec = pl.BlockSpec((t, 4 * LANES), lambda b: (b, 0))
    return pl.pallas_call(
        _attn_ctx_kernel,
        out_shape=[jax.ShapeDtypeStruct((n, 4 * LANES), BF16)] * 2,
        grid=(n // t,),
        in_specs=[q_spec, kv_spec, kv_spec, q_spec, kv_spec, kv_spec,
                  pl.BlockSpec((1, LANES), lambda b: (0, 0))],
        out_specs=[y_spec, y_spec],
        compiler_params=_cparams(("parallel",), "attn_ctx"),
        name="attn_ctx",
    )(qb, kdb, vdb, qc, kdc, vdc, sink_row)


_TQ_B = 256
_TK_B = 1536


def _attn_glob_kernel(q_ref, kd_ref, vd_ref, y_ref, s_scr, *, tk):
    tq = q_ref.shape[1]
    n_blk = kd_ref.shape[1] // tk
    groups = range(KV_HEADS)
    gs = [slice(g * LANES, (g + 1) * LANES) for g in groups]
    q = [_stacked_q(q_ref, g, tq) for g in groups]

    def scores_into(j, slot):
        for g in groups:
            s_scr[slot, g] = _dot_nt(q[g], kd_ref[0, j * tk:(j + 1) * tk, gs[g]])

    rows = GROUP_HEADS * tq
    m = [jnp.full((rows, LANES), NEG, F32) for _ in groups]
    acc = [jnp.zeros((rows, 2 * LANES), F32) for _ in groups]
    ones = jnp.ones((tk, LANES), BF16)
    scores_into(0, 0)
    for j in range(n_blk):
        if j + 1 < n_blk:
            scores_into(j + 1, (j + 1) % 2)
        for g in groups:
            s = s_scr[j % 2, g]
            m_new = jnp.maximum(m[g], jnp.max(s, axis=-1, keepdims=True))
            p = jnp.exp(s - m_new[:, 0:1]).astype(BF16)
            alpha = jnp.exp(m[g] - m_new)
            acc[g] = (jnp.concatenate([alpha, alpha], axis=1) * acc[g]
                      + _dot(p, _with_ones(vd_ref[0, j * tk:(j + 1) * tk, gs[g]], ones)))
            m[g] = m_new
    lane_lo_q = _lane_lo(tq)
    for g in groups:
        o = acc[g][:, :LANES] / acc[g][:, LANES:]
        y_ref[:, 2 * g * LANES:(2 * g + 2) * LANES] = _unstack_group(o, tq, lane_lo_q).astype(y_ref.dtype)


def _attn_glob(q, kd, vd, *, t):
    n = q.shape[1]
    s_len = kd.shape[1]
    tq, tk = math.gcd(t, _TQ_B), math.gcd(s_len, _TK_B)
    per_seq = t // tq
    kv_spec = pl.BlockSpec((1, s_len, 2 * LANES), lambda b, i: (b, 0, 0))
    return pl.pallas_call(
        functools.partial(_attn_glob_kernel, tk=tk),
        out_shape=jax.ShapeDtypeStruct((n, 4 * LANES), BF16),
        grid=(n // t, per_seq),
        in_specs=[pl.BlockSpec((HEADS, tq, LANES), lambda b, i: (0, b * per_seq + i, 0)), kv_spec, kv_spec],
        out_specs=pl.BlockSpec((tq, 4 * LANES), lambda b, i: (b * per_seq + i, 0)),
        scratch_shapes=[pltpu.VMEM((2, KV_HEADS, GROUP_HEADS * tq, tk), F32)],
        compiler_params=_cparams(("parallel", "parallel"), "attn_glob"),
        name="attn_glob",
    )(q, kd, vd)


def _attn_win_kernel(q_ref, kx_ref, vx_ref, kp_ref, kc_ref, kn_ref, vp_ref, vc_ref, vn_ref,
                     sink_ref, y_ref):
    i = pl.program_id(1)
    nblk = pl.num_programs(1)
    tq = q_ref.shape[1]
    rows = GROUP_HEADS * tq
    lane_lo_q = _lane_lo(tq)
    sink_row = sink_ref[...]
    rr_w = lax.broadcasted_iota(jnp.int32, (rows, WINDOW), 0) % tq
    cc_w = lax.broadcasted_iota(jnp.int32, (rows, WINDOW), 1)
    rr_q = lax.broadcasted_iota(jnp.int32, (rows, tq), 0) % tq
    cc_q = lax.broadcasted_iota(jnp.int32, (rows, tq), 1)
    mask_prev = (cc_w >= rr_w) & (i > 0)
    mask_cur = jnp.abs(rr_q - cc_q) <= WINDOW
    mask_next = (cc_w <= rr_w - (tq - WINDOW)) & (i < nblk - 1)
    groups = range(KV_HEADS)
    gs = [slice(g * LANES, (g + 1) * LANES) for g in groups]
    qs = [_stacked_q(q_ref, g, tq) for g in groups]
    p_len = kx_ref.shape[1]
    mask_edge = jnp.concatenate([mask_prev, mask_next], axis=1)
    edge = lambda p_ref, n_ref, g: jnp.concatenate([p_ref[:, gs[g]], n_ref[:, gs[g]]], axis=0)
    s = [jnp.concatenate([_dot_nt(qs[g], kx_ref[0, :, gs[g]]),
                          jnp.where(mask_edge, _dot_nt(qs[g], edge(kp_ref, kn_ref, g)), NEG),
                          jnp.where(mask_cur, _dot_nt(qs[g], kc_ref[:, gs[g]]), NEG)], axis=1) for g in groups]
    parts = ((0, p_len, lambda g: vx_ref[0, :, gs[g]]), (p_len, 2 * WINDOW, lambda g: edge(vp_ref, vn_ref, g)),
             (p_len + 2 * WINDOW, tq, lambda g: vc_ref[:, gs[g]]))
    for g in groups:
        snk = _sink_col(sink_row, g, tq)
        m = jnp.maximum(jnp.max(s[g], axis=-1, keepdims=True), snk)
        p = jnp.exp(s[g] - m).astype(BF16)
        ol = None
        for start, size, val in parts:
            d = _dot(p[:, start:start + size], _with_ones(val(g), jnp.ones((size, LANES), BF16)))
            ol = d if ol is None else ol + d
        o = ol[:, :LANES] / (ol[:, LANES:] + jnp.exp(snk - m))
        y_ref[:, 2 * g * LANES:(2 * g + 2) * LANES] = _unstack_group(o, tq, lane_lo_q).astype(y_ref.dtype)


def _attn_win(q, kd, vd, kx, vx, sink_row, *, t):
    n = q.shape[1]
    p_len = kx.shape[1]
    tq = math.gcd(t, 2 * WINDOW)
    nblk = t // tq
    per = tq // WINDOW
    wblk = t // WINDOW
    x_spec = pl.BlockSpec((1, p_len, 2 * LANES), lambda b, i: (b, 0, 0))
    prev = pl.BlockSpec((WINDOW, 2 * LANES), lambda b, i: (b * wblk + jnp.maximum(i * per - 1, 0), 0))
    cur = pl.BlockSpec((tq, 2 * LANES), lambda b, i: (b * nblk + i, 0))
    nxt = pl.BlockSpec((WINDOW, 2 * LANES), lambda b, i: (b * wblk + jnp.minimum((i + 1) * per, wblk - 1), 0))
    return pl.pallas_call(
        _attn_win_kernel,
        out_shape=jax.ShapeDtypeStruct((n, 4 * LANES), BF16),
        grid=(n // t, nblk),
        in_specs=[pl.BlockSpec((HEADS, tq, LANES), lambda b, i: (0, b * nblk + i, 0)),
                  x_spec, x_spec, prev, cur, nxt, prev, cur, nxt,
                  pl.BlockSpec((1, LANES), lambda b, i: (0, 0))],
        out_specs=pl.BlockSpec((tq, 4 * LANES), lambda b, i: (b * nblk + i, 0)),
        compiler_params=_cparams(("parallel", "parallel"), "attn_win"),
        name="attn_win",
    )(q, kx, vx, kd, kd, kd, vd, vd, vd, sink_row)


def _mix_kernel(x_ref, ofc_ref, obc_ref, ybc_ref, ycc_ref, ofl_ref, obl_ref, ybl_ref, ycl_ref, z_ref,
                ga_ref, gb_ref, gc_ref, mod_ref, anorm_ref, wa_ref, wb_ref, wc_ref, wo_ref, o_ref, *, ctx_tiles):
    d = x_ref.shape[1]
    tm = x_ref.shape[0]
    lane_lo = _lane_lo(tm)
    anorm = anorm_ref[...]
    is_ctx = pl.program_id(0) < ctx_tiles

    def pick(c_ref, l_ref, cols=slice(None)):
        return jnp.where(is_ctx, c_ref[:, cols], l_ref[:, cols])

    ya = []
    for g in range(PAIRS):
        gl = slice(g * LANES, (g + 1) * LANES)
        o = pick(ofc_ref, ofl_ref, gl) + pick(obc_ref, obl_ref, gl)
        ms = _half_sums(o * o, lane_lo) * (1.0 / HEAD_DIM)
        ya.append((o * lax.rsqrt(ms + EPS) * anorm * _silu(z_ref[:, gl])).astype(BF16))
    ya = jnp.concatenate(ya, axis=1)
    merged = (_sigmoid(ga_ref[...]) * _dot(ya, wa_ref[...])
              + _sigmoid(gb_ref[...]) * _dot(pick(ybc_ref, ybl_ref), wb_ref[...])
              + _sigmoid(gc_ref[...]) * _dot(pick(ycc_ref, ycl_ref), wc_ref[...]))
    gate1 = mod_ref[0][:, 2 * d:3 * d]
    o_ref[...] = x_ref[...] + gate1 * _dot(merged.astype(BF16), wo_ref[...])


def _mix(x, ctx_parts, lat_parts, proj, mods, anorm_row, wa, wb, wc, wo, n_ctx, dec_seq):
    nt, d = x.shape
    tm = _row_tile(n_ctx, dec_seq, 512)
    ctx_tiles = n_ctx // tm
    w = PAIRS * LANES
    c_spec = pl.BlockSpec((tm, w), lambda i: (jnp.minimum(i, ctx_tiles - 1), 0))
    l_spec = pl.BlockSpec((tm, w), lambda i: (jnp.maximum(i - ctx_tiles, 0), 0))
    gw = d // LANES

    def gate(k):
        return pl.BlockSpec((tm, d), lambda i, k=k: (i, G_GATES // gw + k))

    def full(a):
        return pl.BlockSpec(a.shape, lambda i: (0, 0))

    return pl.pallas_call(
        functools.partial(_mix_kernel, ctx_tiles=ctx_tiles),
        out_shape=jax.ShapeDtypeStruct((nt, d), F32),
        grid=(nt // tm,),
        in_specs=[pl.BlockSpec((tm, d), lambda i: (i, 0))] + [c_spec] * 4 + [l_spec] * 4
                 + [pl.BlockSpec((tm, w), lambda i: (i, G_Z // PAIRS)), gate(0), gate(1), gate(2),
                    pl.BlockSpec((1, 1, 6 * d), lambda i: (_mod_index(i, tm, n_ctx, dec_seq), 0, 0)),
                    full(anorm_row), full(wa), full(wb), full(wc), full(wo)],
        out_specs=pl.BlockSpec((tm, d), lambda i: (i, 0)),
        compiler_params=_cparams(("parallel",), "mix_out"),
        name="mix_out",
    )(x, *ctx_parts, *lat_parts, proj, proj, proj, proj, mods, anorm_row, wa, wb, wc, wo)


_FF_CHUNK = 1024


def _ffn_kernel(x_ref, mod_ref, ln_ref, w1_ref, w2_ref, o_ref):
    d = x_ref.shape[1]
    m = mod_ref[0]
    x = x_ref[...]
    h = _mod_norm(x, ln_ref[...], m[:, 3 * d:4 * d], m[:, 4 * d:5 * d]).astype(BF16)
    acc = None
    for f in range(0, w1_ref.shape[1], _FF_CHUNK):
        hid = jnp.maximum(_dot(h, w1_ref[:, f:f + _FF_CHUNK]), 0.0)
        part = _dot((hid * hid).astype(BF16), w2_ref[f:f + _FF_CHUNK, :])
        acc = part if acc is None else acc + part
    o_ref[...] = x + m[:, 5 * d:6 * d] * acc


def _ffn(x, mods, ln2, w1, w2, n_ctx, dec_seq):
    nt, d = x.shape
    tm = _row_tile(n_ctx, dec_seq, 512)
    return pl.pallas_call(
        _ffn_kernel,
        out_shape=jax.ShapeDtypeStruct((nt, d), F32),
        grid=(nt // tm,),
        in_specs=[pl.BlockSpec((tm, d), lambda i: (i, 0)),
                  pl.BlockSpec((1, 1, 6 * d), lambda i: (_mod_index(i, tm, n_ctx, dec_seq), 0, 0)),
                  pl.BlockSpec((1, d), lambda i: (0, 0)),
                  pl.BlockSpec(w1.shape, lambda i: (0, 0)),
                  pl.BlockSpec(w2.shape, lambda i: (0, 0))],
        out_specs=pl.BlockSpec((tm, d), lambda i: (i, 0)),
        compiler_params=_cparams(("parallel",), "ffn"),
        name="ffn",
    )(x, mods, ln2, w1, w2)


def _ba_lane_order():
    cols = []
    for p in range(PAIRS):
        for base in (0, 2 * HEADS):
            for d in range(2):
                cols += [base + d * HEADS + 2 * p, base + d * HEADS + 2 * p + 1]
    return cols


def _reorder_w_in(w_in):
    a_w = HEADS * HEAD_DIM
    kvw = KV_HEADS * HEAD_DIM
    o_beta = 4 * a_w
    o_bq = o_beta + 4 * HEADS
    o_bk = o_bq + a_w
    o_bv = o_bk + kvw
    o_cq = o_bv + kvw
    o_ck = o_cq + a_w
    o_cv = o_ck + kvw
    o_g = o_cv + kvw
    seg = lambda a, n: w_in[:, :, a:a + n]
    ba = jnp.take(seg(o_beta, 4 * HEADS), jnp.array(_ba_lane_order(), jnp.int32), axis=2)
    ba = jnp.pad(ba, ((0, 0), (0, 0), (0, 2 * LANES - ba.shape[2])))
    out = jnp.concatenate([seg(0, o_beta), seg(o_bq, a_w), seg(o_cq, a_w), seg(o_g, w_in.shape[2] - o_g),
                           seg(o_bk, kvw), seg(o_bv, kvw), seg(o_ck, kvw), seg(o_cv, kvw), ba], axis=2)
    assert out.shape[2] == IN_COLS_PAD
    return out.astype(BF16)


def _decay_rows(vals):
    rows = []
    for p in range(PAIRS):
        r = jnp.stack([vals[:, 0, 2 * p], vals[:, 0, 2 * p + 1], vals[:, 1, 2 * p], vals[:, 1, 2 * p + 1]], axis=-1)
        rows.append(jnp.pad(r, ((0, 0), (4, 0))))
    out = jnp.concatenate(rows, axis=-1)
    return jnp.pad(out, ((0, 0), (0, LANES - out.shape[1])))[:, None, :]


def _rope_tables(n_tokens):
    rows = n_tokens // GRID_W
    row_id = jnp.repeat(jnp.arange(rows, dtype=F32), GRID_W)
    col_id = jnp.tile(jnp.arange(GRID_W, dtype=F32), rows)
    n_freq = HEAD_DIM // 4
    inv_freq = ROPE_THETA ** (-jnp.arange(n_freq, dtype=F32) / n_freq)
    ang = jnp.concatenate([row_id[:, None] * inv_freq, col_id[:, None] * inv_freq], axis=-1)
    cos, sin = jnp.cos(ang), jnp.sin(ang)
    cs = jnp.tile(cos, (1, 4))
    sn = jnp.tile(jnp.concatenate([-sin, sin], axis=-1), (1, 2))
    return cs, sn


def _dup_kv(x):
    return jnp.concatenate([x[:, :, 0], x[:, :, 0], x[:, :, 1], x[:, :, 1]], axis=-1).astype(BF16)


def _block_diag_states(s):
    b = s.shape[0]
    s = s.reshape(b, 2, PAIRS, 2, HEAD_DIM, HEAD_DIM).transpose(0, 2, 1, 3, 4, 5)
    z = jnp.zeros_like(s[:, :, :, 0])
    top = jnp.concatenate([s[:, :, :, 0], z], axis=-1)
    bot = jnp.concatenate([z, s[:, :, :, 1]], axis=-1)
    return jnp.concatenate([top, bot], axis=-2)


def _diag_states(s):
    a = s[:, :, :, :HEAD_DIM, :HEAD_DIM]
    bb = s[:, :, :, HEAD_DIM:, HEAD_DIM:]
    out = jnp.stack([a, bb], axis=3)
    b = s.shape[0]
    return out.transpose(0, 2, 1, 3, 4, 5).reshape(b, 2, HEADS, HEAD_DIM, HEAD_DIM)


def kernel(x_prompt, x_sample, cache_k_glob, cache_v_glob, cache_k_win, cache_v_win, state_delta, c, c_ctx, w_mod, b_mod, ln1, ln2, w_in, conv_qkv, a_log, dt_bias, a_norm, qk_norm, sink, w_br_a, w_br_b, w_br_c, w_o, w_ff1, w_ff2):
    batch, seq, d = x_prompt.shape
    dec_batch, dec_seq, _ = x_sample.shape
    depth = w_mod.shape[0]
    n_ctx = batch * seq
    n_lat = dec_batch * dec_seq
    assert 1 + dec_batch <= SUBLANES

    w_in_r = _reorder_w_in(w_in)
    wa, wb, wc, wo = (w.astype(BF16) for w in (w_br_a, w_br_b, w_br_c, w_o))
    w1, w2 = w_ff1.astype(BF16), w_ff2.astype(BF16)
    alog_rows = _decay_rows(a_log)
    dtb_rows = _decay_rows(dt_bias)
    anorm_rows = jnp.tile(a_norm, (1, 2))[:, None, :]
    qkn_rows = jnp.tile(qk_norm, (1, 1, 2))
    sink_rows = jnp.pad(sink, ((0, 0), (0, LANES - HEADS)))[:, None, :]
    rope_tables = _rope_tables(dec_seq)
    cvec = jnp.concatenate([c_ctx[None], c, jnp.zeros((SUBLANES - 1 - dec_batch, d), F32)], axis=0)
    mods_all = _mod_vectors(cvec, w_mod, b_mod)
    pad_lat = ((0, 0), (0, dec_seq), (0, 0))

    x = jnp.concatenate([x_prompt.reshape(n_ctx, d), x_sample.reshape(n_lat, d)], axis=0)
    new_kg, new_vg, new_kw, new_vw, new_st = [], [], [], [], []
    for l in range(depth):
        mods = mods_all[l][:, None, :]
        proj = _in_proj(x, mods, ln1[l][None], w_in_r[l], n_ctx, dec_seq)

        feat = _gdn_prep(proj, conv_qkv[l], alog_rows[l], dtb_rows[l], n_ctx, seq, dec_seq)
        of_c, ob_c, s_fin = _gdn_scan(feat, None, t=seq, row0=0, n_seq=batch, want_state=True)
        of_l, ob_l = _gdn_scan(feat, _block_diag_states(state_delta[:, l]),
                               t=dec_seq, row0=n_ctx, n_seq=dec_batch, want_state=False)

        qb_c, qc_c, kdb_c, vdb_c, kdc_c, vdc_c, kb_n, vb_n, kc_n, vc_n = _attn_prep(
            proj, qkn_rows[l], None, None, row0=0, n_rows=n_ctx, t=seq)
        glob_kv = (jnp.pad(_dup_kv(cache_k_glob[:, l]), pad_lat), jnp.pad(_dup_kv(cache_v_glob[:, l]), pad_lat))
        qb_l, qc_l, kd_all, vd_all, kdc_l, vdc_l = _attn_prep(
            proj, qkn_rows[l], rope_tables, glob_kv, row0=n_ctx, n_rows=n_lat, t=dec_seq)

        yb_c, yc_c = _attn_ctx(qb_c, kdb_c, vdb_c, qc_c, kdc_c, vdc_c, sink_rows[l], t=seq)
        yb_l = _attn_glob(qb_l, kd_all, vd_all, t=dec_seq)
        yc_l = _attn_win(qc_l, kdc_l, vdc_l, _dup_kv(cache_k_win[:, l]), _dup_kv(cache_v_win[:, l]),
                         sink_rows[l], t=dec_seq)

        x = _mix(x, (of_c, ob_c, yb_c, yc_c), (of_l, ob_l, yb_l, yc_l), proj, mods, anorm_rows[l],
                 wa[l], wb[l], wc[l], wo[l], n_ctx, dec_seq)
        x = _ffn(x, mods, ln2[l][None], w1[l], w2[l], n_ctx, dec_seq)

        kv_shape = (batch, seq, KV_HEADS, HEAD_DIM)
        new_kg.append(kb_n.reshape(kv_shape))
        new_vg.append(vb_n.reshape(kv_shape))
        new_kw.append(kc_n.reshape(kv_shape))
        new_vw.append(vc_n.reshape(kv_shape))
        new_st.append(_diag_states(s_fin))

    y_prompt = x[:n_ctx].reshape(batch, seq, d)
    y_sample = x[n_ctx:].reshape(dec_batch, dec_seq, d)
    return (y_prompt, y_sample, jnp.stack(new_kg, axis=1), jnp.stack(new_vg, axis=1),
            jnp.stack(new_kw, axis=1), jnp.stack(new_vw, axis=1), jnp.stack(new_st, axis=1))
```

```python
import functools
import math

import jax
import jax.numpy as jnp
from jax import lax
from jax.experimental import pallas as pl
from jax.experimental.pallas import tpu as pltpu

F32 = jnp.float32
BF16 = jnp.bfloat16

LANES = 128
SUBLANES = 8

HEAD_DIM = 64
HEADS = 8
KV_HEADS = 2
GROUP_HEADS = HEADS // KV_HEADS
CHUNK = 64
CONV_K = 5
GRID_W = 64
WINDOW = 128
ROPE_THETA = 10000.0
EPS = 1e-6
NEG = -1e30
PAIRS = HEADS // 2

G_Q, G_K, G_V, G_Z = 0, 4, 8, 12
G_BQ, G_CQ = 16, 20
G_GATES = 24
G_BK, G_BV, G_CK, G_CV = 48, 49, 50, 51
G_BA = 52
N_GROUPS = 54
IN_COLS_PAD = N_GROUPS * LANES


VMEM_MIB_V7X = 64
MXU_TILE = 256

_VMEM_MIB = {
    "mod_vectors": 40,
    "in_proj": 48,
    "gdn_prep": 32,
    "gdn_scan": 48,
    "attn_prep": 32,
    "attn_ctx": 32,
    "attn_glob": 56,
    "attn_win": 32,
    "mix_out": 48,
    "ffn": 56,
}
assert max(_VMEM_MIB.values()) < VMEM_MIB_V7X


def _cparams(sem, call):
    return pltpu.CompilerParams(dimension_semantics=sem, vmem_limit_bytes=_VMEM_MIB[call] << 20)


def _dot(a, b):
    return jnp.dot(a, b, preferred_element_type=F32)


def _dot_nt(a, b):
    return lax.dot_general(a, b, (((1,), (1,)), ((), ())), preferred_element_type=F32)


def _sigmoid(x):
    return 0.5 * jnp.tanh(0.5 * x) + 0.5


def _silu(x):
    return x * _sigmoid(x)


def _softplus(x):
    return jnp.maximum(x, 0.0) + jnp.log1p(jnp.exp(-jnp.abs(x)))


def _half_sums(x, lane_lo):
    s_lo = jnp.sum(jnp.where(lane_lo, x, 0.0), axis=-1, keepdims=True)
    s_hi = jnp.sum(jnp.where(lane_lo, 0.0, x), axis=-1, keepdims=True)
    return jnp.where(lane_lo, s_lo, s_hi)


def _lane_lo(rows):
    return lax.broadcasted_iota(jnp.int32, (rows, LANES), 1) < HEAD_DIM


def _row_tile(n_ctx, dec_seq, pref):
    return math.gcd(math.gcd(n_ctx, dec_seq), pref)


def _mod_index(i, tm, n_ctx, dec_seq):
    ctx_tiles = n_ctx // tm
    return jnp.where(i < ctx_tiles, 0, 1 + (i - ctx_tiles) // (dec_seq // tm))


def _any_spec():
    return pl.BlockSpec(memory_space=pl.ANY)


def _mod_kernel(c_ref, w_ref, b_ref, o_ref):
    s = _silu(c_ref[...])
    w = w_ref[0]
    s_hi = s.astype(BF16)
    s_lo = (s - s_hi.astype(F32)).astype(BF16)
    w_hi = w.astype(BF16)
    w_lo = (w - w_hi.astype(F32)).astype(BF16)
    acc = _dot(s_hi, w_hi) + (_dot(s_lo, w_hi) + _dot(s_hi, w_lo))
    o_ref[0] = acc + b_ref[0]


def _mod_vectors(cvec, w_mod, b_mod):
    depth, d, n = w_mod.shape
    tn = n // 4
    return pl.pallas_call(
        _mod_kernel,
        out_shape=jax.ShapeDtypeStruct((depth, SUBLANES, n), F32),
        grid=(depth, n // tn),
        in_specs=[pl.BlockSpec((SUBLANES, d), lambda l, j: (0, 0)),
                  pl.BlockSpec((1, d, tn), lambda l, j: (l, 0, j)),
                  pl.BlockSpec((1, 1, tn), lambda l, j: (l, 0, j))],
        out_specs=pl.BlockSpec((1, SUBLANES, tn), lambda l, j: (l, 0, j)),
        compiler_params=_cparams(("parallel", "parallel"), "mod_vectors"),
        name="mod_vectors",
    )(cvec, w_mod, b_mod.reshape(depth, 1, n))


def _mod_norm(x, ln, shift, scale):
    ms = jnp.mean(x * x, axis=-1, keepdims=True)
    return (x * lax.rsqrt(ms + EPS) * ln) * (1.0 + scale) + shift


def _inproj_kernel(x_ref, mod_ref, ln_ref, w_ref, o_ref, h_scr):
    d = x_ref.shape[1]

    @pl.when(pl.program_id(1) == 0)
    def _():
        m = mod_ref[0]
        h = _mod_norm(x_ref[...], ln_ref[...], m[:, 0:d], m[:, d:2 * d])
        h_scr[...] = h.astype(BF16)

    o_ref[...] = _dot(h_scr[...], w_ref[...])


def _in_proj(x, mods, ln1, w_in, n_ctx, dec_seq):
    nt, d = x.shape
    n = w_in.shape[1]
    tm = _row_tile(n_ctx, dec_seq, 1024)
    tn = n // 3
    assert tn % MXU_TILE == 0
    return pl.pallas_call(
        _inproj_kernel,
        out_shape=jax.ShapeDtypeStruct((nt, n), F32),
        grid=(nt // tm, n // tn),
        in_specs=[pl.BlockSpec((tm, d), lambda i, j: (i, 0)),
                  pl.BlockSpec((1, 1, 6 * d), lambda i, j: (_mod_index(i, tm, n_ctx, dec_seq), 0, 0)),
                  pl.BlockSpec((1, d), lambda i, j: (0, 0)),
                  pl.BlockSpec((d, tn), lambda i, j: (0, j))],
        out_specs=pl.BlockSpec((tm, tn), lambda i, j: (i, j)),
        scratch_shapes=[pltpu.VMEM((tm, d), BF16)],
        compiler_params=_cparams(("parallel", "arbitrary"), "in_proj"),
        name="in_proj",
    )(x, mods, ln1, w_in)


_TB = 256
_SCAN_ROWS = 512
_SCAN_UNITS = 8


def _gdn_prep_kernel(q_ref, k_ref, v_ref, qp_ref, kp_ref, vp_ref, qx_ref, kx_ref, vx_ref, ba_ref,
                     cw_ref, alog_ref, dtb_ref, qo_ref, ko_ref, vo_ref, bg_ref, ext_scr, *, n_ctx, seq, dec_seq):
    rows = q_ref.shape[0]
    r0 = pl.program_id(0) * rows
    in_ctx = r0 < n_ctx
    off = jnp.where(in_ctx, r0, r0 - n_ctx)
    length = jnp.where(in_ctx, seq, dec_seq)
    is_start = lax.rem(off, length) == 0
    is_end = lax.rem(off + rows, length) == 0
    lane_lo = _lane_lo(rows)
    cw = cw_ref[...]
    width = q_ref.shape[1]

    def conv(x_ref, prev_ref, next_ref, o_ref, a, norm_scale):
        ext_scr[0:SUBLANES, :] = jnp.where(is_start, 0.0, prev_ref[...])
        ext_scr[SUBLANES:SUBLANES + rows, :] = x_ref[...]
        ext_scr[SUBLANES + rows:, :] = jnp.where(is_end, 0.0, next_ref[...])
        acc = None
        for j in range(CONV_K):
            o = SUBLANES - CONV_K // 2 + j
            term = ext_scr[o:o + rows, :] * cw[j:j + 1, a * width:(a + 1) * width]
            acc = term if acc is None else acc + term
        y = _silu(acc)
        for g in range(width // LANES):
            yg = y[:, g * LANES:(g + 1) * LANES]
            if norm_scale is not None:
                yg = yg * lax.rsqrt(_half_sums(yg * yg, lane_lo) + EPS) * norm_scale
            o_ref[:, g * LANES:(g + 1) * LANES] = yg

    conv(q_ref, qp_ref, qx_ref, qo_ref, 0, HEAD_DIM ** -0.5)
    conv(k_ref, kp_ref, kx_ref, ko_ref, 1, 1.0)
    conv(v_ref, vp_ref, vx_ref, vo_ref, 2, None)

    x = ba_ref[...]
    g = -jnp.exp(alog_ref[...]) * _softplus(x + dtb_ref[...])
    row_in_chunk = lax.broadcasted_iota(jnp.int32, (rows, LANES), 0) % CHUNK
    pre = suf = g
    s = 1
    while s < CHUNK:
        pre = pre + jnp.where(row_in_chunk >= s, pltpu.roll(pre, s, axis=0), 0.0)
        suf = suf + jnp.where(row_in_chunk + s < CHUNK, pltpu.roll(suf, rows - s, axis=0), 0.0)
        s *= 2
    c = lax.broadcasted_iota(jnp.int32, (rows, LANES), 1) % 8
    bg_ref[...] = jnp.where(c < 4, _sigmoid(x), jnp.where(c < 6, pre, suf))


def _gdn_prep(proj, conv_w, alog_row, dtb_row, n_ctx, seq, dec_seq):
    nt = proj.shape[0]
    rows = _TB
    assert seq % rows == 0 and dec_seq % rows == 0
    w = PAIRS * LANES
    per = rows // SUBLANES
    last = nt // SUBLANES - 1
    main = [pl.BlockSpec((rows, w), lambda i, a=a: (i, a)) for a in range(3)]
    prev = [pl.BlockSpec((SUBLANES, w), lambda i, a=a: (jnp.maximum(i * per - 1, 0), a)) for a in range(3)]
    nxt = [pl.BlockSpec((SUBLANES, w), lambda i, a=a: (jnp.minimum((i + 1) * per, last), a)) for a in range(3)]
    row = pl.BlockSpec((1, LANES), lambda i: (0, 0))
    return pl.pallas_call(
        functools.partial(_gdn_prep_kernel, n_ctx=n_ctx, seq=seq, dec_seq=dec_seq),
        out_shape=[jax.ShapeDtypeStruct((nt, w), F32)] * 3 + [jax.ShapeDtypeStruct((nt, LANES), F32)],
        grid=(nt // rows,),
        in_specs=main + prev + nxt + [pl.BlockSpec((rows, LANES), lambda i: (i, G_BA)),
                                      pl.BlockSpec(conv_w.shape, lambda i: (0, 0)), row, row],
        out_specs=[pl.BlockSpec((rows, w), lambda i: (i, 0))] * 3 + [pl.BlockSpec((rows, LANES), lambda i: (i, 0))],
        scratch_shapes=[pltpu.VMEM((rows + 2 * SUBLANES, w), F32)],
        compiler_params=_cparams(("parallel",), "gdn_prep"),
        name="gdn_prep",
    )(*([proj] * 10), conv_w, alog_row, dtb_row)


def _run_interleaved(gens):
    gens = list(gens)
    while gens:
        for g in list(gens):
            try:
                next(g)
            except StopIteration:
                gens.remove(g)


def _gdn_scan_kernel(*refs, nch, ns, nsub, has_s0, has_sfin):
    fwd_refs, bwd_refs = refs[0:4], refs[4:8]
    pos = 8
    s0_ref = None
    if has_s0:
        s0_ref = refs[pos]
        pos += 1
    of_ref, ob_ref = refs[pos:pos + 2]
    pos += 2
    sfin_ref = None
    if has_sfin:
        sfin_ref = refs[pos]
        pos += 1
    s_scr = refs[pos]
    j = pl.program_id(1)
    n2 = 2 * CHUNK
    chains = [(sq, p, d) for sq in range(ns) for p in range(PAIRS) for d in range(2)]
    units = [ch + (sub,) for sub in range(nsub) for ch in chains]
    idx = range(len(units))
    n_steps = nch // nsub

    @pl.when(j == 0)
    def _():
        for u, (sq, p, d) in enumerate(chains):
            s_scr[u] = s0_ref[sq, p, d] if has_s0 else jnp.zeros((n2, n2), F32)

    ri = lax.broadcasted_iota(jnp.int32, (n2, n2), 0)
    ci = lax.broadcasted_iota(jnp.int32, (n2, n2), 1)
    same = (ri >= CHUNK) == (ci >= CHUNK)
    incl = (same & (ri >= ci), same & (ri <= ci))
    strict = (same & (ri > ci), same & (ri < ci))
    eye = (ri == ci).astype(F32)
    ril, cil = ri % CHUNK, ci % CHUNK
    level_mask = []
    m = 1
    while m < CHUNK:
        level_mask.append(same & (ril // (2 * m) == cil // (2 * m)) & (ril // m != cil // m))
        m *= 2
    level_sel = [jnp.where(lm, 1.0, 0.0).astype(BF16) for lm in level_mask[1:]]

    def col_stack(arr, ca, cb):
        top = jnp.broadcast_to(arr[:, ca:ca + 1], (CHUNK, LANES))
        bot = jnp.broadcast_to(arr[:, cb:cb + 1], (CHUNK, LANES))
        return jnp.concatenate([top, bot], axis=0)

    def row_stack(arr, row):
        top = jnp.broadcast_to(arr[row:row + 1, :], (CHUNK, LANES))
        bot = jnp.broadcast_to(arr[CHUNK + row:CHUNK + row + 1, :], (CHUNK, LANES))
        return jnp.concatenate([top, bot], axis=0)

    def rows_of(c, sq, d, sub):
        k = c * nsub + sub
        r = (sq * nch + (k if d == 0 else nch - 1 - k)) * CHUNK
        return slice(r, r + CHUNK)

    feats = [dict() for _ in range(n_steps)]
    solved = [dict() for _ in range(n_steps)]
    states = [s_scr[u] for u in range(len(chains))]
    dirs = [u[2] for u in units]

    def features(c):
        f = feats[c]
        src = lambda a, sq, p, d, sub: (fwd_refs, bwd_refs)[d][a][rows_of(c, sq, d, sub), :]
        bg = [src(3, *u) for u in units]
        b_c = [col_stack(bg[i], 8 * p + 2 * d, 8 * p + 2 * d + 1)
               for i, (sq, p, d, sub) in enumerate(units)]
        g_c = [col_stack(bg[i], 8 * p + 4 + 2 * d, 8 * p + 5 + 2 * d)
               for i, (sq, p, d, sub) in enumerate(units)]
        yield
        ld = lambda a, u: (fwd_refs, bwd_refs)[u[2]][a][rows_of(c, u[0], u[2], u[3]), u[1] * LANES:(u[1] + 1) * LANES]
        k2 = [jnp.concatenate([ld(1, u)] * 2, axis=0) for u in units]
        ks = [jnp.where(same, k2[i], 0.0) for i in idx]
        qs = [jnp.where(same, jnp.concatenate([ld(0, u)] * 2, axis=0), 0.0) for u in units]
        ks_b = [ks[i].astype(BF16) for i in idx]
        prod = [_dot_nt(jnp.concatenate([ks_b[i], qs[i].astype(BF16)], axis=0), ks_b[i]) for i in idx]
        yield
        decay = [jnp.exp(jnp.where(incl[dirs[i]], g_c[i] - g_c[i].T, NEG)) for i in idx]
        yield
        eg = [jnp.exp(g_c[i]) for i in idx]
        g_last = [row_stack(g_c[i], CHUNK - 1 if dirs[i] == 0 else 0) for i in idx]
        f["l"] = [jnp.where(strict[dirs[i]], b_c[i] * prod[i][:n2] * decay[i], 0.0) for i in idx]
        yield
        v2 = [jnp.concatenate([ld(2, u)] * 2, axis=0) for u in units]
        f["rhs"] = [(b_c[i] * jnp.where(same, v2[i], eg[i] * pltpu.roll(k2[i], CHUNK, axis=1))).astype(BF16)
                    for i in idx]
        yield
        f["at"] = [(prod[i][n2:] * decay[i]).astype(BF16) for i in idx]
        f["qg"] = [(qs[i] * eg[i]).astype(BF16) for i in idx]
        yield
        f["kt"] = [(ks[i] * jnp.exp(g_last[i] - g_c[i])).T.astype(BF16) for i in idx]
        f["gl"] = [jnp.exp(g_last[i]) for i in idx]
        yield

    def inverse(c):
        f, o = feats[c], solved[c]
        l_mat = f["l"]
        t_inv = [eye - jnp.where(level_mask[0], l_mat[i], 0.0) for i in idx]
        l_b = [l_mat[i].astype(BF16) for i in idx]
        for lm in level_sel:
            t_b = [t_inv[i].astype(BF16) for i in idx]
            y = [_dot(l_b[i] * lm, t_b[i]).astype(BF16) for i in idx]
            yield
            t_inv = [t_inv[i] - _dot(t_b[i], y[i]) for i in idx]
            yield
        xs = [_dot(t_inv[i].astype(BF16), f["rhs"][i]) for i in idx]
        yield
        o["u"] = [jnp.where(same, xs[i], 0.0) for i in idx]
        o["wq"] = [jnp.concatenate([jnp.where(same, pltpu.roll(xs[i], CHUNK, axis=1), 0.0).astype(BF16),
                                    f["qg"][i]], axis=0) for i in idx]
        yield

    def scan(c):
        f, o = feats[c], solved[c]
        for sub in range(nsub):
            ui = [sub * len(chains) + k for k in range(len(chains))]
            ws = [_dot(o["wq"][i], states[k].astype(BF16)) for k, i in enumerate(ui)]
            yield
            u_b = [(o["u"][i] - ws[k][:n2]).astype(BF16) for k, i in enumerate(ui)]
            yield
            o_st = [ws[k][n2:] + _dot(f["at"][i], u_b[k]) for k, i in enumerate(ui)]
            for k, i in enumerate(ui):
                states[k] = f["gl"][i] * states[k] + _dot(f["kt"][i], u_b[k])
            yield
            for k, i in enumerate(ui):
                sq, p, d, _ = units[i]
                dst = of_ref if d == 0 else ob_ref
                dst[rows_of(c, sq, d, sub), p * LANES:(p + 1) * LANES] = o_st[k][:CHUNK] + o_st[k][CHUNK:]
            yield
        feats[c] = solved[c] = None

    for t in range(n_steps + 2):
        gens = []
        if 1 <= t <= n_steps:
            gens.append(inverse(t - 1))
        if t < n_steps:
            gens.append(features(t))
        if 2 <= t:
            gens.append(scan(t - 2))
        _run_interleaved(gens)

    for u in range(len(chains)):
        s_scr[u] = states[u]
    if has_sfin:
        @pl.when(j == pl.num_programs(1) - 1)
        def _():
            for u, (sq, p, d) in enumerate(chains):
                sfin_ref[sq, p, d] = states[u]


def _gdn_scan(feat, s0, *, t, row0, n_seq, want_state):
    tb = math.gcd(t, _SCAN_ROWS)
    nb = t // tb
    ns = math.gcd(n_seq, _SCAN_ROWS // tb) if nb == 1 else 1
    nsub = math.gcd(tb // CHUNK, max(1, _SCAN_UNITS // (ns * PAIRS)))
    rows = ns * tb
    assert row0 % rows == 0
    blk0 = row0 // rows
    n2 = 2 * CHUNK
    w = PAIRS * LANES
    has_s0 = s0 is not None

    def blocks(mirror):
        row = (lambda b, j: b * nb + (nb - 1 - j)) if mirror else (lambda b, j: b * nb + j)
        return ([pl.BlockSpec((rows, w), lambda b, j: (blk0 + row(b, j), 0))] * 3
                + [pl.BlockSpec((rows, LANES), lambda b, j: (blk0 + row(b, j), 0))]), row

    f_specs, f_row = blocks(False)
    b_specs, b_row = blocks(True)
    in_specs = f_specs + b_specs
    args = list(feat) * 2
    state_spec = pl.BlockSpec((ns, PAIRS, 2, n2, n2), lambda b, j: (b, 0, 0, 0, 0))
    if has_s0:
        in_specs.append(state_spec)
        args.append(s0)
    out_shape = [jax.ShapeDtypeStruct((n_seq * t, w), F32)] * 2
    out_specs = [pl.BlockSpec((rows, w), lambda b, j: (f_row(b, j), 0)),
                 pl.BlockSpec((rows, w), lambda b, j: (b_row(b, j), 0))]
    if want_state:
        out_shape.append(jax.ShapeDtypeStruct((n_seq, PAIRS, 2, n2, n2), F32))
        out_specs.append(state_spec)
    return pl.pallas_call(
        functools.partial(_gdn_scan_kernel, nch=tb // CHUNK, ns=ns, nsub=nsub, has_s0=has_s0, has_sfin=want_state),
        out_shape=out_shape,
        grid=(n_seq // ns, nb),
        in_specs=in_specs,
        out_specs=out_specs,
        scratch_shapes=[pltpu.VMEM((ns * 2 * PAIRS, n2, n2), F32)],
        compiler_params=_cparams(("parallel", "arbitrary"), "gdn_scan"),
        name="gdn_scan_ctx" if want_state else "gdn_scan_lat",
    )(*args)


def _prep_kernel(*refs, rope):
    bq_ref, cq_ref, bk_ref, bv_ref, ck_ref, cv_ref, qkn_ref = refs[:7]
    pos = 7
    cs = sn = None
    if rope:
        cs, sn = refs[7][...], refs[8][...]
        pos = 11
    qb_ref, qc_ref, kdb_ref, vdb_ref, kdc_ref, vdc_ref = refs[pos:pos + 6]
    pos += 6
    tr = bq_ref.shape[0]
    lane = lax.broadcasted_iota(jnp.int32, (tr, LANES), 1)
    lane_lo = lane < HEAD_DIM
    first_half = (lane % HEAD_DIM) < HEAD_DIM // 2
    qkn = qkn_ref[...]

    ones_bd = ((lax.broadcasted_iota(jnp.int32, (LANES, LANES), 0) < HEAD_DIM)
               == (lax.broadcasted_iota(jnp.int32, (LANES, LANES), 1) < HEAD_DIM)).astype(BF16)

    def norm(x, w_row):
        sq = x * x
        hi = sq.astype(BF16)
        lo = (sq - hi.astype(F32)).astype(BF16)
        ms = (_dot(hi, ones_bd) + _dot(lo, ones_bd)) * (1.0 / HEAD_DIM)
        return x * lax.rsqrt(ms + EPS) * w_row

    def rot(y):
        if not rope:
            return y
        partner = jnp.where(first_half, pltpu.roll(y, LANES - HEAD_DIM // 2, axis=1),
                            pltpu.roll(y, HEAD_DIM // 2, axis=1))
        return y * cs + partner * sn

    def dup(x, o_ref):
        lo = jnp.where(lane_lo, x, 0.0)
        hi = jnp.where(lane_lo, 0.0, x)
        d0 = (lo + pltpu.roll(lo, HEAD_DIM, axis=1)).astype(BF16)
        d1 = (hi + pltpu.roll(hi, HEAD_DIM, axis=1)).astype(BF16)
        if len(o_ref.shape) == 3:
            o_ref[0, :, 0:LANES] = d0
            o_ref[0, :, LANES:2 * LANES] = d1
        else:
            o_ref[:, 0:LANES] = d0
            o_ref[:, LANES:2 * LANES] = d1

    for q_ref, o_ref, wi in ((bq_ref, qb_ref, 0), (cq_ref, qc_ref, 2)):
        for g in range(PAIRS):
            y = rot(norm(q_ref[:, g * LANES:(g + 1) * LANES], qkn[wi:wi + 1, :])) * HEAD_DIM ** -0.5
            o_ref[2 * g] = jnp.where(lane_lo, y, 0.0).astype(BF16)
            o_ref[2 * g + 1] = jnp.where(lane_lo, 0.0, y).astype(BF16)
    kb_n = norm(bk_ref[...], qkn[1:2, :])
    kc_n = norm(ck_ref[...], qkn[3:4, :])
    if not rope:
        kbn_ref, vbn_ref, kcn_ref, vcn_ref = refs[pos:pos + 4]
        kbn_ref[...] = kb_n
        vbn_ref[...] = bv_ref[...]
        kcn_ref[...] = kc_n
        vcn_ref[...] = cv_ref[...]
    dup(rot(kb_n), kdb_ref)
    dup(bv_ref[...], vdb_ref)
    dup(rot(kc_n), kdc_ref)
    dup(cv_ref[...], vdc_ref)


def _attn_prep(proj, qkn_rows, rope_tables, glob_kv, *, row0, n_rows, t):
    rope = rope_tables is not None
    tr = math.gcd(t, 512)
    blk0 = row0 // tr

    def col(g, width):
        gw = width // LANES
        return pl.BlockSpec((tr, width), lambda i, g=g, gw=gw: (blk0 + i, g // gw))

    in_specs = [col(G_BQ, 4 * LANES), col(G_CQ, 4 * LANES), col(G_BK, LANES), col(G_BV, LANES),
                col(G_CK, LANES), col(G_CV, LANES), pl.BlockSpec((4, LANES), lambda i: (0, 0))]
    args = [proj] * 6 + [qkn_rows]
    q_shape = jax.ShapeDtypeStruct((HEADS, n_rows, LANES), BF16)
    kv_shape = jax.ShapeDtypeStruct((n_rows, 2 * LANES), BF16)
    q_spec = pl.BlockSpec((HEADS, tr, LANES), lambda i: (0, i, 0))
    kv_spec = pl.BlockSpec((tr, 2 * LANES), lambda i: (i, 0))
    aliases = {}
    if rope:
        per_seq = t // tr
        in_specs += [pl.BlockSpec((tr, LANES), lambda i: (i % per_seq, 0))] * 2 + [_any_spec()] * 2
        aliases = {len(args) + 2: 2, len(args) + 3: 3}
        args += list(rope_tables) + list(glob_kv)
        past = glob_kv[0].shape[1] - t
        assert past % tr == 0
        glob_spec = pl.BlockSpec((1, tr, 2 * LANES), lambda i: (i // per_seq, past // tr + i % per_seq, 0))
        glob_shape = jax.ShapeDtypeStruct(glob_kv[0].shape, BF16)
        out_shape = [q_shape, q_shape, glob_shape, glob_shape, kv_shape, kv_shape]
        out_specs = [q_spec, q_spec, glob_spec, glob_spec, kv_spec, kv_spec]
    else:
        out_shape = [q_shape, q_shape] + [kv_shape] * 4 + [jax.ShapeDtypeStruct((n_rows, LANES), F32)] * 4
        out_specs = [q_spec, q_spec] + [kv_spec] * 4 + [pl.BlockSpec((tr, LANES), lambda i: (i, 0))] * 4
    return pl.pallas_call(
        functools.partial(_prep_kernel, rope=rope),
        out_shape=out_shape,
        grid=(n_rows // tr,),
        in_specs=in_specs,
        out_specs=out_specs,
        input_output_aliases=aliases,
        compiler_params=_cparams(("parallel",), "attn_prep"),
        name="attn_prep_lat" if rope else "attn_prep_ctx",
    )(*args)


def _stacked_q(q_ref, g, tq):
    return q_ref[GROUP_HEADS * g:GROUP_HEADS * (g + 1)].reshape(GROUP_HEADS * tq, LANES)


def _unstack_group(o_st, tq, lane_lo):
    a = jnp.where(lane_lo, o_st[0:tq], o_st[tq:2 * tq])
    b = jnp.where(lane_lo, o_st[2 * tq:3 * tq], o_st[3 * tq:4 * tq])
    return jnp.concatenate([a, b], axis=1)


def _with_ones(v, ones):
    return jnp.concatenate([v, ones], axis=1)


def _sink_col(sink_row, g, tq):
    parts = [jnp.broadcast_to(sink_row[:, GROUP_HEADS * g + j:GROUP_HEADS * g + j + 1], (tq, 1))
             for j in range(GROUP_HEADS)]
    return jnp.concatenate(parts, axis=0)


def _attn_ctx_kernel(qb_ref, kdb_ref, vdb_ref, qc_ref, kdc_ref, vdc_ref, sink_ref, yb_ref, yc_ref):
    tq = qb_ref.shape[1]
    lane_lo_q = _lane_lo(tq)
    sink_row = sink_ref[...]
    groups = range(KV_HEADS)
    gs = [slice(g * LANES, (g + 1) * LANES) for g in groups]
    jobs = [(qb_ref, kdb_ref, vdb_ref, yb_ref, False, g) for g in groups]
    jobs += [(qc_ref, kdc_ref, vdc_ref, yc_ref, True, g) for g in groups]
    s = [_dot_nt(_stacked_q(q_ref, g, tq), kd_ref[:, gs[g]]) for q_ref, kd_ref, _, _, _, g in jobs]
    for si, (_, _, vd_ref, y_ref, use_sink, g) in zip(s, jobs):
        m = jnp.max(si, axis=-1, keepdims=True)
        if use_sink:
            snk = _sink_col(sink_row, g, tq)
            m = jnp.maximum(m, snk)
        p = jnp.exp(si - m)
        l = jnp.sum(p, axis=-1, keepdims=True)
        if use_sink:
            l = l + jnp.exp(snk - m)
        o = _dot(p.astype(BF16), vd_ref[:, gs[g]]) / l
        y_ref[:, 2 * g * LANES:(2 * g + 2) * LANES] = _unstack_group(o, tq, lane_lo_q).astype(y_ref.dtype)


def _attn_ctx(qb, kdb, vdb, qc, kdc, vdc, sink_row, *, t):
    n = kdb.shape[0]
    q_spec = pl.BlockSpec((HEADS, t, LANES), lambda b: (0, b, 0))
    kv_spec = pl.BlockSpec((t, 2 * LANES), lambda b: (b, 0))
    y_spec = pl.BlockSpec((t, 4 * LANES), lambda b: (b, 0))
    return pl.pallas_call(
        _attn_ctx_kernel,
        out_shape=[jax.ShapeDtypeStruct((n, 4 * LANES), BF16)] * 2,
        grid=(n // t,),
        in_specs=[q_spec, kv_spec, kv_spec, q_spec, kv_spec, kv_spec,
                  pl.BlockSpec((1, LANES), lambda b: (0, 0))],
        out_specs=[y_spec, y_spec],
        compiler_params=_cparams(("parallel",), "attn_ctx"),
        name="attn_ctx",
    )(qb, kdb, vdb, qc, kdc, vdc, sink_row)


_TQ_B = 256
_TK_B = 1536


def _attn_glob_kernel(q_ref, kd_ref, vd_ref, y_ref, s_scr, *, tk):
    tq = q_ref.shape[1]
    n_blk = kd_ref.shape[1] // tk
    groups = range(KV_HEADS)
    gs = [slice(g * LANES, (g + 1) * LANES) for g in groups]
    q = [_stacked_q(q_ref, g, tq) for g in groups]

    def scores_into(j, slot):
        for g in groups:
            s_scr[slot, g] = _dot_nt(q[g], kd_ref[0, j * tk:(j + 1) * tk, gs[g]])

    rows = GROUP_HEADS * tq
    m = [jnp.full((rows, LANES), NEG, F32) for _ in groups]
    acc = [jnp.zeros((rows, 2 * LANES), F32) for _ in groups]
    ones = jnp.ones((tk, LANES), BF16)
    scores_into(0, 0)
    for j in range(n_blk):
        if j + 1 < n_blk:
            scores_into(j + 1, (j + 1) % 2)
        for g in groups:
            s = s_scr[j % 2, g]
            m_new = jnp.maximum(m[g], jnp.max(s, axis=-1, keepdims=True))
            p = jnp.exp(s - m_new[:, 0:1]).astype(BF16)
            alpha = jnp.exp(m[g] - m_new)
            acc[g] = (jnp.concatenate([alpha, alpha], axis=1) * acc[g]
                      + _dot(p, _with_ones(vd_ref[0, j * tk:(j + 1) * tk, gs[g]], ones)))
            m[g] = m_new
    lane_lo_q = _lane_lo(tq)
    for g in groups:
        o = acc[g][:, :LANES] / acc[g][:, LANES:]
        y_ref[:, 2 * g * LANES:(2 * g + 2) * LANES] = _unstack_group(o, tq, lane_lo_q).astype(y_ref.dtype)


def _attn_glob(q, kd, vd, *, t):
    n = q.shape[1]
    s_len = kd.shape[1]
    tq, tk = math.gcd(t, _TQ_B), math.gcd(s_len, _TK_B)
    per_seq = t // tq
    kv_spec = pl.BlockSpec((1, s_len, 2 * LANES), lambda b, i: (b, 0, 0))
    return pl.pallas_call(
        functools.partial(_attn_glob_kernel, tk=tk),
        out_shape=jax.ShapeDtypeStruct((n, 4 * LANES), BF16),
        grid=(n // t, per_seq),
        in_specs=[pl.BlockSpec((HEADS, tq, LANES), lambda b, i: (0, b * per_seq + i, 0)), kv_spec, kv_spec],
        out_specs=pl.BlockSpec((tq, 4 * LANES), lambda b, i: (b * per_seq + i, 0)),
        scratch_shapes=[pltpu.VMEM((2, KV_HEADS, GROUP_HEADS * tq, tk), F32)],
        compiler_params=_cparams(("parallel", "parallel"), "attn_glob"),
        name="attn_glob",
    )(q, kd, vd)


def _attn_win_kernel(q_ref, kx_ref, vx_ref, kp_ref, kc_ref, kn_ref, vp_ref, vc_ref, vn_ref,
                     sink_ref, y_ref):
    i = pl.program_id(1)
    nblk = pl.num_programs(1)
    tq = q_ref.shape[1]
    rows = GROUP_HEADS * tq
    lane_lo_q = _lane_lo(tq)
    sink_row = sink_ref[...]
    rr_w = lax.broadcasted_iota(jnp.int32, (rows, WINDOW), 0) % tq
    cc_w = lax.broadcasted_iota(jnp.int32, (rows, WINDOW), 1)
    rr_q = lax.broadcasted_iota(jnp.int32, (rows, tq), 0) % tq
    cc_q = lax.broadcasted_iota(jnp.int32, (rows, tq), 1)
    mask_prev = (cc_w >= rr_w) & (i > 0)
    mask_cur = jnp.abs(rr_q - cc_q) <= WINDOW
    mask_next = (cc_w <= rr_w - (tq - WINDOW)) & (i < nblk - 1)
    groups = range(KV_HEADS)
    gs = [slice(g * LANES, (g + 1) * LANES) for g in groups]
    qs = [_stacked_q(q_ref, g, tq) for g in groups]
    p_len = kx_ref.shape[1]
    mask_edge = jnp.concatenate([mask_prev, mask_next], axis=1)
    edge = lambda p_ref, n_ref, g: jnp.concatenate([p_ref[:, gs[g]], n_ref[:, gs[g]]], axis=0)
    s = [jnp.concatenate([_dot_nt(qs[g], kx_ref[0, :, gs[g]]),
                          jnp.where(mask_edge, _dot_nt(qs[g], edge(kp_ref, kn_ref, g)), NEG),
                          jnp.where(mask_cur, _dot_nt(qs[g], kc_ref[:, gs[g]]), NEG)], axis=1) for g in groups]
    parts = ((0, p_len, lambda g: vx_ref[0, :, gs[g]]), (p_len, 2 * WINDOW, lambda g: edge(vp_ref, vn_ref, g)),
             (p_len + 2 * WINDOW, tq, lambda g: vc_ref[:, gs[g]]))
    for g in groups:
        snk = _sink_col(sink_row, g, tq)
        m = jnp.maximum(jnp.max(s[g], axis=-1, keepdims=True), snk)
        p = jnp.exp(s[g] - m).astype(BF16)
        ol = None
        for start, size, val in parts:
            d = _dot(p[:, start:start + size], _with_ones(val(g), jnp.ones((size, LANES), BF16)))
            ol = d if ol is None else ol + d
        o = ol[:, :LANES] / (ol[:, LANES:] + jnp.exp(snk - m))
        y_ref[:, 2 * g * LANES:(2 * g + 2) * LANES] = _unstack_group(o, tq, lane_lo_q).astype(y_ref.dtype)


def _attn_win(q, kd, vd, kx, vx, sink_row, *, t):
    n = q.shape[1]
    p_len = kx.shape[1]
    tq = math.gcd(t, 2 * WINDOW)
    nblk = t // tq
    per = tq // WINDOW
    wblk = t // WINDOW
    x_spec = pl.BlockSpec((1, p_len, 2 * LANES), lambda b, i: (b, 0, 0))
    prev = pl.BlockSpec((WINDOW, 2 * LANES), lambda b, i: (b * wblk + jnp.maximum(i * per - 1, 0), 0))
    cur = pl.BlockSpec((tq, 2 * LANES), lambda b, i: (b * nblk + i, 0))
    nxt = pl.BlockSpec((WINDOW, 2 * LANES), lambda b, i: (b * wblk + jnp.minimum((i + 1) * per, wblk - 1), 0))
    return pl.pallas_call(
        _attn_win_kernel,
        out_shape=jax.ShapeDtypeStruct((n, 4 * LANES), BF16),
        grid=(n // t, nblk),
        in_specs=[pl.BlockSpec((HEADS, tq, LANES), lambda b, i: (0, b * nblk + i, 0)),
                  x_spec, x_spec, prev, cur, nxt, prev, cur, nxt,
                  pl.BlockSpec((1, LANES), lambda b, i: (0, 0))],
        out_specs=pl.BlockSpec((tq, 4 * LANES), lambda b, i: (b * nblk + i, 0)),
        compiler_params=_cparams(("parallel", "parallel"), "attn_win"),
        name="attn_win",
    )(q, kx, vx, kd, kd, kd, vd, vd, vd, sink_row)


def _mix_kernel(x_ref, ofc_ref, obc_ref, ybc_ref, ycc_ref, ofl_ref, obl_ref, ybl_ref, ycl_ref, z_ref,
                ga_ref, gb_ref, gc_ref, mod_ref, anorm_ref, wa_ref, wb_ref, wc_ref, wo_ref, o_ref, *, ctx_tiles):
    d = x_ref.shape[1]
    tm = x_ref.shape[0]
    lane_lo = _lane_lo(tm)
    anorm = anorm_ref[...]
    is_ctx = pl.program_id(0) < ctx_tiles

    def pick(c_ref, l_ref, cols=slice(None)):
        return jnp.where(is_ctx, c_ref[:, cols], l_ref[:, cols])

    ya = []
    for g in range(PAIRS):
        gl = slice(g * LANES, (g + 1) * LANES)
        o = pick(ofc_ref, ofl_ref, gl) + pick(obc_ref, obl_ref, gl)
        ms = _half_sums(o * o, lane_lo) * (1.0 / HEAD_DIM)
        ya.append((o * lax.rsqrt(ms + EPS) * anorm * _silu(z_ref[:, gl])).astype(BF16))
    ya = jnp.concatenate(ya, axis=1)
    merged = (_sigmoid(ga_ref[...]) * _dot(ya, wa_ref[...])
              + _sigmoid(gb_ref[...]) * _dot(pick(ybc_ref, ybl_ref), wb_ref[...])
              + _sigmoid(gc_ref[...]) * _dot(pick(ycc_ref, ycl_ref), wc_ref[...]))
    gate1 = mod_ref[0][:, 2 * d:3 * d]
    o_ref[...] = x_ref[...] + gate1 * _dot(merged.astype(BF16), wo_ref[...])


def _mix(x, ctx_parts, lat_parts, proj, mods, anorm_row, wa, wb, wc, wo, n_ctx, dec_seq):
    nt, d = x.shape
    tm = _row_tile(n_ctx, dec_seq, 512)
    ctx_tiles = n_ctx // tm
    w = PAIRS * LANES
    c_spec = pl.BlockSpec((tm, w), lambda i: (jnp.minimum(i, ctx_tiles - 1), 0))
    l_spec = pl.BlockSpec((tm, w), lambda i: (jnp.maximum(i - ctx_tiles, 0), 0))
    gw = d // LANES

    def gate(k):
        return pl.BlockSpec((tm, d), lambda i, k=k: (i, G_GATES // gw + k))

    def full(a):
        return pl.BlockSpec(a.shape, lambda i: (0, 0))

    return pl.pallas_call(
        functools.partial(_mix_kernel, ctx_tiles=ctx_tiles),
        out_shape=jax.ShapeDtypeStruct((nt, d), F32),
        grid=(nt // tm,),
        in_specs=[pl.BlockSpec((tm, d), lambda i: (i, 0))] + [c_spec] * 4 + [l_spec] * 4
                 + [pl.BlockSpec((tm, w), lambda i: (i, G_Z // PAIRS)), gate(0), gate(1), gate(2),
                    pl.BlockSpec((1, 1, 6 * d), lambda i: (_mod_index(i, tm, n_ctx, dec_seq), 0, 0)),
                    full(anorm_row), full(wa), full(wb), full(wc), full(wo)],
        out_specs=pl.BlockSpec((tm, d), lambda i: (i, 0)),
        compiler_params=_cparams(("parallel",), "mix_out"),
        name="mix_out",
    )(x, *ctx_parts, *lat_parts, proj, proj, proj, proj, mods, anorm_row, wa, wb, wc, wo)


_FF_CHUNK = 1024


def _ffn_kernel(x_ref, mod_ref, ln_ref, w1_ref, w2_ref, o_ref):
    d = x_ref.shape[1]
    m = mod_ref[0]
    x = x_ref[...]
    h = _mod_norm(x, ln_ref[...], m[:, 3 * d:4 * d], m[:, 4 * d:5 * d]).astype(BF16)
    acc = None
    for f in range(0, w1_ref.shape[1], _FF_CHUNK):
        hid = jnp.maximum(_dot(h, w1_ref[:, f:f + _FF_CHUNK]), 0.0)
        part = _dot((hid * hid).astype(BF16), w2_ref[f:f + _FF_CHUNK, :])
        acc = part if acc is None else acc + part
    o_ref[...] = x + m[:, 5 * d:6 * d] * acc


def _ffn(x, mods, ln2, w1, w2, n_ctx, dec_seq):
    nt, d = x.shape
    tm = _row_tile(n_ctx, dec_seq, 512)
    return pl.pallas_call(
        _ffn_kernel,
        out_shape=jax.ShapeDtypeStruct((nt, d), F32),
        grid=(nt // tm,),
        in_specs=[pl.BlockSpec((tm, d), lambda i: (i, 0)),
                  pl.BlockSpec((1, 1, 6 * d), lambda i: (_mod_index(i, tm, n_ctx, dec_seq), 0, 0)),
                  pl.BlockSpec((1, d), lambda i: (0, 0)),
                  pl.BlockSpec(w1.shape, lambda i: (0, 0)),
                  pl.BlockSpec(w2.shape, lambda i: (0, 0))],
        out_specs=pl.BlockSpec((tm, d), lambda i: (i, 0)),
        compiler_params=_cparams(("parallel",), "ffn"),
        name="ffn",
    )(x, mods, ln2, w1, w2)


def _ba_lane_order():
    cols = []
    for p in range(PAIRS):
        for base in (0, 2 * HEADS):
            for d in range(2):
                cols += [base + d * HEADS + 2 * p, base + d * HEADS + 2 * p + 1]
    return cols


def _reorder_w_in(w_in):
    a_w = HEADS * HEAD_DIM
    kvw = KV_HEADS * HEAD_DIM
    o_beta = 4 * a_w
    o_bq = o_beta + 4 * HEADS
    o_bk = o_bq + a_w
    o_bv = o_bk + kvw
    o_cq = o_bv + kvw
    o_ck = o_cq + a_w
    o_cv = o_ck + kvw
    o_g = o_cv + kvw
    seg = lambda a, n: w_in[:, :, a:a + n]
    ba = jnp.take(seg(o_beta, 4 * HEADS), jnp.array(_ba_lane_order(), jnp.int32), axis=2)
    ba = jnp.pad(ba, ((0, 0), (0, 0), (0, 2 * LANES - ba.shape[2])))
    out = jnp.concatenate([seg(0, o_beta), seg(o_bq, a_w), seg(o_cq, a_w), seg(o_g, w_in.shape[2] - o_g),
                           seg(o_bk, kvw), seg(o_bv, kvw), seg(o_ck, kvw), seg(o_cv, kvw), ba], axis=2)
    assert out.shape[2] == IN_COLS_PAD
    return out.astype(BF16)


def _decay_rows(vals):
    rows = []
    for p in range(PAIRS):
        r = jnp.stack([vals[:, 0, 2 * p], vals[:, 0, 2 * p + 1], vals[:, 1, 2 * p], vals[:, 1, 2 * p + 1]], axis=-1)
        rows.append(jnp.pad(r, ((0, 0), (4, 0))))
    out = jnp.concatenate(rows, axis=-1)
    return jnp.pad(out, ((0, 0), (0, LANES - out.shape[1])))[:, None, :]


def _rope_tables(n_tokens):
    rows = n_tokens // GRID_W
    row_id = jnp.repeat(jnp.arange(rows, dtype=F32), GRID_W)
    col_id = jnp.tile(jnp.arange(GRID_W, dtype=F32), rows)
    n_freq = HEAD_DIM // 4
    inv_freq = ROPE_THETA ** (-jnp.arange(n_freq, dtype=F32) / n_freq)
    ang = jnp.concatenate([row_id[:, None] * inv_freq, col_id[:, None] * inv_freq], axis=-1)
    cos, sin = jnp.cos(ang), jnp.sin(ang)
    cs = jnp.tile(cos, (1, 4))
    sn = jnp.tile(jnp.concatenate([-sin, sin], axis=-1), (1, 2))
    return cs, sn


def _dup_kv(x, pad_rows=0):
    x = jnp.swapaxes(x, 0, 1)
    out = jnp.concatenate([x[..., 0, :], x[..., 0, :], x[..., 1, :], x[..., 1, :]], axis=-1).astype(BF16)
    return jnp.pad(out, ((0, 0), (0, 0), (0, pad_rows), (0, 0)))


def _block_diag_states(s):
    b, depth = s.shape[:2]
    s = s.reshape(b, depth, 2, PAIRS, 2, HEAD_DIM, HEAD_DIM).transpose(1, 0, 3, 2, 4, 5, 6)
    z = jnp.zeros_like(s[..., 0, :, :])
    top = jnp.concatenate([s[..., 0, :, :], z], axis=-1)
    bot = jnp.concatenate([z, s[..., 1, :, :]], axis=-1)
    return jnp.concatenate([top, bot], axis=-2)


def _diag_states(s):
    a = s[:, :, :, :HEAD_DIM, :HEAD_DIM]
    bb = s[:, :, :, HEAD_DIM:, HEAD_DIM:]
    out = jnp.stack([a, bb], axis=3)
    b = s.shape[0]
    return out.transpose(0, 2, 1, 3, 4, 5).reshape(b, 2, HEADS, HEAD_DIM, HEAD_DIM)


def kernel(x_prompt, x_sample, cache_k_glob, cache_v_glob, cache_k_win, cache_v_win, state_delta, c, c_ctx, w_mod, b_mod, ln1, ln2, w_in, conv_qkv, a_log, dt_bias, a_norm, qk_norm, sink, w_br_a, w_br_b, w_br_c, w_o, w_ff1, w_ff2):
    batch, seq, d = x_prompt.shape
    dec_batch, dec_seq, _ = x_sample.shape
    depth = w_mod.shape[0]
    n_ctx = batch * seq
    n_lat = dec_batch * dec_seq
    assert 1 + dec_batch <= SUBLANES

    w_in_r = _reorder_w_in(w_in)
    wa, wb, wc, wo = (w.astype(BF16) for w in (w_br_a, w_br_b, w_br_c, w_o))
    w1, w2 = w_ff1.astype(BF16), w_ff2.astype(BF16)
    alog_rows = _decay_rows(a_log)
    dtb_rows = _decay_rows(dt_bias)
    anorm_rows = jnp.tile(a_norm, (1, 2))[:, None, :]
    qkn_rows = jnp.tile(qk_norm, (1, 1, 2))
    sink_rows = jnp.pad(sink, ((0, 0), (0, LANES - HEADS)))[:, None, :]
    rope_tables = _rope_tables(dec_seq)
    cvec = jnp.concatenate([c_ctx[None], c, jnp.zeros((SUBLANES - 1 - dec_batch, d), F32)], axis=0)
    mods_all = _mod_vectors(cvec, w_mod, b_mod)
    glob_k, glob_v = _dup_kv(cache_k_glob, dec_seq), _dup_kv(cache_v_glob, dec_seq)
    win_k, win_v = _dup_kv(cache_k_win), _dup_kv(cache_v_win)
    s0_lat = _block_diag_states(state_delta)

    x = jnp.concatenate([x_prompt.reshape(n_ctx, d), x_sample.reshape(n_lat, d)], axis=0)
    new_kg, new_vg, new_kw, new_vw, new_st = [], [], [], [], []
    for l in range(depth):
        mods = mods_all[l][:, None, :]
        proj = _in_proj(x, mods, ln1[l][None], w_in_r[l], n_ctx, dec_seq)

        feat = _gdn_prep(proj, conv_qkv[l], alog_rows[l], dtb_rows[l], n_ctx, seq, dec_seq)
        of_c, ob_c, s_fin = _gdn_scan(feat, None, t=seq, row0=0, n_seq=batch, want_state=True)
        of_l, ob_l = _gdn_scan(feat, s0_lat[l], t=dec_seq, row0=n_ctx, n_seq=dec_batch, want_state=False)

        qb_c, qc_c, kdb_c, vdb_c, kdc_c, vdc_c, kb_n, vb_n, kc_n, vc_n = _attn_prep(
            proj, qkn_rows[l], None, None, row0=0, n_rows=n_ctx, t=seq)
        qb_l, qc_l, kd_all, vd_all, kdc_l, vdc_l = _attn_prep(
            proj, qkn_rows[l], rope_tables, (glob_k[l], glob_v[l]), row0=n_ctx, n_rows=n_lat, t=dec_seq)

        yb_c, yc_c = _attn_ctx(qb_c, kdb_c, vdb_c, qc_c, kdc_c, vdc_c, sink_rows[l], t=seq)
        yb_l = _attn_glob(qb_l, kd_all, vd_all, t=dec_seq)
        yc_l = _attn_win(qc_l, kdc_l, vdc_l, win_k[l], win_v[l], sink_rows[l], t=dec_seq)

        x = _mix(x, (of_c, ob_c, yb_c, yc_c), (of_l, ob_l, yb_l, yc_l), proj, mods, anorm_rows[l],
                 wa[l], wb[l], wc[l], wo[l], n_ctx, dec_seq)
        x = _ffn(x, mods, ln2[l][None], w1[l], w2[l], n_ctx, dec_seq)

        kv_shape = (batch, seq, KV_HEADS, HEAD_DIM)
        new_kg.append(kb_n.reshape(kv_shape))
        new_vg.append(vb_n.reshape(kv_shape))
        new_kw.append(kc_n.reshape(kv_shape))
        new_vw.append(vc_n.reshape(kv_shape))
        new_st.append(_diag_states(s_fin))

    y_prompt = x[:n_ctx].reshape(batch, seq, d)
    y_sample = x[n_ctx:].reshape(dec_batch, dec_seq, d)
    return (y_prompt, y_sample, jnp.stack(new_kg, axis=1), jnp.stack(new_vg, axis=1),
            jnp.stack(new_kw, axis=1), jnp.stack(new_vw, axis=1), jnp.stack(new_st, axis=1))
```

```python
import functools
import math

import jax
import jax.numpy as jnp
from jax import lax
from jax.experimental import pallas as pl
from jax.experimental.pallas import tpu as pltpu

F32 = jnp.float32
BF16 = jnp.bfloat16

LANES = 128
SUBLANES = 8

HEAD_DIM = 64
HEADS = 8
KV_HEADS = 2
GROUP_HEADS = HEADS // KV_HEADS
CHUNK = 64
CONV_K = 5
GRID_W = 64
WINDOW = 128
ROPE_THETA = 10000.0
EPS = 1e-6
NEG = -1e30
PAIRS = HEADS // 2

G_Q, G_K, G_V, G_Z = 0, 4, 8, 12
G_BQ, G_CQ = 16, 20
G_GATES = 24
G_BK, G_BV, G_CK, G_CV = 48, 49, 50, 51
G_BA = 52
N_GROUPS = 54
IN_COLS_PAD = N_GROUPS * LANES


VMEM_MIB_V7X = 64
MXU_TILE = 256

_VMEM_MIB = {
    "mod_vectors": 40,
    "in_proj": 48,
    "gdn_prep": 32,
    "gdn_scan": 48,
    "attn_prep": 32,
    "attn_ctx": 32,
    "attn_glob": 56,
    "attn_win": 32,
    "mix_out": 48,
    "ffn": 56,
}
assert max(_VMEM_MIB.values()) < VMEM_MIB_V7X


def _cparams(sem, call):
    return pltpu.CompilerParams(dimension_semantics=sem, vmem_limit_bytes=_VMEM_MIB[call] << 20)


def _dot(a, b):
    return jnp.dot(a, b, preferred_element_type=F32)


def _dot_nt(a, b):
    return lax.dot_general(a, b, (((1,), (1,)), ((), ())), preferred_element_type=F32)


def _sigmoid(x):
    return 0.5 * jnp.tanh(0.5 * x) + 0.5


def _silu(x):
    return x * _sigmoid(x)


def _softplus(x):
    return jnp.maximum(x, 0.0) + jnp.log1p(jnp.exp(-jnp.abs(x)))


def _half_sums(x, lane_lo):
    s_lo = jnp.sum(jnp.where(lane_lo, x, 0.0), axis=-1, keepdims=True)
    s_hi = jnp.sum(jnp.where(lane_lo, 0.0, x), axis=-1, keepdims=True)
    return jnp.where(lane_lo, s_lo, s_hi)


def _lane_lo(rows):
    return lax.broadcasted_iota(jnp.int32, (rows, LANES), 1) < HEAD_DIM


def _row_tile(n_ctx, dec_seq, pref):
    return math.gcd(math.gcd(n_ctx, dec_seq), pref)


def _mod_index(i, tm, n_ctx, dec_seq):
    ctx_tiles = n_ctx // tm
    return jnp.where(i < ctx_tiles, 0, 1 + (i - ctx_tiles) // (dec_seq // tm))


def _any_spec():
    return pl.BlockSpec(memory_space=pl.ANY)


def _mod_kernel(c_ref, w_ref, b_ref, o_ref):
    s = _silu(c_ref[...])
    w = w_ref[0]
    s_hi = s.astype(BF16)
    s_lo = (s - s_hi.astype(F32)).astype(BF16)
    w_hi = w.astype(BF16)
    w_lo = (w - w_hi.astype(F32)).astype(BF16)
    acc = _dot(s_hi, w_hi) + (_dot(s_lo, w_hi) + _dot(s_hi, w_lo))
    o_ref[0] = acc + b_ref[0]


def _mod_vectors(cvec, w_mod, b_mod):
    depth, d, n = w_mod.shape
    tn = n // 4
    return pl.pallas_call(
        _mod_kernel,
        out_shape=jax.ShapeDtypeStruct((depth, SUBLANES, n), F32),
        grid=(depth, n // tn),
        in_specs=[pl.BlockSpec((SUBLANES, d), lambda l, j: (0, 0)),
                  pl.BlockSpec((1, d, tn), lambda l, j: (l, 0, j)),
                  pl.BlockSpec((1, 1, tn), lambda l, j: (l, 0, j))],
        out_specs=pl.BlockSpec((1, SUBLANES, tn), lambda l, j: (l, 0, j)),
        compiler_params=_cparams(("parallel", "parallel"), "mod_vectors"),
        name="mod_vectors",
    )(cvec, w_mod, b_mod.reshape(depth, 1, n))


def _mod_norm(x, ln, shift, scale):
    ms = jnp.mean(x * x, axis=-1, keepdims=True)
    return (x * lax.rsqrt(ms + EPS) * ln) * (1.0 + scale) + shift


def _inproj_kernel(x_ref, mod_ref, ln_ref, w_ref, o_ref, h_scr):
    d = x_ref.shape[1]

    @pl.when(pl.program_id(1) == 0)
    def _():
        m = mod_ref[0]
        h = _mod_norm(x_ref[...], ln_ref[...], m[:, 0:d], m[:, d:2 * d])
        h_scr[...] = h.astype(BF16)

    o_ref[...] = _dot(h_scr[...], w_ref[...])


def _in_proj(x, mods, ln1, w_in, layer, n_ctx, dec_seq):
    nt, d = x.shape
    n = w_in.shape[2]
    tm = _row_tile(n_ctx, dec_seq, 1024)
    tn = n // 3
    assert tn % MXU_TILE == 0
    return pl.pallas_call(
        _inproj_kernel,
        out_shape=jax.ShapeDtypeStruct((nt, n), F32),
        grid=(nt // tm, n // tn),
        in_specs=[pl.BlockSpec((tm, d), lambda i, j: (i, 0)),
                  pl.BlockSpec((1, 1, 6 * d), lambda i, j: (_mod_index(i, tm, n_ctx, dec_seq), 0, 0)),
                  pl.BlockSpec((1, d), lambda i, j: (0, 0)),
                  pl.BlockSpec((None, d, tn), lambda i, j: (layer, 0, j))],
        out_specs=pl.BlockSpec((tm, tn), lambda i, j: (i, j)),
        scratch_shapes=[pltpu.VMEM((tm, d), BF16)],
        compiler_params=_cparams(("parallel", "arbitrary"), "in_proj"),
        name="in_proj",
    )(x, mods, ln1, w_in)


_TB = 256
_SCAN_ROWS = 512
_SCAN_UNITS = 8


def _gdn_prep_kernel(q_ref, k_ref, v_ref, qp_ref, kp_ref, vp_ref, qx_ref, kx_ref, vx_ref, ba_ref,
                     cw_ref, alog_ref, dtb_ref, qo_ref, ko_ref, vo_ref, bg_ref, ext_scr, *, n_ctx, seq, dec_seq):
    rows = q_ref.shape[0]
    r0 = pl.program_id(0) * rows
    in_ctx = r0 < n_ctx
    off = jnp.where(in_ctx, r0, r0 - n_ctx)
    length = jnp.where(in_ctx, seq, dec_seq)
    is_start = lax.rem(off, length) == 0
    is_end = lax.rem(off + rows, length) == 0
    lane_lo = _lane_lo(rows)
    cw = cw_ref[...]
    width = q_ref.shape[1]

    def conv(x_ref, prev_ref, next_ref, o_ref, a, norm_scale):
        ext_scr[0:SUBLANES, :] = jnp.where(is_start, 0.0, prev_ref[...])
        ext_scr[SUBLANES:SUBLANES + rows, :] = x_ref[...]
        ext_scr[SUBLANES + rows:, :] = jnp.where(is_end, 0.0, next_ref[...])
        acc = None
        for j in range(CONV_K):
            o = SUBLANES - CONV_K // 2 + j
            term = ext_scr[o:o + rows, :] * cw[j:j + 1, a * width:(a + 1) * width]
            acc = term if acc is None else acc + term
        y = _silu(acc)
        for g in range(width // LANES):
            yg = y[:, g * LANES:(g + 1) * LANES]
            if norm_scale is not None:
                yg = yg * lax.rsqrt(_half_sums(yg * yg, lane_lo) + EPS) * norm_scale
            o_ref[:, g * LANES:(g + 1) * LANES] = yg

    conv(q_ref, qp_ref, qx_ref, qo_ref, 0, HEAD_DIM ** -0.5)
    conv(k_ref, kp_ref, kx_ref, ko_ref, 1, 1.0)
    conv(v_ref, vp_ref, vx_ref, vo_ref, 2, None)

    x = ba_ref[...]
    g = -jnp.exp(alog_ref[...]) * _softplus(x + dtb_ref[...])
    row_in_chunk = lax.broadcasted_iota(jnp.int32, (rows, LANES), 0) % CHUNK
    pre = suf = g
    s = 1
    while s < CHUNK:
        pre = pre + jnp.where(row_in_chunk >= s, pltpu.roll(pre, s, axis=0), 0.0)
        suf = suf + jnp.where(row_in_chunk + s < CHUNK, pltpu.roll(suf, rows - s, axis=0), 0.0)
        s *= 2
    c = lax.broadcasted_iota(jnp.int32, (rows, LANES), 1) % 8
    bg_ref[...] = jnp.where(c < 4, _sigmoid(x), jnp.where(c < 6, pre, suf))


def _gdn_prep(proj, conv_w, alog_row, dtb_row, n_ctx, seq, dec_seq):
    nt = proj.shape[0]
    rows = _TB
    assert seq % rows == 0 and dec_seq % rows == 0
    w = PAIRS * LANES
    per = rows // SUBLANES
    last = nt // SUBLANES - 1
    main = [pl.BlockSpec((rows, w), lambda i, a=a: (i, a)) for a in range(3)]
    prev = [pl.BlockSpec((SUBLANES, w), lambda i, a=a: (jnp.maximum(i * per - 1, 0), a)) for a in range(3)]
    nxt = [pl.BlockSpec((SUBLANES, w), lambda i, a=a: (jnp.minimum((i + 1) * per, last), a)) for a in range(3)]
    row = pl.BlockSpec((1, LANES), lambda i: (0, 0))
    return pl.pallas_call(
        functools.partial(_gdn_prep_kernel, n_ctx=n_ctx, seq=seq, dec_seq=dec_seq),
        out_shape=[jax.ShapeDtypeStruct((nt, w), F32)] * 3 + [jax.ShapeDtypeStruct((nt, LANES), F32)],
        grid=(nt // rows,),
        in_specs=main + prev + nxt + [pl.BlockSpec((rows, LANES), lambda i: (i, G_BA)),
                                      pl.BlockSpec(conv_w.shape, lambda i: (0, 0)), row, row],
        out_specs=[pl.BlockSpec((rows, w), lambda i: (i, 0))] * 3 + [pl.BlockSpec((rows, LANES), lambda i: (i, 0))],
        scratch_shapes=[pltpu.VMEM((rows + 2 * SUBLANES, w), F32)],
        compiler_params=_cparams(("parallel",), "gdn_prep"),
        name="gdn_prep",
    )(*([proj] * 10), conv_w, alog_row, dtb_row)


def _run_interleaved(gens):
    gens = list(gens)
    while gens:
        for g in list(gens):
            try:
                next(g)
            except StopIteration:
                gens.remove(g)


def _gdn_scan_kernel(*refs, nch, ns, nsub, has_s0, has_sfin):
    fwd_refs, bwd_refs = refs[0:4], refs[4:8]
    pos = 8
    s0_ref = None
    if has_s0:
        s0_ref = refs[pos]
        pos += 1
    of_ref, ob_ref = refs[pos:pos + 2]
    pos += 2
    sfin_ref = None
    if has_sfin:
        sfin_ref = refs[pos]
        pos += 1
    s_scr = refs[pos]
    j = pl.program_id(1)
    n2 = 2 * CHUNK
    chains = [(sq, p, d) for sq in range(ns) for p in range(PAIRS) for d in range(2)]
    units = [ch + (sub,) for sub in range(nsub) for ch in chains]
    idx = range(len(units))
    n_steps = nch // nsub

    @pl.when(j == 0)
    def _():
        for u, (sq, p, d) in enumerate(chains):
            s_scr[u] = s0_ref[sq, p, d] if has_s0 else jnp.zeros((n2, n2), F32)

    ri = lax.broadcasted_iota(jnp.int32, (n2, n2), 0)
    ci = lax.broadcasted_iota(jnp.int32, (n2, n2), 1)
    same = (ri >= CHUNK) == (ci >= CHUNK)
    incl = (same & (ri >= ci), same & (ri <= ci))
    strict = (same & (ri > ci), same & (ri < ci))
    eye = (ri == ci).astype(F32)
    ril, cil = ri % CHUNK, ci % CHUNK
    level_mask = []
    m = 1
    while m < CHUNK:
        level_mask.append(same & (ril // (2 * m) == cil // (2 * m)) & (ril // m != cil // m))
        m *= 2
    level_sel = [jnp.where(lm, 1.0, 0.0).astype(BF16) for lm in level_mask[1:]]

    def col_stack(arr, ca, cb):
        top = jnp.broadcast_to(arr[:, ca:ca + 1], (CHUNK, LANES))
        bot = jnp.broadcast_to(arr[:, cb:cb + 1], (CHUNK, LANES))
        return jnp.concatenate([top, bot], axis=0)

    def row_stack(arr, row):
        top = jnp.broadcast_to(arr[row:row + 1, :], (CHUNK, LANES))
        bot = jnp.broadcast_to(arr[CHUNK + row:CHUNK + row + 1, :], (CHUNK, LANES))
        return jnp.concatenate([top, bot], axis=0)

    def rows_of(c, sq, d, sub):
        k = c * nsub + sub
        r = (sq * nch + (k if d == 0 else nch - 1 - k)) * CHUNK
        return slice(r, r + CHUNK)

    feats = [dict() for _ in range(n_steps)]
    solved = [dict() for _ in range(n_steps)]
    states = [s_scr[u] for u in range(len(chains))]
    dirs = [u[2] for u in units]

    def features(c):
        f = feats[c]
        src = lambda a, sq, p, d, sub: (fwd_refs, bwd_refs)[d][a][rows_of(c, sq, d, sub), :]
        bg = [src(3, *u) for u in units]
        b_c = [col_stack(bg[i], 8 * p + 2 * d, 8 * p + 2 * d + 1)
               for i, (sq, p, d, sub) in enumerate(units)]
        g_c = [col_stack(bg[i], 8 * p + 4 + 2 * d, 8 * p + 5 + 2 * d)
               for i, (sq, p, d, sub) in enumerate(units)]
        yield
        ld = lambda a, u: (fwd_refs, bwd_refs)[u[2]][a][rows_of(c, u[0], u[2], u[3]), u[1] * LANES:(u[1] + 1) * LANES]
        k2 = [jnp.concatenate([ld(1, u)] * 2, axis=0) for u in units]
        ks = [jnp.where(same, k2[i], 0.0) for i in idx]
        qs = [jnp.where(same, jnp.concatenate([ld(0, u)] * 2, axis=0), 0.0) for u in units]
        ks_b = [ks[i].astype(BF16) for i in idx]
        prod = [_dot_nt(jnp.concatenate([ks_b[i], qs[i].astype(BF16)], axis=0), ks_b[i]) for i in idx]
        yield
        decay = [jnp.exp(jnp.where(incl[dirs[i]], g_c[i] - g_c[i].T, NEG)) for i in idx]
        yield
        eg = [jnp.exp(g_c[i]) for i in idx]
        g_last = [row_stack(g_c[i], CHUNK - 1 if dirs[i] == 0 else 0) for i in idx]
        f["l"] = [jnp.where(strict[dirs[i]], b_c[i] * prod[i][:n2] * decay[i], 0.0) for i in idx]
        yield
        v2 = [jnp.concatenate([ld(2, u)] * 2, axis=0) for u in units]
        f["rhs"] = [(b_c[i] * jnp.where(same, v2[i], eg[i] * pltpu.roll(k2[i], CHUNK, axis=1))).astype(BF16)
                    for i in idx]
        yield
        f["at"] = [(prod[i][n2:] * decay[i]).astype(BF16) for i in idx]
        f["qg"] = [(qs[i] * eg[i]).astype(BF16) for i in idx]
        yield
        f["kt"] = [(ks[i] * jnp.exp(g_last[i] - g_c[i])).T.astype(BF16) for i in idx]
        f["gl"] = [jnp.exp(g_last[i]) for i in idx]
        yield

    def inverse(c):
        f, o = feats[c], solved[c]
        l_mat = f["l"]
        t_inv = [eye - jnp.where(level_mask[0], l_mat[i], 0.0) for i in idx]
        l_b = [l_mat[i].astype(BF16) for i in idx]
        for lm in level_sel:
            t_b = [t_inv[i].astype(BF16) for i in idx]
            y = [_dot(l_b[i] * lm, t_b[i]).astype(BF16) for i in idx]
            yield
            t_inv = [t_inv[i] - _dot(t_b[i], y[i]) for i in idx]
            yield
        xs = [_dot(t_inv[i].astype(BF16), f["rhs"][i]) for i in idx]
        yield
        o["u"] = [jnp.where(same, xs[i], 0.0) for i in idx]
        o["wq"] = [jnp.concatenate([jnp.where(same, pltpu.roll(xs[i], CHUNK, axis=1), 0.0).astype(BF16),
                                    f["qg"][i]], axis=0) for i in idx]
        yield

    def scan(c):
        f, o = feats[c], solved[c]
        for sub in range(nsub):
            ui = [sub * len(chains) + k for k in range(len(chains))]
            ws = [_dot(o["wq"][i], states[k].astype(BF16)) for k, i in enumerate(ui)]
            yield
            u_b = [(o["u"][i] - ws[k][:n2]).astype(BF16) for k, i in enumerate(ui)]
            yield
            o_st = [ws[k][n2:] + _dot(f["at"][i], u_b[k]) for k, i in enumerate(ui)]
            for k, i in enumerate(ui):
                states[k] = f["gl"][i] * states[k] + _dot(f["kt"][i], u_b[k])
            yield
            for k, i in enumerate(ui):
                sq, p, d, _ = units[i]
                dst = of_ref if d == 0 else ob_ref
                dst[rows_of(c, sq, d, sub), p * LANES:(p + 1) * LANES] = o_st[k][:CHUNK] + o_st[k][CHUNK:]
            yield
        feats[c] = solved[c] = None

    for t in range(n_steps + 2):
        gens = []
        if 1 <= t <= n_steps:
            gens.append(inverse(t - 1))
        if t < n_steps:
            gens.append(features(t))
        if 2 <= t:
            gens.append(scan(t - 2))
        _run_interleaved(gens)

    for u in range(len(chains)):
        s_scr[u] = states[u]
    if has_sfin:
        @pl.when(j == pl.num_programs(1) - 1)
        def _():
            for u, (sq, p, d) in enumerate(chains):
                sfin_ref[sq, p, d] = states[u]


def _gdn_scan(feat, s0, *, t, row0, n_seq, want_state):
    tb = math.gcd(t, _SCAN_ROWS)
    nb = t // tb
    ns = math.gcd(n_seq, _SCAN_ROWS // tb) if nb == 1 else 1
    nsub = math.gcd(tb // CHUNK, max(1, _SCAN_UNITS // (ns * PAIRS)))
    rows = ns * tb
    assert row0 % rows == 0
    blk0 = row0 // rows
    n2 = 2 * CHUNK
    w = PAIRS * LANES
    has_s0 = s0 is not None

    def blocks(mirror):
        row = (lambda b, j: b * nb + (nb - 1 - j)) if mirror else (lambda b, j: b * nb + j)
        return ([pl.BlockSpec((rows, w), lambda b, j: (blk0 + row(b, j), 0))] * 3
                + [pl.BlockSpec((rows, LANES), lambda b, j: (blk0 + row(b, j), 0))]), row

    f_specs, f_row = blocks(False)
    b_specs, b_row = blocks(True)
    in_specs = f_specs + b_specs
    args = list(feat) * 2
    state_spec = pl.BlockSpec((ns, PAIRS, 2, n2, n2), lambda b, j: (b, 0, 0, 0, 0))
    if has_s0:
        in_specs.append(state_spec)
        args.append(s0)
    out_shape = [jax.ShapeDtypeStruct((n_seq * t, w), F32)] * 2
    out_specs = [pl.BlockSpec((rows, w), lambda b, j: (f_row(b, j), 0)),
                 pl.BlockSpec((rows, w), lambda b, j: (b_row(b, j), 0))]
    if want_state:
        out_shape.append(jax.ShapeDtypeStruct((n_seq, PAIRS, 2, n2, n2), F32))
        out_specs.append(state_spec)
    return pl.pallas_call(
        functools.partial(_gdn_scan_kernel, nch=tb // CHUNK, ns=ns, nsub=nsub, has_s0=has_s0, has_sfin=want_state),
        out_shape=out_shape,
        grid=(n_seq // ns, nb),
        in_specs=in_specs,
        out_specs=out_specs,
        scratch_shapes=[pltpu.VMEM((ns * 2 * PAIRS, n2, n2), F32)],
        compiler_params=_cparams(("parallel", "arbitrary"), "gdn_scan"),
        name="gdn_scan_ctx" if want_state else "gdn_scan_lat",
    )(*args)


def _prep_kernel(*refs, rope):
    bq_ref, cq_ref, bk_ref, bv_ref, ck_ref, cv_ref, qkn_ref = refs[:7]
    pos = 7
    cs = sn = None
    if rope:
        cs, sn = refs[7][...], refs[8][...]
        pos = 11
    qb_ref, qc_ref, kdb_ref, vdb_ref, kdc_ref, vdc_ref = refs[pos:pos + 6]
    pos += 6
    tr = bq_ref.shape[0]
    lane = lax.broadcasted_iota(jnp.int32, (tr, LANES), 1)
    lane_lo = lane < HEAD_DIM
    first_half = (lane % HEAD_DIM) < HEAD_DIM // 2
    qkn = qkn_ref[...]

    ones_bd = ((lax.broadcasted_iota(jnp.int32, (LANES, LANES), 0) < HEAD_DIM)
               == (lax.broadcasted_iota(jnp.int32, (LANES, LANES), 1) < HEAD_DIM)).astype(BF16)

    def norm(x, w_row):
        sq = x * x
        hi = sq.astype(BF16)
        lo = (sq - hi.astype(F32)).astype(BF16)
        ms = (_dot(hi, ones_bd) + _dot(lo, ones_bd)) * (1.0 / HEAD_DIM)
        return x * lax.rsqrt(ms + EPS) * w_row

    def rot(y):
        if not rope:
            return y
        partner = jnp.where(first_half, pltpu.roll(y, LANES - HEAD_DIM // 2, axis=1),
                            pltpu.roll(y, HEAD_DIM // 2, axis=1))
        return y * cs + partner * sn

    def dup(x, o_ref):
        lo = jnp.where(lane_lo, x, 0.0)
        hi = jnp.where(lane_lo, 0.0, x)
        d0 = (lo + pltpu.roll(lo, HEAD_DIM, axis=1)).astype(BF16)
        d1 = (hi + pltpu.roll(hi, HEAD_DIM, axis=1)).astype(BF16)
        if len(o_ref.shape) == 3:
            o_ref[0, :, 0:LANES] = d0
            o_ref[0, :, LANES:2 * LANES] = d1
        else:
            o_ref[:, 0:LANES] = d0
            o_ref[:, LANES:2 * LANES] = d1

    for q_ref, o_ref, wi in ((bq_ref, qb_ref, 0), (cq_ref, qc_ref, 2)):
        for g in range(PAIRS):
            y = rot(norm(q_ref[:, g * LANES:(g + 1) * LANES], qkn[wi:wi + 1, :])) * HEAD_DIM ** -0.5
            o_ref[2 * g] = jnp.where(lane_lo, y, 0.0).astype(BF16)
            o_ref[2 * g + 1] = jnp.where(lane_lo, 0.0, y).astype(BF16)
    kb_n = norm(bk_ref[...], qkn[1:2, :])
    kc_n = norm(ck_ref[...], qkn[3:4, :])
    if not rope:
        kbn_ref, vbn_ref, kcn_ref, vcn_ref = refs[pos:pos + 4]
        kbn_ref[...] = kb_n
        vbn_ref[...] = bv_ref[...]
        kcn_ref[...] = kc_n
        vcn_ref[...] = cv_ref[...]
    dup(rot(kb_n), kdb_ref)
    dup(bv_ref[...], vdb_ref)
    dup(rot(kc_n), kdc_ref)
    dup(cv_ref[...], vdc_ref)


def _attn_prep(proj, qkn_rows, rope_tables, glob_kv, *, row0, n_rows, t):
    rope = rope_tables is not None
    tr = math.gcd(t, 512)
    blk0 = row0 // tr

    def col(g, width):
        gw = width // LANES
        return pl.BlockSpec((tr, width), lambda i, g=g, gw=gw: (blk0 + i, g // gw))

    in_specs = [col(G_BQ, 4 * LANES), col(G_CQ, 4 * LANES), col(G_BK, LANES), col(G_BV, LANES),
                col(G_CK, LANES), col(G_CV, LANES), pl.BlockSpec((4, LANES), lambda i: (0, 0))]
    args = [proj] * 6 + [qkn_rows]
    q_shape = jax.ShapeDtypeStruct((HEADS, n_rows, LANES), BF16)
    kv_shape = jax.ShapeDtypeStruct((n_rows, 2 * LANES), BF16)
    q_spec = pl.BlockSpec((HEADS, tr, LANES), lambda i: (0, i, 0))
    kv_spec = pl.BlockSpec((tr, 2 * LANES), lambda i: (i, 0))
    aliases = {}
    if rope:
        per_seq = t // tr
        in_specs += [pl.BlockSpec((tr, LANES), lambda i: (i % per_seq, 0))] * 2 + [_any_spec()] * 2
        aliases = {len(args) + 2: 2, len(args) + 3: 3}
        args += list(rope_tables) + list(glob_kv)
        past = glob_kv[0].shape[1] - t
        assert past % tr == 0
        glob_spec = pl.BlockSpec((1, tr, 2 * LANES), lambda i: (i // per_seq, past // tr + i % per_seq, 0))
        glob_shape = jax.ShapeDtypeStruct(glob_kv[0].shape, BF16)
        out_shape = [q_shape, q_shape, glob_shape, glob_shape, kv_shape, kv_shape]
        out_specs = [q_spec, q_spec, glob_spec, glob_spec, kv_spec, kv_spec]
    else:
        out_shape = [q_shape, q_shape] + [kv_shape] * 4 + [jax.ShapeDtypeStruct((n_rows, LANES), F32)] * 4
        out_specs = [q_spec, q_spec] + [kv_spec] * 4 + [pl.BlockSpec((tr, LANES), lambda i: (i, 0))] * 4
    return pl.pallas_call(
        functools.partial(_prep_kernel, rope=rope),
        out_shape=out_shape,
        grid=(n_rows // tr,),
        in_specs=in_specs,
        out_specs=out_specs,
        input_output_aliases=aliases,
        compiler_params=_cparams(("parallel",), "attn_prep"),
        name="attn_prep_lat" if rope else "attn_prep_ctx",
    )(*args)


def _stacked_q(q_ref, g, tq):
    return q_ref[GROUP_HEADS * g:GROUP_HEADS * (g + 1)].reshape(GROUP_HEADS * tq, LANES)


def _unstack_group(o_st, tq, lane_lo):
    a = jnp.where(lane_lo, o_st[0:tq], o_st[tq:2 * tq])
    b = jnp.where(lane_lo, o_st[2 * tq:3 * tq], o_st[3 * tq:4 * tq])
    return jnp.concatenate([a, b], axis=1)


def _with_ones(v, ones):
    return jnp.concatenate([v, ones], axis=1)


def _sink_col(sink_row, g, tq):
    parts = [jnp.broadcast_to(sink_row[:, GROUP_HEADS * g + j:GROUP_HEADS * g + j + 1], (tq, 1))
             for j in range(GROUP_HEADS)]
    return jnp.concatenate(parts, axis=0)


def _attn_ctx_kernel(qb_ref, kdb_ref, vdb_ref, qc_ref, kdc_ref, vdc_ref, sink_ref, yb_ref, yc_ref):
    tq = qb_ref.shape[1]
    lane_lo_q = _lane_lo(tq)
    sink_row = sink_ref[...]
    groups = range(KV_HEADS)
    gs = [slice(g * LANES, (g + 1) * LANES) for g in groups]
    jobs = [(qb_ref, kdb_ref, vdb_ref, yb_ref, False, g) for g in groups]
    jobs += [(qc_ref, kdc_ref, vdc_ref, yc_ref, True, g) for g in groups]
    s = [_dot_nt(_stacked_q(q_ref, g, tq), kd_ref[:, gs[g]]) for q_ref, kd_ref, _, _, _, g in jobs]
    for si, (_, _, vd_ref, y_ref, use_sink, g) in zip(s, jobs):
        m = jnp.max(si, axis=-1, keepdims=True)
        if use_sink:
            snk = _sink_col(sink_row, g, tq)
            m = jnp.maximum(m, snk)
        p = jnp.exp(si - m)
        l = jnp.sum(p, axis=-1, keepdims=True)
        if use_sink:
            l = l + jnp.exp(snk - m)
        o = _dot(p.astype(BF16), vd_ref[:, gs[g]]) / l
        y_ref[:, 2 * g * LANES:(2 * g + 2) * LANES] = _unstack_group(o, tq, lane_lo_q).astype(y_ref.dtype)


def _attn_ctx(qb, kdb, vdb, qc, kdc, vdc, sink_row, *, t):
    n = kdb.shape[0]
    q_spec = pl.BlockSpec((HEADS, t, LANES), lambda b: (0, b, 0))
    kv_spec = pl.BlockSpec((t, 2 * LANES), lambda b: (b, 0))
    y_spec = pl.BlockSpec((t, 4 * LANES), lambda b: (b, 0))
    return pl.pallas_call(
        _attn_ctx_kernel,
        out_shape=[jax.ShapeDtypeStruct((n, 4 * LANES), BF16)] * 2,
        grid=(n // t,),
        in_specs=[q_spec, kv_spec, kv_spec, q_spec, kv_spec, kv_spec,
                  pl.BlockSpec((1, LANES), lambda b: (0, 0))],
        out_specs=[y_spec, y_spec],
        compiler_params=_cparams(("parallel",), "attn_ctx"),
        name="attn_ctx",
    )(qb, kdb, vdb, qc, kdc, vdc, sink_row)


_TQ_B = 256
_TK_B = 1536


def _attn_glob_kernel(q_ref, kd_ref, vd_ref, y_ref, s_scr, *, tk):
    tq = q_ref.shape[1]
    n_blk = kd_ref.shape[1] // tk
    groups = range(KV_HEADS)
    gs = [slice(g * LANES, (g + 1) * LANES) for g in groups]
    q = [_stacked_q(q_ref, g, tq) for g in groups]

    def scores_into(j, slot):
        for g in groups:
            s_scr[slot, g] = _dot_nt(q[g], kd_ref[0, j * tk:(j + 1) * tk, gs[g]])

    rows = GROUP_HEADS * tq
    m = [jnp.full((rows, LANES), NEG, F32) for _ in groups]
    acc = [jnp.zeros((rows, 2 * LANES), F32) for _ in groups]
    ones = jnp.ones((tk, LANES), BF16)
    scores_into(0, 0)
    for j in range(n_blk):
        if j + 1 < n_blk:
            scores_into(j + 1, (j + 1) % 2)
        for g in groups:
            s = s_scr[j % 2, g]
            m_new = jnp.maximum(m[g], jnp.max(s, axis=-1, keepdims=True))
            p = jnp.exp(s - m_new[:, 0:1]).astype(BF16)
            alpha = jnp.exp(m[g] - m_new)
            acc[g] = (jnp.concatenate([alpha, alpha], axis=1) * acc[g]
                      + _dot(p, _with_ones(vd_ref[0, j * tk:(j + 1) * tk, gs[g]], ones)))
            m[g] = m_new
    lane_lo_q = _lane_lo(tq)
    for g in groups:
        o = acc[g][:, :LANES] / acc[g][:, LANES:]
        y_ref[:, 2 * g * LANES:(2 * g + 2) * LANES] = _unstack_group(o, tq, lane_lo_q).astype(y_ref.dtype)


def _attn_glob(q, kd, vd, *, t):
    n = q.shape[1]
    s_len = kd.shape[1]
    tq, tk = math.gcd(t, _TQ_B), math.gcd(s_len, _TK_B)
    per_seq = t // tq
    kv_spec = pl.BlockSpec((1, s_len, 2 * LANES), lambda b, i: (b, 0, 0))
    return pl.pallas_call(
        functools.partial(_attn_glob_kernel, tk=tk),
        out_shape=jax.ShapeDtypeStruct((n, 4 * LANES), BF16),
        grid=(n // t, per_seq),
        in_specs=[pl.BlockSpec((HEADS, tq, LANES), lambda b, i: (0, b * per_seq + i, 0)), kv_spec, kv_spec],
        out_specs=pl.BlockSpec((tq, 4 * LANES), lambda b, i: (b * per_seq + i, 0)),
        scratch_shapes=[pltpu.VMEM((2, KV_HEADS, GROUP_HEADS * tq, tk), F32)],
        compiler_params=_cparams(("parallel", "parallel"), "attn_glob"),
        name="attn_glob",
    )(q, kd, vd)


def _attn_win_kernel(q_ref, kx_ref, vx_ref, kp_ref, kc_ref, kn_ref, vp_ref, vc_ref, vn_ref,
                     sink_ref, y_ref):
    i = pl.program_id(1)
    nblk = pl.num_programs(1)
    tq = q_ref.shape[1]
    rows = GROUP_HEADS * tq
    lane_lo_q = _lane_lo(tq)
    sink_row = sink_ref[...]
    rr_w = lax.broadcasted_iota(jnp.int32, (rows, WINDOW), 0) % tq
    cc_w = lax.broadcasted_iota(jnp.int32, (rows, WINDOW), 1)
    rr_q = lax.broadcasted_iota(jnp.int32, (rows, tq), 0) % tq
    cc_q = lax.broadcasted_iota(jnp.int32, (rows, tq), 1)
    mask_prev = (cc_w >= rr_w) & (i > 0)
    mask_cur = jnp.abs(rr_q - cc_q) <= WINDOW
    mask_next = (cc_w <= rr_w - (tq - WINDOW)) & (i < nblk - 1)
    groups = range(KV_HEADS)
    gs = [slice(g * LANES, (g + 1) * LANES) for g in groups]
    qs = [_stacked_q(q_ref, g, tq) for g in groups]
    p_len = kx_ref.shape[1]
    mask_edge = jnp.concatenate([mask_prev, mask_next], axis=1)
    edge = lambda p_ref, n_ref, g: jnp.concatenate([p_ref[:, gs[g]], n_ref[:, gs[g]]], axis=0)
    s = [jnp.concatenate([_dot_nt(qs[g], kx_ref[0, :, gs[g]]),
                          jnp.where(mask_edge, _dot_nt(qs[g], edge(kp_ref, kn_ref, g)), NEG),
                          jnp.where(mask_cur, _dot_nt(qs[g], kc_ref[:, gs[g]]), NEG)], axis=1) for g in groups]
    parts = ((0, p_len, lambda g: vx_ref[0, :, gs[g]]), (p_len, 2 * WINDOW, lambda g: edge(vp_ref, vn_ref, g)),
             (p_len + 2 * WINDOW, tq, lambda g: vc_ref[:, gs[g]]))
    for g in groups:
        snk = _sink_col(sink_row, g, tq)
        m = jnp.maximum(jnp.max(s[g], axis=-1, keepdims=True), snk)
        p = jnp.exp(s[g] - m).astype(BF16)
        ol = None
        for start, size, val in parts:
            d = _dot(p[:, start:start + size], _with_ones(val(g), jnp.ones((size, LANES), BF16)))
            ol = d if ol is None else ol + d
        o = ol[:, :LANES] / (ol[:, LANES:] + jnp.exp(snk - m))
        y_ref[:, 2 * g * LANES:(2 * g + 2) * LANES] = _unstack_group(o, tq, lane_lo_q).astype(y_ref.dtype)


def _attn_win(q, kd, vd, kx, vx, sink_row, layer, *, t):
    n = q.shape[1]
    p_len = kx.shape[2]
    tq = math.gcd(t, 2 * WINDOW)
    nblk = t // tq
    per = tq // WINDOW
    wblk = t // WINDOW
    x_spec = pl.BlockSpec((None, 1, p_len, 2 * LANES), lambda b, i: (layer, b, 0, 0))
    prev = pl.BlockSpec((WINDOW, 2 * LANES), lambda b, i: (b * wblk + jnp.maximum(i * per - 1, 0), 0))
    cur = pl.BlockSpec((tq, 2 * LANES), lambda b, i: (b * nblk + i, 0))
    nxt = pl.BlockSpec((WINDOW, 2 * LANES), lambda b, i: (b * wblk + jnp.minimum((i + 1) * per, wblk - 1), 0))
    return pl.pallas_call(
        _attn_win_kernel,
        out_shape=jax.ShapeDtypeStruct((n, 4 * LANES), BF16),
        grid=(n // t, nblk),
        in_specs=[pl.BlockSpec((HEADS, tq, LANES), lambda b, i: (0, b * nblk + i, 0)),
                  x_spec, x_spec, prev, cur, nxt, prev, cur, nxt,
                  pl.BlockSpec((1, LANES), lambda b, i: (0, 0))],
        out_specs=pl.BlockSpec((tq, 4 * LANES), lambda b, i: (b * nblk + i, 0)),
        compiler_params=_cparams(("parallel", "parallel"), "attn_win"),
        name="attn_win",
    )(q, kx, vx, kd, kd, kd, vd, vd, vd, sink_row)


def _mix_kernel(x_ref, ofc_ref, obc_ref, ybc_ref, ycc_ref, ofl_ref, obl_ref, ybl_ref, ycl_ref, z_ref,
                ga_ref, gb_ref, gc_ref, mod_ref, anorm_ref, wa_ref, wb_ref, wc_ref, wo_ref, o_ref, *, ctx_tiles):
    d = x_ref.shape[1]
    tm = x_ref.shape[0]
    lane_lo = _lane_lo(tm)
    anorm = anorm_ref[...]
    is_ctx = pl.program_id(0) < ctx_tiles

    def pick(c_ref, l_ref, cols=slice(None)):
        return jnp.where(is_ctx, c_ref[:, cols], l_ref[:, cols])

    ya = []
    for g in range(PAIRS):
        gl = slice(g * LANES, (g + 1) * LANES)
        o = pick(ofc_ref, ofl_ref, gl) + pick(obc_ref, obl_ref, gl)
        ms = _half_sums(o * o, lane_lo) * (1.0 / HEAD_DIM)
        ya.append((o * lax.rsqrt(ms + EPS) * anorm * _silu(z_ref[:, gl])).astype(BF16))
    ya = jnp.concatenate(ya, axis=1)
    merged = (_sigmoid(ga_ref[...]) * _dot(ya, wa_ref[...])
              + _sigmoid(gb_ref[...]) * _dot(pick(ybc_ref, ybl_ref), wb_ref[...])
              + _sigmoid(gc_ref[...]) * _dot(pick(ycc_ref, ycl_ref), wc_ref[...]))
    gate1 = mod_ref[0][:, 2 * d:3 * d]
    o_ref[...] = x_ref[...] + gate1 * _dot(merged.astype(BF16), wo_ref[...])


def _mix(x, ctx_parts, lat_parts, proj, mods, anorm_row, wa, wb, wc, wo, layer, n_ctx, dec_seq):
    nt, d = x.shape
    tm = _row_tile(n_ctx, dec_seq, 512)
    ctx_tiles = n_ctx // tm
    w = PAIRS * LANES
    c_spec = pl.BlockSpec((tm, w), lambda i: (jnp.minimum(i, ctx_tiles - 1), 0))
    l_spec = pl.BlockSpec((tm, w), lambda i: (jnp.maximum(i - ctx_tiles, 0), 0))
    gw = d // LANES

    def gate(k):
        return pl.BlockSpec((tm, d), lambda i, k=k: (i, G_GATES // gw + k))

    def weight(a):
        return pl.BlockSpec((None,) + a.shape[1:], lambda i: (layer, 0, 0))

    return pl.pallas_call(
        functools.partial(_mix_kernel, ctx_tiles=ctx_tiles),
        out_shape=jax.ShapeDtypeStruct((nt, d), F32),
        grid=(nt // tm,),
        in_specs=[pl.BlockSpec((tm, d), lambda i: (i, 0))] + [c_spec] * 4 + [l_spec] * 4
                 + [pl.BlockSpec((tm, w), lambda i: (i, G_Z // PAIRS)), gate(0), gate(1), gate(2),
                    pl.BlockSpec((1, 1, 6 * d), lambda i: (_mod_index(i, tm, n_ctx, dec_seq), 0, 0)),
                    pl.BlockSpec(anorm_row.shape, lambda i: (0, 0)),
                    weight(wa), weight(wb), weight(wc), weight(wo)],
        out_specs=pl.BlockSpec((tm, d), lambda i: (i, 0)),
        compiler_params=_cparams(("parallel",), "mix_out"),
        name="mix_out",
    )(x, *ctx_parts, *lat_parts, proj, proj, proj, proj, mods, anorm_row, wa, wb, wc, wo)


_FF_CHUNK = 1024


def _ffn_kernel(x_ref, mod_ref, ln_ref, w1_ref, w2_ref, o_ref):
    d = x_ref.shape[1]
    m = mod_ref[0]
    x = x_ref[...]
    h = _mod_norm(x, ln_ref[...], m[:, 3 * d:4 * d], m[:, 4 * d:5 * d]).astype(BF16)
    acc = None
    for f in range(0, w1_ref.shape[1], _FF_CHUNK):
        hid = jnp.maximum(_dot(h, w1_ref[:, f:f + _FF_CHUNK]), 0.0)
        part = _dot((hid * hid).astype(BF16), w2_ref[f:f + _FF_CHUNK, :])
        acc = part if acc is None else acc + part
    o_ref[...] = x + m[:, 5 * d:6 * d] * acc


def _ffn(x, mods, ln2, w1, w2, layer, n_ctx, dec_seq):
    nt, d = x.shape
    tm = _row_tile(n_ctx, dec_seq, 512)
    return pl.pallas_call(
        _ffn_kernel,
        out_shape=jax.ShapeDtypeStruct((nt, d), F32),
        grid=(nt // tm,),
        in_specs=[pl.BlockSpec((tm, d), lambda i: (i, 0)),
                  pl.BlockSpec((1, 1, 6 * d), lambda i: (_mod_index(i, tm, n_ctx, dec_seq), 0, 0)),
                  pl.BlockSpec((1, d), lambda i: (0, 0)),
                  pl.BlockSpec((None,) + w1.shape[1:], lambda i: (layer, 0, 0)),
                  pl.BlockSpec((None,) + w2.shape[1:], lambda i: (layer, 0, 0))],
        out_specs=pl.BlockSpec((tm, d), lambda i: (i, 0)),
        compiler_params=_cparams(("parallel",), "ffn"),
        name="ffn",
    )(x, mods, ln2, w1, w2)


def _ba_lane_order():
    cols = []
    for p in range(PAIRS):
        for base in (0, 2 * HEADS):
            for d in range(2):
                cols += [base + d * HEADS + 2 * p, base + d * HEADS + 2 * p + 1]
    return cols


def _reorder_w_in(w_in):
    a_w = HEADS * HEAD_DIM
    kvw = KV_HEADS * HEAD_DIM
    o_beta = 4 * a_w
    o_bq = o_beta + 4 * HEADS
    o_bk = o_bq + a_w
    o_bv = o_bk + kvw
    o_cq = o_bv + kvw
    o_ck = o_cq + a_w
    o_cv = o_ck + kvw
    o_g = o_cv + kvw
    seg = lambda a, n: w_in[:, :, a:a + n]
    ba = jnp.take(seg(o_beta, 4 * HEADS), jnp.array(_ba_lane_order(), jnp.int32), axis=2)
    ba = jnp.pad(ba, ((0, 0), (0, 0), (0, 2 * LANES - ba.shape[2])))
    out = jnp.concatenate([seg(0, o_beta), seg(o_bq, a_w), seg(o_cq, a_w), seg(o_g, w_in.shape[2] - o_g),
                           seg(o_bk, kvw), seg(o_bv, kvw), seg(o_ck, kvw), seg(o_cv, kvw), ba], axis=2)
    assert out.shape[2] == IN_COLS_PAD
    return out.astype(BF16)


def _decay_rows(vals):
    rows = []
    for p in range(PAIRS):
        r = jnp.stack([vals[:, 0, 2 * p], vals[:, 0, 2 * p + 1], vals[:, 1, 2 * p], vals[:, 1, 2 * p + 1]], axis=-1)
        rows.append(jnp.pad(r, ((0, 0), (4, 0))))
    out = jnp.concatenate(rows, axis=-1)
    return jnp.pad(out, ((0, 0), (0, LANES - out.shape[1])))[:, None, :]


def _rope_tables(n_tokens):
    rows = n_tokens // GRID_W
    row_id = jnp.repeat(jnp.arange(rows, dtype=F32), GRID_W)
    col_id = jnp.tile(jnp.arange(GRID_W, dtype=F32), rows)
    n_freq = HEAD_DIM // 4
    inv_freq = ROPE_THETA ** (-jnp.arange(n_freq, dtype=F32) / n_freq)
    ang = jnp.concatenate([row_id[:, None] * inv_freq, col_id[:, None] * inv_freq], axis=-1)
    cos, sin = jnp.cos(ang), jnp.sin(ang)
    cs = jnp.tile(cos, (1, 4))
    sn = jnp.tile(jnp.concatenate([-sin, sin], axis=-1), (1, 2))
    return cs, sn


def _dup_kv(x, pad_rows=0):
    x = jnp.swapaxes(x, 0, 1)
    out = jnp.concatenate([x[..., 0, :], x[..., 0, :], x[..., 1, :], x[..., 1, :]], axis=-1).astype(BF16)
    return jnp.pad(out, ((0, 0), (0, 0), (0, pad_rows), (0, 0)))


def _block_diag_states(s):
    b, depth = s.shape[:2]
    s = s.reshape(b, depth, 2, PAIRS, 2, HEAD_DIM, HEAD_DIM).transpose(1, 0, 3, 2, 4, 5, 6)
    z = jnp.zeros_like(s[..., 0, :, :])
    top = jnp.concatenate([s[..., 0, :, :], z], axis=-1)
    bot = jnp.concatenate([z, s[..., 1, :, :]], axis=-1)
    return jnp.concatenate([top, bot], axis=-2)


def _diag_states(s):
    a = s[:, :, :, :HEAD_DIM, :HEAD_DIM]
    bb = s[:, :, :, HEAD_DIM:, HEAD_DIM:]
    out = jnp.stack([a, bb], axis=3)
    b = s.shape[0]
    return out.transpose(0, 2, 1, 3, 4, 5).reshape(b, 2, HEADS, HEAD_DIM, HEAD_DIM)


def kernel(x_prompt, x_sample, cache_k_glob, cache_v_glob, cache_k_win, cache_v_win, state_delta, c, c_ctx, w_mod, b_mod, ln1, ln2, w_in, conv_qkv, a_log, dt_bias, a_norm, qk_norm, sink, w_br_a, w_br_b, w_br_c, w_o, w_ff1, w_ff2):
    batch, seq, d = x_prompt.shape
    dec_batch, dec_seq, _ = x_sample.shape
    depth = w_mod.shape[0]
    n_ctx = batch * seq
    n_lat = dec_batch * dec_seq
    assert 1 + dec_batch <= SUBLANES

    w_in_r = _reorder_w_in(w_in)
    wa, wb, wc, wo = (w.astype(BF16) for w in (w_br_a, w_br_b, w_br_c, w_o))
    w1, w2 = w_ff1.astype(BF16), w_ff2.astype(BF16)
    alog_rows = _decay_rows(a_log)
    dtb_rows = _decay_rows(dt_bias)
    anorm_rows = jnp.tile(a_norm, (1, 2))[:, None, :]
    qkn_rows = jnp.tile(qk_norm, (1, 1, 2))
    sink_rows = jnp.pad(sink, ((0, 0), (0, LANES - HEADS)))[:, None, :]
    rope_tables = _rope_tables(dec_seq)
    cvec = jnp.concatenate([c_ctx[None], c, jnp.zeros((SUBLANES - 1 - dec_batch, d), F32)], axis=0)
    mods_all = _mod_vectors(cvec, w_mod, b_mod)
    glob_k, glob_v = _dup_kv(cache_k_glob, dec_seq), _dup_kv(cache_v_glob, dec_seq)
    win_k, win_v = _dup_kv(cache_k_win), _dup_kv(cache_v_win)
    s0_lat = _block_diag_states(state_delta)

    x = jnp.concatenate([x_prompt.reshape(n_ctx, d), x_sample.reshape(n_lat, d)], axis=0)
    new_kg, new_vg, new_kw, new_vw, new_st = [], [], [], [], []
    for l in range(depth):
        mods = mods_all[l][:, None, :]
        proj = _in_proj(x, mods, ln1[l][None], w_in_r, l, n_ctx, dec_seq)

        feat = _gdn_prep(proj, conv_qkv[l], alog_rows[l], dtb_rows[l], n_ctx, seq, dec_seq)
        of_c, ob_c, s_fin = _gdn_scan(feat, None, t=seq, row0=0, n_seq=batch, want_state=True)
        of_l, ob_l = _gdn_scan(feat, s0_lat[l], t=dec_seq, row0=n_ctx, n_seq=dec_batch, want_state=False)

        qb_c, qc_c, kdb_c, vdb_c, kdc_c, vdc_c, kb_n, vb_n, kc_n, vc_n = _attn_prep(
            proj, qkn_rows[l], None, None, row0=0, n_rows=n_ctx, t=seq)
        qb_l, qc_l, kd_all, vd_all, kdc_l, vdc_l = _attn_prep(
            proj, qkn_rows[l], rope_tables, (glob_k[l], glob_v[l]), row0=n_ctx, n_rows=n_lat, t=dec_seq)

        yb_c, yc_c = _attn_ctx(qb_c, kdb_c, vdb_c, qc_c, kdc_c, vdc_c, sink_rows[l], t=seq)
        yb_l = _attn_glob(qb_l, kd_all, vd_all, t=dec_seq)
        yc_l = _attn_win(qc_l, kdc_l, vdc_l, win_k, win_v, sink_rows[l], l, t=dec_seq)

        x = _mix(x, (of_c, ob_c, yb_c, yc_c), (of_l, ob_l, yb_l, yc_l), proj, mods, anorm_rows[l],
                 wa, wb, wc, wo, l, n_ctx, dec_seq)
        x = _ffn(x, mods, ln2[l][None], w1, w2, l, n_ctx, dec_seq)

        kv_shape = (batch, seq, KV_HEADS, HEAD_DIM)
        new_kg.append(kb_n.reshape(kv_shape))
        new_vg.append(vb_n.reshape(kv_shape))
        new_kw.append(kc_n.reshape(kv_shape))
        new_vw.append(vc_n.reshape(kv_shape))
        new_st.append(_diag_states(s_fin))

    y_prompt = x[:n_ctx].reshape(batch, seq, d)
    y_sample = x[n_ctx:].reshape(dec_batch, dec_seq, d)
    return (y_prompt, y_sample, jnp.stack(new_kg, axis=1), jnp.stack(new_vg, axis=1),
            jnp.stack(new_kw, axis=1), jnp.stack(new_vw, axis=1), jnp.stack(new_st, axis=1))
```

```python
import functools
import math

import jax
import jax.numpy as jnp
from jax import lax
from jax.experimental import pallas as pl
from jax.experimental.pallas import tpu as pltpu

F32 = jnp.float32
BF16 = jnp.bfloat16

LANES = 128
SUBLANES = 8

HEAD_DIM = 64
HEADS = 8
KV_HEADS = 2
GROUP_HEADS = HEADS // KV_HEADS
CHUNK = 64
CONV_K = 5
GRID_W = 64
WINDOW = 128
ROPE_THETA = 10000.0
EPS = 1e-6
NEG = -1e30
PAIRS = HEADS // 2

G_Q, G_K, G_V, G_Z = 0, 4, 8, 12
G_BQ, G_CQ = 16, 20
G_GATES = 24
G_BK, G_BV, G_CK, G_CV = 48, 49, 50, 51
G_BA = 52
N_GROUPS = 54
IN_COLS_PAD = N_GROUPS * LANES


VMEM_MIB_V7X = 64
MXU_TILE = 256

_VMEM_MIB = {
    "mod_vectors": 40,
    "w_in_prep": 32,
    "in_proj": 48,
    "gdn_prep": 32,
    "gdn_scan": 48,
    "attn_prep": 32,
    "attn_ctx": 32,
    "attn_glob": 56,
    "attn_win": 32,
    "mix_out": 48,
    "ffn": 56,
}
assert max(_VMEM_MIB.values()) < VMEM_MIB_V7X


def _cparams(sem, call):
    return pltpu.CompilerParams(dimension_semantics=sem, vmem_limit_bytes=_VMEM_MIB[call] << 20)


def _dot(a, b):
    return jnp.dot(a, b, preferred_element_type=F32)


def _dot_nt(a, b):
    return lax.dot_general(a, b, (((1,), (1,)), ((), ())), preferred_element_type=F32)


def _sigmoid(x):
    return 0.5 * jnp.tanh(0.5 * x) + 0.5


def _silu(x):
    return x * _sigmoid(x)


def _softplus(x):
    return jnp.maximum(x, 0.0) + jnp.log1p(jnp.exp(-jnp.abs(x)))


def _half_sums(x, lane_lo):
    s_lo = jnp.sum(jnp.where(lane_lo, x, 0.0), axis=-1, keepdims=True)
    s_hi = jnp.sum(jnp.where(lane_lo, 0.0, x), axis=-1, keepdims=True)
    return jnp.where(lane_lo, s_lo, s_hi)


def _lane_lo(rows):
    return lax.broadcasted_iota(jnp.int32, (rows, LANES), 1) < HEAD_DIM


def _row_tile(n_ctx, dec_seq, pref):
    return math.gcd(math.gcd(n_ctx, dec_seq), pref)


def _mod_index(i, tm, n_ctx, dec_seq):
    ctx_tiles = n_ctx // tm
    return jnp.where(i < ctx_tiles, 0, 1 + (i - ctx_tiles) // (dec_seq // tm))


def _any_spec():
    return pl.BlockSpec(memory_space=pl.ANY)


def _mod_kernel(c_ref, w_ref, b_ref, o_ref):
    s = _silu(c_ref[...])
    w = w_ref[0]
    s_hi = s.astype(BF16)
    s_lo = (s - s_hi.astype(F32)).astype(BF16)
    w_hi = w.astype(BF16)
    w_lo = (w - w_hi.astype(F32)).astype(BF16)
    acc = _dot(s_hi, w_hi) + (_dot(s_lo, w_hi) + _dot(s_hi, w_lo))
    o_ref[0] = acc + b_ref[0]


def _mod_vectors(cvec, w_mod, b_mod):
    depth, d, n = w_mod.shape
    tn = n // 4
    return pl.pallas_call(
        _mod_kernel,
        out_shape=jax.ShapeDtypeStruct((depth, SUBLANES, n), F32),
        grid=(depth, n // tn),
        in_specs=[pl.BlockSpec((SUBLANES, d), lambda l, j: (0, 0)),
                  pl.BlockSpec((1, d, tn), lambda l, j: (l, 0, j)),
                  pl.BlockSpec((1, 1, tn), lambda l, j: (l, 0, j))],
        out_specs=pl.BlockSpec((1, SUBLANES, tn), lambda l, j: (l, 0, j)),
        compiler_params=_cparams(("parallel", "parallel"), "mod_vectors"),
        name="mod_vectors",
    )(cvec, w_mod, b_mod.reshape(depth, 1, n))


def _mod_norm(x, ln, shift, scale):
    ms = jnp.mean(x * x, axis=-1, keepdims=True)
    return (x * lax.rsqrt(ms + EPS) * ln) * (1.0 + scale) + shift


def _inproj_kernel(x_ref, mod_ref, ln_ref, w_ref, o_ref, h_scr):
    d = x_ref.shape[1]

    @pl.when(pl.program_id(1) == 0)
    def _():
        m = mod_ref[0]
        h = _mod_norm(x_ref[...], ln_ref[...], m[:, 0:d], m[:, d:2 * d])
        h_scr[...] = h.astype(BF16)

    o_ref[...] = _dot(h_scr[...], w_ref[...])


def _in_proj(x, mods, ln1, w_in, layer, n_ctx, dec_seq):
    nt, d = x.shape
    n = w_in.shape[2]
    tm = _row_tile(n_ctx, dec_seq, 1024)
    tn = n // 3
    assert tn % MXU_TILE == 0
    return pl.pallas_call(
        _inproj_kernel,
        out_shape=jax.ShapeDtypeStruct((nt, n), F32),
        grid=(nt // tm, n // tn),
        in_specs=[pl.BlockSpec((tm, d), lambda i, j: (i, 0)),
                  pl.BlockSpec((1, 1, 6 * d), lambda i, j: (_mod_index(i, tm, n_ctx, dec_seq), 0, 0)),
                  pl.BlockSpec((1, d), lambda i, j: (0, 0)),
                  pl.BlockSpec((None, d, tn), lambda i, j: (layer, 0, j))],
        out_specs=pl.BlockSpec((tm, tn), lambda i, j: (i, j)),
        scratch_shapes=[pltpu.VMEM((tm, d), BF16)],
        compiler_params=_cparams(("parallel", "arbitrary"), "in_proj"),
        name="in_proj",
    )(x, mods, ln1, w_in)


_TB = 256
_SCAN_ROWS = 512
_SCAN_UNITS = 8


def _gdn_prep_kernel(q_ref, k_ref, v_ref, qp_ref, kp_ref, vp_ref, qx_ref, kx_ref, vx_ref, ba_ref,
                     cw_ref, alog_ref, dtb_ref, qo_ref, ko_ref, vo_ref, bg_ref, ext_scr, *, n_ctx, seq, dec_seq):
    rows = q_ref.shape[0]
    r0 = pl.program_id(0) * rows
    in_ctx = r0 < n_ctx
    off = jnp.where(in_ctx, r0, r0 - n_ctx)
    length = jnp.where(in_ctx, seq, dec_seq)
    is_start = lax.rem(off, length) == 0
    is_end = lax.rem(off + rows, length) == 0
    lane_lo = _lane_lo(rows)
    cw = cw_ref[...]
    width = q_ref.shape[1]

    def conv(x_ref, prev_ref, next_ref, o_ref, a, norm_scale):
        ext_scr[0:SUBLANES, :] = jnp.where(is_start, 0.0, prev_ref[...])
        ext_scr[SUBLANES:SUBLANES + rows, :] = x_ref[...]
        ext_scr[SUBLANES + rows:, :] = jnp.where(is_end, 0.0, next_ref[...])
        acc = None
        for j in range(CONV_K):
            o = SUBLANES - CONV_K // 2 + j
            term = ext_scr[o:o + rows, :] * cw[j:j + 1, a * width:(a + 1) * width]
            acc = term if acc is None else acc + term
        y = _silu(acc)
        for g in range(width // LANES):
            yg = y[:, g * LANES:(g + 1) * LANES]
            if norm_scale is not None:
                yg = yg * lax.rsqrt(_half_sums(yg * yg, lane_lo) + EPS) * norm_scale
            o_ref[:, g * LANES:(g + 1) * LANES] = yg

    conv(q_ref, qp_ref, qx_ref, qo_ref, 0, HEAD_DIM ** -0.5)
    conv(k_ref, kp_ref, kx_ref, ko_ref, 1, 1.0)
    conv(v_ref, vp_ref, vx_ref, vo_ref, 2, None)

    x = ba_ref[...]
    g = -jnp.exp(alog_ref[...]) * _softplus(x + dtb_ref[...])
    row_in_chunk = lax.broadcasted_iota(jnp.int32, (rows, LANES), 0) % CHUNK
    pre = suf = g
    s = 1
    while s < CHUNK:
        pre = pre + jnp.where(row_in_chunk >= s, pltpu.roll(pre, s, axis=0), 0.0)
        suf = suf + jnp.where(row_in_chunk + s < CHUNK, pltpu.roll(suf, rows - s, axis=0), 0.0)
        s *= 2
    c = lax.broadcasted_iota(jnp.int32, (rows, LANES), 1) % 8
    bg_ref[...] = jnp.where(c < 4, _sigmoid(x), jnp.where(c < 6, pre, suf))


def _gdn_prep(proj, conv_w, alog_row, dtb_row, n_ctx, seq, dec_seq):
    nt = proj.shape[0]
    rows = _TB
    assert seq % rows == 0 and dec_seq % rows == 0
    w = PAIRS * LANES
    per = rows // SUBLANES
    last = nt // SUBLANES - 1
    main = [pl.BlockSpec((rows, w), lambda i, a=a: (i, a)) for a in range(3)]
    prev = [pl.BlockSpec((SUBLANES, w), lambda i, a=a: (jnp.maximum(i * per - 1, 0), a)) for a in range(3)]
    nxt = [pl.BlockSpec((SUBLANES, w), lambda i, a=a: (jnp.minimum((i + 1) * per, last), a)) for a in range(3)]
    row = pl.BlockSpec((1, LANES), lambda i: (0, 0))
    return pl.pallas_call(
        functools.partial(_gdn_prep_kernel, n_ctx=n_ctx, seq=seq, dec_seq=dec_seq),
        out_shape=[jax.ShapeDtypeStruct((nt, w), F32)] * 3 + [jax.ShapeDtypeStruct((nt, LANES), F32)],
        grid=(nt // rows,),
        in_specs=main + prev + nxt + [pl.BlockSpec((rows, LANES), lambda i: (i, G_BA)),
                                      pl.BlockSpec(conv_w.shape, lambda i: (0, 0)), row, row],
        out_specs=[pl.BlockSpec((rows, w), lambda i: (i, 0))] * 3 + [pl.BlockSpec((rows, LANES), lambda i: (i, 0))],
        scratch_shapes=[pltpu.VMEM((rows + 2 * SUBLANES, w), F32)],
        compiler_params=_cparams(("parallel",), "gdn_prep"),
        name="gdn_prep",
    )(*([proj] * 10), conv_w, alog_row, dtb_row)


def _run_interleaved(gens):
    gens = list(gens)
    while gens:
        for g in list(gens):
            try:
                next(g)
            except StopIteration:
                gens.remove(g)


def _gdn_scan_kernel(*refs, nch, ns, nsub, has_s0, has_sfin):
    fwd_refs, bwd_refs = refs[0:4], refs[4:8]
    pos = 8
    s0_ref = None
    if has_s0:
        s0_ref = refs[pos]
        pos += 1
    of_ref, ob_ref = refs[pos:pos + 2]
    pos += 2
    sfin_ref = None
    if has_sfin:
        sfin_ref = refs[pos]
        pos += 1
    s_scr = refs[pos]
    j = pl.program_id(1)
    n2 = 2 * CHUNK
    chains = [(sq, p, d) for sq in range(ns) for p in range(PAIRS) for d in range(2)]
    units = [ch + (sub,) for sub in range(nsub) for ch in chains]
    idx = range(len(units))
    n_steps = nch // nsub

    @pl.when(j == 0)
    def _():
        for u, (sq, p, d) in enumerate(chains):
            s_scr[u] = s0_ref[sq, p, d] if has_s0 else jnp.zeros((n2, n2), F32)

    ri = lax.broadcasted_iota(jnp.int32, (n2, n2), 0)
    ci = lax.broadcasted_iota(jnp.int32, (n2, n2), 1)
    same = (ri >= CHUNK) == (ci >= CHUNK)
    incl = (same & (ri >= ci), same & (ri <= ci))
    strict = (same & (ri > ci), same & (ri < ci))
    eye = (ri == ci).astype(F32)
    ril, cil = ri % CHUNK, ci % CHUNK
    level_mask = []
    m = 1
    while m < CHUNK:
        level_mask.append(same & (ril // (2 * m) == cil // (2 * m)) & (ril // m != cil // m))
        m *= 2
    level_sel = [jnp.where(lm, 1.0, 0.0).astype(BF16) for lm in level_mask[1:]]

    def col_stack(arr, ca, cb):
        top = jnp.broadcast_to(arr[:, ca:ca + 1], (CHUNK, LANES))
        bot = jnp.broadcast_to(arr[:, cb:cb + 1], (CHUNK, LANES))
        return jnp.concatenate([top, bot], axis=0)

    def row_stack(arr, row):
        top = jnp.broadcast_to(arr[row:row + 1, :], (CHUNK, LANES))
        bot = jnp.broadcast_to(arr[CHUNK + row:CHUNK + row + 1, :], (CHUNK, LANES))
        return jnp.concatenate([top, bot], axis=0)

    def rows_of(c, sq, d, sub):
        k = c * nsub + sub
        r = (sq * nch + (k if d == 0 else nch - 1 - k)) * CHUNK
        return slice(r, r + CHUNK)

    feats = [dict() for _ in range(n_steps)]
    solved = [dict() for _ in range(n_steps)]
    states = [s_scr[u] for u in range(len(chains))]
    dirs = [u[2] for u in units]

    def features(c):
        f = feats[c]
        src = lambda a, sq, p, d, sub: (fwd_refs, bwd_refs)[d][a][rows_of(c, sq, d, sub), :]
        bg = [src(3, *u) for u in units]
        b_c = [col_stack(bg[i], 8 * p + 2 * d, 8 * p + 2 * d + 1)
               for i, (sq, p, d, sub) in enumerate(units)]
        g_c = [col_stack(bg[i], 8 * p + 4 + 2 * d, 8 * p + 5 + 2 * d)
               for i, (sq, p, d, sub) in enumerate(units)]
        yield
        ld = lambda a, u: (fwd_refs, bwd_refs)[u[2]][a][rows_of(c, u[0], u[2], u[3]), u[1] * LANES:(u[1] + 1) * LANES]
        k2 = [jnp.concatenate([ld(1, u)] * 2, axis=0) for u in units]
        ks = [jnp.where(same, k2[i], 0.0) for i in idx]
        qs = [jnp.where(same, jnp.concatenate([ld(0, u)] * 2, axis=0), 0.0) for u in units]
        ks_b = [ks[i].astype(BF16) for i in idx]
        prod = [_dot_nt(jnp.concatenate([ks_b[i], qs[i].astype(BF16)], axis=0), ks_b[i]) for i in idx]
        yield
        decay = [jnp.exp(jnp.where(incl[dirs[i]], g_c[i] - g_c[i].T, NEG)) for i in idx]
        yield
        eg = [jnp.exp(g_c[i]) for i in idx]
        g_last = [row_stack(g_c[i], CHUNK - 1 if dirs[i] == 0 else 0) for i in idx]
        f["l"] = [jnp.where(strict[dirs[i]], b_c[i] * prod[i][:n2] * decay[i], 0.0) for i in idx]
        yield
        v2 = [jnp.concatenate([ld(2, u)] * 2, axis=0) for u in units]
        f["rhs"] = [(b_c[i] * jnp.where(same, v2[i], eg[i] * pltpu.roll(k2[i], CHUNK, axis=1))).astype(BF16)
                    for i in idx]
        yield
        f["at"] = [(prod[i][n2:] * decay[i]).astype(BF16) for i in idx]
        f["qg"] = [(qs[i] * eg[i]).astype(BF16) for i in idx]
        yield
        f["kt"] = [(ks[i] * jnp.exp(g_last[i] - g_c[i])).T.astype(BF16) for i in idx]
        f["gl"] = [jnp.exp(g_last[i]) for i in idx]
        yield

    def inverse(c):
        f, o = feats[c], solved[c]
        l_mat = f["l"]
        t_inv = [eye - jnp.where(level_mask[0], l_mat[i], 0.0) for i in idx]
        l_b = [l_mat[i].astype(BF16) for i in idx]
        for lm in level_sel:
            t_b = [t_inv[i].astype(BF16) for i in idx]
            y = [_dot(l_b[i] * lm, t_b[i]).astype(BF16) for i in idx]
            yield
            t_inv = [t_inv[i] - _dot(t_b[i], y[i]) for i in idx]
            yield
        xs = [_dot(t_inv[i].astype(BF16), f["rhs"][i]) for i in idx]
        yield
        o["u"] = [jnp.where(same, xs[i], 0.0) for i in idx]
        o["wq"] = [jnp.concatenate([jnp.where(same, pltpu.roll(xs[i], CHUNK, axis=1), 0.0).astype(BF16),
                                    f["qg"][i]], axis=0) for i in idx]
        yield

    def scan(c):
        f, o = feats[c], solved[c]
        for sub in range(nsub):
            ui = [sub * len(chains) + k for k in range(len(chains))]
            ws = [_dot(o["wq"][i], states[k].astype(BF16)) for k, i in enumerate(ui)]
            yield
            u_b = [(o["u"][i] - ws[k][:n2]).astype(BF16) for k, i in enumerate(ui)]
            yield
            o_st = [ws[k][n2:] + _dot(f["at"][i], u_b[k]) for k, i in enumerate(ui)]
            for k, i in enumerate(ui):
                states[k] = f["gl"][i] * states[k] + _dot(f["kt"][i], u_b[k])
            yield
            for k, i in enumerate(ui):
                sq, p, d, _ = units[i]
                dst = of_ref if d == 0 else ob_ref
                dst[rows_of(c, sq, d, sub), p * LANES:(p + 1) * LANES] = o_st[k][:CHUNK] + o_st[k][CHUNK:]
            yield
        feats[c] = solved[c] = None

    for t in range(n_steps + 2):
        gens = []
        if 1 <= t <= n_steps:
            gens.append(inverse(t - 1))
        if t < n_steps:
            gens.append(features(t))
        if 2 <= t:
            gens.append(scan(t - 2))
        _run_interleaved(gens)

    for u in range(len(chains)):
        s_scr[u] = states[u]
    if has_sfin:
        @pl.when(j == pl.num_programs(1) - 1)
        def _():
            for u, (sq, p, d) in enumerate(chains):
                sfin_ref[sq, p, d] = states[u]


def _gdn_scan(feat, s0, *, t, row0, n_seq, want_state):
    tb = math.gcd(t, _SCAN_ROWS)
    nb = t // tb
    ns = math.gcd(n_seq, _SCAN_ROWS // tb) if nb == 1 else 1
    nsub = math.gcd(tb // CHUNK, max(1, _SCAN_UNITS // (ns * PAIRS)))
    rows = ns * tb
    assert row0 % rows == 0
    blk0 = row0 // rows
    n2 = 2 * CHUNK
    w = PAIRS * LANES
    has_s0 = s0 is not None

    def blocks(mirror):
        row = (lambda b, j: b * nb + (nb - 1 - j)) if mirror else (lambda b, j: b * nb + j)
        return ([pl.BlockSpec((rows, w), lambda b, j: (blk0 + row(b, j), 0))] * 3
                + [pl.BlockSpec((rows, LANES), lambda b, j: (blk0 + row(b, j), 0))]), row

    f_specs, f_row = blocks(False)
    b_specs, b_row = blocks(True)
    in_specs = f_specs + b_specs
    args = list(feat) * 2
    state_spec = pl.BlockSpec((ns, PAIRS, 2, n2, n2), lambda b, j: (b, 0, 0, 0, 0))
    if has_s0:
        in_specs.append(state_spec)
        args.append(s0)
    out_shape = [jax.ShapeDtypeStruct((n_seq * t, w), F32)] * 2
    out_specs = [pl.BlockSpec((rows, w), lambda b, j: (f_row(b, j), 0)),
                 pl.BlockSpec((rows, w), lambda b, j: (b_row(b, j), 0))]
    if want_state:
        out_shape.append(jax.ShapeDtypeStruct((n_seq, PAIRS, 2, n2, n2), F32))
        out_specs.append(state_spec)
    return pl.pallas_call(
        functools.partial(_gdn_scan_kernel, nch=tb // CHUNK, ns=ns, nsub=nsub, has_s0=has_s0, has_sfin=want_state),
        out_shape=out_shape,
        grid=(n_seq // ns, nb),
        in_specs=in_specs,
        out_specs=out_specs,
        scratch_shapes=[pltpu.VMEM((ns * 2 * PAIRS, n2, n2), F32)],
        compiler_params=_cparams(("parallel", "arbitrary"), "gdn_scan"),
        name="gdn_scan_ctx" if want_state else "gdn_scan_lat",
    )(*args)


def _prep_kernel(*refs, rope):
    bq_ref, cq_ref, bk_ref, bv_ref, ck_ref, cv_ref, qkn_ref = refs[:7]
    pos = 7
    cs = sn = None
    if rope:
        cs, sn = refs[7][...], refs[8][...]
        pos = 11
    qb_ref, qc_ref, kdb_ref, vdb_ref, kdc_ref, vdc_ref = refs[pos:pos + 6]
    pos += 6
    tr = bq_ref.shape[0]
    lane = lax.broadcasted_iota(jnp.int32, (tr, LANES), 1)
    lane_lo = lane < HEAD_DIM
    first_half = (lane % HEAD_DIM) < HEAD_DIM // 2
    qkn = qkn_ref[...]

    ones_bd = ((lax.broadcasted_iota(jnp.int32, (LANES, LANES), 0) < HEAD_DIM)
               == (lax.broadcasted_iota(jnp.int32, (LANES, LANES), 1) < HEAD_DIM)).astype(BF16)

    def norm(x, w_row):
        sq = x * x
        hi = sq.astype(BF16)
        lo = (sq - hi.astype(F32)).astype(BF16)
        ms = (_dot(hi, ones_bd) + _dot(lo, ones_bd)) * (1.0 / HEAD_DIM)
        return x * lax.rsqrt(ms + EPS) * w_row

    def rot(y):
        if not rope:
            return y
        partner = jnp.where(first_half, pltpu.roll(y, LANES - HEAD_DIM // 2, axis=1),
                            pltpu.roll(y, HEAD_DIM // 2, axis=1))
        return y * cs + partner * sn

    def dup(x, o_ref):
        lo = jnp.where(lane_lo, x, 0.0)
        hi = jnp.where(lane_lo, 0.0, x)
        d0 = (lo + pltpu.roll(lo, HEAD_DIM, axis=1)).astype(BF16)
        d1 = (hi + pltpu.roll(hi, HEAD_DIM, axis=1)).astype(BF16)
        if len(o_ref.shape) == 3:
            o_ref[0, :, 0:LANES] = d0
            o_ref[0, :, LANES:2 * LANES] = d1
        else:
            o_ref[:, 0:LANES] = d0
            o_ref[:, LANES:2 * LANES] = d1

    for q_ref, o_ref, wi in ((bq_ref, qb_ref, 0), (cq_ref, qc_ref, 2)):
        for g in range(PAIRS):
            y = rot(norm(q_ref[:, g * LANES:(g + 1) * LANES], qkn[wi:wi + 1, :])) * HEAD_DIM ** -0.5
            o_ref[2 * g] = jnp.where(lane_lo, y, 0.0).astype(BF16)
            o_ref[2 * g + 1] = jnp.where(lane_lo, 0.0, y).astype(BF16)
    kb_n = norm(bk_ref[...], qkn[1:2, :])
    kc_n = norm(ck_ref[...], qkn[3:4, :])
    if not rope:
        kbn_ref, vbn_ref, kcn_ref, vcn_ref = refs[pos:pos + 4]
        kbn_ref[...] = kb_n
        vbn_ref[...] = bv_ref[...]
        kcn_ref[...] = kc_n
        vcn_ref[...] = cv_ref[...]
    dup(rot(kb_n), kdb_ref)
    dup(bv_ref[...], vdb_ref)
    dup(rot(kc_n), kdc_ref)
    dup(cv_ref[...], vdc_ref)


def _attn_prep(proj, qkn_rows, rope_tables, glob_kv, *, row0, n_rows, t):
    rope = rope_tables is not None
    tr = math.gcd(t, 512)
    blk0 = row0 // tr

    def col(g, width):
        gw = width // LANES
        return pl.BlockSpec((tr, width), lambda i, g=g, gw=gw: (blk0 + i, g // gw))

    in_specs = [col(G_BQ, 4 * LANES), col(G_CQ, 4 * LANES), col(G_BK, LANES), col(G_BV, LANES),
                col(G_CK, LANES), col(G_CV, LANES), pl.BlockSpec((4, LANES), lambda i: (0, 0))]
    args = [proj] * 6 + [qkn_rows]
    q_shape = jax.ShapeDtypeStruct((HEADS, n_rows, LANES), BF16)
    kv_shape = jax.ShapeDtypeStruct((n_rows, 2 * LANES), BF16)
    q_spec = pl.BlockSpec((HEADS, tr, LANES), lambda i: (0, i, 0))
    kv_spec = pl.BlockSpec((tr, 2 * LANES), lambda i: (i, 0))
    aliases = {}
    if rope:
        per_seq = t // tr
        in_specs += [pl.BlockSpec((tr, LANES), lambda i: (i % per_seq, 0))] * 2 + [_any_spec()] * 2
        aliases = {len(args) + 2: 2, len(args) + 3: 3}
        args += list(rope_tables) + list(glob_kv)
        past = glob_kv[0].shape[1] - t
        assert past % tr == 0
        glob_spec = pl.BlockSpec((1, tr, 2 * LANES), lambda i: (i // per_seq, past // tr + i % per_seq, 0))
        glob_shape = jax.ShapeDtypeStruct(glob_kv[0].shape, BF16)
        out_shape = [q_shape, q_shape, glob_shape, glob_shape, kv_shape, kv_shape]
        out_specs = [q_spec, q_spec, glob_spec, glob_spec, kv_spec, kv_spec]
    else:
        out_shape = [q_shape, q_shape] + [kv_shape] * 4 + [jax.ShapeDtypeStruct((n_rows, LANES), F32)] * 4
        out_specs = [q_spec, q_spec] + [kv_spec] * 4 + [pl.BlockSpec((tr, LANES), lambda i: (i, 0))] * 4
    return pl.pallas_call(
        functools.partial(_prep_kernel, rope=rope),
        out_shape=out_shape,
        grid=(n_rows // tr,),
        in_specs=in_specs,
        out_specs=out_specs,
        input_output_aliases=aliases,
        compiler_params=_cparams(("parallel",), "attn_prep"),
        name="attn_prep_lat" if rope else "attn_prep_ctx",
    )(*args)


def _stacked_q(q_ref, g, tq):
    return q_ref[GROUP_HEADS * g:GROUP_HEADS * (g + 1)].reshape(GROUP_HEADS * tq, LANES)


def _unstack_group(o_st, tq, lane_lo):
    a = jnp.where(lane_lo, o_st[0:tq], o_st[tq:2 * tq])
    b = jnp.where(lane_lo, o_st[2 * tq:3 * tq], o_st[3 * tq:4 * tq])
    return jnp.concatenate([a, b], axis=1)


def _with_ones(v, ones):
    return jnp.concatenate([v, ones], axis=1)


def _sink_col(sink_row, g, tq):
    parts = [jnp.broadcast_to(sink_row[:, GROUP_HEADS * g + j:GROUP_HEADS * g + j + 1], (tq, 1))
             for j in range(GROUP_HEADS)]
    return jnp.concatenate(parts, axis=0)


def _attn_ctx_kernel(qb_ref, kdb_ref, vdb_ref, qc_ref, kdc_ref, vdc_ref, sink_ref, yb_ref, yc_ref):
    tq = qb_ref.shape[1]
    lane_lo_q = _lane_lo(tq)
    sink_row = sink_ref[...]
    groups = range(KV_HEADS)
    gs = [slice(g * LANES, (g + 1) * LANES) for g in groups]
    jobs = [(qb_ref, kdb_ref, vdb_ref, yb_ref, False, g) for g in groups]
    jobs += [(qc_ref, kdc_ref, vdc_ref, yc_ref, True, g) for g in groups]
    s = [_dot_nt(_stacked_q(q_ref, g, tq), kd_ref[:, gs[g]]) for q_ref, kd_ref, _, _, _, g in jobs]
    for si, (_, _, vd_ref, y_ref, use_sink, g) in zip(s, jobs):
        m = jnp.max(si, axis=-1, keepdims=True)
        if use_sink:
            snk = _sink_col(sink_row, g, tq)
            m = jnp.maximum(m, snk)
        p = jnp.exp(si - m)
        l = jnp.sum(p, axis=-1, keepdims=True)
        if use_sink:
            l = l + jnp.exp(snk - m)
        o = _dot(p.astype(BF16), vd_ref[:, gs[g]]) / l
        y_ref[:, 2 * g * LANES:(2 * g + 2) * LANES] = _unstack_group(o, tq, lane_lo_q).astype(y_ref.dtype)


def _attn_ctx(qb, kdb, vdb, qc, kdc, vdc, sink_row, *, t):
    n = kdb.shape[0]
    q_spec = pl.BlockSpec((HEADS, t, LANES), lambda b: (0, b, 0))
    kv_spec = pl.BlockSpec((t, 2 * LANES), lambda b: (b, 0))
    y_spec = pl.BlockSpec((t, 4 * LANES), lambda b: (b, 0))
    return pl.pallas_call(
        _attn_ctx_kernel,
        out_shape=[jax.ShapeDtypeStruct((n, 4 * LANES), BF16)] * 2,
        grid=(n // t,),
        in_specs=[q_spec, kv_spec, kv_spec, q_spec, kv_spec, kv_spec,
                  pl.BlockSpec((1, LANES), lambda b: (0, 0))],
        out_specs=[y_spec, y_spec],
        compiler_params=_cparams(("parallel",), "attn_ctx"),
        name="attn_ctx",
    )(qb, kdb, vdb, qc, kdc, vdc, sink_row)


_TQ_B = 256
_TK_B = 1536


def _attn_glob_kernel(q_ref, kd_ref, vd_ref, y_ref, s_scr, *, tk):
    tq = q_ref.shape[1]
    n_blk = kd_ref.shape[1] // tk
    groups = range(KV_HEADS)
    gs = [slice(g * LANES, (g + 1) * LANES) for g in groups]
    q = [_stacked_q(q_ref, g, tq) for g in groups]

    def scores_into(j, slot):
        for g in groups:
            s_scr[slot, g] = _dot_nt(q[g], kd_ref[0, j * tk:(j + 1) * tk, gs[g]])

    rows = GROUP_HEADS * tq
    m = [jnp.full((rows, LANES), NEG, F32) for _ in groups]
    acc = [jnp.zeros((rows, 2 * LANES), F32) for _ in groups]
    ones = jnp.ones((tk, LANES), BF16)
    scores_into(0, 0)
    for j in range(n_blk):
        if j + 1 < n_blk:
            scores_into(j + 1, (j + 1) % 2)
        for g in groups:
            s = s_scr[j % 2, g]
            m_new = jnp.maximum(m[g], jnp.max(s, axis=-1, keepdims=True))
            p = jnp.exp(s - m_new[:, 0:1]).astype(BF16)
            alpha = jnp.exp(m[g] - m_new)
            acc[g] = (jnp.concatenate([alpha, alpha], axis=1) * acc[g]
                      + _dot(p, _with_ones(vd_ref[0, j * tk:(j + 1) * tk, gs[g]], ones)))
            m[g] = m_new
    lane_lo_q = _lane_lo(tq)
    for g in groups:
        o = acc[g][:, :LANES] / acc[g][:, LANES:]
        y_ref[:, 2 * g * LANES:(2 * g + 2) * LANES] = _unstack_group(o, tq, lane_lo_q).astype(y_ref.dtype)


def _attn_glob(q, kd, vd, *, t):
    n = q.shape[1]
    s_len = kd.shape[1]
    tq, tk = math.gcd(t, _TQ_B), math.gcd(s_len, _TK_B)
    per_seq = t // tq
    kv_spec = pl.BlockSpec((1, s_len, 2 * LANES), lambda b, i: (b, 0, 0))
    return pl.pallas_call(
        functools.partial(_attn_glob_kernel, tk=tk),
        out_shape=jax.ShapeDtypeStruct((n, 4 * LANES), BF16),
        grid=(n // t, per_seq),
        in_specs=[pl.BlockSpec((HEADS, tq, LANES), lambda b, i: (0, b * per_seq + i, 0)), kv_spec, kv_spec],
        out_specs=pl.BlockSpec((tq, 4 * LANES), lambda b, i: (b * per_seq + i, 0)),
        scratch_shapes=[pltpu.VMEM((2, KV_HEADS, GROUP_HEADS * tq, tk), F32)],
        compiler_params=_cparams(("parallel", "parallel"), "attn_glob"),
        name="attn_glob",
    )(q, kd, vd)


def _attn_win_kernel(q_ref, kx_ref, vx_ref, kp_ref, kc_ref, kn_ref, vp_ref, vc_ref, vn_ref,
                     sink_ref, y_ref):
    i = pl.program_id(1)
    nblk = pl.num_programs(1)
    tq = q_ref.shape[1]
    rows = GROUP_HEADS * tq
    lane_lo_q = _lane_lo(tq)
    sink_row = sink_ref[...]
    rr_w = lax.broadcasted_iota(jnp.int32, (rows, WINDOW), 0) % tq
    cc_w = lax.broadcasted_iota(jnp.int32, (rows, WINDOW), 1)
    rr_q = lax.broadcasted_iota(jnp.int32, (rows, tq), 0) % tq
    cc_q = lax.broadcasted_iota(jnp.int32, (rows, tq), 1)
    mask_prev = (cc_w >= rr_w) & (i > 0)
    mask_cur = jnp.abs(rr_q - cc_q) <= WINDOW
    mask_next = (cc_w <= rr_w - (tq - WINDOW)) & (i < nblk - 1)
    groups = range(KV_HEADS)
    gs = [slice(g * LANES, (g + 1) * LANES) for g in groups]
    qs = [_stacked_q(q_ref, g, tq) for g in groups]
    p_len = kx_ref.shape[1]
    mask_edge = jnp.concatenate([mask_prev, mask_next], axis=1)
    edge = lambda p_ref, n_ref, g: jnp.concatenate([p_ref[:, gs[g]], n_ref[:, gs[g]]], axis=0)
    s = [jnp.concatenate([_dot_nt(qs[g], kx_ref[0, :, gs[g]]),
                          jnp.where(mask_edge, _dot_nt(qs[g], edge(kp_ref, kn_ref, g)), NEG),
                          jnp.where(mask_cur, _dot_nt(qs[g], kc_ref[:, gs[g]]), NEG)], axis=1) for g in groups]
    parts = ((0, p_len, lambda g: vx_ref[0, :, gs[g]]), (p_len, 2 * WINDOW, lambda g: edge(vp_ref, vn_ref, g)),
             (p_len + 2 * WINDOW, tq, lambda g: vc_ref[:, gs[g]]))
    for g in groups:
        snk = _sink_col(sink_row, g, tq)
        m = jnp.maximum(jnp.max(s[g], axis=-1, keepdims=True), snk)
        p = jnp.exp(s[g] - m).astype(BF16)
        ol = None
        for start, size, val in parts:
            d = _dot(p[:, start:start + size], _with_ones(val(g), jnp.ones((size, LANES), BF16)))
            ol = d if ol is None else ol + d
        o = ol[:, :LANES] / (ol[:, LANES:] + jnp.exp(snk - m))
        y_ref[:, 2 * g * LANES:(2 * g + 2) * LANES] = _unstack_group(o, tq, lane_lo_q).astype(y_ref.dtype)


def _attn_win(q, kd, vd, kx, vx, sink_row, layer, *, t):
    n = q.shape[1]
    p_len = kx.shape[2]
    tq = math.gcd(t, 2 * WINDOW)
    nblk = t // tq
    per = tq // WINDOW
    wblk = t // WINDOW
    x_spec = pl.BlockSpec((None, 1, p_len, 2 * LANES), lambda b, i: (layer, b, 0, 0))
    prev = pl.BlockSpec((WINDOW, 2 * LANES), lambda b, i: (b * wblk + jnp.maximum(i * per - 1, 0), 0))
    cur = pl.BlockSpec((tq, 2 * LANES), lambda b, i: (b * nblk + i, 0))
    nxt = pl.BlockSpec((WINDOW, 2 * LANES), lambda b, i: (b * wblk + jnp.minimum((i + 1) * per, wblk - 1), 0))
    return pl.pallas_call(
        _attn_win_kernel,
        out_shape=jax.ShapeDtypeStruct((n, 4 * LANES), BF16),
        grid=(n // t, nblk),
        in_specs=[pl.BlockSpec((HEADS, tq, LANES), lambda b, i: (0, b * nblk + i, 0)),
                  x_spec, x_spec, prev, cur, nxt, prev, cur, nxt,
                  pl.BlockSpec((1, LANES), lambda b, i: (0, 0))],
        out_specs=pl.BlockSpec((tq, 4 * LANES), lambda b, i: (b * nblk + i, 0)),
        compiler_params=_cparams(("parallel", "parallel"), "attn_win"),
        name="attn_win",
    )(q, kx, vx, kd, kd, kd, vd, vd, vd, sink_row)


def _mix_kernel(x_ref, ofc_ref, obc_ref, ybc_ref, ycc_ref, ofl_ref, obl_ref, ybl_ref, ycl_ref, z_ref,
                ga_ref, gb_ref, gc_ref, mod_ref, anorm_ref, wa_ref, wb_ref, wc_ref, wo_ref, o_ref, *, ctx_tiles):
    d = x_ref.shape[1]
    tm = x_ref.shape[0]
    lane_lo = _lane_lo(tm)
    anorm = anorm_ref[...]
    is_ctx = pl.program_id(0) < ctx_tiles

    def pick(c_ref, l_ref, cols=slice(None)):
        return jnp.where(is_ctx, c_ref[:, cols], l_ref[:, cols])

    ya = []
    for g in range(PAIRS):
        gl = slice(g * LANES, (g + 1) * LANES)
        o = pick(ofc_ref, ofl_ref, gl) + pick(obc_ref, obl_ref, gl)
        ms = _half_sums(o * o, lane_lo) * (1.0 / HEAD_DIM)
        ya.append((o * lax.rsqrt(ms + EPS) * anorm * _silu(z_ref[:, gl])).astype(BF16))
    ya = jnp.concatenate(ya, axis=1)
    merged = (_sigmoid(ga_ref[...]) * _dot(ya, wa_ref[...])
              + _sigmoid(gb_ref[...]) * _dot(pick(ybc_ref, ybl_ref), wb_ref[...])
              + _sigmoid(gc_ref[...]) * _dot(pick(ycc_ref, ycl_ref), wc_ref[...]))
    gate1 = mod_ref[0][:, 2 * d:3 * d]
    o_ref[...] = x_ref[...] + gate1 * _dot(merged.astype(BF16), wo_ref[...])


def _mix(x, ctx_parts, lat_parts, proj, mods, anorm_row, wa, wb, wc, wo, layer, n_ctx, dec_seq):
    nt, d = x.shape
    tm = _row_tile(n_ctx, dec_seq, 512)
    ctx_tiles = n_ctx // tm
    w = PAIRS * LANES
    c_spec = pl.BlockSpec((tm, w), lambda i: (jnp.minimum(i, ctx_tiles - 1), 0))
    l_spec = pl.BlockSpec((tm, w), lambda i: (jnp.maximum(i - ctx_tiles, 0), 0))
    gw = d // LANES

    def gate(k):
        return pl.BlockSpec((tm, d), lambda i, k=k: (i, G_GATES // gw + k))

    def weight(a):
        return pl.BlockSpec((None,) + a.shape[1:], lambda i: (layer, 0, 0))

    return pl.pallas_call(
        functools.partial(_mix_kernel, ctx_tiles=ctx_tiles),
        out_shape=jax.ShapeDtypeStruct((nt, d), F32),
        grid=(nt // tm,),
        in_specs=[pl.BlockSpec((tm, d), lambda i: (i, 0))] + [c_spec] * 4 + [l_spec] * 4
                 + [pl.BlockSpec((tm, w), lambda i: (i, G_Z // PAIRS)), gate(0), gate(1), gate(2),
                    pl.BlockSpec((1, 1, 6 * d), lambda i: (_mod_index(i, tm, n_ctx, dec_seq), 0, 0)),
                    pl.BlockSpec(anorm_row.shape, lambda i: (0, 0)),
                    weight(wa), weight(wb), weight(wc), weight(wo)],
        out_specs=pl.BlockSpec((tm, d), lambda i: (i, 0)),
        compiler_params=_cparams(("parallel",), "mix_out"),
        name="mix_out",
    )(x, *ctx_parts, *lat_parts, proj, proj, proj, proj, mods, anorm_row, wa, wb, wc, wo)


_FF_CHUNK = 1024


def _ffn_kernel(x_ref, mod_ref, ln_ref, w1_ref, w2_ref, o_ref):
    d = x_ref.shape[1]
    m = mod_ref[0]
    x = x_ref[...]
    h = _mod_norm(x, ln_ref[...], m[:, 3 * d:4 * d], m[:, 4 * d:5 * d]).astype(BF16)
    acc = None
    for f in range(0, w1_ref.shape[1], _FF_CHUNK):
        hid = jnp.maximum(_dot(h, w1_ref[:, f:f + _FF_CHUNK]), 0.0)
        part = _dot((hid * hid).astype(BF16), w2_ref[f:f + _FF_CHUNK, :])
        acc = part if acc is None else acc + part
    o_ref[...] = x + m[:, 5 * d:6 * d] * acc


def _ffn(x, mods, ln2, w1, w2, layer, n_ctx, dec_seq):
    nt, d = x.shape
    tm = _row_tile(n_ctx, dec_seq, 512)
    return pl.pallas_call(
        _ffn_kernel,
        out_shape=jax.ShapeDtypeStruct((nt, d), F32),
        grid=(nt // tm,),
        in_specs=[pl.BlockSpec((tm, d), lambda i: (i, 0)),
                  pl.BlockSpec((1, 1, 6 * d), lambda i: (_mod_index(i, tm, n_ctx, dec_seq), 0, 0)),
                  pl.BlockSpec((1, d), lambda i: (0, 0)),
                  pl.BlockSpec((None,) + w1.shape[1:], lambda i: (layer, 0, 0)),
                  pl.BlockSpec((None,) + w2.shape[1:], lambda i: (layer, 0, 0))],
        out_specs=pl.BlockSpec((tm, d), lambda i: (i, 0)),
        compiler_params=_cparams(("parallel",), "ffn"),
        name="ffn",
    )(x, mods, ln2, w1, w2)


def _ba_lane_order():
    cols = []
    for p in range(PAIRS):
        for base in (0, 2 * HEADS):
            for d in range(2):
                cols += [base + d * HEADS + 2 * p, base + d * HEADS + 2 * p + 1]
    return cols


def _w_in_segments(n_cols):
    a_w = HEADS * HEAD_DIM
    kvw = KV_HEADS * HEAD_DIM
    o_beta = 4 * a_w
    o_bq = o_beta + 4 * HEADS
    o_bk = o_bq + a_w
    o_bv = o_bk + kvw
    o_cq = o_bv + kvw
    o_ck = o_cq + a_w
    o_cv = o_ck + kvw
    o_g = o_cv + kvw
    segs = [(0, 0, o_beta), (G_BQ * LANES, o_bq, a_w), (G_CQ * LANES, o_cq, a_w),
            (G_GATES * LANES, o_g, n_cols - o_g), (G_BK * LANES, o_bk, kvw), (G_BV * LANES, o_bv, kvw),
            (G_CK * LANES, o_ck, kvw), (G_CV * LANES, o_cv, kvw)]
    return segs, o_beta


def _w_in_kernel(w_ref, perm_ref, o_ref):
    segs, o_beta = _w_in_segments(w_ref.shape[2])
    for dst, src, width in segs:
        o_ref[0, :, dst:dst + width] = w_ref[0, :, src:src + width].astype(BF16)
    ba = w_ref[0, :, o_beta:o_beta + 4 * HEADS].astype(BF16)
    o_ref[0, :, G_BA * LANES:(G_BA + 1) * LANES] = _dot(ba, perm_ref[...]).astype(BF16)
    o_ref[0, :, (G_BA + 1) * LANES:] = jnp.zeros((w_ref.shape[1], IN_COLS_PAD - (G_BA + 1) * LANES), BF16)


def _reorder_w_in(w_in):
    depth, d, n = w_in.shape
    rows = math.gcd(d, 256)
    order = _ba_lane_order()
    perm = jnp.zeros((4 * HEADS, LANES), BF16).at[jnp.array(order), jnp.arange(len(order))].set(1.0)
    return pl.pallas_call(
        _w_in_kernel,
        out_shape=jax.ShapeDtypeStruct((depth, d, IN_COLS_PAD), BF16),
        grid=(depth, d // rows),
        in_specs=[pl.BlockSpec((1, rows, n), lambda l, i: (l, i, 0)),
                  pl.BlockSpec(perm.shape, lambda l, i: (0, 0))],
        out_specs=pl.BlockSpec((1, rows, IN_COLS_PAD), lambda l, i: (l, i, 0)),
        compiler_params=_cparams(("parallel", "parallel"), "w_in_prep"),
        name="w_in_prep",
    )(w_in, perm)


def _decay_rows(vals):
    rows = []
    for p in range(PAIRS):
        r = jnp.stack([vals[:, 0, 2 * p], vals[:, 0, 2 * p + 1], vals[:, 1, 2 * p], vals[:, 1, 2 * p + 1]], axis=-1)
        rows.append(jnp.pad(r, ((0, 0), (4, 0))))
    out = jnp.concatenate(rows, axis=-1)
    return jnp.pad(out, ((0, 0), (0, LANES - out.shape[1])))[:, None, :]


def _rope_tables(n_tokens):
    rows = n_tokens // GRID_W
    row_id = jnp.repeat(jnp.arange(rows, dtype=F32), GRID_W)
    col_id = jnp.tile(jnp.arange(GRID_W, dtype=F32), rows)
    n_freq = HEAD_DIM // 4
    inv_freq = ROPE_THETA ** (-jnp.arange(n_freq, dtype=F32) / n_freq)
    ang = jnp.concatenate([row_id[:, None] * inv_freq, col_id[:, None] * inv_freq], axis=-1)
    cos, sin = jnp.cos(ang), jnp.sin(ang)
    cs = jnp.tile(cos, (1, 4))
    sn = jnp.tile(jnp.concatenate([-sin, sin], axis=-1), (1, 2))
    return cs, sn


def _dup_kv(x, pad_rows=0):
    x = jnp.swapaxes(x, 0, 1)
    out = jnp.concatenate([x[..., 0, :], x[..., 0, :], x[..., 1, :], x[..., 1, :]], axis=-1).astype(BF16)
    return jnp.pad(out, ((0, 0), (0, 0), (0, pad_rows), (0, 0)))


def _block_diag_states(s):
    b, depth = s.shape[:2]
    s = s.reshape(b, depth, 2, PAIRS, 2, HEAD_DIM, HEAD_DIM).transpose(1, 0, 3, 2, 4, 5, 6)
    z = jnp.zeros_like(s[..., 0, :, :])
    top = jnp.concatenate([s[..., 0, :, :], z], axis=-1)
    bot = jnp.concatenate([z, s[..., 1, :, :]], axis=-1)
    return jnp.concatenate([top, bot], axis=-2)


def _diag_states(s):
    a = s[:, :, :, :HEAD_DIM, :HEAD_DIM]
    bb = s[:, :, :, HEAD_DIM:, HEAD_DIM:]
    out = jnp.stack([a, bb], axis=3)
    b = s.shape[0]
    return out.transpose(0, 2, 1, 3, 4, 5).reshape(b, 2, HEADS, HEAD_DIM, HEAD_DIM)


def kernel(x_prompt, x_sample, cache_k_glob, cache_v_glob, cache_k_win, cache_v_win, state_delta, c, c_ctx, w_mod, b_mod, ln1, ln2, w_in, conv_qkv, a_log, dt_bias, a_norm, qk_norm, sink, w_br_a, w_br_b, w_br_c, w_o, w_ff1, w_ff2):
    batch, seq, d = x_prompt.shape
    dec_batch, dec_seq, _ = x_sample.shape
    depth = w_mod.shape[0]
    n_ctx = batch * seq
    n_lat = dec_batch * dec_seq
    assert 1 + dec_batch <= SUBLANES

    w_in_r = _reorder_w_in(w_in)
    wa, wb, wc, wo = (w.astype(BF16) for w in (w_br_a, w_br_b, w_br_c, w_o))
    w1, w2 = w_ff1.astype(BF16), w_ff2.astype(BF16)
    alog_rows = _decay_rows(a_log)
    dtb_rows = _decay_rows(dt_bias)
    anorm_rows = jnp.tile(a_norm, (1, 2))[:, None, :]
    qkn_rows = jnp.tile(qk_norm, (1, 1, 2))
    sink_rows = jnp.pad(sink, ((0, 0), (0, LANES - HEADS)))[:, None, :]
    rope_tables = _rope_tables(dec_seq)
    cvec = jnp.concatenate([c_ctx[None], c, jnp.zeros((SUBLANES - 1 - dec_batch, d), F32)], axis=0)
    mods_all = _mod_vectors(cvec, w_mod, b_mod)
    glob_k, glob_v = _dup_kv(cache_k_glob, dec_seq), _dup_kv(cache_v_glob, dec_seq)
    win_k, win_v = _dup_kv(cache_k_win), _dup_kv(cache_v_win)
    s0_lat = _block_diag_states(state_delta)

    x = jnp.concatenate([x_prompt.reshape(n_ctx, d), x_sample.reshape(n_lat, d)], axis=0)
    new_kg, new_vg, new_kw, new_vw, new_st = [], [], [], [], []
    for l in range(depth):
        mods = mods_all[l][:, None, :]
        proj = _in_proj(x, mods, ln1[l][None], w_in_r, l, n_ctx, dec_seq)

        feat = _gdn_prep(proj, conv_qkv[l], alog_rows[l], dtb_rows[l], n_ctx, seq, dec_seq)
        of_c, ob_c, s_fin = _gdn_scan(feat, None, t=seq, row0=0, n_seq=batch, want_state=True)
        of_l, ob_l = _gdn_scan(feat, s0_lat[l], t=dec_seq, row0=n_ctx, n_seq=dec_batch, want_state=False)

        qb_c, qc_c, kdb_c, vdb_c, kdc_c, vdc_c, kb_n, vb_n, kc_n, vc_n = _attn_prep(
            proj, qkn_rows[l], None, None, row0=0, n_rows=n_ctx, t=seq)
        qb_l, qc_l, kd_all, vd_all, kdc_l, vdc_l = _attn_prep(
            proj, qkn_rows[l], rope_tables, (glob_k[l], glob_v[l]), row0=n_ctx, n_rows=n_lat, t=dec_seq)

        yb_c, yc_c = _attn_ctx(qb_c, kdb_c, vdb_c, qc_c, kdc_c, vdc_c, sink_rows[l], t=seq)
        yb_l = _attn_glob(qb_l, kd_all, vd_all, t=dec_seq)
        yc_l = _attn_win(qc_l, kdc_l, vdc_l, win_k, win_v, sink_rows[l], l, t=dec_seq)

        x = _mix(x, (of_c, ob_c, yb_c, yc_c), (of_l, ob_l, yb_l, yc_l), proj, mods, anorm_rows[l],
                 wa, wb, wc, wo, l, n_ctx, dec_seq)
        x = _ffn(x, mods, ln2[l][None], w1, w2, l, n_ctx, dec_seq)

        kv_shape = (batch, seq, KV_HEADS, HEAD_DIM)
        new_kg.append(kb_n.reshape(kv_shape))
        new_vg.append(vb_n.reshape(kv_shape))
        new_kw.append(kc_n.reshape(kv_shape))
        new_vw.append(vc_n.reshape(kv_shape))
        new_st.append(_diag_states(s_fin))

    y_prompt = x[:n_ctx].reshape(batch, seq, d)
    y_sample = x[n_ctx:].reshape(dec_batch, dec_seq, d)
    return (y_prompt, y_sample, jnp.stack(new_kg, axis=1), jnp.stack(new_vg, axis=1),
            jnp.stack(new_kw, axis=1), jnp.stack(new_vw, axis=1), jnp.stack(new_st, axis=1))
```

```python
import functools
import math

import jax
import jax.numpy as jnp
from jax import lax
from jax.experimental import pallas as pl
from jax.experimental.pallas import tpu as pltpu

F32 = jnp.float32
BF16 = jnp.bfloat16

LANES = 128
SUBLANES = 8

HEAD_DIM = 64
HEADS = 8
KV_HEADS = 2
GROUP_HEADS = HEADS // KV_HEADS
CHUNK = 64
CONV_K = 5
GRID_W = 64
WINDOW = 128
ROPE_THETA = 10000.0
EPS = 1e-6
NEG = -1e30
PAIRS = HEADS // 2

G_Q, G_K, G_V, G_Z = 0, 4, 8, 12
G_BQ, G_CQ = 16, 20
G_GATES = 24
G_BK, G_BV, G_CK, G_CV = 48, 49, 50, 51
G_BA = 52
N_GROUPS = 54
IN_COLS_PAD = N_GROUPS * LANES


VMEM_MIB_V7X = 64
MXU_TILE = 256

_VMEM_MIB = {
    "mod_vectors": 40,
    "in_proj": 48,
    "gdn_prep": 32,
    "gdn_scan": 48,
    "attn_prep": 32,
    "attn_ctx": 32,
    "attn_glob": 56,
    "attn_win": 32,
    "mix_out": 48,
    "ffn": 56,
}
assert max(_VMEM_MIB.values()) < VMEM_MIB_V7X


def _cparams(sem, call):
    return pltpu.CompilerParams(dimension_semantics=sem, vmem_limit_bytes=_VMEM_MIB[call] << 20)


def _dot(a, b):
    return jnp.dot(a, b, preferred_element_type=F32)


def _dot_nt(a, b):
    return lax.dot_general(a, b, (((1,), (1,)), ((), ())), preferred_element_type=F32)


def _sigmoid(x):
    return 0.5 * jnp.tanh(0.5 * x) + 0.5


def _silu(x):
    return x * _sigmoid(x)


def _softplus(x):
    return jnp.maximum(x, 0.0) + jnp.log1p(jnp.exp(-jnp.abs(x)))


def _half_sums(x, lane_lo):
    s_lo = jnp.sum(jnp.where(lane_lo, x, 0.0), axis=-1, keepdims=True)
    s_hi = jnp.sum(jnp.where(lane_lo, 0.0, x), axis=-1, keepdims=True)
    return jnp.where(lane_lo, s_lo, s_hi)


def _lane_lo(rows):
    return lax.broadcasted_iota(jnp.int32, (rows, LANES), 1) < HEAD_DIM


def _row_tile(n_ctx, dec_seq, pref):
    return math.gcd(math.gcd(n_ctx, dec_seq), pref)


def _mod_index(i, tm, n_ctx, dec_seq):
    ctx_tiles = n_ctx // tm
    return jnp.where(i < ctx_tiles, 0, 1 + (i - ctx_tiles) // (dec_seq // tm))


def _any_spec():
    return pl.BlockSpec(memory_space=pl.ANY)


def _mod_kernel(c_ref, w_ref, b_ref, o_ref):
    s = _silu(c_ref[...])
    w = w_ref[0]
    s_hi = s.astype(BF16)
    s_lo = (s - s_hi.astype(F32)).astype(BF16)
    w_hi = w.astype(BF16)
    w_lo = (w - w_hi.astype(F32)).astype(BF16)
    acc = _dot(s_hi, w_hi) + (_dot(s_lo, w_hi) + _dot(s_hi, w_lo))
    o_ref[0] = acc + b_ref[0]


def _mod_vectors(cvec, w_mod, b_mod):
    depth, d, n = w_mod.shape
    tn = n // 4
    return pl.pallas_call(
        _mod_kernel,
        out_shape=jax.ShapeDtypeStruct((depth, SUBLANES, n), F32),
        grid=(depth, n // tn),
        in_specs=[pl.BlockSpec((SUBLANES, d), lambda l, j: (0, 0)),
                  pl.BlockSpec((1, d, tn), lambda l, j: (l, 0, j)),
                  pl.BlockSpec((1, 1, tn), lambda l, j: (l, 0, j))],
        out_specs=pl.BlockSpec((1, SUBLANES, tn), lambda l, j: (l, 0, j)),
        compiler_params=_cparams(("parallel", "parallel"), "mod_vectors"),
        name="mod_vectors",
    )(cvec, w_mod, b_mod.reshape(depth, 1, n))


def _mod_norm(x, ln, shift, scale):
    ms = jnp.mean(x * x, axis=-1, keepdims=True)
    return (x * lax.rsqrt(ms + EPS) * ln) * (1.0 + scale) + shift


def _inproj_kernel(x_ref, mod_ref, ln_ref, w_ref, o_ref, h_scr):
    d = x_ref.shape[1]

    @pl.when(pl.program_id(1) == 0)
    def _():
        m = mod_ref[0]
        h = _mod_norm(x_ref[...], ln_ref[...], m[:, 0:d], m[:, d:2 * d])
        h_scr[...] = h.astype(BF16)

    o_ref[...] = _dot(h_scr[...], w_ref[...])


def _in_proj(x, mods, ln1, w_in, layer, n_ctx, dec_seq):
    nt, d = x.shape
    n = w_in.shape[2]
    tm = _row_tile(n_ctx, dec_seq, 1024)
    tn = n // 3
    assert tn % MXU_TILE == 0
    return pl.pallas_call(
        _inproj_kernel,
        out_shape=jax.ShapeDtypeStruct((nt, n), F32),
        grid=(nt // tm, n // tn),
        in_specs=[pl.BlockSpec((tm, d), lambda i, j: (i, 0)),
                  pl.BlockSpec((1, 1, 6 * d), lambda i, j: (_mod_index(i, tm, n_ctx, dec_seq), 0, 0)),
                  pl.BlockSpec((1, d), lambda i, j: (0, 0)),
                  pl.BlockSpec((None, d, tn), lambda i, j: (layer, 0, j))],
        out_specs=pl.BlockSpec((tm, tn), lambda i, j: (i, j)),
        scratch_shapes=[pltpu.VMEM((tm, d), BF16)],
        compiler_params=_cparams(("parallel", "arbitrary"), "in_proj"),
        name="in_proj",
    )(x, mods, ln1, w_in)


_TB = 256
_SCAN_ROWS = 512
_SCAN_UNITS = 8


def _gdn_prep_kernel(q_ref, k_ref, v_ref, qp_ref, kp_ref, vp_ref, qx_ref, kx_ref, vx_ref, ba_ref,
                     cw_ref, alog_ref, dtb_ref, qo_ref, ko_ref, vo_ref, bg_ref, ext_scr, *, n_ctx, seq, dec_seq):
    rows = q_ref.shape[0]
    r0 = pl.program_id(0) * rows
    in_ctx = r0 < n_ctx
    off = jnp.where(in_ctx, r0, r0 - n_ctx)
    length = jnp.where(in_ctx, seq, dec_seq)
    is_start = lax.rem(off, length) == 0
    is_end = lax.rem(off + rows, length) == 0
    lane_lo = _lane_lo(rows)
    cw = cw_ref[...]
    width = q_ref.shape[1]

    def conv(x_ref, prev_ref, next_ref, o_ref, a, norm_scale):
        ext_scr[0:SUBLANES, :] = jnp.where(is_start, 0.0, prev_ref[...])
        ext_scr[SUBLANES:SUBLANES + rows, :] = x_ref[...]
        ext_scr[SUBLANES + rows:, :] = jnp.where(is_end, 0.0, next_ref[...])
        acc = None
        for j in range(CONV_K):
            o = SUBLANES - CONV_K // 2 + j
            term = ext_scr[o:o + rows, :] * cw[j:j + 1, a * width:(a + 1) * width]
            acc = term if acc is None else acc + term
        y = _silu(acc)
        for g in range(width // LANES):
            yg = y[:, g * LANES:(g + 1) * LANES]
            if norm_scale is not None:
                yg = yg * lax.rsqrt(_half_sums(yg * yg, lane_lo) + EPS) * norm_scale
            o_ref[:, g * LANES:(g + 1) * LANES] = yg

    conv(q_ref, qp_ref, qx_ref, qo_ref, 0, HEAD_DIM ** -0.5)
    conv(k_ref, kp_ref, kx_ref, ko_ref, 1, 1.0)
    conv(v_ref, vp_ref, vx_ref, vo_ref, 2, None)

    x = ba_ref[...]
    g = -jnp.exp(alog_ref[...]) * _softplus(x + dtb_ref[...])
    row_in_chunk = lax.broadcasted_iota(jnp.int32, (rows, LANES), 0) % CHUNK
    pre = suf = g
    s = 1
    while s < CHUNK:
        pre = pre + jnp.where(row_in_chunk >= s, pltpu.roll(pre, s, axis=0), 0.0)
        suf = suf + jnp.where(row_in_chunk + s < CHUNK, pltpu.roll(suf, rows - s, axis=0), 0.0)
        s *= 2
    c = lax.broadcasted_iota(jnp.int32, (rows, LANES), 1) % 8
    bg_ref[...] = jnp.where(c < 4, _sigmoid(x), jnp.where(c < 6, pre, suf))


def _gdn_prep(proj, conv_w, alog_row, dtb_row, n_ctx, seq, dec_seq):
    nt = proj.shape[0]
    rows = _TB
    assert seq % rows == 0 and dec_seq % rows == 0
    w = PAIRS * LANES
    per = rows // SUBLANES
    last = nt // SUBLANES - 1
    main = [pl.BlockSpec((rows, w), lambda i, a=a: (i, a)) for a in range(3)]
    prev = [pl.BlockSpec((SUBLANES, w), lambda i, a=a: (jnp.maximum(i * per - 1, 0), a)) for a in range(3)]
    nxt = [pl.BlockSpec((SUBLANES, w), lambda i, a=a: (jnp.minimum((i + 1) * per, last), a)) for a in range(3)]
    row = pl.BlockSpec((1, LANES), lambda i: (0, 0))
    return pl.pallas_call(
        functools.partial(_gdn_prep_kernel, n_ctx=n_ctx, seq=seq, dec_seq=dec_seq),
        out_shape=[jax.ShapeDtypeStruct((nt, w), F32)] * 3 + [jax.ShapeDtypeStruct((nt, LANES), F32)],
        grid=(nt // rows,),
        in_specs=main + prev + nxt + [pl.BlockSpec((rows, LANES), lambda i: (i, G_BA)),
                                      pl.BlockSpec(conv_w.shape, lambda i: (0, 0)), row, row],
        out_specs=[pl.BlockSpec((rows, w), lambda i: (i, 0))] * 3 + [pl.BlockSpec((rows, LANES), lambda i: (i, 0))],
        scratch_shapes=[pltpu.VMEM((rows + 2 * SUBLANES, w), F32)],
        compiler_params=_cparams(("parallel",), "gdn_prep"),
        name="gdn_prep",
    )(*([proj] * 10), conv_w, alog_row, dtb_row)


def _run_interleaved(gens):
    gens = list(gens)
    while gens:
        for g in list(gens):
            try:
                next(g)
            except StopIteration:
                gens.remove(g)


def _gdn_scan_kernel(*refs, nch, ns, nsub, has_s0, has_sfin):
    fwd_refs, bwd_refs = refs[0:4], refs[4:8]
    pos = 8
    s0_ref = None
    if has_s0:
        s0_ref = refs[pos]
        pos += 1
    of_ref, ob_ref = refs[pos:pos + 2]
    pos += 2
    sfin_ref = None
    if has_sfin:
        sfin_ref = refs[pos]
        pos += 1
    s_scr = refs[pos]
    j = pl.program_id(1)
    n2 = 2 * CHUNK
    chains = [(sq, p, d) for sq in range(ns) for p in range(PAIRS) for d in range(2)]
    units = [ch + (sub,) for sub in range(nsub) for ch in chains]
    idx = range(len(units))
    n_steps = nch // nsub

    @pl.when(j == 0)
    def _():
        for u, (sq, p, d) in enumerate(chains):
            s_scr[u] = s0_ref[sq, p, d] if has_s0 else jnp.zeros((n2, n2), F32)

    ri = lax.broadcasted_iota(jnp.int32, (n2, n2), 0)
    ci = lax.broadcasted_iota(jnp.int32, (n2, n2), 1)
    same = (ri >= CHUNK) == (ci >= CHUNK)
    incl = (same & (ri >= ci), same & (ri <= ci))
    strict = (same & (ri > ci), same & (ri < ci))
    eye = (ri == ci).astype(F32)
    ril, cil = ri % CHUNK, ci % CHUNK
    level_mask = []
    m = 1
    while m < CHUNK:
        level_mask.append(same & (ril // (2 * m) == cil // (2 * m)) & (ril // m != cil // m))
        m *= 2
    level_sel = [jnp.where(lm, 1.0, 0.0).astype(BF16) for lm in level_mask[1:]]

    def col_stack(arr, ca, cb):
        top = jnp.broadcast_to(arr[:, ca:ca + 1], (CHUNK, LANES))
        bot = jnp.broadcast_to(arr[:, cb:cb + 1], (CHUNK, LANES))
        return jnp.concatenate([top, bot], axis=0)

    def row_stack(arr, row):
        top = jnp.broadcast_to(arr[row:row + 1, :], (CHUNK, LANES))
        bot = jnp.broadcast_to(arr[CHUNK + row:CHUNK + row + 1, :], (CHUNK, LANES))
        return jnp.concatenate([top, bot], axis=0)

    def rows_of(c, sq, d, sub):
        k = c * nsub + sub
        r = (sq * nch + (k if d == 0 else nch - 1 - k)) * CHUNK
        return slice(r, r + CHUNK)

    feats = [dict() for _ in range(n_steps)]
    solved = [dict() for _ in range(n_steps)]
    states = [s_scr[u] for u in range(len(chains))]
    dirs = [u[2] for u in units]

    def features(c):
        f = feats[c]
        src = lambda a, sq, p, d, sub: (fwd_refs, bwd_refs)[d][a][rows_of(c, sq, d, sub), :]
        bg = [src(3, *u) for u in units]
        b_c = [col_stack(bg[i], 8 * p + 2 * d, 8 * p + 2 * d + 1)
               for i, (sq, p, d, sub) in enumerate(units)]
        g_c = [col_stack(bg[i], 8 * p + 4 + 2 * d, 8 * p + 5 + 2 * d)
               for i, (sq, p, d, sub) in enumerate(units)]
        yield
        ld = lambda a, u: (fwd_refs, bwd_refs)[u[2]][a][rows_of(c, u[0], u[2], u[3]), u[1] * LANES:(u[1] + 1) * LANES]
        k2 = [jnp.concatenate([ld(1, u)] * 2, axis=0) for u in units]
        ks = [jnp.where(same, k2[i], 0.0) for i in idx]
        qs = [jnp.where(same, jnp.concatenate([ld(0, u)] * 2, axis=0), 0.0) for u in units]
        ks_b = [ks[i].astype(BF16) for i in idx]
        prod = [_dot_nt(jnp.concatenate([ks_b[i], qs[i].astype(BF16)], axis=0), ks_b[i]) for i in idx]
        yield
        decay = [jnp.exp(jnp.where(incl[dirs[i]], g_c[i] - g_c[i].T, NEG)) for i in idx]
        yield
        eg = [jnp.exp(g_c[i]) for i in idx]
        g_last = [row_stack(g_c[i], CHUNK - 1 if dirs[i] == 0 else 0) for i in idx]
        f["l"] = [jnp.where(strict[dirs[i]], b_c[i] * prod[i][:n2] * decay[i], 0.0) for i in idx]
        yield
        v2 = [jnp.concatenate([ld(2, u)] * 2, axis=0) for u in units]
        f["rhs"] = [(b_c[i] * jnp.where(same, v2[i], eg[i] * pltpu.roll(k2[i], CHUNK, axis=1))).astype(BF16)
                    for i in idx]
        yield
        f["at"] = [(prod[i][n2:] * decay[i]).astype(BF16) for i in idx]
        f["qg"] = [(qs[i] * eg[i]).astype(BF16) for i in idx]
        yield
        f["kt"] = [(ks[i] * jnp.exp(g_last[i] - g_c[i])).T.astype(BF16) for i in idx]
        f["gl"] = [jnp.exp(g_last[i]) for i in idx]
        yield

    def inverse(c):
        f, o = feats[c], solved[c]
        l_mat = f["l"]
        t_inv = [eye - jnp.where(level_mask[0], l_mat[i], 0.0) for i in idx]
        l_b = [l_mat[i].astype(BF16) for i in idx]
        for lm in level_sel:
            t_b = [t_inv[i].astype(BF16) for i in idx]
            y = [_dot(l_b[i] * lm, t_b[i]).astype(BF16) for i in idx]
            yield
            t_inv = [t_inv[i] - _dot(t_b[i], y[i]) for i in idx]
            yield
        xs = [_dot(t_inv[i].astype(BF16), f["rhs"][i]) for i in idx]
        yield
        o["u"] = [jnp.where(same, xs[i], 0.0) for i in idx]
        o["wq"] = [jnp.concatenate([jnp.where(same, pltpu.roll(xs[i], CHUNK, axis=1), 0.0).astype(BF16),
                                    f["qg"][i]], axis=0) for i in idx]
        yield

    def scan(c):
        f, o = feats[c], solved[c]
        for sub in range(nsub):
            ui = [sub * len(chains) + k for k in range(len(chains))]
            ws = [_dot(o["wq"][i], states[k].astype(BF16)) for k, i in enumerate(ui)]
            yield
            u_b = [(o["u"][i] - ws[k][:n2]).astype(BF16) for k, i in enumerate(ui)]
            yield
            o_st = [ws[k][n2:] + _dot(f["at"][i], u_b[k]) for k, i in enumerate(ui)]
            for k, i in enumerate(ui):
                states[k] = f["gl"][i] * states[k] + _dot(f["kt"][i], u_b[k])
            yield
            for k, i in enumerate(ui):
                sq, p, d, _ = units[i]
                dst = of_ref if d == 0 else ob_ref
                dst[rows_of(c, sq, d, sub), p * LANES:(p + 1) * LANES] = o_st[k][:CHUNK] + o_st[k][CHUNK:]
            yield
        feats[c] = solved[c] = None

    for t in range(n_steps + 2):
        gens = []
        if 1 <= t <= n_steps:
            gens.append(inverse(t - 1))
        if t < n_steps:
            gens.append(features(t))
        if 2 <= t:
            gens.append(scan(t - 2))
        _run_interleaved(gens)

    for u in range(len(chains)):
        s_scr[u] = states[u]
    if has_sfin:
        @pl.when(j == pl.num_programs(1) - 1)
        def _():
            for u, (sq, p, d) in enumerate(chains):
                sfin_ref[sq, p, d] = states[u]


def _gdn_scan(feat, s0, *, t, row0, n_seq, want_state):
    tb = math.gcd(t, _SCAN_ROWS)
    nb = t // tb
    ns = math.gcd(n_seq, _SCAN_ROWS // tb) if nb == 1 else 1
    nsub = math.gcd(tb // CHUNK, max(1, _SCAN_UNITS // (ns * PAIRS)))
    rows = ns * tb
    assert row0 % rows == 0
    blk0 = row0 // rows
    n2 = 2 * CHUNK
    w = PAIRS * LANES
    has_s0 = s0 is not None

    def blocks(mirror):
        row = (lambda b, j: b * nb + (nb - 1 - j)) if mirror else (lambda b, j: b * nb + j)
        return ([pl.BlockSpec((rows, w), lambda b, j: (blk0 + row(b, j), 0))] * 3
                + [pl.BlockSpec((rows, LANES), lambda b, j: (blk0 + row(b, j), 0))]), row

    f_specs, f_row = blocks(False)
    b_specs, b_row = blocks(True)
    in_specs = f_specs + b_specs
    args = list(feat) * 2
    state_spec = pl.BlockSpec((ns, PAIRS, 2, n2, n2), lambda b, j: (b, 0, 0, 0, 0))
    if has_s0:
        in_specs.append(state_spec)
        args.append(s0)
    out_shape = [jax.ShapeDtypeStruct((n_seq * t, w), F32)] * 2
    out_specs = [pl.BlockSpec((rows, w), lambda b, j: (f_row(b, j), 0)),
                 pl.BlockSpec((rows, w), lambda b, j: (b_row(b, j), 0))]
    if want_state:
        out_shape.append(jax.ShapeDtypeStruct((n_seq, PAIRS, 2, n2, n2), F32))
        out_specs.append(state_spec)
    return pl.pallas_call(
        functools.partial(_gdn_scan_kernel, nch=tb // CHUNK, ns=ns, nsub=nsub, has_s0=has_s0, has_sfin=want_state),
        out_shape=out_shape,
        grid=(n_seq // ns, nb),
        in_specs=in_specs,
        out_specs=out_specs,
        scratch_shapes=[pltpu.VMEM((ns * 2 * PAIRS, n2, n2), F32)],
        compiler_params=_cparams(("parallel", "arbitrary"), "gdn_scan"),
        name="gdn_scan_ctx" if want_state else "gdn_scan_lat",
    )(*args)


def _prep_kernel(*refs, rope):
    bq_ref, cq_ref, bk_ref, bv_ref, ck_ref, cv_ref, qkn_ref = refs[:7]
    pos = 7
    cs = sn = None
    if rope:
        cs, sn = refs[7][...], refs[8][...]
        pos = 11
    qb_ref, qc_ref, kdb_ref, vdb_ref, kdc_ref, vdc_ref = refs[pos:pos + 6]
    pos += 6
    tr = bq_ref.shape[0]
    lane = lax.broadcasted_iota(jnp.int32, (tr, LANES), 1)
    lane_lo = lane < HEAD_DIM
    first_half = (lane % HEAD_DIM) < HEAD_DIM // 2
    qkn = qkn_ref[...]

    ones_bd = ((lax.broadcasted_iota(jnp.int32, (LANES, LANES), 0) < HEAD_DIM)
               == (lax.broadcasted_iota(jnp.int32, (LANES, LANES), 1) < HEAD_DIM)).astype(BF16)

    def norm(x, w_row):
        sq = x * x
        hi = sq.astype(BF16)
        lo = (sq - hi.astype(F32)).astype(BF16)
        ms = (_dot(hi, ones_bd) + _dot(lo, ones_bd)) * (1.0 / HEAD_DIM)
        return x * lax.rsqrt(ms + EPS) * w_row

    def rot(y):
        if not rope:
            return y
        partner = jnp.where(first_half, pltpu.roll(y, LANES - HEAD_DIM // 2, axis=1),
                            pltpu.roll(y, HEAD_DIM // 2, axis=1))
        return y * cs + partner * sn

    def dup(x, o_ref):
        lo = jnp.where(lane_lo, x, 0.0)
        hi = jnp.where(lane_lo, 0.0, x)
        d0 = (lo + pltpu.roll(lo, HEAD_DIM, axis=1)).astype(BF16)
        d1 = (hi + pltpu.roll(hi, HEAD_DIM, axis=1)).astype(BF16)
        if len(o_ref.shape) == 3:
            o_ref[0, :, 0:LANES] = d0
            o_ref[0, :, LANES:2 * LANES] = d1
        else:
            o_ref[:, 0:LANES] = d0
            o_ref[:, LANES:2 * LANES] = d1

    for q_ref, o_ref, wi in ((bq_ref, qb_ref, 0), (cq_ref, qc_ref, 2)):
        for g in range(PAIRS):
            y = rot(norm(q_ref[:, g * LANES:(g + 1) * LANES], qkn[wi:wi + 1, :])) * HEAD_DIM ** -0.5
            o_ref[2 * g] = jnp.where(lane_lo, y, 0.0).astype(BF16)
            o_ref[2 * g + 1] = jnp.where(lane_lo, 0.0, y).astype(BF16)
    kb_n = norm(bk_ref[...], qkn[1:2, :])
    kc_n = norm(ck_ref[...], qkn[3:4, :])
    if not rope:
        kbn_ref, vbn_ref, kcn_ref, vcn_ref = refs[pos:pos + 4]
        kbn_ref[...] = kb_n
        vbn_ref[...] = bv_ref[...]
        kcn_ref[...] = kc_n
        vcn_ref[...] = cv_ref[...]
    dup(rot(kb_n), kdb_ref)
    dup(bv_ref[...], vdb_ref)
    dup(rot(kc_n), kdc_ref)
    dup(cv_ref[...], vdc_ref)


def _attn_prep(proj, qkn_rows, rope_tables, glob_kv, *, row0, n_rows, t):
    rope = rope_tables is not None
    tr = math.gcd(t, 512)
    blk0 = row0 // tr

    def col(g, width):
        gw = width // LANES
        return pl.BlockSpec((tr, width), lambda i, g=g, gw=gw: (blk0 + i, g // gw))

    in_specs = [col(G_BQ, 4 * LANES), col(G_CQ, 4 * LANES), col(G_BK, LANES), col(G_BV, LANES),
                col(G_CK, LANES), col(G_CV, LANES), pl.BlockSpec((4, LANES), lambda i: (0, 0))]
    args = [proj] * 6 + [qkn_rows]
    q_shape = jax.ShapeDtypeStruct((HEADS, n_rows, LANES), BF16)
    kv_shape = jax.ShapeDtypeStruct((n_rows, 2 * LANES), BF16)
    q_spec = pl.BlockSpec((HEADS, tr, LANES), lambda i: (0, i, 0))
    kv_spec = pl.BlockSpec((tr, 2 * LANES), lambda i: (i, 0))
    aliases = {}
    if rope:
        per_seq = t // tr
        in_specs += [pl.BlockSpec((tr, LANES), lambda i: (i % per_seq, 0))] * 2 + [_any_spec()] * 2
        aliases = {len(args) + 2: 2, len(args) + 3: 3}
        args += list(rope_tables) + list(glob_kv)
        past = glob_kv[0].shape[1] - t
        assert past % tr == 0
        glob_spec = pl.BlockSpec((1, tr, 2 * LANES), lambda i: (i // per_seq, past // tr + i % per_seq, 0))
        glob_shape = jax.ShapeDtypeStruct(glob_kv[0].shape, BF16)
        out_shape = [q_shape, q_shape, glob_shape, glob_shape, kv_shape, kv_shape]
        out_specs = [q_spec, q_spec, glob_spec, glob_spec, kv_spec, kv_spec]
    else:
        out_shape = [q_shape, q_shape] + [kv_shape] * 4 + [jax.ShapeDtypeStruct((n_rows, LANES), F32)] * 4
        out_specs = [q_spec, q_spec] + [kv_spec] * 4 + [pl.BlockSpec((tr, LANES), lambda i: (i, 0))] * 4
    return pl.pallas_call(
        functools.partial(_prep_kernel, rope=rope),
        out_shape=out_shape,
        grid=(n_rows // tr,),
        in_specs=in_specs,
        out_specs=out_specs,
        input_output_aliases=aliases,
        compiler_params=_cparams(("parallel",), "attn_prep"),
        name="attn_prep_lat" if rope else "attn_prep_ctx",
    )(*args)


def _stacked_q(q_ref, g, tq):
    return q_ref[GROUP_HEADS * g:GROUP_HEADS * (g + 1)].reshape(GROUP_HEADS * tq, LANES)


def _unstack_group(o_st, tq, lane_lo):
    a = jnp.where(lane_lo, o_st[0:tq], o_st[tq:2 * tq])
    b = jnp.where(lane_lo, o_st[2 * tq:3 * tq], o_st[3 * tq:4 * tq])
    return jnp.concatenate([a, b], axis=1)


def _with_ones(v, ones):
    return jnp.concatenate([v, ones], axis=1)


def _sink_col(sink_row, g, tq):
    parts = [jnp.broadcast_to(sink_row[:, GROUP_HEADS * g + j:GROUP_HEADS * g + j + 1], (tq, 1))
             for j in range(GROUP_HEADS)]
    return jnp.concatenate(parts, axis=0)


def _attn_ctx_kernel(qb_ref, kdb_ref, vdb_ref, qc_ref, kdc_ref, vdc_ref, sink_ref, yb_ref, yc_ref):
    tq = qb_ref.shape[1]
    lane_lo_q = _lane_lo(tq)
    sink_row = sink_ref[...]
    groups = range(KV_HEADS)
    gs = [slice(g * LANES, (g + 1) * LANES) for g in groups]
    jobs = [(qb_ref, kdb_ref, vdb_ref, yb_ref, False, g) for g in groups]
    jobs += [(qc_ref, kdc_ref, vdc_ref, yc_ref, True, g) for g in groups]
    s = [_dot_nt(_stacked_q(q_ref, g, tq), kd_ref[:, gs[g]]) for q_ref, kd_ref, _, _, _, g in jobs]
    for si, (_, _, vd_ref, y_ref, use_sink, g) in zip(s, jobs):
        m = jnp.max(si, axis=-1, keepdims=True)
        if use_sink:
            snk = _sink_col(sink_row, g, tq)
            m = jnp.maximum(m, snk)
        p = jnp.exp(si - m)
        l = jnp.sum(p, axis=-1, keepdims=True)
        if use_sink:
            l = l + jnp.exp(snk - m)
        o = _dot(p.astype(BF16), vd_ref[:, gs[g]]) / l
        y_ref[:, 2 * g * LANES:(2 * g + 2) * LANES] = _unstack_group(o, tq, lane_lo_q).astype(y_ref.dtype)


def _attn_ctx(qb, kdb, vdb, qc, kdc, vdc, sink_row, *, t):
    n = kdb.shape[0]
    q_spec = pl.BlockSpec((HEADS, t, LANES), lambda b: (0, b, 0))
    kv_spec = pl.BlockSpec((t, 2 * LANES), lambda b: (b, 0))
    y_spec = pl.BlockSpec((t, 4 * LANES), lambda b: (b, 0))
    return pl.pallas_call(
        _attn_ctx_kernel,
        out_shape=[jax.ShapeDtypeStruct((n, 4 * LANES), BF16)] * 2,
        grid=(n // t,),
        in_specs=[q_spec, kv_spec, kv_spec, q_spec, kv_spec, kv_spec,
                  pl.BlockSpec((1, LANES), lambda b: (0, 0))],
        out_specs=[y_spec, y_spec],
        compiler_params=_cparams(("parallel",), "attn_ctx"),
        name="attn_ctx",
    )(qb, kdb, vdb, qc, kdc, vdc, sink_row)


_TQ_B = 256
_TK_B = 1536


def _attn_glob_kernel(q_ref, kd_ref, vd_ref, y_ref, s_scr, *, tk):
    tq = q_ref.shape[1]
    n_blk = kd_ref.shape[1] // tk
    groups = range(KV_HEADS)
    gs = [slice(g * LANES, (g + 1) * LANES) for g in groups]
    q = [_stacked_q(q_ref, g, tq) for g in groups]

    def scores_into(j, slot):
        for g in groups:
            s_scr[slot, g] = _dot_nt(q[g], kd_ref[0, j * tk:(j + 1) * tk, gs[g]])

    rows = GROUP_HEADS * tq
    m = [jnp.full((rows, LANES), NEG, F32) for _ in groups]
    acc = [jnp.zeros((rows, 2 * LANES), F32) for _ in groups]
    ones = jnp.ones((tk, LANES), BF16)
    scores_into(0, 0)
    for j in range(n_blk):
        if j + 1 < n_blk:
            scores_into(j + 1, (j + 1) % 2)
        for g in groups:
            s = s_scr[j % 2, g]
            m_new = jnp.maximum(m[g], jnp.max(s, axis=-1, keepdims=True))
            p = jnp.exp(s - m_new[:, 0:1]).astype(BF16)
            alpha = jnp.exp(m[g] - m_new)
            acc[g] = (jnp.concatenate([alpha, alpha], axis=1) * acc[g]
                      + _dot(p, _with_ones(vd_ref[0, j * tk:(j + 1) * tk, gs[g]], ones)))
            m[g] = m_new
    lane_lo_q = _lane_lo(tq)
    for g in groups:
        o = acc[g][:, :LANES] / acc[g][:, LANES:]
        y_ref[:, 2 * g * LANES:(2 * g + 2) * LANES] = _unstack_group(o, tq, lane_lo_q).astype(y_ref.dtype)


def _attn_glob(q, kd, vd, *, t):
    n = q.shape[1]
    s_len = kd.shape[1]
    tq, tk = math.gcd(t, _TQ_B), math.gcd(s_len, _TK_B)
    per_seq = t // tq
    kv_spec = pl.BlockSpec((1, s_len, 2 * LANES), lambda b, i: (b, 0, 0))
    return pl.pallas_call(
        functools.partial(_attn_glob_kernel, tk=tk),
        out_shape=jax.ShapeDtypeStruct((n, 4 * LANES), BF16),
        grid=(n // t, per_seq),
        in_specs=[pl.BlockSpec((HEADS, tq, LANES), lambda b, i: (0, b * per_seq + i, 0)), kv_spec, kv_spec],
        out_specs=pl.BlockSpec((tq, 4 * LANES), lambda b, i: (b * per_seq + i, 0)),
        scratch_shapes=[pltpu.VMEM((2, KV_HEADS, GROUP_HEADS * tq, tk), F32)],
        compiler_params=_cparams(("parallel", "parallel"), "attn_glob"),
        name="attn_glob",
    )(q, kd, vd)


def _attn_win_kernel(q_ref, kx_ref, vx_ref, kp_ref, kc_ref, kn_ref, vp_ref, vc_ref, vn_ref,
                     sink_ref, y_ref):
    i = pl.program_id(1)
    nblk = pl.num_programs(1)
    tq = q_ref.shape[1]
    rows = GROUP_HEADS * tq
    lane_lo_q = _lane_lo(tq)
    sink_row = sink_ref[...]
    rr_w = lax.broadcasted_iota(jnp.int32, (rows, WINDOW), 0) % tq
    cc_w = lax.broadcasted_iota(jnp.int32, (rows, WINDOW), 1)
    rr_q = lax.broadcasted_iota(jnp.int32, (rows, tq), 0) % tq
    cc_q = lax.broadcasted_iota(jnp.int32, (rows, tq), 1)
    mask_prev = (cc_w >= rr_w) & (i > 0)
    mask_cur = jnp.abs(rr_q - cc_q) <= WINDOW
    mask_next = (cc_w <= rr_w - (tq - WINDOW)) & (i < nblk - 1)
    groups = range(KV_HEADS)
    gs = [slice(g * LANES, (g + 1) * LANES) for g in groups]
    qs = [_stacked_q(q_ref, g, tq) for g in groups]
    p_len = kx_ref.shape[1]
    mask_edge = jnp.concatenate([mask_prev, mask_next], axis=1)
    edge = lambda p_ref, n_ref, g: jnp.concatenate([p_ref[:, gs[g]], n_ref[:, gs[g]]], axis=0)
    s = [jnp.concatenate([_dot_nt(qs[g], kx_ref[0, :, gs[g]]),
                          jnp.where(mask_edge, _dot_nt(qs[g], edge(kp_ref, kn_ref, g)), NEG),
                          jnp.where(mask_cur, _dot_nt(qs[g], kc_ref[:, gs[g]]), NEG)], axis=1) for g in groups]
    parts = ((0, p_len, lambda g: vx_ref[0, :, gs[g]]), (p_len, 2 * WINDOW, lambda g: edge(vp_ref, vn_ref, g)),
             (p_len + 2 * WINDOW, tq, lambda g: vc_ref[:, gs[g]]))
    for g in groups:
        snk = _sink_col(sink_row, g, tq)
        m = jnp.maximum(jnp.max(s[g], axis=-1, keepdims=True), snk)
        p = jnp.exp(s[g] - m).astype(BF16)
        ol = None
        for start, size, val in parts:
            d = _dot(p[:, start:start + size], _with_ones(val(g), jnp.ones((size, LANES), BF16)))
            ol = d if ol is None else ol + d
        o = ol[:, :LANES] / (ol[:, LANES:] + jnp.exp(snk - m))
        y_ref[:, 2 * g * LANES:(2 * g + 2) * LANES] = _unstack_group(o, tq, lane_lo_q).astype(y_ref.dtype)


def _attn_win(q, kd, vd, kx, vx, sink_row, layer, *, t):
    n = q.shape[1]
    p_len = kx.shape[2]
    tq = math.gcd(t, 2 * WINDOW)
    nblk = t // tq
    per = tq // WINDOW
    wblk = t // WINDOW
    x_spec = pl.BlockSpec((None, 1, p_len, 2 * LANES), lambda b, i: (layer, b, 0, 0))
    prev = pl.BlockSpec((WINDOW, 2 * LANES), lambda b, i: (b * wblk + jnp.maximum(i * per - 1, 0), 0))
    cur = pl.BlockSpec((tq, 2 * LANES), lambda b, i: (b * nblk + i, 0))
    nxt = pl.BlockSpec((WINDOW, 2 * LANES), lambda b, i: (b * wblk + jnp.minimum((i + 1) * per, wblk - 1), 0))
    return pl.pallas_call(
        _attn_win_kernel,
        out_shape=jax.ShapeDtypeStruct((n, 4 * LANES), BF16),
        grid=(n // t, nblk),
        in_specs=[pl.BlockSpec((HEADS, tq, LANES), lambda b, i: (0, b * nblk + i, 0)),
                  x_spec, x_spec, prev, cur, nxt, prev, cur, nxt,
                  pl.BlockSpec((1, LANES), lambda b, i: (0, 0))],
        out_specs=pl.BlockSpec((tq, 4 * LANES), lambda b, i: (b * nblk + i, 0)),
        compiler_params=_cparams(("parallel", "parallel"), "attn_win"),
        name="attn_win",
    )(q, kx, vx, kd, kd, kd, vd, vd, vd, sink_row)


def _mix_kernel(x_ref, ofc_ref, obc_ref, ybc_ref, ycc_ref, ofl_ref, obl_ref, ybl_ref, ycl_ref, z_ref,
                ga_ref, gb_ref, gc_ref, mod_ref, anorm_ref, wa_ref, wb_ref, wc_ref, wo_ref, o_ref, *, ctx_tiles):
    d = x_ref.shape[1]
    tm = x_ref.shape[0]
    lane_lo = _lane_lo(tm)
    anorm = anorm_ref[...]
    is_ctx = pl.program_id(0) < ctx_tiles

    def pick(c_ref, l_ref, cols=slice(None)):
        return jnp.where(is_ctx, c_ref[:, cols], l_ref[:, cols])

    ya = []
    for g in range(PAIRS):
        gl = slice(g * LANES, (g + 1) * LANES)
        o = pick(ofc_ref, ofl_ref, gl) + pick(obc_ref, obl_ref, gl)
        ms = _half_sums(o * o, lane_lo) * (1.0 / HEAD_DIM)
        ya.append((o * lax.rsqrt(ms + EPS) * anorm * _silu(z_ref[:, gl])).astype(BF16))
    ya = jnp.concatenate(ya, axis=1)
    merged = (_sigmoid(ga_ref[...]) * _dot(ya, wa_ref[...])
              + _sigmoid(gb_ref[...]) * _dot(pick(ybc_ref, ybl_ref), wb_ref[...])
              + _sigmoid(gc_ref[...]) * _dot(pick(ycc_ref, ycl_ref), wc_ref[...]))
    gate1 = mod_ref[0][:, 2 * d:3 * d]
    o_ref[...] = x_ref[...] + gate1 * _dot(merged.astype(BF16), wo_ref[...])


def _mix(x, ctx_parts, lat_parts, proj, mods, anorm_row, wa, wb, wc, wo, layer, n_ctx, dec_seq):
    nt, d = x.shape
    tm = _row_tile(n_ctx, dec_seq, 512)
    ctx_tiles = n_ctx // tm
    w = PAIRS * LANES
    c_spec = pl.BlockSpec((tm, w), lambda i: (jnp.minimum(i, ctx_tiles - 1), 0))
    l_spec = pl.BlockSpec((tm, w), lambda i: (jnp.maximum(i - ctx_tiles, 0), 0))
    gw = d // LANES

    def gate(k):
        return pl.BlockSpec((tm, d), lambda i, k=k: (i, G_GATES // gw + k))

    def weight(a):
        return pl.BlockSpec((None,) + a.shape[1:], lambda i: (layer, 0, 0))

    return pl.pallas_call(
        functools.partial(_mix_kernel, ctx_tiles=ctx_tiles),
        out_shape=jax.ShapeDtypeStruct((nt, d), F32),
        grid=(nt // tm,),
        in_specs=[pl.BlockSpec((tm, d), lambda i: (i, 0))] + [c_spec] * 4 + [l_spec] * 4
                 + [pl.BlockSpec((tm, w), lambda i: (i, G_Z // PAIRS)), gate(0), gate(1), gate(2),
                    pl.BlockSpec((1, 1, 6 * d), lambda i: (_mod_index(i, tm, n_ctx, dec_seq), 0, 0)),
                    pl.BlockSpec(anorm_row.shape, lambda i: (0, 0)),
                    weight(wa), weight(wb), weight(wc), weight(wo)],
        out_specs=pl.BlockSpec((tm, d), lambda i: (i, 0)),
        compiler_params=_cparams(("parallel",), "mix_out"),
        name="mix_out",
    )(x, *ctx_parts, *lat_parts, proj, proj, proj, proj, mods, anorm_row, wa, wb, wc, wo)


_FF_CHUNK = 1024


def _ffn_kernel(x_ref, mod_ref, ln_ref, w1_ref, w2_ref, o_ref):
    d = x_ref.shape[1]
    m = mod_ref[0]
    x = x_ref[...]
    h = _mod_norm(x, ln_ref[...], m[:, 3 * d:4 * d], m[:, 4 * d:5 * d]).astype(BF16)
    acc = None
    for f in range(0, w1_ref.shape[1], _FF_CHUNK):
        hid = jnp.maximum(_dot(h, w1_ref[:, f:f + _FF_CHUNK]), 0.0)
        part = _dot((hid * hid).astype(BF16), w2_ref[f:f + _FF_CHUNK, :])
        acc = part if acc is None else acc + part
    o_ref[...] = x + m[:, 5 * d:6 * d] * acc


def _ffn(x, mods, ln2, w1, w2, layer, n_ctx, dec_seq):
    nt, d = x.shape
    tm = _row_tile(n_ctx, dec_seq, 1024)
    return pl.pallas_call(
        _ffn_kernel,
        out_shape=jax.ShapeDtypeStruct((nt, d), F32),
        grid=(nt // tm,),
        in_specs=[pl.BlockSpec((tm, d), lambda i: (i, 0)),
                  pl.BlockSpec((1, 1, 6 * d), lambda i: (_mod_index(i, tm, n_ctx, dec_seq), 0, 0)),
                  pl.BlockSpec((1, d), lambda i: (0, 0)),
                  pl.BlockSpec((None,) + w1.shape[1:], lambda i: (layer, 0, 0), pipeline_mode=pl.Buffered(1)),
                  pl.BlockSpec((None,) + w2.shape[1:], lambda i: (layer, 0, 0), pipeline_mode=pl.Buffered(1))],
        out_specs=pl.BlockSpec((tm, d), lambda i: (i, 0)),
        compiler_params=_cparams(("parallel",), "ffn"),
        name="ffn",
    )(x, mods, ln2, w1, w2)


def _ba_lane_order():
    cols = []
    for p in range(PAIRS):
        for base in (0, 2 * HEADS):
            for d in range(2):
                cols += [base + d * HEADS + 2 * p, base + d * HEADS + 2 * p + 1]
    return cols


def _reorder_w_in(w_in):
    a_w = HEADS * HEAD_DIM
    kvw = KV_HEADS * HEAD_DIM
    o_beta = 4 * a_w
    o_bq = o_beta + 4 * HEADS
    o_bk = o_bq + a_w
    o_bv = o_bk + kvw
    o_cq = o_bv + kvw
    o_ck = o_cq + a_w
    o_cv = o_ck + kvw
    o_g = o_cv + kvw
    seg = lambda a, n: w_in[:, :, a:a + n]
    ba = jnp.take(seg(o_beta, 4 * HEADS), jnp.array(_ba_lane_order(), jnp.int32), axis=2)
    ba = jnp.pad(ba, ((0, 0), (0, 0), (0, 2 * LANES - ba.shape[2])))
    out = jnp.concatenate([seg(0, o_beta), seg(o_bq, a_w), seg(o_cq, a_w), seg(o_g, w_in.shape[2] - o_g),
                           seg(o_bk, kvw), seg(o_bv, kvw), seg(o_ck, kvw), seg(o_cv, kvw), ba], axis=2)
    assert out.shape[2] == IN_COLS_PAD
    return out.astype(BF16)


def _decay_rows(vals):
    rows = []
    for p in range(PAIRS):
        r = jnp.stack([vals[:, 0, 2 * p], vals[:, 0, 2 * p + 1], vals[:, 1, 2 * p], vals[:, 1, 2 * p + 1]], axis=-1)
        rows.append(jnp.pad(r, ((0, 0), (4, 0))))
    out = jnp.concatenate(rows, axis=-1)
    return jnp.pad(out, ((0, 0), (0, LANES - out.shape[1])))[:, None, :]


def _rope_tables(n_tokens):
    rows = n_tokens // GRID_W
    row_id = jnp.repeat(jnp.arange(rows, dtype=F32), GRID_W)
    col_id = jnp.tile(jnp.arange(GRID_W, dtype=F32), rows)
    n_freq = HEAD_DIM // 4
    inv_freq = ROPE_THETA ** (-jnp.arange(n_freq, dtype=F32) / n_freq)
    ang = jnp.concatenate([row_id[:, None] * inv_freq, col_id[:, None] * inv_freq], axis=-1)
    cos, sin = jnp.cos(ang), jnp.sin(ang)
    cs = jnp.tile(cos, (1, 4))
    sn = jnp.tile(jnp.concatenate([-sin, sin], axis=-1), (1, 2))
    return cs, sn


def _dup_kv(x, pad_rows=0):
    x = jnp.swapaxes(x, 0, 1)
    out = jnp.concatenate([x[..., 0, :], x[..., 0, :], x[..., 1, :], x[..., 1, :]], axis=-1).astype(BF16)
    return jnp.pad(out, ((0, 0), (0, 0), (0, pad_rows), (0, 0)))


def _block_diag_states(s):
    b, depth = s.shape[:2]
    s = s.reshape(b, depth, 2, PAIRS, 2, HEAD_DIM, HEAD_DIM).transpose(1, 0, 3, 2, 4, 5, 6)
    z = jnp.zeros_like(s[..., 0, :, :])
    top = jnp.concatenate([s[..., 0, :, :], z], axis=-1)
    bot = jnp.concatenate([z, s[..., 1, :, :]], axis=-1)
    return jnp.concatenate([top, bot], axis=-2)


def _diag_states(s):
    a = s[:, :, :, :HEAD_DIM, :HEAD_DIM]
    bb = s[:, :, :, HEAD_DIM:, HEAD_DIM:]
    out = jnp.stack([a, bb], axis=3)
    b = s.shape[0]
    return out.transpose(0, 2, 1, 3, 4, 5).reshape(b, 2, HEADS, HEAD_DIM, HEAD_DIM)


def kernel(x_prompt, x_sample, cache_k_glob, cache_v_glob, cache_k_win, cache_v_win, state_delta, c, c_ctx, w_mod, b_mod, ln1, ln2, w_in, conv_qkv, a_log, dt_bias, a_norm, qk_norm, sink, w_br_a, w_br_b, w_br_c, w_o, w_ff1, w_ff2):
    batch, seq, d = x_prompt.shape
    dec_batch, dec_seq, _ = x_sample.shape
    depth = w_mod.shape[0]
    n_ctx = batch * seq
    n_lat = dec_batch * dec_seq
    assert 1 + dec_batch <= SUBLANES

    w_in_r = _reorder_w_in(w_in)
    wa, wb, wc, wo = (w.astype(BF16) for w in (w_br_a, w_br_b, w_br_c, w_o))
    w1, w2 = w_ff1.astype(BF16), w_ff2.astype(BF16)
    alog_rows = _decay_rows(a_log)
    dtb_rows = _decay_rows(dt_bias)
    anorm_rows = jnp.tile(a_norm, (1, 2))[:, None, :]
    qkn_rows = jnp.tile(qk_norm, (1, 1, 2))
    sink_rows = jnp.pad(sink, ((0, 0), (0, LANES - HEADS)))[:, None, :]
    rope_tables = _rope_tables(dec_seq)
    cvec = jnp.concatenate([c_ctx[None], c, jnp.zeros((SUBLANES - 1 - dec_batch, d), F32)], axis=0)
    mods_all = _mod_vectors(cvec, w_mod, b_mod)
    glob_k, glob_v = _dup_kv(cache_k_glob, dec_seq), _dup_kv(cache_v_glob, dec_seq)
    win_k, win_v = _dup_kv(cache_k_win), _dup_kv(cache_v_win)
    s0_lat = _block_diag_states(state_delta)

    x = jnp.concatenate([x_prompt.reshape(n_ctx, d), x_sample.reshape(n_lat, d)], axis=0)
    new_kg, new_vg, new_kw, new_vw, new_st = [], [], [], [], []
    for l in range(depth):
        mods = mods_all[l][:, None, :]
        proj = _in_proj(x, mods, ln1[l][None], w_in_r, l, n_ctx, dec_seq)

        feat = _gdn_prep(proj, conv_qkv[l], alog_rows[l], dtb_rows[l], n_ctx, seq, dec_seq)
        of_c, ob_c, s_fin = _gdn_scan(feat, None, t=seq, row0=0, n_seq=batch, want_state=True)
        of_l, ob_l = _gdn_scan(feat, s0_lat[l], t=dec_seq, row0=n_ctx, n_seq=dec_batch, want_state=False)

        qb_c, qc_c, kdb_c, vdb_c, kdc_c, vdc_c, kb_n, vb_n, kc_n, vc_n = _attn_prep(
            proj, qkn_rows[l], None, None, row0=0, n_rows=n_ctx, t=seq)
        qb_l, qc_l, kd_all, vd_all, kdc_l, vdc_l = _attn_prep(
            proj, qkn_rows[l], rope_tables, (glob_k[l], glob_v[l]), row0=n_ctx, n_rows=n_lat, t=dec_seq)

        yb_c, yc_c = _attn_ctx(qb_c, kdb_c, vdb_c, qc_c, kdc_c, vdc_c, sink_rows[l], t=seq)
        yb_l = _attn_glob(qb_l, kd_all, vd_all, t=dec_seq)
        yc_l = _attn_win(qc_l, kdc_l, vdc_l, win_k, win_v, sink_rows[l], l, t=dec_seq)

        x = _mix(x, (of_c, ob_c, yb_c, yc_c), (of_l, ob_l, yb_l, yc_l), proj, mods, anorm_rows[l],
                 wa, wb, wc, wo, l, n_ctx, dec_seq)
        x = _ffn(x, mods, ln2[l][None], w1, w2, l, n_ctx, dec_seq)

        kv_shape = (batch, seq, KV_HEADS, HEAD_DIM)
        new_kg.append(kb_n.reshape(kv_shape))
        new_vg.append(vb_n.reshape(kv_shape))
        new_kw.append(kc_n.reshape(kv_shape))
        new_vw.append(vc_n.reshape(kv_shape))
        new_st.append(_diag_states(s_fin))

    y_prompt = x[:n_ctx].reshape(batch, seq, d)
    y_sample = x[n_ctx:].reshape(dec_batch, dec_seq, d)
    return (y_prompt, y_sample, jnp.stack(new_kg, axis=1), jnp.stack(new_vg, axis=1),
            jnp.stack(new_kw, axis=1), jnp.stack(new_vw, axis=1), jnp.stack(new_st, axis=1))
```
